```python
import jax, jax.numpy as jnp
from jax import lax
import numpy as np

D_MODEL = 1024
BATCH = 8
SEQ = 4096
DEPTH = 2

GRID_W = 64
CTX_LEN = 256
EPS = 1e-6
N_MOD = 6

GLA_HEADS = 4
GLA_DK = D_MODEL // (2 * GLA_HEADS)
GLA_DV = D_MODEL // GLA_HEADS
GLA_RANK = 16
GLA_GATE_NORM = 16.0
CHUNK = 64

LRU_WIDTH = D_MODEL
LRU_BLOCKS = 4
LRU_BLOCK = LRU_WIDTH // LRU_BLOCKS
LRU_C = 8.0
CONV_W = 4
CONV_LEFT = 2
CONV_RIGHT = CONV_W - 1 - CONV_LEFT

RET_HEADS = 4
RET_DK = D_MODEL // (2 * RET_HEADS)
RET_DV = D_MODEL // RET_HEADS
ROPE_BASE = 10000.0

N_BRANCH = 3
BRANCH_WIDTH = D_MODEL

N_EXPERTS = 32
TOP_K = 4
D_EXPERT = D_MODEL
SWIGLU_LIMIT = 7.0
SWIGLU_ALPHA = 1.702
MOE_BLOCK = 256

IN_SIZES = (
    GLA_HEADS * GLA_DK, GLA_HEADS * GLA_DK, GLA_HEADS * GLA_DV, GLA_HEADS * GLA_DV,
    GLA_RANK, GLA_RANK,
    LRU_WIDTH, LRU_WIDTH,
    RET_HEADS * RET_DK, RET_HEADS * RET_DK, RET_HEADS * RET_DV, RET_HEADS * RET_DV,
    N_BRANCH * D_MODEL,
)
IN_TOTAL = int(sum(IN_SIZES))
IN_OFFSETS = tuple(int(o) for o in np.cumsum(IN_SIZES)[:-1])

kernel_name = 'hybrid_gla_rglru_retention_moe_dit'


def rms_norm(x, gain):
    xf = x.astype(jnp.float32)
    y = xf * lax.rsqrt(jnp.mean(xf * xf, axis=-1, keepdims=True) + EPS)
    return (y * gain.astype(jnp.float32)).astype(x.dtype)


def modulate(h, shift, scale):
    return h * (1.0 + scale) + shift


def to_heads(a, n):
    b, t, _ = a.shape
    return a.reshape(b, t, n, -1).transpose(0, 2, 1, 3)


def from_heads(a):
    b, n, t, d = a.shape
    return a.transpose(0, 2, 1, 3).reshape(b, t, n * d)


def head_norm(o, gain, center):
    o = o.astype(jnp.float32)
    if center:
        o = o - jnp.mean(o, axis=-1, keepdims=True)
    o = o * lax.rsqrt(jnp.mean(o * o, axis=-1, keepdims=True) + EPS)
    return from_heads(o) * gain.astype(jnp.float32)


def chunked_linear_attn(q, k, v, g, s0, strict):
    b, h, t, _ = q.shape
    dv = v.shape[-1]
    n = t // CHUNK

    def blocks(a):
        return a.reshape(a.shape[0], a.shape[1], n, CHUNK, a.shape[-1])

    qc, kc, vc = blocks(q), blocks(k), blocks(v)
    G = jnp.cumsum(blocks(g).astype(jnp.float32), axis=3)
    G_last = G[:, :, :, -1:, :]
    q_dec = qc * jnp.exp(G)
    k_inv = kc * jnp.exp(-G)
    k_end = kc * jnp.exp(G_last - G)
    mask = jnp.tril(jnp.ones((CHUNK, CHUNK), bool), k=-1 if strict else 0)
    scores = jnp.where(mask, jnp.einsum('bhncd,bhnsd->bhncs', q_dec, k_inv), 0.0)
    o_intra = jnp.einsum('bhncs,bhnse->bhnce', scores, vc)
    kv = jnp.einsum('bhncd,bhnce->bhnde', k_end, vc)
    decay = jnp.exp(G_last[:, :, :, 0, :])

    def step(state, inp):
        kv_n, dec_n = inp
        return dec_n[..., None] * state + kv_n, state

    final, states_in = lax.scan(step, s0, (jnp.moveaxis(kv, 2, 0), jnp.moveaxis(decay, 2, 0)))
    states_in = jnp.moveaxis(states_in, 0, 2)
    o_inter = jnp.einsum('bhncd,bhnde->bhnce', q_dec, states_in)
    return (o_intra + o_inter).reshape(b, h, t, dv), final


def bidir_linear_attn(q, k, v, g_f, g_b, s0_f, s0_b):
    flip = lambda a: jnp.flip(a, axis=2)
    o_f, s_f = chunked_linear_attn(q, k, v, g_f, s0_f, strict=False)
    o_b, s_b = chunked_linear_attn(flip(q), flip(k), flip(v), flip(g_b), s0_b, strict=True)
    return o_f + flip(o_b), s_f, s_b


def gla_inputs(q, k, v, lr_f, lr_b, wa2, ba):
    qh = to_heads(q, GLA_HEADS).astype(jnp.float32) * GLA_DK ** -0.5
    kh = to_heads(k, GLA_HEADS)
    vh = to_heads(v, GLA_HEADS)
    g_f = to_heads(jax.nn.log_sigmoid((lr_f @ wa2[0] + ba[0]).astype(jnp.float32)) / GLA_GATE_NORM, GLA_HEADS)
    g_b = to_heads(jax.nn.log_sigmoid((lr_b @ wa2[1] + ba[1]).astype(jnp.float32)) / GLA_GATE_NORM, GLA_HEADS)
    return qh, kh, vh, g_f, g_b


def axial_rope(rows, head_dim):
    row = jnp.broadcast_to(jnp.arange(rows)[:, None], (rows, GRID_W)).reshape(-1).astype(jnp.float32)
    col = jnp.broadcast_to(jnp.arange(GRID_W)[None, :], (rows, GRID_W)).reshape(-1).astype(jnp.float32)
    n_freq = head_dim // 4
    inv = ROPE_BASE ** (-jnp.arange(n_freq, dtype=jnp.float32) / n_freq)
    ang = jnp.concatenate([row[:, None] * inv, col[:, None] * inv], axis=-1)
    return jnp.cos(ang), jnp.sin(ang)


def apply_rope(x, cos, sin):
    half = x.shape[-1] // 2
    x1, x2 = x[..., :half], x[..., half:]
    return jnp.concatenate([x1 * cos - x2 * sin, x1 * sin + x2 * cos], axis=-1)


def ret_inputs(q, k, v, cos, sin):
    qh = to_heads(q, RET_HEADS).astype(jnp.float32)
    kh = to_heads(k, RET_HEADS).astype(jnp.float32) * RET_DK ** -0.5
    if cos is not None:
        qh = apply_rope(qh, cos, sin)
        kh = apply_rope(kh, cos, sin)
    return qh, kh, to_heads(v, RET_HEADS)


def ret_log_decay(n_tokens):
    gamma = 1.0 - 2.0 ** (-5.0 - jnp.arange(RET_HEADS, dtype=jnp.float32))
    return jnp.broadcast_to(jnp.log(gamma)[None, :, None, None], (1, RET_HEADS, n_tokens, 1))


def depthwise_conv(x, w, b):
    y = lax.conv_general_dilated(x, w[:, None, :].astype(x.dtype), window_strides=(1,),
                                 padding=[(CONV_LEFT, CONV_RIGHT)],
                                 dimension_numbers=('NWC', 'WIO', 'NWC'),
                                 feature_group_count=x.shape[-1])
    return y + b


def block_diag(x, w):
    b, t, c = x.shape
    return jnp.einsum('btgi,gij->btgj', x.reshape(b, t, LRU_BLOCKS, LRU_BLOCK), w).reshape(b, t, c)


def linear_scan(a, b, h0):
    b = b.at[:, 0].add(a[:, 0] * h0)

    def combine(e1, e2):
        a1, b1 = e1
        a2, b2 = e2
        return a1 * a2, a2 * b1 + b2

    _, h = lax.associative_scan(combine, (a, b), axis=1)
    return h


def rglru_scan(xc, wa, ba, wi, bi, lam, h0):
    r = jax.nn.sigmoid((block_diag(xc, wa) + ba).astype(jnp.float32))
    i = jax.nn.sigmoid((block_diag(xc, wi) + bi).astype(jnp.float32))
    log_a = LRU_C * r * jax.nn.log_sigmoid(lam.astype(jnp.float32))
    a = jnp.exp(log_a)
    b = jnp.sqrt(-jnp.expm1(2.0 * log_a)) * (i * xc.astype(jnp.float32))
    h = linear_scan(a, b, h0)
    return h, h[:, -1]


def rglru_bidir(xc, wa, ba, wi, bi, lam, h0_f, h0_b):
    h_f, last_f = rglru_scan(xc, wa[0], ba[0], wi[0], bi[0], lam[0], h0_f)
    h_b, last_b = rglru_scan(jnp.flip(xc, 1), wa[1], ba[1], wi[1], bi[1], lam[1], h0_b)
    return h_f + jnp.flip(h_b, 1), last_f, last_b


def hybrid_mixer(xn, cn, rope_cos, rope_sin, w_in, gla_wa2, gla_ba, gla_norm, lru_conv_w, lru_conv_b,
                 lru_wa, lru_ba, lru_wi, lru_bi, lru_lam, ret_norm, w_branch, w_out, with_ctx_out):
    f32 = jnp.float32
    nb = xn.shape[0]
    px = jnp.split(xn @ w_in, IN_OFFSETS, axis=-1)
    pc = jnp.split(cn @ w_in, IN_OFFSETS, axis=-1)

    zero_gla = jnp.zeros((nb, GLA_HEADS, GLA_DK, GLA_DV), f32)
    o_gla_c, gs_f, gs_b = bidir_linear_attn(*gla_inputs(*pc[0:3], *pc[4:6], gla_wa2, gla_ba), zero_gla, zero_gla)
    o_gla_x, _, _ = bidir_linear_attn(*gla_inputs(*px[0:3], *px[4:6], gla_wa2, gla_ba), gs_f, gs_b)

    lru_p = (lru_wa, lru_ba, lru_wi, lru_bi, lru_lam)
    zero_lru = jnp.zeros((nb, LRU_WIDTH), f32)
    h_c, hs_f, hs_b = rglru_bidir(depthwise_conv(pc[6], lru_conv_w, lru_conv_b), *lru_p, zero_lru, zero_lru)
    h_x, _, _ = rglru_bidir(depthwise_conv(px[6], lru_conv_w, lru_conv_b), *lru_p, hs_f, hs_b)

    zero_ret = jnp.zeros((nb, RET_HEADS, RET_DK, RET_DV), f32)
    g_c = ret_log_decay(cn.shape[1])
    o_ret_c, rs_f, rs_b = bidir_linear_attn(*ret_inputs(*pc[8:11], None, None), g_c, g_c, zero_ret, zero_ret)
    g_x = ret_log_decay(xn.shape[1])
    o_ret_x, _, _ = bidir_linear_attn(*ret_inputs(*px[8:11], rope_cos, rope_sin), g_x, g_x, rs_f, rs_b)

    def finish(p, o_gla, h_lru, o_ret, dtype):
        gla = head_norm(o_gla, gla_norm, False) * jax.nn.silu(p[3].astype(f32))
        lru = h_lru * jax.nn.gelu(p[7].astype(f32))
        ret = head_norm(o_ret, ret_norm, True) * jax.nn.silu(p[11].astype(f32))
        g_gla, g_lru, g_ret = jnp.split(jax.nn.sigmoid(p[12].astype(f32)), N_BRANCH, axis=-1)
        merged = (g_gla * (gla.astype(dtype) @ w_branch[0])
                  + g_lru * (lru.astype(dtype) @ w_branch[1])
                  + g_ret * (ret.astype(dtype) @ w_branch[2]))
        return merged.astype(dtype) @ w_out

    out_x = finish(px, o_gla_x, h_x, o_ret_x, xn.dtype)
    out_c = finish(pc, o_gla_c, h_c, o_ret_c, cn.dtype) if with_ctx_out else None
    return out_x, out_c


def moe_ffn(h, w_router, b_router, w_gu, b_gu, w_down, b_down):
    n_tok, d = h.shape
    n_assign = n_tok * TOP_K
    logits = (h @ w_router).astype(jnp.float32) + b_router.astype(jnp.float32)
    top_logits, top_idx = lax.top_k(logits, TOP_K)
    top_w = jax.nn.softmax(top_logits, axis=-1)
    flat_e = top_idx.reshape(-1)
    order = jnp.argsort(flat_e)
    sorted_e = flat_e[order]
    sorted_tok = (order // TOP_K).astype(jnp.int32)
    sorted_w = top_w.reshape(-1)[order]
    counts = jnp.bincount(flat_e, length=N_EXPERTS)
    padded = (counts + MOE_BLOCK - 1) // MOE_BLOCK * MOE_BLOCK
    pad_end = jnp.cumsum(padded)
    pad_start = pad_end - padded
    grp_start = jnp.cumsum(counts) - counts
    slot = pad_start[sorted_e] + jnp.arange(n_assign) - grp_start[sorted_e]
    n_blocks = -(-n_assign // MOE_BLOCK) + N_EXPERTS
    n_slots = n_blocks * MOE_BLOCK
    slot_tok = jnp.full((n_slots,), n_tok, jnp.int32).at[slot].set(sorted_tok)
    slot_w = jnp.zeros((n_slots,), jnp.float32).at[slot].set(sorted_w)
    block_e = jnp.minimum(jnp.searchsorted(pad_end, jnp.arange(n_blocks) * MOE_BLOCK, side='right'),
                          N_EXPERTS - 1)
    h_pad = jnp.concatenate([h, jnp.zeros((1, d), h.dtype)], axis=0)

    def expert_block(acc, blk):
        tok, wgt, e = blk
        gu = h_pad[tok] @ w_gu[e] + b_gu[e]
        gate = jnp.minimum(gu[:, 0::2], SWIGLU_LIMIT)
        up = jnp.clip(gu[:, 1::2], -SWIGLU_LIMIT, SWIGLU_LIMIT)
        act = gate * jax.nn.sigmoid(SWIGLU_ALPHA * gate) * (up + 1.0)
        y = act @ w_down[e] + b_down[e]
        return acc.at[tok].add(y.astype(jnp.float32) * wgt[:, None]), None

    acc, _ = lax.scan(expert_block, jnp.zeros((n_tok + 1, d), jnp.float32),
                      (slot_tok.reshape(n_blocks, MOE_BLOCK), slot_w.reshape(n_blocks, MOE_BLOCK), block_e))
    return acc[:n_tok].astype(h.dtype)


def setup_inputs(seed: int = 0) -> dict:
    key = jax.random.key(seed)
    ks = jax.random.split(key, 32)
    f32 = jnp.float32
    D = D_MODEL
    L = DEPTH

    def nrm(k, shape, scale):
        return jax.random.normal(k, shape, f32) * scale

    a0 = jax.random.uniform(ks[17], (L, 2, LRU_WIDTH), f32, 0.9, 0.999)
    root = a0 ** (1.0 / LRU_C)
    lru_lam = jnp.log(root) - jnp.log1p(-root)
    return {
        'x': nrm(ks[0], (BATCH, SEQ, D), 1.0),
        'c': nrm(ks[1], (BATCH, D), 1.0),
        'ctx': nrm(ks[2], (BATCH, CTX_LEN, D), 1.0),
        'c_ctx': nrm(ks[3], (D,), 1.0),
        'w_ada': nrm(ks[4], (L, D, N_MOD * D), 0.5 * D ** -0.5),
        'b_ada': nrm(ks[5], (L, N_MOD * D), 0.02),
        'norm1': 1.0 + nrm(ks[6], (L, D), 0.02),
        'norm2': 1.0 + nrm(ks[7], (L, D), 0.02),
        'w_in': nrm(ks[8], (L, D, IN_TOTAL), D ** -0.5),
        'gla_wa2': nrm(ks[9], (L, 2, GLA_RANK, GLA_HEADS * GLA_DK), GLA_RANK ** -0.5),
        'gla_ba': nrm(ks[10], (L, 2, GLA_HEADS * GLA_DK), 0.1),
        'gla_norm': 1.0 + nrm(ks[11], (L, GLA_HEADS * GLA_DV), 0.02),
        'lru_conv_w': nrm(ks[12], (L, CONV_W, LRU_WIDTH), CONV_W ** -0.5),
        'lru_conv_b': nrm(ks[13], (L, LRU_WIDTH), 0.02),
        'lru_wa': nrm(ks[14], (L, 2, LRU_BLOCKS, LRU_BLOCK, LRU_BLOCK), LRU_BLOCK ** -0.5),
        'lru_ba': nrm(ks[15], (L, 2, LRU_WIDTH), 0.02),
        'lru_wi': nrm(ks[16], (L, 2, LRU_BLOCKS, LRU_BLOCK, LRU_BLOCK), LRU_BLOCK ** -0.5),
        'lru_bi': nrm(ks[18], (L, 2, LRU_WIDTH), 0.02),
        'lru_lam': lru_lam,
        'ret_norm': 1.0 + nrm(ks[19], (L, RET_HEADS * RET_DV), 0.02),
        'w_branch': nrm(ks[20], (L, N_BRANCH, BRANCH_WIDTH, D), BRANCH_WIDTH ** -0.5),
        'w_out': nrm(ks[21], (L, D, D), D ** -0.5),
        'w_router': nrm(ks[22], (L, D, N_EXPERTS), D ** -0.5),
        'b_router': nrm(ks[23], (L, N_EXPERTS), 0.01),
        'w_gu': nrm(ks[24], (L, N_EXPERTS, D, 2 * D_EXPERT), D ** -0.5),
        'b_gu': nrm(ks[25], (L, N_EXPERTS, 2 * D_EXPERT), 0.02),
        'w_down': nrm(ks[26], (L, N_EXPERTS, D_EXPERT, D), D_EXPERT ** -0.5),
        'b_down': nrm(ks[27], (L, N_EXPERTS, D), 0.02),
        'final_norm': 1.0 + nrm(ks[28], (D,), 0.02),
    }


def reference(x, c, ctx, c_ctx, w_ada, b_ada, norm1, norm2, w_in, gla_wa2, gla_ba, gla_norm,
              lru_conv_w, lru_conv_b, lru_wa, lru_ba, lru_wi, lru_bi, lru_lam, ret_norm,
              w_branch, w_out, w_router, b_router, w_gu, b_gu, w_down, b_down, final_norm):
    nb, n_lat, d = x.shape
    rows = n_lat // GRID_W
    rope_cos, rope_sin = axial_rope(rows, RET_DK)
    silu_c = jax.nn.silu(c)
    silu_cc = jax.nn.silu(c_ctx)
    for li in range(DEPTH):
        last = li == DEPTH - 1
        mx = (silu_c @ w_ada[li] + b_ada[li]).reshape(nb, N_MOD, 1, d)
        mc = (silu_cc @ w_ada[li] + b_ada[li]).reshape(N_MOD, d)
        xn = modulate(rms_norm(x, norm1[li]), mx[:, 0], mx[:, 1])
        cn = modulate(rms_norm(ctx, norm1[li]), mc[0], mc[1])
        mix_x, mix_c = hybrid_mixer(xn, cn, rope_cos, rope_sin, w_in[li], gla_wa2[li], gla_ba[li], gla_norm[li],
                                    lru_conv_w[li], lru_conv_b[li], lru_wa[li], lru_ba[li], lru_wi[li],
                                    lru_bi[li], lru_lam[li], ret_norm[li], w_branch[li], w_out[li],
                                    not last)
        x = x + mx[:, 2] * mix_x
        xn2 = modulate(rms_norm(x, norm2[li]), mx[:, 3], mx[:, 4])
        moe_p = (w_router[li], b_router[li], w_gu[li], b_gu[li], w_down[li], b_down[li])
        if last:
            x = x + mx[:, 5] * moe_ffn(xn2.reshape(-1, d), *moe_p).reshape(x.shape)
        else:
            ctx = ctx + mc[2] * mix_c
            cn2 = modulate(rms_norm(ctx, norm2[li]), mc[3], mc[4])
            y = moe_ffn(jnp.concatenate([xn2.reshape(-1, d), cn2.reshape(-1, d)], axis=0), *moe_p)
            x = x + mx[:, 5] * y[:nb * n_lat].reshape(x.shape)
            ctx = ctx + mc[5] * y[nb * n_lat:].reshape(ctx.shape)
    return rms_norm(x, final_norm)
```

```python
import functools
import math

import numpy as np
import jax
import jax.numpy as jnp
from jax import lax
from jax.experimental import pallas as pl
from jax.experimental.pallas import tpu as pltpu

F32 = jnp.float32
_MXU = jnp.bfloat16

EPS = 1e-6
N_MOD = 6
GRID_W = 64
CHUNK = 64
HEADS = 4
DK = 128
DV = 256
GLA_RANK = 16
GLA_GATE_NORM = 16.0
LRU_BLOCKS = 4
LRU_C = 8.0
ROPE_BASE = 10000.0
N_EXPERTS = 32
TOP_K = 4
SWIGLU_LIMIT = 7.0
SWIGLU_ALPHA = 1.702
LANES = 128
SUBLANES = 8
NEG_BIG = -1e30

C_GLA_Q, C_GLA_K, C_GLA_V, C_GLA_G = 0, 512, 1024, 2048
C_LRU_X, C_LRU_G = 3072, 4096
C_RET_Q, C_RET_K, C_RET_V, C_RET_G = 5120, 5632, 6144, 7168
C_MERGE = 8192
C_LR = 11264
N_PROJ = 11520
PROJ_TN = 1280


def _pick(n, prefs):
    for p in prefs:
        if n % p == 0:
            return p
    raise ValueError(f"no tile for {n} in {prefs}")


def _cparams(sem, vmem_mb):
    return pltpu.CompilerParams(dimension_semantics=sem, vmem_limit_bytes=vmem_mb * 1024 * 1024)


def _dot(a, b):
    return jnp.dot(a, b, preferred_element_type=F32)


def _dot_nt(a, b):
    return lax.dot_general(a, b, (((1,), (1,)), ((), ())), preferred_element_type=F32)


def _dot_tn(a, b):
    return lax.dot_general(a, b, (((0,), (0,)), ((), ())), preferred_element_type=F32)


def _sigmoid(x):
    return 1.0 / (1.0 + jnp.exp(-x))


def _log_sigmoid(x):
    return jnp.minimum(x, 0.0) - jnp.log1p(jnp.exp(-jnp.abs(x)))


def _silu(x):
    return x * _sigmoid(x)


def _gelu_tanh(x):
    return 0.5 * x * (1.0 + jnp.tanh(math.sqrt(2.0 / math.pi) * (x + 0.044715 * (x * x * x))))


def _rms(x, gain):
    return x * lax.rsqrt(jnp.mean(x * x, axis=-1, keepdims=True) + EPS) * gain


def _dot01_exact(tri, g):
    hi = g.astype(_MXU)
    r1 = g - hi.astype(F32)
    mid = r1.astype(_MXU)
    lo = (r1 - mid.astype(F32)).astype(_MXU)
    return _dot(tri, hi) + _dot(tri, mid) + _dot(tri, lo)


def _ada_kernel(c_ref, w_ref, b_ref, o_ref):
    s = _silu(c_ref[...])
    o_ref[...] = _dot(s.astype(_MXU), w_ref[...]) + b_ref[...]


def _ada_call(cc, w_ada, b_ada):
    depth, d, n = w_ada.shape
    rows = cc.shape[0]
    tn = _pick(n, (1536, 1024, 512, 128))
    return pl.pallas_call(
        _ada_kernel,
        grid=(depth, n // tn),
        in_specs=[pl.BlockSpec((rows, d), lambda l, j: (0, 0)),
                  pl.BlockSpec((None, d, tn), lambda l, j: (l, 0, j)),
                  pl.BlockSpec((None, 1, tn), lambda l, j: (l, 0, j))],
        out_specs=pl.BlockSpec((None, rows, tn), lambda l, j: (l, 0, j)),
        out_shape=jax.ShapeDtypeStruct((depth, rows, n), F32),
        compiler_params=_cparams(("arbitrary", "arbitrary"), 24),
        name="ada",
    )(cc, w_ada, b_ada.reshape(depth, 1, n))


def _proj_kernel(x_ref, mod_ref, g_ref, w_ref, o_ref, xn_ref):
    @pl.when(pl.program_id(1) == 0)
    def _():
        y = _rms(x_ref[...], g_ref[...])
        xn_ref[...] = (y * (1.0 + mod_ref[1:2, :]) + mod_ref[0:1, :]).astype(xn_ref.dtype)

    o_ref[...] = _dot(xn_ref[...], w_ref[...]).astype(o_ref.dtype)


def _proj_call(xs, mod, gain, w_in_r, n_lat_rows, nl, nb):
    t, d = xs.shape
    tm = _pick(math.gcd(t, nl), (1024, 512, 256))
    return pl.pallas_call(
        _proj_kernel,
        grid=(t // tm, N_PROJ // PROJ_TN),
        in_specs=[pl.BlockSpec((tm, d), lambda i, j: (i, 0)),
                  pl.BlockSpec((None, N_MOD, d), lambda i, j: (jnp.minimum(i * tm // nl, nb), 0, 0)),
                  pl.BlockSpec((1, d), lambda i, j: (0, 0)),
                  pl.BlockSpec((d, PROJ_TN), lambda i, j: (0, j))],
        out_specs=pl.BlockSpec((tm, PROJ_TN), lambda i, j: (i, j)),
        out_shape=jax.ShapeDtypeStruct((t, N_PROJ), _MXU),
        scratch_shapes=[pltpu.VMEM((tm, d), _MXU)],
        compiler_params=_cparams(("arbitrary", "arbitrary"), 40),
        name="proj",
    )(xs, mod, gain, w_in_r)


def _la_kernel(*refs, kind, tt, n_ct, n_lt):
    if kind == "gla":
        (q_ref, k_ref, v_ref, gate_ref, lr_ref, wa2_ref, ba_ref, gain_ref,
         out_ref, st_ref, of_ref) = refs
    else:
        (q_ref, k_ref, v_ref, gate_ref, cos_ref, sin_ref, gain_ref,
         out_ref, st_ref, of_ref) = refs
    ph = pl.program_id(1)
    s = pl.program_id(2)
    n_chunks = tt // CHUNK

    @pl.when(s == 0)
    def _():
        st_ref[...] = jnp.zeros_like(st_ref)

    row = lax.broadcasted_iota(jnp.int32, (CHUNK, CHUNK), 0)
    col = lax.broadcasted_iota(jnp.int32, (CHUNK, CHUNK), 1)
    tpos = lax.broadcasted_iota(jnp.int32, (CHUNK, 1), 0).astype(F32)

    def chunk_outputs(c, backward):
        r0 = c * CHUNK
        rows = slice(r0, r0 + CHUNK)
        if backward:
            mask = col > row
            tri = (col >= row).astype(_MXU)
        else:
            mask = col <= row
            tri = (col <= row).astype(_MXU)
        if kind == "gla":
            d = 1 if backward else 0
            z = _dot(lr_ref[rows, :], wa2_ref[d]) + ba_ref[d]
            g = _log_sigmoid(z) * (1.0 / GLA_GATE_NORM)
            big_g = _dot01_exact(tri, g)
            g_last = big_g[0:1, :] if backward else big_g[CHUNK - 1:CHUNK, :]
            e_pos = jnp.exp(big_g)
            e_neg = jnp.exp(-big_g)
            e_end = jnp.exp(g_last - big_g)
            dec = jnp.exp(g_last)
        outs = []
        for h in range(HEADS):
            ks = slice(h * DK, (h + 1) * DK)
            vs = slice(h * DV, (h + 1) * DV)
            qh = q_ref[rows, ks].astype(F32)
            kh = k_ref[rows, ks].astype(F32)
            vh = v_ref[rows, vs]
            st = st_ref[h]
            if kind == "gla":
                qh = qh * DK ** -0.5
                q_dec = (qh * e_pos[:, ks]).astype(_MXU)
                k_inv = (kh * e_neg[:, ks]).astype(_MXU)
                k_end = (kh * e_end[:, ks]).astype(_MXU)
                sc = jnp.where(mask, _dot_nt(q_dec, k_inv), 0.0)
                dec_h = dec[:, ks]
            else:
                kh = kh * DK ** -0.5
                cos = cos_ref[rows, :]
                sin = sin_ref[rows, :]
                qh = qh * cos + pltpu.roll(qh, DK // 2, 1) * sin
                kh = kh * cos + pltpu.roll(kh, DK // 2, 1) * sin
                lg = math.log(1.0 - 2.0 ** (-5.0 - h))
                steps = (CHUNK - tpos) if backward else (tpos + 1.0)
                dist = (col - row) if backward else (row - col)
                dmat = jnp.where(mask, jnp.exp(dist.astype(F32) * lg), 0.0)
                q_dec = (qh * jnp.exp(steps * lg)).astype(_MXU)
                k_end = (kh * jnp.exp((CHUNK - steps) * lg)).astype(_MXU)
                sc = _dot_nt(qh.astype(_MXU), kh.astype(_MXU)) * dmat
                dec_h = math.exp(CHUNK * lg)
            o = _dot(sc.astype(_MXU), vh) + _dot_nt(q_dec, st.astype(_MXU))
            st_ref[h] = st * dec_h + _dot_tn(vh, k_end)
            outs.append(o)
        return jnp.concatenate(outs, axis=1)

    @pl.when(ph == 0)
    def _():
        base = pl.multiple_of(s * tt, tt)
        for c in range(n_chunks):
            of_ref[pl.ds(base + c * CHUNK, CHUNK), :] = chunk_outputs(c, False)

    @pl.when(ph == 1)
    def _():
        loc = jnp.where(s < n_ct, n_ct - 1 - s, n_ct + n_lt - 1 - (s - n_ct))
        base = pl.multiple_of(loc * tt, tt)
        for c in reversed(range(n_chunks)):
            rows = slice(c * CHUNK, (c + 1) * CHUNK)
            o = of_ref[pl.ds(base + c * CHUNK, CHUNK), :] + chunk_outputs(c, True)
            gate = gate_ref[rows, :].astype(F32)
            parts = []
            for h in range(HEADS):
                oh = o[:, h * DV:(h + 1) * DV]
                if kind == "ret":
                    oh = oh - jnp.mean(oh, axis=-1, keepdims=True)
                parts.append(oh * lax.rsqrt(jnp.mean(oh * oh, axis=-1, keepdims=True) + EPS))
            normed = jnp.concatenate(parts, axis=1) * gain_ref[...]
            out_ref[rows, :] = (normed * _silu(gate)).astype(out_ref.dtype)


def _la_call(kind, p, extra, gain, nb, nl, nc):
    t = p.shape[0]
    tt = _pick(math.gcd(nl, nc), (256, 128, 64))
    n_ct, n_lt = nc // tt, nl // tt
    lat_tiles = nb * n_lt
    cq, ck, cv, cg = ((C_GLA_Q, C_GLA_K, C_GLA_V, C_GLA_G) if kind == "gla"
                      else (C_RET_Q, C_RET_K, C_RET_V, C_RET_G))
    hd = HEADS * DK
    hv = HEADS * DV

    def loc_of(ph, s):
        back = jnp.where(s < n_ct, n_ct - 1 - s, n_ct + n_lt - 1 - (s - n_ct))
        return jnp.where(ph == 0, s, back)

    def row_blk(b, loc):
        return jnp.where(loc < n_ct, lat_tiles + b * n_ct + loc, b * n_lt + (loc - n_ct))

    def in_map(cblk):
        return lambda b, ph, s: (row_blk(b, loc_of(ph, s)), cblk)

    def second_pass_map(cblk):
        return lambda b, ph, s: (row_blk(b, loc_of(1, jnp.where(ph == 0, 0, s))), cblk)

    in_specs = [pl.BlockSpec((tt, hd), in_map(cq // hd)),
                pl.BlockSpec((tt, hd), in_map(ck // hd)),
                pl.BlockSpec((tt, hv), in_map(cv // hv)),
                pl.BlockSpec((tt, hv), second_pass_map(cg // hv))]
    args = [p, p, p, p]
    if kind == "gla":
        wa2p, ba = extra
        in_specs += [pl.BlockSpec((tt, LANES), in_map(C_LR // LANES)),
                     pl.BlockSpec((2, LANES, hd), lambda b, ph, s: (0, 0, 0)),
                     pl.BlockSpec((2, 1, hd), lambda b, ph, s: (0, 0, 0))]
        args += [p, wa2p, ba]
    else:
        cos_t, sin_t = extra

        def rope_map(b, ph, s):
            loc = loc_of(ph, s)
            return (jnp.where(loc < n_ct, n_lt + loc, loc - n_ct), 0)
        in_specs += [pl.BlockSpec((tt, DK), rope_map), pl.BlockSpec((tt, DK), rope_map)]
        args += [cos_t, sin_t]
    in_specs.append(pl.BlockSpec((1, hv), lambda b, ph, s: (0, 0)))
    args.append(gain)
    return pl.pallas_call(
        functools.partial(_la_kernel, kind=kind, tt=tt, n_ct=n_ct, n_lt=n_lt),
        grid=(nb, 2, n_ct + n_lt),
        in_specs=in_specs,
        out_specs=pl.BlockSpec((tt, hv), second_pass_map(0)),
        out_shape=jax.ShapeDtypeStruct((t, hv), _MXU),
        scratch_shapes=[pltpu.VMEM((HEADS, DV, DK), F32),
                        pltpu.VMEM((nc + nl, hv), F32)],
        compiler_params=_cparams(("arbitrary", "arbitrary", "arbitrary"), 48),
        name=kind,
    )(*args)


def _scan_group(a, b, h, reverse):
    row = lax.broadcasted_iota(jnp.int32, a.shape, 0)
    for sft in (1, 2, 4):
        if reverse:
            a_sh = pltpu.roll(a, SUBLANES - sft, 0)
            b_sh = pltpu.roll(b, SUBLANES - sft, 0)
            m = row < SUBLANES - sft
        else:
            a_sh = pltpu.roll(a, sft, 0)
            b_sh = pltpu.roll(b, sft, 0)
            m = row >= sft
        b = jnp.where(m, a * b_sh + b, b)
        a = jnp.where(m, a * a_sh, a)
    hh = a * h + b
    return hh, (hh[0:1, :] if reverse else hh[SUBLANES - 1:SUBLANES, :])


def _lru_kernel(x_ref, gel_ref, cw_ref, cb_ref, wa_ref, ba_ref, wi_ref, bi_ref, lam_ref, h0_ref,
                out_ref, hn_ref, xpad_ref, xc_ref, hf_ref, *, seg, tile):
    n_tiles = seg // tile
    groups = tile // SUBLANES
    cw = cw_ref[...]
    cb = cb_ref[...]
    zeros8 = jnp.zeros((SUBLANES, xpad_ref.shape[1]), F32)
    xpad_ref[0:SUBLANES, :] = zeros8
    xpad_ref[SUBLANES + seg:2 * SUBLANES + seg, :] = zeros8

    def copy_in(i, carry):
        t0 = pl.multiple_of(i * tile, tile)
        xpad_ref[pl.ds(t0 + SUBLANES, tile), :] = x_ref[pl.ds(t0, tile), :].astype(F32)
        return carry
    lax.fori_loop(0, n_tiles, copy_in, 0)

    def gates(xc, d):
        xb = xc.astype(_MXU)
        r = _sigmoid(_dot(xb, wa_ref[d]) + ba_ref[d:d + 1, :])
        ig = _sigmoid(_dot(xb, wi_ref[d]) + bi_ref[d:d + 1, :])
        log_a = LRU_C * r * _log_sigmoid(lam_ref[d:d + 1, :])
        a = jnp.exp(log_a)
        b = jnp.sqrt(-jnp.tanh(log_a) * (a * a + 1.0)) * (ig * xc)
        return a, b

    def fwd_tile(i, h):
        t0 = pl.multiple_of(i * tile, tile)
        win = xpad_ref[pl.ds(t0, tile + 2 * SUBLANES), :]
        xc = (cw[0:1, :] * win[6:6 + tile] + cw[1:2, :] * win[7:7 + tile]
              + cw[2:3, :] * win[8:8 + tile] + cw[3:4, :] * win[9:9 + tile] + cb)
        xc_ref[pl.ds(t0, tile), :] = xc
        a, b = gates(xc, 0)
        for g in range(groups):
            rs = slice(g * SUBLANES, (g + 1) * SUBLANES)
            hh, h = _scan_group(a[rs], b[rs], h, False)
            hf_ref[pl.ds(t0 + g * SUBLANES, SUBLANES), :] = hh
        return h
    h_f = lax.fori_loop(0, n_tiles, fwd_tile, h0_ref[0:1, :])

    def bwd_tile(j, h):
        i = n_tiles - 1 - j
        t0 = pl.multiple_of(i * tile, tile)
        xc = xc_ref[pl.ds(t0, tile), :]
        a, b = gates(xc, 1)
        for g in reversed(range(groups)):
            rs = slice(g * SUBLANES, (g + 1) * SUBLANES)
            hh, h = _scan_group(a[rs], b[rs], h, True)
            rows = pl.ds(t0 + g * SUBLANES, SUBLANES)
            hsum = hf_ref[rows, :] + hh
            hf_ref[rows, :] = hsum
        gel = gel_ref[pl.ds(t0, tile), :].astype(F32)
        out_ref[pl.ds(t0, tile), :] = (hf_ref[pl.ds(t0, tile), :] * _gelu_tanh(gel)).astype(out_ref.dtype)
        return h
    h_b = lax.fori_loop(0, n_tiles, bwd_tile, h0_ref[1:2, :])
    hn_ref[0:1, :] = h_f
    hn_ref[1:2, :] = h_b


def _lru_call(p, prev_out, h0, lw, nb, seg, row_blk0):
    cw, cb, wa, ba, wi, bi, lam = lw
    t = p.shape[0]
    width = LRU_BLOCKS * DV
    cbw = width // LRU_BLOCKS
    tile = _pick(seg, (128, 64))
    xblk = C_LRU_X // cbw
    gblk = C_LRU_G // cbw
    in_specs = [pl.BlockSpec((seg, cbw), lambda b, c: (row_blk0 + b, xblk + c)),
                pl.BlockSpec((seg, cbw), lambda b, c: (row_blk0 + b, gblk + c)),
                pl.BlockSpec((4, cbw), lambda b, c: (0, c)),
                pl.BlockSpec((1, cbw), lambda b, c: (0, c)),
                pl.BlockSpec((2, None, cbw, cbw), lambda b, c: (0, c, 0, 0)),
                pl.BlockSpec((2, cbw), lambda b, c: (0, c)),
                pl.BlockSpec((2, None, cbw, cbw), lambda b, c: (0, c, 0, 0)),
                pl.BlockSpec((2, cbw), lambda b, c: (0, c)),
                pl.BlockSpec((2, cbw), lambda b, c: (0, c)),
                pl.BlockSpec((None, 2, cbw), lambda b, c: (b, 0, c))]
    in_specs.append(pl.BlockSpec(memory_space=pl.ANY))
    args = [p, p, cw, cb, wa, ba, wi, bi, lam, h0, prev_out]

    def body(*refs):
        _lru_kernel(*(refs[:10] + refs[11:]), seg=seg, tile=tile)

    return pl.pallas_call(
        body,
        grid=(nb, LRU_BLOCKS),
        in_specs=in_specs,
        out_specs=[pl.BlockSpec((seg, cbw), lambda b, c: (row_blk0 + b, c)),
                   pl.BlockSpec((None, 2, cbw), lambda b, c: (b, 0, c))],
        out_shape=[jax.ShapeDtypeStruct((t, width), _MXU),
                   jax.ShapeDtypeStruct((nb, 2, width), F32)],
        scratch_shapes=[pltpu.VMEM((seg + 2 * SUBLANES, cbw), F32),
                        pltpu.VMEM((seg, cbw), F32),
                        pltpu.VMEM((seg, cbw), F32)],
        input_output_aliases={10: 0},
        compiler_params=_cparams(("arbitrary", "arbitrary"), 48),
        name="lru",
    )(*args)


def _finish_kernel(gla_ref, lru_ref, ret_ref, m0_ref, m1_ref, m2_ref, wb_ref, wo_ref, xs_ref,
                   mod_ref, g2_ref, wr_ref, br_ref, tril_ref,
                   xo_ref, xn_ref, idx_ref, wgt_ref, rank_ref, cnt_ref, carry_ref):
    @pl.when(pl.program_id(0) == 0)
    def _():
        carry_ref[...] = jnp.zeros_like(carry_ref)

    merged = (_sigmoid(m0_ref[...].astype(F32)) * _dot(gla_ref[...], wb_ref[0])
              + _sigmoid(m1_ref[...].astype(F32)) * _dot(lru_ref[...], wb_ref[1])
              + _sigmoid(m2_ref[...].astype(F32)) * _dot(ret_ref[...], wb_ref[2]))
    x = xs_ref[...] + mod_ref[2:3, :] * _dot(merged.astype(_MXU), wo_ref[...])
    xo_ref[...] = x
    xn = _rms(x, g2_ref[...]) * (1.0 + mod_ref[4:5, :]) + mod_ref[3:4, :]
    xn_ref[...] = xn

    logits = _dot(xn.astype(_MXU), wr_ref[...]) + br_ref[...]
    lane = lax.broadcasted_iota(jnp.int32, logits.shape, 1)
    ids, vals = [], []
    for _ in range(TOP_K):
        m = jnp.max(logits, axis=1, keepdims=True)
        sel = jnp.min(jnp.where(logits == m, lane.astype(F32), float(LANES)), axis=1,
                      keepdims=True).astype(jnp.int32)
        ids.append(sel)
        vals.append(m)
        logits = jnp.where(lane == sel, -jnp.inf, logits)
    ex = [jnp.exp(v - vals[0]) for v in vals]
    denom = ex[0] + ex[1] + ex[2] + ex[3]
    onehot = jnp.zeros(logits.shape, F32)
    for sel in ids:
        onehot = onehot + (lane == sel).astype(F32)
    before = _dot(tril_ref[...], onehot.astype(_MXU)) + carry_ref[0:1, :]
    idx_o = jnp.zeros(logits.shape, jnp.int32)
    rank_o = jnp.zeros(logits.shape, jnp.int32)
    wgt_o = jnp.zeros(logits.shape, F32)
    for k in range(TOP_K):
        rk = jnp.sum(jnp.where(lane == ids[k], before, 0.0), axis=1, keepdims=True)
        idx_o = jnp.where(lane == k, ids[k], idx_o)
        rank_o = jnp.where(lane == k, rk.astype(jnp.int32), rank_o)
        wgt_o = jnp.where(lane == k, ex[k] / denom, wgt_o)
    idx_ref[...] = idx_o
    rank_ref[...] = rank_o
    wgt_ref[...] = wgt_o
    total = carry_ref[0:1, :] + jnp.sum(onehot, axis=0, keepdims=True)
    carry_ref[...] = jnp.broadcast_to(total, carry_ref.shape)
    cnt_ref[...] = jnp.broadcast_to(total, cnt_ref.shape)


def _finish_call(gla, lru, ret, p, wb, wo, xs, mod, g2, wr, br, n_rows, nl, nb):
    d = xs.shape[1]
    tm = _pick(math.gcd(n_rows, nl), (512, 256))
    tril = jnp.asarray(np.tril(np.ones((tm, tm), np.float32), -1), _MXU)
    row = lambda i: (i, 0)
    const2 = lambda i: (0, 0)
    mblk = C_MERGE // d
    in_specs = [pl.BlockSpec((tm, d), row), pl.BlockSpec((tm, d), row), pl.BlockSpec((tm, d), row),
                pl.BlockSpec((tm, d), lambda i: (i, mblk)),
                pl.BlockSpec((tm, d), lambda i: (i, mblk + 1)),
                pl.BlockSpec((tm, d), lambda i: (i, mblk + 2)),
                pl.BlockSpec((3, d, d), lambda i: (0, 0, 0)),
                pl.BlockSpec((d, d), const2),
                pl.BlockSpec((tm, d), row),
                pl.BlockSpec((None, N_MOD, d), lambda i: (jnp.minimum(i * tm // nl, nb), 0, 0)),
                pl.BlockSpec((1, d), const2),
                pl.BlockSpec((d, LANES), const2),
                pl.BlockSpec((1, LANES), const2),
                pl.BlockSpec((tm, tm), const2)]
    out_specs = [pl.BlockSpec((tm, d), row), pl.BlockSpec((tm, d), row),
                 pl.BlockSpec((tm, LANES), row), pl.BlockSpec((tm, LANES), row),
                 pl.BlockSpec((tm, LANES), row), pl.BlockSpec((SUBLANES, LANES), const2)]
    out_shape = [jax.ShapeDtypeStruct((n_rows, d), F32), jax.ShapeDtypeStruct((n_rows, d), F32),
                 jax.ShapeDtypeStruct((n_rows, LANES), jnp.int32),
                 jax.ShapeDtypeStruct((n_rows, LANES), F32),
                 jax.ShapeDtypeStruct((n_rows, LANES), jnp.int32),
                 jax.ShapeDtypeStruct((SUBLANES, LANES), F32)]
    return pl.pallas_call(
        _finish_kernel,
        grid=(n_rows // tm,),
        in_specs=in_specs,
        out_specs=out_specs,
        out_shape=out_shape,
        scratch_shapes=[pltpu.VMEM((SUBLANES, LANES), F32)],
        compiler_params=_cparams(("arbitrary",), 56),
        name="finish",
    )(gla, lru, ret, p, p, p, wb, wo, xs, mod, g2, wr, br, tril)


def _slot_kernel(idx_ref, rank_ref, start_ref, slot_ref):
    idx = idx_ref[...].astype(F32)
    lane = lax.broadcasted_iota(jnp.int32, idx.shape, 1)
    start = start_ref[...]
    out = jnp.zeros(idx.shape, F32)
    for k in range(TOP_K):
        sel = jnp.sum(jnp.where(lane == k, idx, 0.0), axis=1, keepdims=True).astype(jnp.int32)
        st = jnp.sum(jnp.where(lane == sel, start, 0.0), axis=1, keepdims=True)
        out = jnp.where(lane == k, st, out)
    slot_ref[...] = out.astype(jnp.int32) + rank_ref[...]


def _slot_call(idx, rank, pad_start):
    n = idx.shape[0]
    tm = _pick(n, (2048, 1024, 512, 256))
    row = lambda i: (i, 0)
    return pl.pallas_call(
        _slot_kernel,
        grid=(n // tm,),
        in_specs=[pl.BlockSpec((tm, LANES), row), pl.BlockSpec((tm, LANES), row),
                  pl.BlockSpec((1, LANES), lambda i: (0, 0))],
        out_specs=pl.BlockSpec((tm, LANES), row),
        out_shape=jax.ShapeDtypeStruct((n, LANES), jnp.int32),
        compiler_params=_cparams(("arbitrary",), 24),
        name="slots",
    )(idx, rank, pad_start)


def _dispatch_kernel(slot_ref, x_ref, dst_in_ref, dst_ref, sem, *, tm):
    del dst_in_ref

    def row_copy(t, k):
        return pltpu.make_async_copy(x_ref.at[pl.ds(t, 1)],
                                     dst_ref.at[pl.ds(slot_ref[0, t * TOP_K + k], 1)], sem)

    def issue(t, carry):
        for k in range(TOP_K):
            row_copy(t, k).start()
        return carry
    lax.fori_loop(0, tm, issue, 0)

    def drain(t, carry):
        for k in range(TOP_K):
            row_copy(t, k).wait()
        return carry
    lax.fori_loop(0, tm, drain, 0)


def _dispatch_call(slots3, xn, zeros_sorted):
    n, d = xn.shape
    tm = slots3.shape[2] // TOP_K
    return pl.pallas_call(
        functools.partial(_dispatch_kernel, tm=tm),
        grid=(n // tm,),
        in_specs=[pl.BlockSpec((None, 1, tm * TOP_K), lambda i: (i, 0, 0), memory_space=pltpu.SMEM),
                  pl.BlockSpec((tm, d), lambda i: (i, 0)),
                  pl.BlockSpec(memory_space=pl.ANY)],
        out_specs=pl.BlockSpec(memory_space=pl.ANY),
        out_shape=jax.ShapeDtypeStruct(zeros_sorted.shape, zeros_sorted.dtype),
        scratch_shapes=[pltpu.SemaphoreType.DMA(())],
        input_output_aliases={2: 0},
        compiler_params=_cparams(("arbitrary",), 24),
        name="dispatch",
    )(slots3, xn, zeros_sorted)


def _experts_kernel(be_ref, nu_ref, x_ref, wgu_ref, bgu_ref, wd_ref, bd_ref, y_ref):
    del be_ref
    f = wd_ref.shape[0]

    @pl.when(pl.program_id(0) < nu_ref[0])
    def _():
        gu = _dot(x_ref[...].astype(_MXU), wgu_ref[...]) + bgu_ref[...]
        gate = jnp.minimum(gu[:, :f], SWIGLU_LIMIT)
        up = jnp.clip(gu[:, f:], -SWIGLU_LIMIT, SWIGLU_LIMIT)
        act = gate * _sigmoid(SWIGLU_ALPHA * gate) * (up + 1.0)
        y_ref[...] = _dot(act.astype(_MXU), wd_ref[...]) + bd_ref[...]

    @pl.when(pl.program_id(0) >= nu_ref[0])
    def _():
        y_ref[...] = jnp.zeros_like(y_ref)


def _experts_call(block_e, n_used, x_sorted, wgu, bgu, wd, bd, bm):
    n_slots, d = x_sorted.shape
    f = wd.shape[1]

    def xmap(j, be, nu):
        return (jnp.minimum(j, nu[0] - 1), 0)

    def wmap(j, be, nu):
        return (be[j], 0, 0)

    grid_spec = pltpu.PrefetchScalarGridSpec(
        num_scalar_prefetch=2,
        grid=(n_slots // bm,),
        in_specs=[pl.BlockSpec((bm, d), xmap),
                  pl.BlockSpec((None, d, 2 * f), wmap),
                  pl.BlockSpec((None, 1, 2 * f), wmap),
                  pl.BlockSpec((None, f, d), wmap),
                  pl.BlockSpec((None, 1, d), wmap)],
        out_specs=pl.BlockSpec((bm, d), lambda j, be, nu: (j, 0)))
    return pl.pallas_call(
        _experts_kernel,
        grid_spec=grid_spec,
        out_shape=jax.ShapeDtypeStruct((n_slots, d), F32),
        compiler_params=_cparams(("arbitrary",), 48),
        name="experts",
    )(block_e, n_used, x_sorted, wgu, bgu, wd, bd)


def _combine_kernel(slot_ref, wgt_ref, xs_ref, mod_ref, fin_ref, y_ref, out_ref, buf_ref, sem,
                    *, tm, final):
    def row_copy(t, k):
        return pltpu.make_async_copy(y_ref.at[pl.ds(slot_ref[0, t * TOP_K + k], 1)],
                                     buf_ref.at[k, pl.ds(t, 1)], sem)

    def issue(t, carry):
        for k in range(TOP_K):
            row_copy(t, k).start()
        return carry
    lax.fori_loop(0, tm, issue, 0)

    def drain(t, carry):
        for k in range(TOP_K):
            row_copy(t, k).wait()
        return carry
    lax.fori_loop(0, tm, drain, 0)

    wgt = wgt_ref[...]
    acc = wgt[:, 0:1] * buf_ref[0]
    for k in range(1, TOP_K):
        acc = acc + wgt[:, k:k + 1] * buf_ref[k]
    x = xs_ref[...] + mod_ref[5:6, :] * acc
    out_ref[...] = _rms(x, fin_ref[...]) if final else x


def _combine_call(slots3, wgt, xs, mod, fin, y_sorted, nl, nb, final):
    n, d = xs.shape
    tm = slots3.shape[2] // TOP_K
    return pl.pallas_call(
        functools.partial(_combine_kernel, tm=tm, final=final),
        grid=(n // tm,),
        in_specs=[pl.BlockSpec((None, 1, tm * TOP_K), lambda i: (i, 0, 0), memory_space=pltpu.SMEM),
                  pl.BlockSpec((tm, LANES), lambda i: (i, 0)),
                  pl.BlockSpec((tm, d), lambda i: (i, 0)),
                  pl.BlockSpec((None, N_MOD, d), lambda i: (jnp.minimum(i * tm // nl, nb), 0, 0)),
                  pl.BlockSpec((1, d), lambda i: (0, 0)),
                  pl.BlockSpec(memory_space=pl.ANY)],
        out_specs=pl.BlockSpec((tm, d), lambda i: (i, 0)),
        out_shape=jax.ShapeDtypeStruct((n, d), F32),
        scratch_shapes=[pltpu.VMEM((TOP_K, tm, d), F32), pltpu.SemaphoreType.DMA(())],
        compiler_params=_cparams(("arbitrary",), 32),
        name="combine",
    )(slots3, wgt, xs, mod, fin, y_sorted)


def _moe(xn, idx, wgt, rank, counts, xs_mid, mod, fin, wgu, bgu, wd, bd, nl, nb, final):
    n, d = xn.shape
    bm = 256
    tm = _pick(math.gcd(n, nl), (256, 128))
    n_blocks = -(-(n * TOP_K) // bm) + N_EXPERTS
    cnt = counts[0, :N_EXPERTS].astype(jnp.int32)
    padded = (cnt + bm - 1) // bm * bm
    pad_end = jnp.cumsum(padded)
    pad_start = pad_end - padded
    n_used = (pad_end[-1] // bm).astype(jnp.int32)
    blk = jnp.arange(n_blocks, dtype=jnp.int32)
    block_e = jnp.minimum(jnp.searchsorted(pad_end, jnp.minimum(blk, n_used - 1) * bm, side="right"),
                          N_EXPERTS - 1).astype(jnp.int32)
    start_row = jnp.zeros((1, LANES), F32).at[0, :N_EXPERTS].set(pad_start.astype(F32))
    slots = _slot_call(idx, rank, start_row)
    slots3 = slots[:, :TOP_K].reshape(n // tm, 1, tm * TOP_K)
    x_sorted = _dispatch_call(slots3, xn, jnp.zeros((n_blocks * bm, d), F32))
    y_sorted = _experts_call(block_e, n_used.reshape(1), x_sorted, wgu, bgu, wd, bd, bm)
    return _combine_call(slots3, wgt, xs_mid, mod, fin, y_sorted, nl, nb, final)


def _rope_tables(nl, nc):
    rows = nl // GRID_W
    r = np.broadcast_to(np.arange(rows)[:, None], (rows, GRID_W)).reshape(-1).astype(np.float32)
    c = np.broadcast_to(np.arange(GRID_W)[None, :], (rows, GRID_W)).reshape(-1).astype(np.float32)
    n_freq = DK // 4
    inv = (ROPE_BASE ** (-jnp.arange(n_freq, dtype=F32) / n_freq))
    ang = jnp.concatenate([jnp.asarray(r)[:, None] * inv, jnp.asarray(c)[:, None] * inv], axis=-1)
    cos, sin = jnp.cos(ang), jnp.sin(ang)
    cos2 = jnp.concatenate([cos, cos], axis=-1)
    sin2 = jnp.concatenate([-sin, sin], axis=-1)
    cos_t = jnp.concatenate([cos2, jnp.ones((nc, DK), F32)], axis=0)
    sin_t = jnp.concatenate([sin2, jnp.zeros((nc, DK), F32)], axis=0)
    return cos_t, sin_t


def kernel(x, c, ctx, c_ctx, w_ada, b_ada, norm1, norm2, w_in, gla_wa2, gla_ba, gla_norm, lru_conv_w, lru_conv_b, lru_wa, lru_ba, lru_wi, lru_bi, lru_lam, ret_norm, w_branch, w_out, w_router, b_router, w_gu, b_gu, w_down, b_down, final_norm):
    nb, nl, d = x.shape
    nc = ctx.shape[1]
    depth = w_ada.shape[0]
    n_lat = nb * nl
    f = w_down.shape[2]

    xs = jnp.concatenate([x.reshape(n_lat, d), ctx.reshape(nb * nc, d)], axis=0)
    mod_rows = -(-(nb + 1) // SUBLANES) * SUBLANES
    cc = jnp.zeros((mod_rows, d), F32).at[:nb].set(c).at[nb].set(c_ctx)
    mod_all = _ada_call(cc, w_ada.astype(_MXU), b_ada).reshape(depth, mod_rows, N_MOD, d)
    cos_t, sin_t = _rope_tables(nl, nc)

    out = None
    for li in range(depth):
        last = li == depth - 1
        mod = mod_all[li]
        w = w_in[li]
        w_in_r = jnp.concatenate(
            [w[:, :3072], w[:, 3104:], w[:, 3072:3104],
             jnp.zeros((d, N_PROJ - w.shape[1]), w.dtype)], axis=1).astype(_MXU)
        p = _proj_call(xs, mod, norm1[li].reshape(1, d), w_in_r, n_lat, nl, nb)

        wa2p = jnp.zeros((2, LANES, HEADS * DK), F32)
        wa2p = wa2p.at[0, :GLA_RANK].set(gla_wa2[li, 0]).at[1, GLA_RANK:2 * GLA_RANK].set(gla_wa2[li, 1])
        gla = _la_call("gla", p, (wa2p.astype(_MXU), gla_ba[li].reshape(2, 1, HEADS * DK)),
                       gla_norm[li].reshape(1, HEADS * DV), nb, nl, nc)
        ret = _la_call("ret", p, (cos_t, sin_t), ret_norm[li].reshape(1, HEADS * DV), nb, nl, nc)

        lw = (lru_conv_w[li], lru_conv_b[li].reshape(1, -1), lru_wa[li].astype(_MXU), lru_ba[li],
              lru_wi[li].astype(_MXU), lru_bi[li], lru_lam[li])
        width = lru_conv_w.shape[2]
        lru_c, h_ctx = _lru_call(p, jnp.zeros((xs.shape[0], width), _MXU),
                                 jnp.zeros((nb, 2, width), F32), lw, nb, nc, n_lat // nc)
        lru, _ = _lru_call(p, lru_c, h_ctx, lw, nb, nl, 0)

        n_rows = n_lat if last else xs.shape[0]
        wr = jnp.zeros((d, LANES), F32).at[:, :N_EXPERTS].set(w_router[li]).astype(_MXU)
        br = jnp.full((1, LANES), NEG_BIG, F32).at[0, :N_EXPERTS].set(b_router[li])
        xs_mid, xn2, idx, wgt, rank, counts = _finish_call(
            gla, lru, ret, p, w_branch[li].astype(_MXU), w_out[li].astype(_MXU), xs, mod,
            norm2[li].reshape(1, d), wr, br, n_rows, nl, nb)

        wgu = jnp.concatenate([w_gu[li, :, :, 0::2], w_gu[li, :, :, 1::2]], axis=-1).astype(_MXU)
        bgu = jnp.concatenate([b_gu[li, :, 0::2], b_gu[li, :, 1::2]], axis=-1).reshape(N_EXPERTS, 1, 2 * f)
        out = _moe(xn2, idx, wgt, rank, counts, xs_mid, mod, final_norm.reshape(1, d), wgu, bgu,
                   w_down[li].astype(_MXU), b_down[li].reshape(N_EXPERTS, 1, d), nl, nb, last)
        xs = out
    return out.reshape(nb, nl, d)
```

```python
import functools
import math

import numpy as np
import jax
import jax.numpy as jnp
from jax import lax
from jax.experimental import pallas as pl
from jax.experimental.pallas import tpu as pltpu

F32 = jnp.float32
_MXU = jnp.bfloat16

EPS = 1e-6
N_MOD = 6
GRID_W = 64
CHUNK = 64
HEADS = 4
DK = 128
DV = 256
GLA_RANK = 16
GLA_GATE_NORM = 16.0
LRU_BLOCKS = 4
LRU_C = 8.0
ROPE_BASE = 10000.0
N_EXPERTS = 32
TOP_K = 4
SWIGLU_LIMIT = 7.0
SWIGLU_ALPHA = 1.702
LANES = 128
SUBLANES = 8
NEG_BIG = -1e30

C_GLA_Q, C_GLA_K, C_GLA_V, C_GLA_G = 0, 512, 1024, 2048
C_LRU_X, C_LRU_G = 3072, 4096
C_RET_Q, C_RET_K, C_RET_V, C_RET_G = 5120, 5632, 6144, 7168
C_MERGE = 8192
C_LR = 11264
N_PROJ = 11520
PROJ_TN = 1280


def _pick(n, prefs):
    for p in prefs:
        if n % p == 0:
            return p
    raise ValueError(f"no tile for {n} in {prefs}")


def _cparams(sem, vmem_mb):
    return pltpu.CompilerParams(dimension_semantics=sem, vmem_limit_bytes=vmem_mb * 1024 * 1024)


def _dot(a, b):
    return jnp.dot(a, b, preferred_element_type=F32)


def _dot_nt(a, b):
    return lax.dot_general(a, b, (((1,), (1,)), ((), ())), preferred_element_type=F32)


def _dot_tn(a, b):
    return lax.dot_general(a, b, (((0,), (0,)), ((), ())), preferred_element_type=F32)


def _sigmoid(x):
    return 1.0 / (1.0 + jnp.exp(-x))


def _log_sigmoid(x):
    return jnp.minimum(x, 0.0) - jnp.log1p(jnp.exp(-jnp.abs(x)))


def _silu(x):
    return x * _sigmoid(x)


def _gelu_tanh(x):
    return 0.5 * x * (1.0 + jnp.tanh(math.sqrt(2.0 / math.pi) * (x + 0.044715 * (x * x * x))))


def _rms(x, gain):
    return x * lax.rsqrt(jnp.mean(x * x, axis=-1, keepdims=True) + EPS) * gain


def _dot01_exact(tri, g):
    hi = g.astype(_MXU)
    r1 = g - hi.astype(F32)
    mid = r1.astype(_MXU)
    lo = (r1 - mid.astype(F32)).astype(_MXU)
    return _dot(tri, hi) + _dot(tri, mid) + _dot(tri, lo)


def _ada_kernel(c_ref, w_ref, b_ref, o_ref):
    s = _silu(c_ref[...])
    o_ref[...] = _dot(s.astype(_MXU), w_ref[...]) + b_ref[...]


def _ada_call(cc, w_ada, b_ada):
    depth, d, n = w_ada.shape
    rows = cc.shape[0]
    tn = _pick(n, (1536, 1024, 512, 128))
    return pl.pallas_call(
        _ada_kernel,
        grid=(depth, n // tn),
        in_specs=[pl.BlockSpec((rows, d), lambda l, j: (0, 0)),
                  pl.BlockSpec((None, d, tn), lambda l, j: (l, 0, j)),
                  pl.BlockSpec((None, 1, tn), lambda l, j: (l, 0, j))],
        out_specs=pl.BlockSpec((None, rows, tn), lambda l, j: (l, 0, j)),
        out_shape=jax.ShapeDtypeStruct((depth, rows, n), F32),
        compiler_params=_cparams(("arbitrary", "arbitrary"), 24),
        name="ada",
    )(cc, w_ada, b_ada.reshape(depth, 1, n))


def _proj_kernel(x_ref, mod_ref, g_ref, w_ref, o_ref, xn_ref):
    @pl.when(pl.program_id(1) == 0)
    def _():
        y = _rms(x_ref[...], g_ref[...])
        xn_ref[...] = (y * (1.0 + mod_ref[1:2, :]) + mod_ref[0:1, :]).astype(xn_ref.dtype)

    o_ref[...] = _dot(xn_ref[...], w_ref[...]).astype(o_ref.dtype)


def _proj_call(xs, mod, gain, w_in_r, n_lat_rows, nl, nb):
    t, d = xs.shape
    tm = _pick(math.gcd(t, nl), (1024, 512, 256))
    return pl.pallas_call(
        _proj_kernel,
        grid=(t // tm, N_PROJ // PROJ_TN),
        in_specs=[pl.BlockSpec((tm, d), lambda i, j: (i, 0)),
                  pl.BlockSpec((None, N_MOD, d), lambda i, j: (jnp.minimum(i * tm // nl, nb), 0, 0)),
                  pl.BlockSpec((1, d), lambda i, j: (0, 0)),
                  pl.BlockSpec((d, PROJ_TN), lambda i, j: (0, j))],
        out_specs=pl.BlockSpec((tm, PROJ_TN), lambda i, j: (i, j)),
        out_shape=jax.ShapeDtypeStruct((t, N_PROJ), _MXU),
        scratch_shapes=[pltpu.VMEM((tm, d), _MXU)],
        compiler_params=_cparams(("arbitrary", "arbitrary"), 40),
        name="proj",
    )(xs, mod, gain, w_in_r)


def _la_kernel(*refs, kind, tt, n_ct, n_lt):
    if kind == "gla":
        (q_ref, k_ref, v_ref, gate_ref, lr_ref, wa2_ref, ba_ref, gain_ref,
         out_ref, st_ref, of_ref) = refs
    else:
        (q_ref, k_ref, v_ref, gate_ref, cos_ref, sin_ref, gain_ref,
         out_ref, st_ref, of_ref) = refs
    ph = pl.program_id(1)
    s = pl.program_id(2)
    n_chunks = tt // CHUNK

    @pl.when(s == 0)
    def _():
        st_ref[...] = jnp.zeros_like(st_ref)

    row = lax.broadcasted_iota(jnp.int32, (CHUNK, CHUNK), 0)
    col = lax.broadcasted_iota(jnp.int32, (CHUNK, CHUNK), 1)
    tpos = lax.broadcasted_iota(jnp.int32, (CHUNK, 1), 0).astype(F32)

    def chunk_outputs(c, backward):
        r0 = c * CHUNK
        rows = slice(r0, r0 + CHUNK)
        if backward:
            mask = col > row
            tri = (col >= row).astype(_MXU)
        else:
            mask = col <= row
            tri = (col <= row).astype(_MXU)
        if kind == "gla":
            d = 1 if backward else 0
            z = _dot(lr_ref[rows, :], wa2_ref[d]) + ba_ref[d]
            g = _log_sigmoid(z) * (1.0 / GLA_GATE_NORM)
            big_g = _dot01_exact(tri, g)
            g_last = big_g[0:1, :] if backward else big_g[CHUNK - 1:CHUNK, :]
            e_pos = jnp.exp(big_g)
            e_neg = jnp.exp(-big_g)
            e_end = jnp.exp(g_last - big_g)
            dec = jnp.exp(g_last)
        outs = []
        for h in range(HEADS):
            ks = slice(h * DK, (h + 1) * DK)
            vs = slice(h * DV, (h + 1) * DV)
            qh = q_ref[rows, ks].astype(F32)
            kh = k_ref[rows, ks].astype(F32)
            vh = v_ref[rows, vs]
            st = st_ref[h]
            if kind == "gla":
                qh = qh * DK ** -0.5
                q_dec = (qh * e_pos[:, ks]).astype(_MXU)
                k_inv = (kh * e_neg[:, ks]).astype(_MXU)
                k_end = (kh * e_end[:, ks]).astype(_MXU)
                sc = jnp.where(mask, _dot_nt(q_dec, k_inv), 0.0)
                dec_h = dec[:, ks]
            else:
                kh = kh * DK ** -0.5
                cos = cos_ref[rows, :]
                sin = sin_ref[rows, :]
                qh = qh * cos + pltpu.roll(qh, DK // 2, 1) * sin
                kh = kh * cos + pltpu.roll(kh, DK // 2, 1) * sin
                lg = math.log(1.0 - 2.0 ** (-5.0 - h))
                steps = (CHUNK - tpos) if backward else (tpos + 1.0)
                dist = (col - row) if backward else (row - col)
                dmat = jnp.where(mask, jnp.exp(dist.astype(F32) * lg), 0.0)
                q_dec = (qh * jnp.exp(steps * lg)).astype(_MXU)
                k_end = (kh * jnp.exp((CHUNK - steps) * lg)).astype(_MXU)
                sc = _dot_nt(qh.astype(_MXU), kh.astype(_MXU)) * dmat
                dec_h = math.exp(CHUNK * lg)
            o = _dot(sc.astype(_MXU), vh) + _dot_nt(q_dec, st.astype(_MXU))
            st_ref[h] = st * dec_h + _dot_tn(vh, k_end)
            outs.append(o)
        return jnp.concatenate(outs, axis=1)

    @pl.when(ph == 0)
    def _():
        base = pl.multiple_of(s * tt, tt)
        for c in range(n_chunks):
            of_ref[pl.ds(base + c * CHUNK, CHUNK), :] = chunk_outputs(c, False)

    @pl.when(ph == 1)
    def _():
        loc = jnp.where(s < n_ct, n_ct - 1 - s, n_ct + n_lt - 1 - (s - n_ct))
        base = pl.multiple_of(loc * tt, tt)
        for c in reversed(range(n_chunks)):
            rows = slice(c * CHUNK, (c + 1) * CHUNK)
            o = of_ref[pl.ds(base + c * CHUNK, CHUNK), :] + chunk_outputs(c, True)
            gate = gate_ref[rows, :].astype(F32)
            parts = []
            for h in range(HEADS):
                oh = o[:, h * DV:(h + 1) * DV]
                if kind == "ret":
                    oh = oh - jnp.mean(oh, axis=-1, keepdims=True)
                parts.append(oh * lax.rsqrt(jnp.mean(oh * oh, axis=-1, keepdims=True) + EPS))
            normed = jnp.concatenate(parts, axis=1) * gain_ref[...]
            out_ref[rows, :] = (normed * _silu(gate)).astype(out_ref.dtype)


def _la_call(kind, p, extra, gain, nb, nl, nc):
    t = p.shape[0]
    tt = _pick(math.gcd(nl, nc), (256, 128, 64))
    n_ct, n_lt = nc // tt, nl // tt
    lat_tiles = nb * n_lt
    cq, ck, cv, cg = ((C_GLA_Q, C_GLA_K, C_GLA_V, C_GLA_G) if kind == "gla"
                      else (C_RET_Q, C_RET_K, C_RET_V, C_RET_G))
    hd = HEADS * DK
    hv = HEADS * DV

    def loc_of(ph, s):
        back = jnp.where(s < n_ct, n_ct - 1 - s, n_ct + n_lt - 1 - (s - n_ct))
        return jnp.where(ph == 0, s, back)

    def row_blk(b, loc):
        return jnp.where(loc < n_ct, lat_tiles + b * n_ct + loc, b * n_lt + (loc - n_ct))

    def in_map(cblk):
        return lambda b, ph, s: (row_blk(b, loc_of(ph, s)), cblk)

    def second_pass_map(cblk):
        return lambda b, ph, s: (row_blk(b, loc_of(1, jnp.where(ph == 0, 0, s))), cblk)

    in_specs = [pl.BlockSpec((tt, hd), in_map(cq // hd)),
                pl.BlockSpec((tt, hd), in_map(ck // hd)),
                pl.BlockSpec((tt, hv), in_map(cv // hv)),
                pl.BlockSpec((tt, hv), second_pass_map(cg // hv))]
    args = [p, p, p, p]
    if kind == "gla":
        wa2p, ba = extra
        in_specs += [pl.BlockSpec((tt, LANES), in_map(C_LR // LANES)),
                     pl.BlockSpec((2, LANES, hd), lambda b, ph, s: (0, 0, 0)),
                     pl.BlockSpec((2, 1, hd), lambda b, ph, s: (0, 0, 0))]
        args += [p, wa2p, ba]
    else:
        cos_t, sin_t = extra

        def rope_map(b, ph, s):
            loc = loc_of(ph, s)
            return (jnp.where(loc < n_ct, n_lt + loc, loc - n_ct), 0)
        in_specs += [pl.BlockSpec((tt, DK), rope_map), pl.BlockSpec((tt, DK), rope_map)]
        args += [cos_t, sin_t]
    in_specs.append(pl.BlockSpec((1, hv), lambda b, ph, s: (0, 0)))
    args.append(gain)
    return pl.pallas_call(
        functools.partial(_la_kernel, kind=kind, tt=tt, n_ct=n_ct, n_lt=n_lt),
        grid=(nb, 2, n_ct + n_lt),
        in_specs=in_specs,
        out_specs=pl.BlockSpec((tt, hv), second_pass_map(0)),
        out_shape=jax.ShapeDtypeStruct((t, hv), _MXU),
        scratch_shapes=[pltpu.VMEM((HEADS, DV, DK), F32),
                        pltpu.VMEM((nc + nl, hv), F32)],
        compiler_params=_cparams(("arbitrary", "arbitrary", "arbitrary"), 48),
        name=kind,
    )(*args)


def _scan_group(a, b, h, reverse):
    row = lax.broadcasted_iota(jnp.int32, a.shape, 0)
    for sft in (1, 2, 4):
        if reverse:
            a_sh = pltpu.roll(a, SUBLANES - sft, 0)
            b_sh = pltpu.roll(b, SUBLANES - sft, 0)
            m = row < SUBLANES - sft
        else:
            a_sh = pltpu.roll(a, sft, 0)
            b_sh = pltpu.roll(b, sft, 0)
            m = row >= sft
        b = jnp.where(m, a * b_sh + b, b)
        a = jnp.where(m, a * a_sh, a)
    hh = a * h + b
    return hh, (hh[0:1, :] if reverse else hh[SUBLANES - 1:SUBLANES, :])


def _lru_kernel(x_ref, gel_ref, cw_ref, cb_ref, wa_ref, ba_ref, wi_ref, bi_ref, lam_ref, h0_ref,
                out_ref, hn_ref, xpad_ref, xc_ref, hf_ref, *, seg, tile):
    n_tiles = seg // tile
    groups = tile // SUBLANES
    cw = cw_ref[...]
    cb = cb_ref[...]
    zeros8 = jnp.zeros((SUBLANES, xpad_ref.shape[1]), F32)
    xpad_ref[0:SUBLANES, :] = zeros8
    xpad_ref[SUBLANES + seg:2 * SUBLANES + seg, :] = zeros8

    def copy_in(i, carry):
        t0 = pl.multiple_of(i * tile, tile)
        xpad_ref[pl.ds(t0 + SUBLANES, tile), :] = x_ref[pl.ds(t0, tile), :].astype(F32)
        return carry
    lax.fori_loop(0, n_tiles, copy_in, 0)

    def gates(xc, d):
        xb = xc.astype(_MXU)
        r = _sigmoid(_dot(xb, wa_ref[d]) + ba_ref[d:d + 1, :])
        ig = _sigmoid(_dot(xb, wi_ref[d]) + bi_ref[d:d + 1, :])
        log_a = LRU_C * r * _log_sigmoid(lam_ref[d:d + 1, :])
        a = jnp.exp(log_a)
        b = jnp.sqrt(-jnp.tanh(log_a) * (a * a + 1.0)) * (ig * xc)
        return a, b

    def fwd_tile(i, h):
        t0 = pl.multiple_of(i * tile, tile)
        win = xpad_ref[pl.ds(t0, tile + 2 * SUBLANES), :]
        xc = (cw[0:1, :] * win[6:6 + tile] + cw[1:2, :] * win[7:7 + tile]
              + cw[2:3, :] * win[8:8 + tile] + cw[3:4, :] * win[9:9 + tile] + cb)
        xc_ref[pl.ds(t0, tile), :] = xc
        a, b = gates(xc, 0)
        for g in range(groups):
            rs = slice(g * SUBLANES, (g + 1) * SUBLANES)
            hh, h = _scan_group(a[rs], b[rs], h, False)
            hf_ref[pl.ds(t0 + g * SUBLANES, SUBLANES), :] = hh
        return h
    h_f = lax.fori_loop(0, n_tiles, fwd_tile, h0_ref[0:1, :])

    def bwd_tile(j, h):
        i = n_tiles - 1 - j
        t0 = pl.multiple_of(i * tile, tile)
        xc = xc_ref[pl.ds(t0, tile), :]
        a, b = gates(xc, 1)
        for g in reversed(range(groups)):
            rs = slice(g * SUBLANES, (g + 1) * SUBLANES)
            hh, h = _scan_group(a[rs], b[rs], h, True)
            rows = pl.ds(t0 + g * SUBLANES, SUBLANES)
            hsum = hf_ref[rows, :] + hh
            hf_ref[rows, :] = hsum
        gel = gel_ref[pl.ds(t0, tile), :].astype(F32)
        out_ref[pl.ds(t0, tile), :] = (hf_ref[pl.ds(t0, tile), :] * _gelu_tanh(gel)).astype(out_ref.dtype)
        return h
    h_b = lax.fori_loop(0, n_tiles, bwd_tile, h0_ref[1:2, :])
    hn_ref[0:1, :] = h_f
    hn_ref[1:2, :] = h_b


def _lru_call(p, prev_out, h0, lw, nb, seg, row_blk0):
    cw, cb, wa, ba, wi, bi, lam = lw
    t = p.shape[0]
    width = LRU_BLOCKS * DV
    cbw = width // LRU_BLOCKS
    tile = _pick(seg, (128, 64))
    xblk = C_LRU_X // cbw
    gblk = C_LRU_G // cbw
    in_specs = [pl.BlockSpec((seg, cbw), lambda b, c: (row_blk0 + b, xblk + c)),
                pl.BlockSpec((seg, cbw), lambda b, c: (row_blk0 + b, gblk + c)),
                pl.BlockSpec((4, cbw), lambda b, c: (0, c)),
                pl.BlockSpec((1, cbw), lambda b, c: (0, c)),
                pl.BlockSpec((2, None, cbw, cbw), lambda b, c: (0, c, 0, 0)),
                pl.BlockSpec((2, cbw), lambda b, c: (0, c)),
                pl.BlockSpec((2, None, cbw, cbw), lambda b, c: (0, c, 0, 0)),
                pl.BlockSpec((2, cbw), lambda b, c: (0, c)),
                pl.BlockSpec((2, cbw), lambda b, c: (0, c)),
                pl.BlockSpec((None, 2, cbw), lambda b, c: (b, 0, c))]
    in_specs.append(pl.BlockSpec(memory_space=pl.ANY))
    args = [p, p, cw, cb, wa, ba, wi, bi, lam, h0, prev_out]

    def body(*refs):
        _lru_kernel(*(refs[:10] + refs[11:]), seg=seg, tile=tile)

    return pl.pallas_call(
        body,
        grid=(nb, LRU_BLOCKS),
        in_specs=in_specs,
        out_specs=[pl.BlockSpec((seg, cbw), lambda b, c: (row_blk0 + b, c)),
                   pl.BlockSpec((None, 2, cbw), lambda b, c: (b, 0, c))],
        out_shape=[jax.ShapeDtypeStruct((t, width), _MXU),
                   jax.ShapeDtypeStruct((nb, 2, width), F32)],
        scratch_shapes=[pltpu.VMEM((seg + 2 * SUBLANES, cbw), F32),
                        pltpu.VMEM((seg, cbw), F32),
                        pltpu.VMEM((seg, cbw), F32)],
        input_output_aliases={10: 0},
        compiler_params=_cparams(("arbitrary", "arbitrary"), 48),
        name="lru",
    )(*args)


def _finish_kernel(gla_ref, lru_ref, ret_ref, m0_ref, m1_ref, m2_ref, wb_ref, wo_ref, xs_ref,
                   mod_ref, g2_ref, wr_ref, br_ref, tril_ref,
                   xo_ref, xn_ref, idx_ref, wgt_ref, rank_ref, cnt_ref, carry_ref):
    @pl.when(pl.program_id(0) == 0)
    def _():
        carry_ref[...] = jnp.zeros_like(carry_ref)

    merged = (_sigmoid(m0_ref[...].astype(F32)) * _dot(gla_ref[...], wb_ref[0])
              + _sigmoid(m1_ref[...].astype(F32)) * _dot(lru_ref[...], wb_ref[1])
              + _sigmoid(m2_ref[...].astype(F32)) * _dot(ret_ref[...], wb_ref[2]))
    x = xs_ref[...] + mod_ref[2:3, :] * _dot(merged.astype(_MXU), wo_ref[...])
    xo_ref[...] = x
    xn = _rms(x, g2_ref[...]) * (1.0 + mod_ref[4:5, :]) + mod_ref[3:4, :]
    xn_ref[...] = xn

    logits = _dot(xn.astype(_MXU), wr_ref[...]) + br_ref[...]
    lane = lax.broadcasted_iota(jnp.int32, logits.shape, 1)
    ids, vals = [], []
    for _ in range(TOP_K):
        m = jnp.max(logits, axis=1, keepdims=True)
        sel = jnp.min(jnp.where(logits == m, lane.astype(F32), float(LANES)), axis=1,
                      keepdims=True).astype(jnp.int32)
        ids.append(sel)
        vals.append(m)
        logits = jnp.where(lane == sel, -jnp.inf, logits)
    ex = [jnp.exp(v - vals[0]) for v in vals]
    denom = ex[0] + ex[1] + ex[2] + ex[3]
    onehot = jnp.zeros(logits.shape, F32)
    for sel in ids:
        onehot = onehot + (lane == sel).astype(F32)
    before = _dot(tril_ref[...], onehot.astype(_MXU)) + carry_ref[0:1, :]
    idx_o = jnp.zeros(logits.shape, jnp.int32)
    rank_o = jnp.zeros(logits.shape, jnp.int32)
    wgt_o = jnp.zeros(logits.shape, F32)
    for k in range(TOP_K):
        rk = jnp.sum(jnp.where(lane == ids[k], before, 0.0), axis=1, keepdims=True)
        idx_o = jnp.where(lane == k, ids[k], idx_o)
        rank_o = jnp.where(lane == k, rk.astype(jnp.int32), rank_o)
        wgt_o = jnp.where(lane == k, ex[k] / denom, wgt_o)
    idx_ref[...] = idx_o
    rank_ref[...] = rank_o
    wgt_ref[...] = wgt_o
    total = carry_ref[0:1, :] + jnp.sum(onehot, axis=0, keepdims=True)
    carry_ref[...] = jnp.broadcast_to(total, carry_ref.shape)
    cnt_ref[...] = jnp.broadcast_to(total, cnt_ref.shape)


def _finish_call(gla, lru, ret, p, wb, wo, xs, mod, g2, wr, br, n_rows, nl, nb):
    d = xs.shape[1]
    tm = _pick(math.gcd(n_rows, nl), (512, 256))
    tril = jnp.asarray(np.tril(np.ones((tm, tm), np.float32), -1), _MXU)
    row = lambda i: (i, 0)
    const2 = lambda i: (0, 0)
    mblk = C_MERGE // d
    in_specs = [pl.BlockSpec((tm, d), row), pl.BlockSpec((tm, d), row), pl.BlockSpec((tm, d), row),
                pl.BlockSpec((tm, d), lambda i: (i, mblk)),
                pl.BlockSpec((tm, d), lambda i: (i, mblk + 1)),
                pl.BlockSpec((tm, d), lambda i: (i, mblk + 2)),
                pl.BlockSpec((3, d, d), lambda i: (0, 0, 0)),
                pl.BlockSpec((d, d), const2),
                pl.BlockSpec((tm, d), row),
                pl.BlockSpec((None, N_MOD, d), lambda i: (jnp.minimum(i * tm // nl, nb), 0, 0)),
                pl.BlockSpec((1, d), const2),
                pl.BlockSpec((d, LANES), const2),
                pl.BlockSpec((1, LANES), const2),
                pl.BlockSpec((tm, tm), const2)]
    out_specs = [pl.BlockSpec((tm, d), row), pl.BlockSpec((tm, d), row),
                 pl.BlockSpec((tm, LANES), row), pl.BlockSpec((tm, LANES), row),
                 pl.BlockSpec((tm, LANES), row), pl.BlockSpec((SUBLANES, LANES), const2)]
    out_shape = [jax.ShapeDtypeStruct((n_rows, d), F32), jax.ShapeDtypeStruct((n_rows, d), F32),
                 jax.ShapeDtypeStruct((n_rows, LANES), jnp.int32),
                 jax.ShapeDtypeStruct((n_rows, LANES), F32),
                 jax.ShapeDtypeStruct((n_rows, LANES), jnp.int32),
                 jax.ShapeDtypeStruct((SUBLANES, LANES), F32)]
    return pl.pallas_call(
        _finish_kernel,
        grid=(n_rows // tm,),
        in_specs=in_specs,
        out_specs=out_specs,
        out_shape=out_shape,
        scratch_shapes=[pltpu.VMEM((SUBLANES, LANES), F32)],
        compiler_params=_cparams(("arbitrary",), 56),
        name="finish",
    )(gla, lru, ret, p, p, p, wb, wo, xs, mod, g2, wr, br, tril)


def _slot_kernel(idx_ref, rank_ref, start_ref, slot_ref):
    idx = idx_ref[...].astype(F32)
    lane = lax.broadcasted_iota(jnp.int32, idx.shape, 1)
    start = start_ref[...]
    out = jnp.zeros(idx.shape, F32)
    for k in range(TOP_K):
        sel = jnp.sum(jnp.where(lane == k, idx, 0.0), axis=1, keepdims=True).astype(jnp.int32)
        st = jnp.sum(jnp.where(lane == sel, start, 0.0), axis=1, keepdims=True)
        out = jnp.where(lane == k, st, out)
    slot_ref[...] = out.astype(jnp.int32) + rank_ref[...]


def _slot_call(idx, rank, pad_start):
    n = idx.shape[0]
    tm = _pick(n, (2048, 1024, 512, 256))
    row = lambda i: (i, 0)
    return pl.pallas_call(
        _slot_kernel,
        grid=(n // tm,),
        in_specs=[pl.BlockSpec((tm, LANES), row), pl.BlockSpec((tm, LANES), row),
                  pl.BlockSpec((1, LANES), lambda i: (0, 0))],
        out_specs=pl.BlockSpec((tm, LANES), row),
        out_shape=jax.ShapeDtypeStruct((n, LANES), jnp.int32),
        compiler_params=_cparams(("arbitrary",), 24),
        name="slots",
    )(idx, rank, pad_start)


def _dispatch_kernel(slot_ref, x_ref, dst_in_ref, dst_ref, sem, *, tm):
    del dst_in_ref

    def row_copy(t, k):
        return pltpu.make_async_copy(x_ref.at[pl.ds(t, 1)],
                                     dst_ref.at[pl.ds(slot_ref[0, t * TOP_K + k], 1)], sem)

    def issue(t, carry):
        for k in range(TOP_K):
            row_copy(t, k).start()
        return carry
    lax.fori_loop(0, tm, issue, 0)

    def drain(t, carry):
        for k in range(TOP_K):
            row_copy(t, k).wait()
        return carry
    lax.fori_loop(0, tm, drain, 0)


def _dispatch_call(slots3, xn, zeros_sorted):
    n, d = xn.shape
    tm = slots3.shape[2] // TOP_K
    return pl.pallas_call(
        functools.partial(_dispatch_kernel, tm=tm),
        grid=(n // tm,),
        in_specs=[pl.BlockSpec((None, 1, tm * TOP_K), lambda i: (i, 0, 0), memory_space=pltpu.SMEM),
                  pl.BlockSpec((tm, d), lambda i: (i, 0)),
                  pl.BlockSpec(memory_space=pl.ANY)],
        out_specs=pl.BlockSpec(memory_space=pl.ANY),
        out_shape=jax.ShapeDtypeStruct(zeros_sorted.shape, zeros_sorted.dtype),
        scratch_shapes=[pltpu.SemaphoreType.DMA(())],
        input_output_aliases={2: 0},
        compiler_params=_cparams(("arbitrary",), 24),
        name="dispatch",
    )(slots3, xn, zeros_sorted)


def _experts_kernel(be_ref, nu_ref, x_ref, wgu_ref, bgu_ref, wd_ref, bd_ref, perm_ref, y_ref,
                    wgu_s, wd_s):
    j = pl.program_id(0)
    f = wd_ref.shape[0]
    grp = perm_ref.shape[0]
    half = grp // 2
    active = j < nu_ref[0]
    changed = jnp.logical_or(j == 0, be_ref[j] != be_ref[jnp.maximum(j - 1, 0)])

    @pl.when(jnp.logical_and(active, changed))
    def _():
        for g in range(2 * f // grp):
            cols = slice(g * grp, (g + 1) * grp)
            wgu_s[:, cols] = _dot(wgu_ref[:, cols].astype(_MXU), perm_ref[...]).astype(_MXU)
        wd_s[...] = wd_ref[...].astype(_MXU)

    @pl.when(active)
    def _():
        gu = _dot(x_ref[...].astype(_MXU), wgu_s[...]) + bgu_ref[...]
        acts = []
        for g in range(2 * f // grp):
            gate = jnp.minimum(gu[:, g * grp:g * grp + half], SWIGLU_LIMIT)
            up = jnp.clip(gu[:, g * grp + half:(g + 1) * grp], -SWIGLU_LIMIT, SWIGLU_LIMIT)
            acts.append((gate * _sigmoid(SWIGLU_ALPHA * gate) * (up + 1.0)).astype(_MXU))
        y_ref[...] = _dot(jnp.concatenate(acts, axis=1), wd_s[...]) + bd_ref[...]

    @pl.when(jnp.logical_not(active))
    def _():
        y_ref[...] = jnp.zeros_like(y_ref)


def _experts_call(li, block_e, n_used, x_sorted, w_gu, bgu_r, w_down, bd, bm):
    n_slots, d = x_sorted.shape
    f = w_down.shape[2]
    grp = 2 * LANES
    src = np.concatenate([np.arange(0, grp, 2), np.arange(1, grp, 2)])
    perm_np = np.zeros((grp, grp), np.float32)
    perm_np[src, np.arange(grp)] = 1.0
    perm = jnp.asarray(perm_np, _MXU)

    def xmap(j, be, nu):
        return (jnp.minimum(j, nu[0] - 1), 0)

    def wmap(j, be, nu):
        return (li, be[j], 0, 0)

    grid_spec = pltpu.PrefetchScalarGridSpec(
        num_scalar_prefetch=2,
        grid=(n_slots // bm,),
        in_specs=[pl.BlockSpec((bm, d), xmap),
                  pl.BlockSpec((None, None, d, 2 * f), wmap),
                  pl.BlockSpec((None, None, 1, 2 * f), wmap),
                  pl.BlockSpec((None, None, f, d), wmap),
                  pl.BlockSpec((None, None, 1, d), wmap),
                  pl.BlockSpec((grp, grp), lambda j, be, nu: (0, 0))],
        out_specs=pl.BlockSpec((bm, d), lambda j, be, nu: (j, 0)),
        scratch_shapes=[pltpu.VMEM((d, 2 * f), _MXU), pltpu.VMEM((f, d), _MXU)])
    return pl.pallas_call(
        _experts_kernel,
        grid_spec=grid_spec,
        out_shape=jax.ShapeDtypeStruct((n_slots, d), F32),
        compiler_params=_cparams(("arbitrary",), 56),
        name="experts",
    )(block_e, n_used, x_sorted, w_gu, bgu_r, w_down, bd, perm)


def _combine_kernel(slot_ref, wgt_ref, xs_ref, mod_ref, fin_ref, y_ref, out_ref, buf_ref, sem,
                    *, tm, final):
    def row_copy(t, k):
        return pltpu.make_async_copy(y_ref.at[pl.ds(slot_ref[0, t * TOP_K + k], 1)],
                                     buf_ref.at[k, pl.ds(t, 1)], sem)

    def issue(t, carry):
        for k in range(TOP_K):
            row_copy(t, k).start()
        return carry
    lax.fori_loop(0, tm, issue, 0)

    def drain(t, carry):
        for k in range(TOP_K):
            row_copy(t, k).wait()
        return carry
    lax.fori_loop(0, tm, drain, 0)

    wgt = wgt_ref[...]
    acc = wgt[:, 0:1] * buf_ref[0]
    for k in range(1, TOP_K):
        acc = acc + wgt[:, k:k + 1] * buf_ref[k]
    x = xs_ref[...] + mod_ref[5:6, :] * acc
    out_ref[...] = _rms(x, fin_ref[...]) if final else x


def _combine_call(slots3, wgt, xs, mod, fin, y_sorted, nl, nb, final):
    n, d = xs.shape
    tm = slots3.shape[2] // TOP_K
    return pl.pallas_call(
        functools.partial(_combine_kernel, tm=tm, final=final),
        grid=(n // tm,),
        in_specs=[pl.BlockSpec((None, 1, tm * TOP_K), lambda i: (i, 0, 0), memory_space=pltpu.SMEM),
                  pl.BlockSpec((tm, LANES), lambda i: (i, 0)),
                  pl.BlockSpec((tm, d), lambda i: (i, 0)),
                  pl.BlockSpec((None, N_MOD, d), lambda i: (jnp.minimum(i * tm // nl, nb), 0, 0)),
                  pl.BlockSpec((1, d), lambda i: (0, 0)),
                  pl.BlockSpec(memory_space=pl.ANY)],
        out_specs=pl.BlockSpec((tm, d), lambda i: (i, 0)),
        out_shape=jax.ShapeDtypeStruct((n, d), F32),
        scratch_shapes=[pltpu.VMEM((TOP_K, tm, d), F32), pltpu.SemaphoreType.DMA(())],
        compiler_params=_cparams(("arbitrary",), 32),
        name="combine",
    )(slots3, wgt, xs, mod, fin, y_sorted)


def _moe(li, xn, idx, wgt, rank, counts, xs_mid, mod, fin, w_gu, bgu_r, w_down, bd, nl, nb, final):
    n, d = xn.shape
    bm = 256
    tm = _pick(math.gcd(n, nl), (256, 128))
    n_blocks = -(-(n * TOP_K) // bm) + N_EXPERTS
    cnt = counts[0, :N_EXPERTS].astype(jnp.int32)
    padded = (cnt + bm - 1) // bm * bm
    pad_end = jnp.cumsum(padded)
    pad_start = pad_end - padded
    n_used = (pad_end[-1] // bm).astype(jnp.int32)
    blk = jnp.arange(n_blocks, dtype=jnp.int32)
    first_row = jnp.minimum(blk, n_used - 1) * bm
    block_e = jnp.minimum(jnp.sum(pad_end[None, :] <= first_row[:, None], axis=1),
                          N_EXPERTS - 1).astype(jnp.int32)
    start_row = jnp.zeros((1, LANES), F32).at[0, :N_EXPERTS].set(pad_start.astype(F32))
    slots = _slot_call(idx, rank, start_row)
    slots3 = slots[:, :TOP_K].reshape(n // tm, 1, tm * TOP_K)
    x_sorted = _dispatch_call(slots3, xn, jnp.zeros((n_blocks * bm, d), F32))
    y_sorted = _experts_call(li, block_e, n_used.reshape(1), x_sorted, w_gu, bgu_r, w_down, bd, bm)
    return _combine_call(slots3, wgt, xs_mid, mod, fin, y_sorted, nl, nb, final)


def _rope_tables(nl, nc):
    rows = nl // GRID_W
    r = np.broadcast_to(np.arange(rows)[:, None], (rows, GRID_W)).reshape(-1).astype(np.float32)
    c = np.broadcast_to(np.arange(GRID_W)[None, :], (rows, GRID_W)).reshape(-1).astype(np.float32)
    n_freq = DK // 4
    inv = (ROPE_BASE ** (-jnp.arange(n_freq, dtype=F32) / n_freq))
    ang = jnp.concatenate([jnp.asarray(r)[:, None] * inv, jnp.asarray(c)[:, None] * inv], axis=-1)
    cos, sin = jnp.cos(ang), jnp.sin(ang)
    cos2 = jnp.concatenate([cos, cos], axis=-1)
    sin2 = jnp.concatenate([-sin, sin], axis=-1)
    cos_t = jnp.concatenate([cos2, jnp.ones((nc, DK), F32)], axis=0)
    sin_t = jnp.concatenate([sin2, jnp.zeros((nc, DK), F32)], axis=0)
    return cos_t, sin_t


def kernel(x, c, ctx, c_ctx, w_ada, b_ada, norm1, norm2, w_in, gla_wa2, gla_ba, gla_norm, lru_conv_w, lru_conv_b, lru_wa, lru_ba, lru_wi, lru_bi, lru_lam, ret_norm, w_branch, w_out, w_router, b_router, w_gu, b_gu, w_down, b_down, final_norm):
    nb, nl, d = x.shape
    nc = ctx.shape[1]
    depth = w_ada.shape[0]
    n_lat = nb * nl
    f = w_down.shape[2]

    xs = jnp.concatenate([x.reshape(n_lat, d), ctx.reshape(nb * nc, d)], axis=0)
    mod_rows = -(-(nb + 1) // SUBLANES) * SUBLANES
    cc = jnp.zeros((mod_rows, d), F32).at[:nb].set(c).at[nb].set(c_ctx)
    mod_all = _ada_call(cc, w_ada.astype(_MXU), b_ada).reshape(depth, mod_rows, N_MOD, d)
    cos_t, sin_t = _rope_tables(nl, nc)
    bgu_r = b_gu.reshape(depth, N_EXPERTS, 2 * f // (2 * LANES), LANES, 2)
    bgu_r = jnp.swapaxes(bgu_r, -1, -2).reshape(depth, N_EXPERTS, 1, 2 * f)
    bd_r = b_down.reshape(depth, N_EXPERTS, 1, d)

    out = None
    for li in range(depth):
        last = li == depth - 1
        mod = mod_all[li]
        w = w_in[li]
        w_in_r = jnp.concatenate(
            [w[:, :3072], w[:, 3104:], w[:, 3072:3104],
             jnp.zeros((d, N_PROJ - w.shape[1]), w.dtype)], axis=1).astype(_MXU)
        p = _proj_call(xs, mod, norm1[li].reshape(1, d), w_in_r, n_lat, nl, nb)

        wa2p = jnp.zeros((2, LANES, HEADS * DK), F32)
        wa2p = wa2p.at[0, :GLA_RANK].set(gla_wa2[li, 0]).at[1, GLA_RANK:2 * GLA_RANK].set(gla_wa2[li, 1])
        gla = _la_call("gla", p, (wa2p.astype(_MXU), gla_ba[li].reshape(2, 1, HEADS * DK)),
                       gla_norm[li].reshape(1, HEADS * DV), nb, nl, nc)
        ret = _la_call("ret", p, (cos_t, sin_t), ret_norm[li].reshape(1, HEADS * DV), nb, nl, nc)

        lw = (lru_conv_w[li], lru_conv_b[li].reshape(1, -1), lru_wa[li].astype(_MXU), lru_ba[li],
              lru_wi[li].astype(_MXU), lru_bi[li], lru_lam[li])
        width = lru_conv_w.shape[2]
        lru_c, h_ctx = _lru_call(p, jnp.zeros((xs.shape[0], width), _MXU),
                                 jnp.zeros((nb, 2, width), F32), lw, nb, nc, n_lat // nc)
        lru, _ = _lru_call(p, lru_c, h_ctx, lw, nb, nl, 0)

        n_rows = n_lat if last else xs.shape[0]
        wr = jnp.zeros((d, LANES), F32).at[:, :N_EXPERTS].set(w_router[li]).astype(_MXU)
        br = jnp.full((1, LANES), NEG_BIG, F32).at[0, :N_EXPERTS].set(b_router[li])
        xs_mid, xn2, idx, wgt, rank, counts = _finish_call(
            gla, lru, ret, p, w_branch[li].astype(_MXU), w_out[li].astype(_MXU), xs, mod,
            norm2[li].reshape(1, d), wr, br, n_rows, nl, nb)

        out = _moe(li, xn2, idx, wgt, rank, counts, xs_mid, mod, final_norm.reshape(1, d), w_gu,
                   bgu_r, w_down, bd_r, nl, nb, last)
        xs = out
    return out.reshape(nb, nl, d)
```

```python
import functools
import math

import numpy as np
import jax
import jax.numpy as jnp
from jax import lax
from jax.experimental import pallas as pl
from jax.experimental.pallas import tpu as pltpu

F32 = jnp.float32
_MXU = jnp.bfloat16

EPS = 1e-6
N_MOD = 6
GRID_W = 64
CHUNK = 64
HEADS = 4
DK = 128
DV = 256
GLA_RANK = 16
GLA_GATE_NORM = 16.0
LRU_BLOCKS = 4
LRU_C = 8.0
ROPE_BASE = 10000.0
N_EXPERTS = 32
TOP_K = 4
SWIGLU_LIMIT = 7.0
SWIGLU_ALPHA = 1.702
LANES = 128
SUBLANES = 8
NEG_BIG = -1e30

C_GLA_Q, C_GLA_K, C_GLA_V, C_GLA_G = 0, 512, 1024, 2048
C_LRU_X, C_LRU_G = 3072, 4096
C_RET_Q, C_RET_K, C_RET_V, C_RET_G = 5120, 5632, 6144, 7168
C_MERGE = 8192
C_LR = 11264
N_PROJ = 11520
PROJ_TN = 1280


def _pick(n, prefs):
    for p in prefs:
        if n % p == 0:
            return p
    raise ValueError(f"no tile for {n} in {prefs}")


def _cparams(sem, vmem_mb):
    return pltpu.CompilerParams(dimension_semantics=sem, vmem_limit_bytes=vmem_mb * 1024 * 1024)


def _dot(a, b):
    return jnp.dot(a, b, preferred_element_type=F32)


def _dot_nt(a, b):
    return lax.dot_general(a, b, (((1,), (1,)), ((), ())), preferred_element_type=F32)


def _dot_tn(a, b):
    return lax.dot_general(a, b, (((0,), (0,)), ((), ())), preferred_element_type=F32)


def _sigmoid(x):
    return 0.5 * jnp.tanh(0.5 * x) + 0.5


def _log_sigmoid(x):
    return jnp.minimum(x, 0.0) - jnp.log1p(jnp.exp(-jnp.abs(x)))


def _silu(x):
    return x * _sigmoid(x)


def _gelu_tanh(x):
    return 0.5 * x * (1.0 + jnp.tanh(math.sqrt(2.0 / math.pi) * (x + 0.044715 * (x * x * x))))


def _rms(x, gain):
    return x * lax.rsqrt(jnp.mean(x * x, axis=-1, keepdims=True) + EPS) * gain


def _dot01_exact(tri, g):
    hi = g.astype(_MXU)
    r1 = g - hi.astype(F32)
    mid = r1.astype(_MXU)
    lo = (r1 - mid.astype(F32)).astype(_MXU)
    return _dot(tri, hi) + _dot(tri, mid) + _dot(tri, lo)


def _ada_kernel(c_ref, w_ref, b_ref, o_ref):
    s = _silu(c_ref[...])
    o_ref[...] = _dot(s.astype(_MXU), w_ref[...]) + b_ref[...]


def _ada_call(cc, w_ada, b_ada):
    depth, d, n = w_ada.shape
    rows = cc.shape[0]
    tn = _pick(n, (1536, 1024, 512, 128))
    return pl.pallas_call(
        _ada_kernel,
        grid=(depth, n // tn),
        in_specs=[pl.BlockSpec((rows, d), lambda l, j: (0, 0)),
                  pl.BlockSpec((None, d, tn), lambda l, j: (l, 0, j)),
                  pl.BlockSpec((None, 1, tn), lambda l, j: (l, 0, j))],
        out_specs=pl.BlockSpec((None, rows, tn), lambda l, j: (l, 0, j)),
        out_shape=jax.ShapeDtypeStruct((depth, rows, n), F32),
        compiler_params=_cparams(("arbitrary", "arbitrary"), 24),
        name="ada",
    )(cc, w_ada, b_ada.reshape(depth, 1, n))


def _proj_kernel(x_ref, mod_ref, g_ref, w_ref, o_ref, xn_ref):
    @pl.when(pl.program_id(1) == 0)
    def _():
        y = _rms(x_ref[...], g_ref[...])
        xn_ref[...] = (y * (1.0 + mod_ref[1:2, :]) + mod_ref[0:1, :]).astype(xn_ref.dtype)

    o_ref[...] = _dot(xn_ref[...], w_ref[...]).astype(o_ref.dtype)


def _proj_call(xs, mod, gain, w_in_r, n_lat_rows, nl, nb):
    t, d = xs.shape
    tm = _pick(math.gcd(t, nl), (1024, 512, 256))
    return pl.pallas_call(
        _proj_kernel,
        grid=(t // tm, N_PROJ // PROJ_TN),
        in_specs=[pl.BlockSpec((tm, d), lambda i, j: (i, 0)),
                  pl.BlockSpec((None, N_MOD, d), lambda i, j: (jnp.minimum(i * tm // nl, nb), 0, 0)),
                  pl.BlockSpec((1, d), lambda i, j: (0, 0)),
                  pl.BlockSpec((d, PROJ_TN), lambda i, j: (0, j))],
        out_specs=pl.BlockSpec((tm, PROJ_TN), lambda i, j: (i, j)),
        out_shape=jax.ShapeDtypeStruct((t, N_PROJ), _MXU),
        scratch_shapes=[pltpu.VMEM((tm, d), _MXU)],
        compiler_params=_cparams(("arbitrary", "arbitrary"), 40),
        name="proj",
    )(xs, mod, gain, w_in_r)


def _la_kernel(*refs, kind, tt, n_ct, n_lt):
    if kind == "gla":
        (q_ref, k_ref, v_ref, gate_ref, lr_ref, wa2_ref, ba_ref, gain_ref,
         out_ref, st_ref, of_ref) = refs
    else:
        (q_ref, k_ref, v_ref, gate_ref, cos_ref, sin_ref, gain_ref,
         out_ref, st_ref, of_ref) = refs
    ph = pl.program_id(1)
    s = pl.program_id(2)
    ck = CHUNK if kind == "gla" else tt
    n_chunks = tt // ck

    @pl.when(s == 0)
    def _():
        st_ref[...] = jnp.zeros_like(st_ref)

    row = lax.broadcasted_iota(jnp.int32, (ck, ck), 0)
    col = lax.broadcasted_iota(jnp.int32, (ck, ck), 1)
    tpos = lax.broadcasted_iota(jnp.int32, (ck, 1), 0).astype(F32)

    def chunk_outputs(c, backward):
        r0 = c * ck
        rows = slice(r0, r0 + ck)
        if backward:
            mask = col > row
            tri = (col >= row).astype(_MXU)
        else:
            mask = col <= row
            tri = (col <= row).astype(_MXU)
        if kind == "gla":
            d = 1 if backward else 0
            z = _dot(lr_ref[rows, :], wa2_ref[d]) + ba_ref[d]
            g = _log_sigmoid(z) * (1.0 / GLA_GATE_NORM)
            big_g = _dot01_exact(tri, g)
            g_last = big_g[0:1, :] if backward else big_g[ck - 1:ck, :]
            e_pos = jnp.exp(big_g)
            e_neg = jnp.exp(-big_g)
            e_end = jnp.exp(g_last - big_g)
            dec = jnp.exp(g_last)
        outs = []
        for h in range(HEADS):
            ks = slice(h * DK, (h + 1) * DK)
            vs = slice(h * DV, (h + 1) * DV)
            qh = q_ref[rows, ks].astype(F32)
            kh = k_ref[rows, ks].astype(F32)
            vh = v_ref[rows, vs]
            st = st_ref[h]
            if kind == "gla":
                qh = qh * DK ** -0.5
                q_dec = (qh * e_pos[:, ks]).astype(_MXU)
                k_inv = (kh * e_neg[:, ks]).astype(_MXU)
                k_end = (kh * e_end[:, ks]).astype(_MXU)
                sc = jnp.where(mask, _dot_nt(q_dec, k_inv), 0.0)
                dec_h = dec[:, ks]
            else:
                kh = kh * DK ** -0.5
                cos = cos_ref[rows, :]
                sin = sin_ref[rows, :]
                qh = qh * cos + pltpu.roll(qh, DK // 2, 1) * sin
                kh = kh * cos + pltpu.roll(kh, DK // 2, 1) * sin
                lg = math.log(1.0 - 2.0 ** (-5.0 - h))
                steps = (ck - tpos) if backward else (tpos + 1.0)
                dist = (col - row) if backward else (row - col)
                dmat = jnp.where(mask, jnp.exp(dist.astype(F32) * lg), 0.0)
                q_dec = (qh * jnp.exp(steps * lg)).astype(_MXU)
                k_end = (kh * jnp.exp((ck - steps) * lg)).astype(_MXU)
                sc = _dot_nt(qh.astype(_MXU), kh.astype(_MXU)) * dmat
                dec_h = math.exp(ck * lg)
            o = _dot(sc.astype(_MXU), vh) + _dot_nt(q_dec, st.astype(_MXU))
            st_ref[h] = st * dec_h + _dot_tn(vh, k_end)
            outs.append(o)
        return jnp.concatenate(outs, axis=1)

    @pl.when(ph == 0)
    def _():
        base = pl.multiple_of(s * tt, tt)
        for c in range(n_chunks):
            of_ref[pl.ds(base + c * ck, ck), :] = chunk_outputs(c, False)

    @pl.when(ph == 1)
    def _():
        loc = jnp.where(s < n_ct, n_ct - 1 - s, n_ct + n_lt - 1 - (s - n_ct))
        base = pl.multiple_of(loc * tt, tt)
        for c in reversed(range(n_chunks)):
            rows = slice(c * ck, (c + 1) * ck)
            o = of_ref[pl.ds(base + c * ck, ck), :] + chunk_outputs(c, True)
            gate = gate_ref[rows, :].astype(F32)
            parts = []
            for h in range(HEADS):
                oh = o[:, h * DV:(h + 1) * DV]
                if kind == "ret":
                    oh = oh - jnp.mean(oh, axis=-1, keepdims=True)
                parts.append(oh * lax.rsqrt(jnp.mean(oh * oh, axis=-1, keepdims=True) + EPS))
            normed = jnp.concatenate(parts, axis=1) * gain_ref[...]
            out_ref[rows, :] = (normed * _silu(gate)).astype(out_ref.dtype)


def _la_call(kind, p, extra, gain, nb, nl, nc):
    t = p.shape[0]
    tt = _pick(math.gcd(nl, nc), (256, 128, 64))
    n_ct, n_lt = nc // tt, nl // tt
    lat_tiles = nb * n_lt
    cq, ck, cv, cg = ((C_GLA_Q, C_GLA_K, C_GLA_V, C_GLA_G) if kind == "gla"
                      else (C_RET_Q, C_RET_K, C_RET_V, C_RET_G))
    hd = HEADS * DK
    hv = HEADS * DV

    def loc_of(ph, s):
        back = jnp.where(s < n_ct, n_ct - 1 - s, n_ct + n_lt - 1 - (s - n_ct))
        return jnp.where(ph == 0, s, back)

    def row_blk(b, loc):
        return jnp.where(loc < n_ct, lat_tiles + b * n_ct + loc, b * n_lt + (loc - n_ct))

    def in_map(cblk):
        return lambda b, ph, s: (row_blk(b, loc_of(ph, s)), cblk)

    def second_pass_map(cblk):
        return lambda b, ph, s: (row_blk(b, loc_of(1, jnp.where(ph == 0, 0, s))), cblk)

    in_specs = [pl.BlockSpec((tt, hd), in_map(cq // hd)),
                pl.BlockSpec((tt, hd), in_map(ck // hd)),
                pl.BlockSpec((tt, hv), in_map(cv // hv)),
                pl.BlockSpec((tt, hv), second_pass_map(cg // hv))]
    args = [p, p, p, p]
    if kind == "gla":
        wa2p, ba = extra
        in_specs += [pl.BlockSpec((tt, LANES), in_map(C_LR // LANES)),
                     pl.BlockSpec((2, LANES, hd), lambda b, ph, s: (0, 0, 0)),
                     pl.BlockSpec((2, 1, hd), lambda b, ph, s: (0, 0, 0))]
        args += [p, wa2p, ba]
    else:
        cos_t, sin_t = extra

        def rope_map(b, ph, s):
            loc = loc_of(ph, s)
            return (jnp.where(loc < n_ct, n_lt + loc, loc - n_ct), 0)
        in_specs += [pl.BlockSpec((tt, DK), rope_map), pl.BlockSpec((tt, DK), rope_map)]
        args += [cos_t, sin_t]
    in_specs.append(pl.BlockSpec((1, hv), lambda b, ph, s: (0, 0)))
    args.append(gain)
    return pl.pallas_call(
        functools.partial(_la_kernel, kind=kind, tt=tt, n_ct=n_ct, n_lt=n_lt),
        grid=(nb, 2, n_ct + n_lt),
        in_specs=in_specs,
        out_specs=pl.BlockSpec((tt, hv), second_pass_map(0)),
        out_shape=jax.ShapeDtypeStruct((t, hv), _MXU),
        scratch_shapes=[pltpu.VMEM((HEADS, DV, DK), F32),
                        pltpu.VMEM((nc + nl, hv), F32)],
        compiler_params=_cparams(("arbitrary", "arbitrary", "arbitrary"), 48),
        name=kind,
    )(*args)


def _scan_group(a, b, h, reverse):
    row = lax.broadcasted_iota(jnp.int32, a.shape, 0)
    for sft in (1, 2, 4):
        if reverse:
            a_sh = pltpu.roll(a, SUBLANES - sft, 0)
            b_sh = pltpu.roll(b, SUBLANES - sft, 0)
            m = row < SUBLANES - sft
        else:
            a_sh = pltpu.roll(a, sft, 0)
            b_sh = pltpu.roll(b, sft, 0)
            m = row >= sft
        b = jnp.where(m, a * b_sh + b, b)
        a = jnp.where(m, a * a_sh, a)
    hh = a * h + b
    return hh, (hh[0:1, :] if reverse else hh[SUBLANES - 1:SUBLANES, :])


def _lru_kernel(x_ref, gel_ref, cw_ref, cb_ref, wa_ref, ba_ref, wi_ref, bi_ref, lam_ref, h0_ref,
                out_ref, hn_ref, xpad_ref, xc_ref, hf_ref, *, seg, tile):
    n_tiles = seg // tile
    groups = tile // SUBLANES
    cw = cw_ref[...]
    cb = cb_ref[...]
    zeros8 = jnp.zeros((SUBLANES, xpad_ref.shape[1]), F32)
    xpad_ref[0:SUBLANES, :] = zeros8
    xpad_ref[SUBLANES + seg:2 * SUBLANES + seg, :] = zeros8

    def copy_in(i, carry):
        t0 = pl.multiple_of(i * tile, tile)
        xpad_ref[pl.ds(t0 + SUBLANES, tile), :] = x_ref[pl.ds(t0, tile), :].astype(F32)
        return carry
    lax.fori_loop(0, n_tiles, copy_in, 0)

    def gates(xc, d):
        xb = xc.astype(_MXU)
        r = _sigmoid(_dot(xb, wa_ref[d]) + ba_ref[d:d + 1, :])
        ig = _sigmoid(_dot(xb, wi_ref[d]) + bi_ref[d:d + 1, :])
        log_a = LRU_C * r * _log_sigmoid(lam_ref[d:d + 1, :])
        a = jnp.exp(log_a)
        b = jnp.sqrt(-jnp.tanh(log_a) * (a * a + 1.0)) * (ig * xc)
        return a, b

    def fwd_tile(i, h):
        t0 = pl.multiple_of(i * tile, tile)
        win = xpad_ref[pl.ds(t0, tile + 2 * SUBLANES), :]
        xc = (cw[0:1, :] * win[6:6 + tile] + cw[1:2, :] * win[7:7 + tile]
              + cw[2:3, :] * win[8:8 + tile] + cw[3:4, :] * win[9:9 + tile] + cb)
        xc_ref[pl.ds(t0, tile), :] = xc
        a, b = gates(xc, 0)
        for g in range(groups):
            rs = slice(g * SUBLANES, (g + 1) * SUBLANES)
            hh, h = _scan_group(a[rs], b[rs], h, False)
            hf_ref[pl.ds(t0 + g * SUBLANES, SUBLANES), :] = hh
        return h
    h_f = lax.fori_loop(0, n_tiles, fwd_tile, h0_ref[0:1, :])

    def bwd_tile(j, h):
        i = n_tiles - 1 - j
        t0 = pl.multiple_of(i * tile, tile)
        xc = xc_ref[pl.ds(t0, tile), :]
        a, b = gates(xc, 1)
        for g in reversed(range(groups)):
            rs = slice(g * SUBLANES, (g + 1) * SUBLANES)
            hh, h = _scan_group(a[rs], b[rs], h, True)
            rows = pl.ds(t0 + g * SUBLANES, SUBLANES)
            hsum = hf_ref[rows, :] + hh
            hf_ref[rows, :] = hsum
        gel = gel_ref[pl.ds(t0, tile), :].astype(F32)
        out_ref[pl.ds(t0, tile), :] = (hf_ref[pl.ds(t0, tile), :] * _gelu_tanh(gel)).astype(out_ref.dtype)
        return h
    h_b = lax.fori_loop(0, n_tiles, bwd_tile, h0_ref[1:2, :])
    hn_ref[0:1, :] = h_f
    hn_ref[1:2, :] = h_b


def _lru_call(p, prev_out, h0, lw, nb, seg, row_blk0):
    cw, cb, wa, ba, wi, bi, lam = lw
    t = p.shape[0]
    width = LRU_BLOCKS * DV
    cbw = width // LRU_BLOCKS
    tile = _pick(seg, (128, 64))
    xblk = C_LRU_X // cbw
    gblk = C_LRU_G // cbw
    in_specs = [pl.BlockSpec((seg, cbw), lambda b, c: (row_blk0 + b, xblk + c)),
                pl.BlockSpec((seg, cbw), lambda b, c: (row_blk0 + b, gblk + c)),
                pl.BlockSpec((4, cbw), lambda b, c: (0, c)),
                pl.BlockSpec((1, cbw), lambda b, c: (0, c)),
                pl.BlockSpec((2, None, cbw, cbw), lambda b, c: (0, c, 0, 0)),
                pl.BlockSpec((2, cbw), lambda b, c: (0, c)),
                pl.BlockSpec((2, None, cbw, cbw), lambda b, c: (0, c, 0, 0)),
                pl.BlockSpec((2, cbw), lambda b, c: (0, c)),
                pl.BlockSpec((2, cbw), lambda b, c: (0, c)),
                pl.BlockSpec((None, 2, cbw), lambda b, c: (b, 0, c))]
    in_specs.append(pl.BlockSpec(memory_space=pl.ANY))
    args = [p, p, cw, cb, wa, ba, wi, bi, lam, h0, prev_out]

    def body(*refs):
        _lru_kernel(*(refs[:10] + refs[11:]), seg=seg, tile=tile)

    return pl.pallas_call(
        body,
        grid=(nb, LRU_BLOCKS),
        in_specs=in_specs,
        out_specs=[pl.BlockSpec((seg, cbw), lambda b, c: (row_blk0 + b, c)),
                   pl.BlockSpec((None, 2, cbw), lambda b, c: (b, 0, c))],
        out_shape=[jax.ShapeDtypeStruct((t, width), _MXU),
                   jax.ShapeDtypeStruct((nb, 2, width), F32)],
        scratch_shapes=[pltpu.VMEM((seg + 2 * SUBLANES, cbw), F32),
                        pltpu.VMEM((seg, cbw), F32),
                        pltpu.VMEM((seg, cbw), F32)],
        input_output_aliases={10: 0},
        compiler_params=_cparams(("arbitrary", "arbitrary"), 48),
        name="lru",
    )(*args)


def _finish_kernel(gla_ref, lru_ref, ret_ref, m0_ref, m1_ref, m2_ref, wb_ref, wo_ref, xs_ref,
                   mod_ref, g2_ref, wr_ref, br_ref, tril_ref,
                   xo_ref, xn_ref, idx_ref, wgt_ref, rank_ref, cnt_ref, carry_ref):
    @pl.when(pl.program_id(0) == 0)
    def _():
        carry_ref[...] = jnp.zeros_like(carry_ref)

    merged = (_sigmoid(m0_ref[...].astype(F32)) * _dot(gla_ref[...], wb_ref[0])
              + _sigmoid(m1_ref[...].astype(F32)) * _dot(lru_ref[...], wb_ref[1])
              + _sigmoid(m2_ref[...].astype(F32)) * _dot(ret_ref[...], wb_ref[2]))
    x = xs_ref[...] + mod_ref[2:3, :] * _dot(merged.astype(_MXU), wo_ref[...])
    xo_ref[...] = x
    xn = _rms(x, g2_ref[...]) * (1.0 + mod_ref[4:5, :]) + mod_ref[3:4, :]
    xn_ref[...] = xn

    logits = _dot(xn.astype(_MXU), wr_ref[...]) + br_ref[...]
    lane = lax.broadcasted_iota(jnp.int32, logits.shape, 1)
    ids, vals = [], []
    for _ in range(TOP_K):
        m = jnp.max(logits, axis=1, keepdims=True)
        sel = jnp.min(jnp.where(logits == m, lane.astype(F32), float(LANES)), axis=1,
                      keepdims=True).astype(jnp.int32)
        ids.append(sel)
        vals.append(m)
        logits = jnp.where(lane == sel, -jnp.inf, logits)
    ex = [jnp.exp(v - vals[0]) for v in vals]
    denom = ex[0] + ex[1] + ex[2] + ex[3]
    onehot = jnp.zeros(logits.shape, F32)
    for sel in ids:
        onehot = onehot + (lane == sel).astype(F32)
    before = _dot(tril_ref[...], onehot.astype(_MXU)) + carry_ref[0:1, :]
    idx_o = jnp.zeros(logits.shape, jnp.int32)
    rank_o = jnp.zeros(logits.shape, jnp.int32)
    wgt_o = jnp.zeros(logits.shape, F32)
    for k in range(TOP_K):
        rk = jnp.sum(jnp.where(lane == ids[k], before, 0.0), axis=1, keepdims=True)
        idx_o = jnp.where(lane == k, ids[k], idx_o)
        rank_o = jnp.where(lane == k, rk.astype(jnp.int32), rank_o)
        wgt_o = jnp.where(lane == k, ex[k] / denom, wgt_o)
    idx_ref[...] = idx_o
    rank_ref[...] = rank_o
    wgt_ref[...] = wgt_o
    total = carry_ref[0:1, :] + jnp.sum(onehot, axis=0, keepdims=True)
    carry_ref[...] = jnp.broadcast_to(total, carry_ref.shape)
    cnt_ref[...] = jnp.broadcast_to(total, cnt_ref.shape)


def _finish_call(gla, lru, ret, p, wb, wo, xs, mod, g2, wr, br, n_rows, nl, nb):
    d = xs.shape[1]
    tm = _pick(math.gcd(n_rows, nl), (512, 256))
    tril = jnp.asarray(np.tril(np.ones((tm, tm), np.float32), -1), _MXU)
    row = lambda i: (i, 0)
    const2 = lambda i: (0, 0)
    mblk = C_MERGE // d
    in_specs = [pl.BlockSpec((tm, d), row), pl.BlockSpec((tm, d), row), pl.BlockSpec((tm, d), row),
                pl.BlockSpec((tm, d), lambda i: (i, mblk)),
                pl.BlockSpec((tm, d), lambda i: (i, mblk + 1)),
                pl.BlockSpec((tm, d), lambda i: (i, mblk + 2)),
                pl.BlockSpec((3, d, d), lambda i: (0, 0, 0)),
                pl.BlockSpec((d, d), const2),
                pl.BlockSpec((tm, d), row),
                pl.BlockSpec((None, N_MOD, d), lambda i: (jnp.minimum(i * tm // nl, nb), 0, 0)),
                pl.BlockSpec((1, d), const2),
                pl.BlockSpec((d, LANES), const2),
                pl.BlockSpec((1, LANES), const2),
                pl.BlockSpec((tm, tm), const2)]
    out_specs = [pl.BlockSpec((tm, d), row), pl.BlockSpec((tm, d), row),
                 pl.BlockSpec((tm, LANES), row), pl.BlockSpec((tm, LANES), row),
                 pl.BlockSpec((tm, LANES), row), pl.BlockSpec((SUBLANES, LANES), const2)]
    out_shape = [jax.ShapeDtypeStruct((n_rows, d), F32), jax.ShapeDtypeStruct((n_rows, d), F32),
                 jax.ShapeDtypeStruct((n_rows, LANES), jnp.int32),
                 jax.ShapeDtypeStruct((n_rows, LANES), F32),
                 jax.ShapeDtypeStruct((n_rows, LANES), jnp.int32),
                 jax.ShapeDtypeStruct((SUBLANES, LANES), F32)]
    return pl.pallas_call(
        _finish_kernel,
        grid=(n_rows // tm,),
        in_specs=in_specs,
        out_specs=out_specs,
        out_shape=out_shape,
        scratch_shapes=[pltpu.VMEM((SUBLANES, LANES), F32)],
        compiler_params=_cparams(("arbitrary",), 56),
        name="finish",
    )(gla, lru, ret, p, p, p, wb, wo, xs, mod, g2, wr, br, tril)


def _slot_kernel(idx_ref, rank_ref, start_ref, slot_ref):
    idx = idx_ref[...].astype(F32)
    lane = lax.broadcasted_iota(jnp.int32, idx.shape, 1)
    start = start_ref[...]
    out = jnp.zeros(idx.shape, F32)
    for k in range(TOP_K):
        sel = jnp.sum(jnp.where(lane == k, idx, 0.0), axis=1, keepdims=True).astype(jnp.int32)
        st = jnp.sum(jnp.where(lane == sel, start, 0.0), axis=1, keepdims=True)
        out = jnp.where(lane == k, st, out)
    slot_ref[...] = out.astype(jnp.int32) + rank_ref[...]


def _slot_call(idx, rank, pad_start):
    n = idx.shape[0]
    tm = _pick(n, (2048, 1024, 512, 256))
    row = lambda i: (i, 0)
    return pl.pallas_call(
        _slot_kernel,
        grid=(n // tm,),
        in_specs=[pl.BlockSpec((tm, LANES), row), pl.BlockSpec((tm, LANES), row),
                  pl.BlockSpec((1, LANES), lambda i: (0, 0))],
        out_specs=pl.BlockSpec((tm, LANES), row),
        out_shape=jax.ShapeDtypeStruct((n, LANES), jnp.int32),
        compiler_params=_cparams(("arbitrary",), 24),
        name="slots",
    )(idx, rank, pad_start)


def _dispatch_kernel(slot_ref, x_ref, dst_in_ref, dst_ref, sem, *, tm):
    del dst_in_ref

    def row_copy(t, k):
        return pltpu.make_async_copy(x_ref.at[pl.ds(t, 1)],
                                     dst_ref.at[pl.ds(slot_ref[0, t * TOP_K + k], 1)], sem)

    def issue(t, carry):
        for k in range(TOP_K):
            row_copy(t, k).start(priority=k % 2)
        return carry
    lax.fori_loop(0, tm, issue, 0, unroll=8)
    for k in range(TOP_K):
        pltpu.make_async_copy(x_ref, dst_ref.at[pl.ds(0, tm)], sem).wait()


def _dispatch_call(slots3, xn, zeros_sorted):
    n, d = xn.shape
    tm = slots3.shape[2] // TOP_K
    return pl.pallas_call(
        functools.partial(_dispatch_kernel, tm=tm),
        grid=(n // tm,),
        in_specs=[pl.BlockSpec((None, 1, tm * TOP_K), lambda i: (i, 0, 0), memory_space=pltpu.SMEM),
                  pl.BlockSpec((tm, d), lambda i: (i, 0)),
                  pl.BlockSpec(memory_space=pl.ANY)],
        out_specs=pl.BlockSpec(memory_space=pl.ANY),
        out_shape=jax.ShapeDtypeStruct(zeros_sorted.shape, zeros_sorted.dtype),
        scratch_shapes=[pltpu.SemaphoreType.DMA(())],
        input_output_aliases={2: 0},
        compiler_params=_cparams(("arbitrary",), 24),
        name="dispatch",
    )(slots3, xn, zeros_sorted)


def _experts_kernel(be_ref, nu_ref, x_ref, wgu_ref, bgu_ref, wd_ref, bd_ref, perm_ref, y_ref,
                    wgu_s, wd_s):
    j = pl.program_id(0)
    f = wd_ref.shape[0]
    grp = perm_ref.shape[0]
    half = grp // 2
    active = j < nu_ref[0]
    changed = jnp.logical_or(j == 0, be_ref[j] != be_ref[jnp.maximum(j - 1, 0)])

    @pl.when(jnp.logical_and(active, changed))
    def _():
        for g in range(2 * f // grp):
            cols = slice(g * grp, (g + 1) * grp)
            wgu_s[:, cols] = _dot(wgu_ref[:, cols].astype(_MXU), perm_ref[...]).astype(_MXU)
        wd_s[...] = wd_ref[...].astype(_MXU)

    @pl.when(active)
    def _():
        gu = _dot(x_ref[...].astype(_MXU), wgu_s[...]) + bgu_ref[...]
        acts = []
        for g in range(2 * f // grp):
            gate = jnp.minimum(gu[:, g * grp:g * grp + half], SWIGLU_LIMIT)
            up = jnp.clip(gu[:, g * grp + half:(g + 1) * grp], -SWIGLU_LIMIT, SWIGLU_LIMIT)
            acts.append((gate * _sigmoid(SWIGLU_ALPHA * gate) * (up + 1.0)).astype(_MXU))
        y_ref[...] = _dot(jnp.concatenate(acts, axis=1), wd_s[...]) + bd_ref[...]

    @pl.when(jnp.logical_not(active))
    def _():
        y_ref[...] = jnp.zeros_like(y_ref)


def _experts_call(li, block_e, n_used, x_sorted, w_gu, bgu_r, w_down, bd, bm):
    n_slots, d = x_sorted.shape
    f = w_down.shape[2]
    grp = 2 * LANES
    src = np.concatenate([np.arange(0, grp, 2), np.arange(1, grp, 2)])
    perm_np = np.zeros((grp, grp), np.float32)
    perm_np[src, np.arange(grp)] = 1.0
    perm = jnp.asarray(perm_np, _MXU)

    def xmap(j, be, nu):
        return (jnp.minimum(j, nu[0] - 1), 0)

    def wmap(j, be, nu):
        return (li, be[j], 0, 0)

    grid_spec = pltpu.PrefetchScalarGridSpec(
        num_scalar_prefetch=2,
        grid=(n_slots // bm,),
        in_specs=[pl.BlockSpec((bm, d), xmap),
                  pl.BlockSpec((None, None, d, 2 * f), wmap),
                  pl.BlockSpec((None, None, 1, 2 * f), wmap),
                  pl.BlockSpec((None, None, f, d), wmap),
                  pl.BlockSpec((None, None, 1, d), wmap),
                  pl.BlockSpec((grp, grp), lambda j, be, nu: (0, 0))],
        out_specs=pl.BlockSpec((bm, d), lambda j, be, nu: (j, 0)),
        scratch_shapes=[pltpu.VMEM((d, 2 * f), _MXU), pltpu.VMEM((f, d), _MXU)])
    return pl.pallas_call(
        _experts_kernel,
        grid_spec=grid_spec,
        out_shape=jax.ShapeDtypeStruct((n_slots, d), F32),
        compiler_params=_cparams(("arbitrary",), 56),
        name="experts",
    )(block_e, n_used, x_sorted, w_gu, bgu_r, w_down, bd, perm)


def _combine_kernel(slot_ref, wgt_ref, xs_ref, mod_ref, fin_ref, y_ref, out_ref, buf_ref, sem,
                    *, tm, final):
    def row_copy(t, k):
        return pltpu.make_async_copy(y_ref.at[pl.ds(slot_ref[0, t * TOP_K + k], 1)],
                                     buf_ref.at[k, pl.ds(t, 1)], sem)

    def issue(t, carry):
        for k in range(TOP_K):
            row_copy(t, k).start(priority=k % 2)
        return carry
    lax.fori_loop(0, tm, issue, 0, unroll=8)
    for k in range(TOP_K):
        pltpu.make_async_copy(y_ref.at[pl.ds(0, tm)], buf_ref.at[k], sem).wait()

    wgt = wgt_ref[...]
    acc = wgt[:, 0:1] * buf_ref[0]
    for k in range(1, TOP_K):
        acc = acc + wgt[:, k:k + 1] * buf_ref[k]
    x = xs_ref[...] + mod_ref[5:6, :] * acc
    out_ref[...] = _rms(x, fin_ref[...]) if final else x


def _combine_call(slots3, wgt, xs, mod, fin, y_sorted, nl, nb, final):
    n, d = xs.shape
    tm = slots3.shape[2] // TOP_K
    return pl.pallas_call(
        functools.partial(_combine_kernel, tm=tm, final=final),
        grid=(n // tm,),
        in_specs=[pl.BlockSpec((None, 1, tm * TOP_K), lambda i: (i, 0, 0), memory_space=pltpu.SMEM),
                  pl.BlockSpec((tm, LANES), lambda i: (i, 0)),
                  pl.BlockSpec((tm, d), lambda i: (i, 0)),
                  pl.BlockSpec((None, N_MOD, d), lambda i: (jnp.minimum(i * tm // nl, nb), 0, 0)),
                  pl.BlockSpec((1, d), lambda i: (0, 0)),
                  pl.BlockSpec(memory_space=pl.ANY)],
        out_specs=pl.BlockSpec((tm, d), lambda i: (i, 0)),
        out_shape=jax.ShapeDtypeStruct((n, d), F32),
        scratch_shapes=[pltpu.VMEM((TOP_K, tm, d), F32), pltpu.SemaphoreType.DMA(())],
        compiler_params=_cparams(("arbitrary",), 32),
        name="combine",
    )(slots3, wgt, xs, mod, fin, y_sorted)


def _moe(li, xn, idx, wgt, rank, counts, xs_mid, mod, fin, w_gu, bgu_r, w_down, bd, nl, nb, final):
    n, d = xn.shape
    bm = 512
    tm = _pick(math.gcd(n, nl), (256, 128))
    n_blocks = -(-(n * TOP_K) // bm) + N_EXPERTS
    cnt = counts[0, :N_EXPERTS].astype(jnp.int32)
    padded = (cnt + bm - 1) // bm * bm
    pad_end = jnp.cumsum(padded)
    pad_start = pad_end - padded
    n_used = (pad_end[-1] // bm).astype(jnp.int32)
    blk = jnp.arange(n_blocks, dtype=jnp.int32)
    first_row = jnp.minimum(blk, n_used - 1) * bm
    block_e = jnp.minimum(jnp.sum(pad_end[None, :] <= first_row[:, None], axis=1),
                          N_EXPERTS - 1).astype(jnp.int32)
    start_row = jnp.zeros((1, LANES), F32).at[0, :N_EXPERTS].set(pad_start.astype(F32))
    slots = _slot_call(idx, rank, start_row)
    slots3 = slots[:, :TOP_K].reshape(n // tm, 1, tm * TOP_K)
    x_sorted = _dispatch_call(slots3, xn, jnp.zeros((n_blocks * bm, d), F32))
    y_sorted = _experts_call(li, block_e, n_used.reshape(1), x_sorted, w_gu, bgu_r, w_down, bd, bm)
    return _combine_call(slots3, wgt, xs_mid, mod, fin, y_sorted, nl, nb, final)


def _rope_tables(nl, nc):
    rows = nl // GRID_W
    r = np.broadcast_to(np.arange(rows)[:, None], (rows, GRID_W)).reshape(-1).astype(np.float32)
    c = np.broadcast_to(np.arange(GRID_W)[None, :], (rows, GRID_W)).reshape(-1).astype(np.float32)
    n_freq = DK // 4
    inv = (ROPE_BASE ** (-jnp.arange(n_freq, dtype=F32) / n_freq))
    ang = jnp.concatenate([jnp.asarray(r)[:, None] * inv, jnp.asarray(c)[:, None] * inv], axis=-1)
    cos, sin = jnp.cos(ang), jnp.sin(ang)
    cos2 = jnp.concatenate([cos, cos], axis=-1)
    sin2 = jnp.concatenate([-sin, sin], axis=-1)
    cos_t = jnp.concatenate([cos2, jnp.ones((nc, DK), F32)], axis=0)
    sin_t = jnp.concatenate([sin2, jnp.zeros((nc, DK), F32)], axis=0)
    return cos_t, sin_t


def kernel(x, c, ctx, c_ctx, w_ada, b_ada, norm1, norm2, w_in, gla_wa2, gla_ba, gla_norm, lru_conv_w, lru_conv_b, lru_wa, lru_ba, lru_wi, lru_bi, lru_lam, ret_norm, w_branch, w_out, w_router, b_router, w_gu, b_gu, w_down, b_down, final_norm):
    nb, nl, d = x.shape
    nc = ctx.shape[1]
    depth = w_ada.shape[0]
    n_lat = nb * nl
    f = w_down.shape[2]

    xs = jnp.concatenate([x.reshape(n_lat, d), ctx.reshape(nb * nc, d)], axis=0)
    mod_rows = -(-(nb + 1) // SUBLANES) * SUBLANES
    cc = jnp.zeros((mod_rows, d), F32).at[:nb].set(c).at[nb].set(c_ctx)
    mod_all = _ada_call(cc, w_ada.astype(_MXU), b_ada).reshape(depth, mod_rows, N_MOD, d)
    cos_t, sin_t = _rope_tables(nl, nc)
    bgu_r = b_gu.reshape(depth, N_EXPERTS, 2 * f // (2 * LANES), LANES, 2)
    bgu_r = jnp.swapaxes(bgu_r, -1, -2).reshape(depth, N_EXPERTS, 1, 2 * f)
    bd_r = b_down.reshape(depth, N_EXPERTS, 1, d)

    out = None
    for li in range(depth):
        last = li == depth - 1
        mod = mod_all[li]
        w = w_in[li]
        w_in_r = jnp.concatenate(
            [w[:, :3072], w[:, 3104:], w[:, 3072:3104],
             jnp.zeros((d, N_PROJ - w.shape[1]), w.dtype)], axis=1).astype(_MXU)
        p = _proj_call(xs, mod, norm1[li].reshape(1, d), w_in_r, n_lat, nl, nb)

        wa2p = jnp.zeros((2, LANES, HEADS * DK), F32)
        wa2p = wa2p.at[0, :GLA_RANK].set(gla_wa2[li, 0]).at[1, GLA_RANK:2 * GLA_RANK].set(gla_wa2[li, 1])
        gla = _la_call("gla", p, (wa2p.astype(_MXU), gla_ba[li].reshape(2, 1, HEADS * DK)),
                       gla_norm[li].reshape(1, HEADS * DV), nb, nl, nc)
        ret = _la_call("ret", p, (cos_t, sin_t), ret_norm[li].reshape(1, HEADS * DV), nb, nl, nc)

        lw = (lru_conv_w[li], lru_conv_b[li].reshape(1, -1), lru_wa[li].astype(_MXU), lru_ba[li],
              lru_wi[li].astype(_MXU), lru_bi[li], lru_lam[li])
        width = lru_conv_w.shape[2]
        lru_c, h_ctx = _lru_call(p, jnp.zeros((xs.shape[0], width), _MXU),
                                 jnp.zeros((nb, 2, width), F32), lw, nb, nc, n_lat // nc)
        lru, _ = _lru_call(p, lru_c, h_ctx, lw, nb, nl, 0)

        n_rows = n_lat if last else xs.shape[0]
        wr = jnp.zeros((d, LANES), F32).at[:, :N_EXPERTS].set(w_router[li]).astype(_MXU)
        br = jnp.full((1, LANES), NEG_BIG, F32).at[0, :N_EXPERTS].set(b_router[li])
        xs_mid, xn2, idx, wgt, rank, counts = _finish_call(
            gla, lru, ret, p, w_branch[li].astype(_MXU), w_out[li].astype(_MXU), xs, mod,
            norm2[li].reshape(1, d), wr, br, n_rows, nl, nb)

        out = _moe(li, xn2, idx, wgt, rank, counts, xs_mid, mod, final_norm.reshape(1, d), w_gu,
                   bgu_r, w_down, bd_r, nl, nb, last)
        xs = out
    return out.reshape(nb, nl, d)
```

```python
import functools
import math

import numpy as np
import jax
import jax.numpy as jnp
from jax import lax
from jax.experimental import pallas as pl
from jax.experimental.pallas import tpu as pltpu

F32 = jnp.float32
_MXU = jnp.bfloat16

EPS = 1e-6
N_MOD = 6
GRID_W = 64
CHUNK = 64
HEADS = 4
DK = 128
DV = 256
GLA_RANK = 16
GLA_GATE_NORM = 16.0
LRU_BLOCKS = 4
LRU_C = 8.0
ROPE_BASE = 10000.0
N_EXPERTS = 32
TOP_K = 4
SWIGLU_LIMIT = 7.0
SWIGLU_ALPHA = 1.702
LANES = 128
SUBLANES = 8
NEG_BIG = -1e30

C_GLA_Q, C_GLA_K, C_GLA_V, C_GLA_G = 0, 512, 1024, 2048
C_LRU_X, C_LRU_G = 3072, 4096
C_RET_Q, C_RET_K, C_RET_V, C_RET_G = 5120, 5632, 6144, 7168
C_MERGE = 8192
C_LR = 11264
N_PROJ = 11520
PROJ_TN = 1280


def _pick(n, prefs):
    for p in prefs:
        if n % p == 0:
            return p
    raise ValueError(f"no tile for {n} in {prefs}")


def _cparams(sem, vmem_mb):
    return pltpu.CompilerParams(dimension_semantics=sem, vmem_limit_bytes=vmem_mb * 1024 * 1024)


def _dot(a, b):
    return jnp.dot(a, b, preferred_element_type=F32)


def _dot_nt(a, b):
    return lax.dot_general(a, b, (((1,), (1,)), ((), ())), preferred_element_type=F32)


def _dot_tn(a, b):
    return lax.dot_general(a, b, (((0,), (0,)), ((), ())), preferred_element_type=F32)


def _sigmoid(x):
    return 0.5 * jnp.tanh(0.5 * x) + 0.5


def _log_sigmoid(x):
    return jnp.minimum(x, 0.0) - jnp.log1p(jnp.exp(-jnp.abs(x)))


def _silu(x):
    return x * _sigmoid(x)


def _gelu_tanh(x):
    return 0.5 * x * (1.0 + jnp.tanh(math.sqrt(2.0 / math.pi) * (x + 0.044715 * (x * x * x))))


def _rms(x, gain):
    return x * lax.rsqrt(jnp.mean(x * x, axis=-1, keepdims=True) + EPS) * gain


def _dot01_exact(tri, g):
    hi = g.astype(_MXU)
    r1 = g - hi.astype(F32)
    mid = r1.astype(_MXU)
    lo = (r1 - mid.astype(F32)).astype(_MXU)
    return _dot(tri, hi) + _dot(tri, mid) + _dot(tri, lo)


def _ada_kernel(c_ref, w_ref, b_ref, o_ref):
    s = _silu(c_ref[...])
    o_ref[...] = _dot(s.astype(_MXU), w_ref[...]) + b_ref[...]


def _ada_call(cc, w_ada, b_ada):
    depth, d, n = w_ada.shape
    rows = cc.shape[0]
    tn = _pick(n, (1536, 1024, 512, 128))
    return pl.pallas_call(
        _ada_kernel,
        grid=(depth, n // tn),
        in_specs=[pl.BlockSpec((rows, d), lambda l, j: (0, 0)),
                  pl.BlockSpec((None, d, tn), lambda l, j: (l, 0, j)),
                  pl.BlockSpec((None, 1, tn), lambda l, j: (l, 0, j))],
        out_specs=pl.BlockSpec((None, rows, tn), lambda l, j: (l, 0, j)),
        out_shape=jax.ShapeDtypeStruct((depth, rows, n), F32),
        compiler_params=_cparams(("arbitrary", "arbitrary"), 24),
        name="ada",
    )(cc, w_ada, b_ada.reshape(depth, 1, n))


def _proj_kernel(x_ref, mod_ref, g_ref, w_ref, o_ref, xn_ref):
    @pl.when(pl.program_id(1) == 0)
    def _():
        y = _rms(x_ref[...], g_ref[...])
        xn_ref[...] = (y * (1.0 + mod_ref[1:2, :]) + mod_ref[0:1, :]).astype(xn_ref.dtype)

    o_ref[...] = _dot(xn_ref[...], w_ref[...]).astype(o_ref.dtype)


def _proj_call(xs, mod, gain, w_in_r, n_lat_rows, nl, nb):
    t, d = xs.shape
    tm = _pick(math.gcd(t, nl), (1024, 512, 256))
    return pl.pallas_call(
        _proj_kernel,
        grid=(t // tm, N_PROJ // PROJ_TN),
        in_specs=[pl.BlockSpec((tm, d), lambda i, j: (i, 0)),
                  pl.BlockSpec((None, N_MOD, d), lambda i, j: (jnp.minimum(i * tm // nl, nb), 0, 0)),
                  pl.BlockSpec((1, d), lambda i, j: (0, 0)),
                  pl.BlockSpec((d, PROJ_TN), lambda i, j: (0, j))],
        out_specs=pl.BlockSpec((tm, PROJ_TN), lambda i, j: (i, j)),
        out_shape=jax.ShapeDtypeStruct((t, N_PROJ), _MXU),
        scratch_shapes=[pltpu.VMEM((tm, d), _MXU)],
        compiler_params=_cparams(("arbitrary", "arbitrary"), 40),
        name="proj",
    )(xs, mod, gain, w_in_r)


def _la_kernel(*refs, kind, tt, n_ct, n_lt):
    if kind == "gla":
        (q_ref, k_ref, v_ref, gate_ref, lr_ref, wa2_ref, ba_ref, gain_ref,
         out_ref, st_ref, of_ref) = refs
    else:
        (q_ref, k_ref, v_ref, gate_ref, cos_ref, sin_ref, gain_ref,
         out_ref, st_ref, of_ref) = refs
    ph = pl.program_id(1)
    s = pl.program_id(2)
    ck = CHUNK if kind == "gla" else tt
    n_chunks = tt // ck

    @pl.when(s == 0)
    def _():
        st_ref[...] = jnp.zeros_like(st_ref)

    row = lax.broadcasted_iota(jnp.int32, (tt, tt), 0)
    col = lax.broadcasted_iota(jnp.int32, (tt, tt), 1)
    shift = ck.bit_length() - 1
    same_chunk = (row >> shift) == (col >> shift)
    tpos = lax.broadcasted_iota(jnp.int32, (tt, 1), 0).astype(F32)

    def tile_outputs(backward):
        if backward:
            mask = jnp.logical_and(same_chunk, col > row)
            tri = jnp.logical_and(same_chunk, col >= row).astype(_MXU)
        else:
            mask = jnp.logical_and(same_chunk, col <= row)
            tri = jnp.logical_and(same_chunk, col <= row).astype(_MXU)
        order = list(reversed(range(n_chunks))) if backward else list(range(n_chunks))
        if kind == "gla":
            d = 1 if backward else 0
            z = _dot(lr_ref[...], wa2_ref[d]) + ba_ref[d]
            g = _log_sigmoid(z) * (1.0 / GLA_GATE_NORM)
            big_g = _dot01_exact(tri, g)
            lasts = [big_g[c * ck:c * ck + 1, :] if backward else big_g[(c + 1) * ck - 1:(c + 1) * ck, :]
                     for c in range(n_chunks)]
            g_last = jnp.concatenate([jnp.broadcast_to(r, (ck, r.shape[1])) for r in lasts], axis=0)
            e_pos = jnp.exp(big_g)
            e_neg = jnp.exp(-big_g)
            e_end = jnp.exp(g_last - big_g)
            decs = [jnp.exp(r) for r in lasts]
        outs = []
        for h in range(HEADS):
            ks = slice(h * DK, (h + 1) * DK)
            vs = slice(h * DV, (h + 1) * DV)
            qh = q_ref[:, ks].astype(F32)
            kh = k_ref[:, ks].astype(F32)
            vh = v_ref[:, vs]
            if kind == "gla":
                qh = qh * DK ** -0.5
                q_dec = (qh * e_pos[:, ks]).astype(_MXU)
                k_inv = (kh * e_neg[:, ks]).astype(_MXU)
                k_end = (kh * e_end[:, ks]).astype(_MXU)
                sc = jnp.where(mask, _dot_nt(q_dec, k_inv), 0.0)
                dec_h = [dcy[:, ks] for dcy in decs]
            else:
                kh = kh * DK ** -0.5
                cos = cos_ref[...]
                sin = sin_ref[...]
                qh = qh * cos + pltpu.roll(qh, DK // 2, 1) * sin
                kh = kh * cos + pltpu.roll(kh, DK // 2, 1) * sin
                lg = math.log(1.0 - 2.0 ** (-5.0 - h))
                steps = (ck - tpos) if backward else (tpos + 1.0)
                dist = (col - row) if backward else (row - col)
                dmat = jnp.where(mask, jnp.exp(dist.astype(F32) * lg), 0.0)
                q_dec = (qh * jnp.exp(steps * lg)).astype(_MXU)
                k_end = (kh * jnp.exp((ck - steps) * lg)).astype(_MXU)
                sc = _dot_nt(qh.astype(_MXU), kh.astype(_MXU)) * dmat
                dec_h = [math.exp(ck * lg)] * n_chunks
            o_intra = _dot(sc.astype(_MXU), vh)
            st = st_ref[h]
            parts = [None] * n_chunks
            for c in order:
                rs = slice(c * ck, (c + 1) * ck)
                parts[c] = o_intra[rs] + _dot_nt(q_dec[rs], st.astype(_MXU))
                st = st * dec_h[c] + _dot_tn(vh[rs], k_end[rs])
            st_ref[h] = st
            outs.append(parts[0] if n_chunks == 1 else jnp.concatenate(parts, axis=0))
        return jnp.concatenate(outs, axis=1)

    @pl.when(ph == 0)
    def _():
        base = pl.multiple_of(s * tt, tt)
        of_ref[pl.ds(base, tt), :] = tile_outputs(False)

    @pl.when(ph == 1)
    def _():
        loc = jnp.where(s < n_ct, n_ct - 1 - s, n_ct + n_lt - 1 - (s - n_ct))
        base = pl.multiple_of(loc * tt, tt)
        o = of_ref[pl.ds(base, tt), :] + tile_outputs(True)
        gate = gate_ref[...].astype(F32)
        parts = []
        for h in range(HEADS):
            oh = o[:, h * DV:(h + 1) * DV]
            if kind == "ret":
                oh = oh - jnp.mean(oh, axis=-1, keepdims=True)
            parts.append(oh * lax.rsqrt(jnp.mean(oh * oh, axis=-1, keepdims=True) + EPS))
        normed = jnp.concatenate(parts, axis=1) * gain_ref[...]
        out_ref[...] = (normed * _silu(gate)).astype(out_ref.dtype)


def _la_call(kind, p, extra, gain, nb, nl, nc):
    t = p.shape[0]
    tt = _pick(math.gcd(nl, nc), (256, 128, 64))
    n_ct, n_lt = nc // tt, nl // tt
    lat_tiles = nb * n_lt
    cq, ck, cv, cg = ((C_GLA_Q, C_GLA_K, C_GLA_V, C_GLA_G) if kind == "gla"
                      else (C_RET_Q, C_RET_K, C_RET_V, C_RET_G))
    hd = HEADS * DK
    hv = HEADS * DV

    def loc_of(ph, s):
        back = jnp.where(s < n_ct, n_ct - 1 - s, n_ct + n_lt - 1 - (s - n_ct))
        return jnp.where(ph == 0, s, back)

    def row_blk(b, loc):
        return jnp.where(loc < n_ct, lat_tiles + b * n_ct + loc, b * n_lt + (loc - n_ct))

    def in_map(cblk):
        return lambda b, ph, s: (row_blk(b, loc_of(ph, s)), cblk)

    def second_pass_map(cblk):
        return lambda b, ph, s: (row_blk(b, loc_of(1, jnp.where(ph == 0, 0, s))), cblk)

    in_specs = [pl.BlockSpec((tt, hd), in_map(cq // hd)),
                pl.BlockSpec((tt, hd), in_map(ck // hd)),
                pl.BlockSpec((tt, hv), in_map(cv // hv)),
                pl.BlockSpec((tt, hv), second_pass_map(cg // hv))]
    args = [p, p, p, p]
    if kind == "gla":
        wa2p, ba = extra
        in_specs += [pl.BlockSpec((tt, LANES), in_map(C_LR // LANES)),
                     pl.BlockSpec((2, LANES, hd), lambda b, ph, s: (0, 0, 0)),
                     pl.BlockSpec((2, 1, hd), lambda b, ph, s: (0, 0, 0))]
        args += [p, wa2p, ba]
    else:
        cos_t, sin_t = extra

        def rope_map(b, ph, s):
            loc = loc_of(ph, s)
            return (jnp.where(loc < n_ct, n_lt + loc, loc - n_ct), 0)
        in_specs += [pl.BlockSpec((tt, DK), rope_map), pl.BlockSpec((tt, DK), rope_map)]
        args += [cos_t, sin_t]
    in_specs.append(pl.BlockSpec((1, hv), lambda b, ph, s: (0, 0)))
    args.append(gain)
    return pl.pallas_call(
        functools.partial(_la_kernel, kind=kind, tt=tt, n_ct=n_ct, n_lt=n_lt),
        grid=(nb, 2, n_ct + n_lt),
        in_specs=in_specs,
        out_specs=pl.BlockSpec((tt, hv), second_pass_map(0)),
        out_shape=jax.ShapeDtypeStruct((t, hv), _MXU),
        scratch_shapes=[pltpu.VMEM((HEADS, DV, DK), F32),
                        pltpu.VMEM((nc + nl, hv), F32)],
        compiler_params=_cparams(("arbitrary", "arbitrary", "arbitrary"), 48),
        name=kind,
    )(*args)


def _scan_group(a, b, h, reverse):
    row = lax.broadcasted_iota(jnp.int32, a.shape, 0)
    for sft in (1, 2, 4):
        if reverse:
            a_sh = pltpu.roll(a, SUBLANES - sft, 0)
            b_sh = pltpu.roll(b, SUBLANES - sft, 0)
            m = row < SUBLANES - sft
        else:
            a_sh = pltpu.roll(a, sft, 0)
            b_sh = pltpu.roll(b, sft, 0)
            m = row >= sft
        b = jnp.where(m, a * b_sh + b, b)
        a = jnp.where(m, a * a_sh, a)
    hh = a * h + b
    return hh, (hh[0:1, :] if reverse else hh[SUBLANES - 1:SUBLANES, :])


def _lru_kernel(x_ref, gel_ref, cw_ref, cb_ref, wa_ref, ba_ref, wi_ref, bi_ref, lam_ref, h0_ref,
                out_ref, hn_ref, xpad_ref, xc_ref, hf_ref, *, seg, tile):
    n_tiles = seg // tile
    groups = tile // SUBLANES
    cw = cw_ref[...]
    cb = cb_ref[...]
    zeros8 = jnp.zeros((SUBLANES, xpad_ref.shape[1]), F32)
    xpad_ref[0:SUBLANES, :] = zeros8
    xpad_ref[SUBLANES + seg:2 * SUBLANES + seg, :] = zeros8

    def copy_in(i, carry):
        t0 = pl.multiple_of(i * tile, tile)
        xpad_ref[pl.ds(t0 + SUBLANES, tile), :] = x_ref[pl.ds(t0, tile), :].astype(F32)
        return carry
    lax.fori_loop(0, n_tiles, copy_in, 0)

    def gates(xc, d):
        xb = xc.astype(_MXU)
        r = _sigmoid(_dot(xb, wa_ref[d]) + ba_ref[d:d + 1, :])
        ig = _sigmoid(_dot(xb, wi_ref[d]) + bi_ref[d:d + 1, :])
        log_a = LRU_C * r * _log_sigmoid(lam_ref[d:d + 1, :])
        a = jnp.exp(log_a)
        b = jnp.sqrt(-jnp.tanh(log_a) * (a * a + 1.0)) * (ig * xc)
        return a, b

    def fwd_tile(i, h):
        t0 = pl.multiple_of(i * tile, tile)
        win = xpad_ref[pl.ds(t0, tile + 2 * SUBLANES), :]
        xc = (cw[0:1, :] * win[6:6 + tile] + cw[1:2, :] * win[7:7 + tile]
              + cw[2:3, :] * win[8:8 + tile] + cw[3:4, :] * win[9:9 + tile] + cb)
        xc_ref[pl.ds(t0, tile), :] = xc
        a, b = gates(xc, 0)
        for g in range(groups):
            rs = slice(g * SUBLANES, (g + 1) * SUBLANES)
            hh, h = _scan_group(a[rs], b[rs], h, False)
            hf_ref[pl.ds(t0 + g * SUBLANES, SUBLANES), :] = hh
        return h
    h_f = lax.fori_loop(0, n_tiles, fwd_tile, h0_ref[0:1, :])

    def bwd_tile(j, h):
        i = n_tiles - 1 - j
        t0 = pl.multiple_of(i * tile, tile)
        xc = xc_ref[pl.ds(t0, tile), :]
        a, b = gates(xc, 1)
        for g in reversed(range(groups)):
            rs = slice(g * SUBLANES, (g + 1) * SUBLANES)
            hh, h = _scan_group(a[rs], b[rs], h, True)
            rows = pl.ds(t0 + g * SUBLANES, SUBLANES)
            hsum = hf_ref[rows, :] + hh
            hf_ref[rows, :] = hsum
        gel = gel_ref[pl.ds(t0, tile), :].astype(F32)
        out_ref[pl.ds(t0, tile), :] = (hf_ref[pl.ds(t0, tile), :] * _gelu_tanh(gel)).astype(out_ref.dtype)
        return h
    h_b = lax.fori_loop(0, n_tiles, bwd_tile, h0_ref[1:2, :])
    hn_ref[0:1, :] = h_f
    hn_ref[1:2, :] = h_b


def _lru_call(p, prev_out, h0, lw, nb, seg, row_blk0):
    cw, cb, wa, ba, wi, bi, lam = lw
    t = p.shape[0]
    width = LRU_BLOCKS * DV
    cbw = width // LRU_BLOCKS
    tile = _pick(seg, (128, 64))
    xblk = C_LRU_X // cbw
    gblk = C_LRU_G // cbw
    in_specs = [pl.BlockSpec((seg, cbw), lambda b, c: (row_blk0 + b, xblk + c)),
                pl.BlockSpec((seg, cbw), lambda b, c: (row_blk0 + b, gblk + c)),
                pl.BlockSpec((4, cbw), lambda b, c: (0, c)),
                pl.BlockSpec((1, cbw), lambda b, c: (0, c)),
                pl.BlockSpec((2, None, cbw, cbw), lambda b, c: (0, c, 0, 0)),
                pl.BlockSpec((2, cbw), lambda b, c: (0, c)),
                pl.BlockSpec((2, None, cbw, cbw), lambda b, c: (0, c, 0, 0)),
                pl.BlockSpec((2, cbw), lambda b, c: (0, c)),
                pl.BlockSpec((2, cbw), lambda b, c: (0, c)),
                pl.BlockSpec((None, 2, cbw), lambda b, c: (b, 0, c))]
    in_specs.append(pl.BlockSpec(memory_space=pl.ANY))
    args = [p, p, cw, cb, wa, ba, wi, bi, lam, h0, prev_out]

    def body(*refs):
        _lru_kernel(*(refs[:10] + refs[11:]), seg=seg, tile=tile)

    return pl.pallas_call(
        body,
        grid=(nb, LRU_BLOCKS),
        in_specs=in_specs,
        out_specs=[pl.BlockSpec((seg, cbw), lambda b, c: (row_blk0 + b, c)),
                   pl.BlockSpec((None, 2, cbw), lambda b, c: (b, 0, c))],
        out_shape=[jax.ShapeDtypeStruct((t, width), _MXU),
                   jax.ShapeDtypeStruct((nb, 2, width), F32)],
        scratch_shapes=[pltpu.VMEM((seg + 2 * SUBLANES, cbw), F32),
                        pltpu.VMEM((seg, cbw), F32),
                        pltpu.VMEM((seg, cbw), F32)],
        input_output_aliases={10: 0},
        compiler_params=_cparams(("arbitrary", "arbitrary"), 48),
        name="lru",
    )(*args)


def _finish_kernel(gla_ref, lru_ref, ret_ref, m0_ref, m1_ref, m2_ref, wb_ref, wo_ref, xs_ref,
                   mod_ref, g2_ref, wr_ref, br_ref, tril_ref,
                   xo_ref, xn_ref, idx_ref, wgt_ref, rank_ref, cnt_ref, carry_ref):
    @pl.when(pl.program_id(0) == 0)
    def _():
        carry_ref[...] = jnp.zeros_like(carry_ref)

    merged = (_sigmoid(m0_ref[...].astype(F32)) * _dot(gla_ref[...], wb_ref[0])
              + _sigmoid(m1_ref[...].astype(F32)) * _dot(lru_ref[...], wb_ref[1])
              + _sigmoid(m2_ref[...].astype(F32)) * _dot(ret_ref[...], wb_ref[2]))
    x = xs_ref[...] + mod_ref[2:3, :] * _dot(merged.astype(_MXU), wo_ref[...])
    xo_ref[...] = x
    xn = _rms(x, g2_ref[...]) * (1.0 + mod_ref[4:5, :]) + mod_ref[3:4, :]
    xn_ref[...] = xn

    logits = _dot(xn.astype(_MXU), wr_ref[...]) + br_ref[...]
    lane = lax.broadcasted_iota(jnp.int32, logits.shape, 1)
    ids, vals = [], []
    for _ in range(TOP_K):
        m = jnp.max(logits, axis=1, keepdims=True)
        sel = jnp.min(jnp.where(logits == m, lane.astype(F32), float(LANES)), axis=1,
                      keepdims=True).astype(jnp.int32)
        ids.append(sel)
        vals.append(m)
        logits = jnp.where(lane == sel, -jnp.inf, logits)
    ex = [jnp.exp(v - vals[0]) for v in vals]
    denom = ex[0] + ex[1] + ex[2] + ex[3]
    onehot = jnp.zeros(logits.shape, F32)
    for sel in ids:
        onehot = onehot + (lane == sel).astype(F32)
    before = _dot(tril_ref[...], onehot.astype(_MXU)) + carry_ref[0:1, :]
    idx_o = jnp.zeros(logits.shape, jnp.int32)
    rank_o = jnp.zeros(logits.shape, jnp.int32)
    wgt_o = jnp.zeros(logits.shape, F32)
    for k in range(TOP_K):
        rk = jnp.sum(jnp.where(lane == ids[k], before, 0.0), axis=1, keepdims=True)
        idx_o = jnp.where(lane == k, ids[k], idx_o)
        rank_o = jnp.where(lane == k, rk.astype(jnp.int32), rank_o)
        wgt_o = jnp.where(lane == k, ex[k] / denom, wgt_o)
    idx_ref[...] = idx_o
    rank_ref[...] = rank_o
    wgt_ref[...] = wgt_o
    total = carry_ref[0:1, :] + jnp.sum(onehot, axis=0, keepdims=True)
    carry_ref[...] = jnp.broadcast_to(total, carry_ref.shape)
    cnt_ref[...] = jnp.broadcast_to(total, cnt_ref.shape)


def _finish_call(gla, lru, ret, p, wb, wo, xs, mod, g2, wr, br, n_rows, nl, nb):
    d = xs.shape[1]
    tm = _pick(math.gcd(n_rows, nl), (512, 256))
    tril = jnp.asarray(np.tril(np.ones((tm, tm), np.float32), -1), _MXU)
    row = lambda i: (i, 0)
    const2 = lambda i: (0, 0)
    mblk = C_MERGE // d
    in_specs = [pl.BlockSpec((tm, d), row), pl.BlockSpec((tm, d), row), pl.BlockSpec((tm, d), row),
                pl.BlockSpec((tm, d), lambda i: (i, mblk)),
                pl.BlockSpec((tm, d), lambda i: (i, mblk + 1)),
                pl.BlockSpec((tm, d), lambda i: (i, mblk + 2)),
                pl.BlockSpec((3, d, d), lambda i: (0, 0, 0)),
                pl.BlockSpec((d, d), const2),
                pl.BlockSpec((tm, d), row),
                pl.BlockSpec((None, N_MOD, d), lambda i: (jnp.minimum(i * tm // nl, nb), 0, 0)),
                pl.BlockSpec((1, d), const2),
                pl.BlockSpec((d, LANES), const2),
                pl.BlockSpec((1, LANES), const2),
                pl.BlockSpec((tm, tm), const2)]
    out_specs = [pl.BlockSpec((tm, d), row), pl.BlockSpec((tm, d), row),
                 pl.BlockSpec((tm, LANES), row), pl.BlockSpec((tm, LANES), row),
                 pl.BlockSpec((tm, LANES), row), pl.BlockSpec((SUBLANES, LANES), const2)]
    out_shape = [jax.ShapeDtypeStruct((n_rows, d), F32), jax.ShapeDtypeStruct((n_rows, d), F32),
                 jax.ShapeDtypeStruct((n_rows, LANES), jnp.int32),
                 jax.ShapeDtypeStruct((n_rows, LANES), F32),
                 jax.ShapeDtypeStruct((n_rows, LANES), jnp.int32),
                 jax.ShapeDtypeStruct((SUBLANES, LANES), F32)]
    return pl.pallas_call(
        _finish_kernel,
        grid=(n_rows // tm,),
        in_specs=in_specs,
        out_specs=out_specs,
        out_shape=out_shape,
        scratch_shapes=[pltpu.VMEM((SUBLANES, LANES), F32)],
        compiler_params=_cparams(("arbitrary",), 56),
        name="finish",
    )(gla, lru, ret, p, p, p, wb, wo, xs, mod, g2, wr, br, tril)


def _slot_kernel(idx_ref, rank_ref, start_ref, slot_ref):
    idx = idx_ref[...].astype(F32)
    lane = lax.broadcasted_iota(jnp.int32, idx.shape, 1)
    start = start_ref[...]
    out = jnp.zeros(idx.shape, F32)
    for k in range(TOP_K):
        sel = jnp.sum(jnp.where(lane == k, idx, 0.0), axis=1, keepdims=True).astype(jnp.int32)
        st = jnp.sum(jnp.where(lane == sel, start, 0.0), axis=1, keepdims=True)
        out = jnp.where(lane == k, st, out)
    slot_ref[...] = out.astype(jnp.int32) + rank_ref[...]


def _slot_call(idx, rank, pad_start):
    n = idx.shape[0]
    tm = _pick(n, (2048, 1024, 512, 256))
    row = lambda i: (i, 0)
    return pl.pallas_call(
        _slot_kernel,
        grid=(n // tm,),
        in_specs=[pl.BlockSpec((tm, LANES), row), pl.BlockSpec((tm, LANES), row),
                  pl.BlockSpec((1, LANES), lambda i: (0, 0))],
        out_specs=pl.BlockSpec((tm, LANES), row),
        out_shape=jax.ShapeDtypeStruct((n, LANES), jnp.int32),
        compiler_params=_cparams(("arbitrary",), 24),
        name="slots",
    )(idx, rank, pad_start)


def _dispatch_kernel(slot_ref, x_ref, dst_in_ref, dst_ref, sem, *, tm):
    del dst_in_ref

    def row_copy(t, k):
        return pltpu.make_async_copy(x_ref.at[pl.ds(t, 1)],
                                     dst_ref.at[pl.ds(slot_ref[0, t * TOP_K + k], 1)], sem)

    def issue(t, carry):
        for k in range(TOP_K):
            row_copy(t, k).start(priority=k % 2)
        return carry
    lax.fori_loop(0, tm, issue, 0, unroll=8)
    for k in range(TOP_K):
        pltpu.make_async_copy(x_ref, dst_ref.at[pl.ds(0, tm)], sem).wait()


def _dispatch_call(slots3, xn, zeros_sorted):
    n, d = xn.shape
    tm = slots3.shape[2] // TOP_K
    return pl.pallas_call(
        functools.partial(_dispatch_kernel, tm=tm),
        grid=(n // tm,),
        in_specs=[pl.BlockSpec((None, 1, tm * TOP_K), lambda i: (i, 0, 0), memory_space=pltpu.SMEM),
                  pl.BlockSpec((tm, d), lambda i: (i, 0)),
                  pl.BlockSpec(memory_space=pl.ANY)],
        out_specs=pl.BlockSpec(memory_space=pl.ANY),
        out_shape=jax.ShapeDtypeStruct(zeros_sorted.shape, zeros_sorted.dtype),
        scratch_shapes=[pltpu.SemaphoreType.DMA(())],
        input_output_aliases={2: 0},
        compiler_params=_cparams(("arbitrary",), 24),
        name="dispatch",
    )(slots3, xn, zeros_sorted)


def _experts_kernel(be_ref, nu_ref, x_ref, wgu_ref, bgu_ref, wd_ref, bd_ref, perm_ref, y_ref,
                    wgu_s, wd_s):
    j = pl.program_id(0)
    f = wd_ref.shape[0]
    grp = perm_ref.shape[0]
    half = grp // 2
    active = j < nu_ref[0]
    changed = jnp.logical_or(j == 0, be_ref[j] != be_ref[jnp.maximum(j - 1, 0)])

    @pl.when(jnp.logical_and(active, changed))
    def _():
        for g in range(2 * f // grp):
            cols = slice(g * grp, (g + 1) * grp)
            wgu_s[:, cols] = _dot(wgu_ref[:, cols].astype(_MXU), perm_ref[...]).astype(_MXU)
        wd_s[...] = wd_ref[...].astype(_MXU)

    @pl.when(active)
    def _():
        gu = _dot(x_ref[...].astype(_MXU), wgu_s[...]) + bgu_ref[...]
        acts = []
        for g in range(2 * f // grp):
            gate = jnp.minimum(gu[:, g * grp:g * grp + half], SWIGLU_LIMIT)
            up = jnp.clip(gu[:, g * grp + half:(g + 1) * grp], -SWIGLU_LIMIT, SWIGLU_LIMIT)
            acts.append((gate * _sigmoid(SWIGLU_ALPHA * gate) * (up + 1.0)).astype(_MXU))
        y_ref[...] = _dot(jnp.concatenate(acts, axis=1), wd_s[...]) + bd_ref[...]

    @pl.when(jnp.logical_not(active))
    def _():
        y_ref[...] = jnp.zeros_like(y_ref)


def _experts_call(li, block_e, n_used, x_sorted, w_gu, bgu_r, w_down, bd, bm):
    n_slots, d = x_sorted.shape
    f = w_down.shape[2]
    grp = 2 * LANES
    src = np.concatenate([np.arange(0, grp, 2), np.arange(1, grp, 2)])
    perm_np = np.zeros((grp, grp), np.float32)
    perm_np[src, np.arange(grp)] = 1.0
    perm = jnp.asarray(perm_np, _MXU)

    def xmap(j, be, nu):
        return (jnp.minimum(j, nu[0] - 1), 0)

    def wmap(j, be, nu):
        return (li, be[j], 0, 0)

    grid_spec = pltpu.PrefetchScalarGridSpec(
        num_scalar_prefetch=2,
        grid=(n_slots // bm,),
        in_specs=[pl.BlockSpec((bm, d), xmap),
                  pl.BlockSpec((None, None, d, 2 * f), wmap),
                  pl.BlockSpec((None, None, 1, 2 * f), wmap),
                  pl.BlockSpec((None, None, f, d), wmap),
                  pl.BlockSpec((None, None, 1, d), wmap),
                  pl.BlockSpec((grp, grp), lambda j, be, nu: (0, 0))],
        out_specs=pl.BlockSpec((bm, d), lambda j, be, nu: (j, 0)),
        scratch_shapes=[pltpu.VMEM((d, 2 * f), _MXU), pltpu.VMEM((f, d), _MXU)])
    return pl.pallas_call(
        _experts_kernel,
        grid_spec=grid_spec,
        out_shape=jax.ShapeDtypeStruct((n_slots, d), F32),
        compiler_params=_cparams(("arbitrary",), 56),
        name="experts",
    )(block_e, n_used, x_sorted, w_gu, bgu_r, w_down, bd, perm)


def _combine_kernel(slot_ref, wgt_ref, xs_ref, mod_ref, fin_ref, y_ref, out_ref, buf_ref, sem,
                    *, tm, final):
    def row_copy(t, k):
        return pltpu.make_async_copy(y_ref.at[pl.ds(slot_ref[0, t * TOP_K + k], 1)],
                                     buf_ref.at[k, pl.ds(t, 1)], sem)

    def issue(t, carry):
        for k in range(TOP_K):
            row_copy(t, k).start(priority=k % 2)
        return carry
    lax.fori_loop(0, tm, issue, 0, unroll=8)
    for k in range(TOP_K):
        pltpu.make_async_copy(y_ref.at[pl.ds(0, tm)], buf_ref.at[k], sem).wait()

    wgt = wgt_ref[...]
    acc = wgt[:, 0:1] * buf_ref[0]
    for k in range(1, TOP_K):
        acc = acc + wgt[:, k:k + 1] * buf_ref[k]
    x = xs_ref[...] + mod_ref[5:6, :] * acc
    out_ref[...] = _rms(x, fin_ref[...]) if final else x


def _combine_call(slots3, wgt, xs, mod, fin, y_sorted, nl, nb, final):
    n, d = xs.shape
    tm = slots3.shape[2] // TOP_K
    return pl.pallas_call(
        functools.partial(_combine_kernel, tm=tm, final=final),
        grid=(n // tm,),
        in_specs=[pl.BlockSpec((None, 1, tm * TOP_K), lambda i: (i, 0, 0), memory_space=pltpu.SMEM),
                  pl.BlockSpec((tm, LANES), lambda i: (i, 0)),
                  pl.BlockSpec((tm, d), lambda i: (i, 0)),
                  pl.BlockSpec((None, N_MOD, d), lambda i: (jnp.minimum(i * tm // nl, nb), 0, 0)),
                  pl.BlockSpec((1, d), lambda i: (0, 0)),
                  pl.BlockSpec(memory_space=pl.ANY)],
        out_specs=pl.BlockSpec((tm, d), lambda i: (i, 0)),
        out_shape=jax.ShapeDtypeStruct((n, d), F32),
        scratch_shapes=[pltpu.VMEM((TOP_K, tm, d), F32), pltpu.SemaphoreType.DMA(())],
        compiler_params=_cparams(("arbitrary",), 32),
        name="combine",
    )(slots3, wgt, xs, mod, fin, y_sorted)


def _moe(li, xn, idx, wgt, rank, counts, xs_mid, mod, fin, w_gu, bgu_r, w_down, bd, nl, nb, final):
    n, d = xn.shape
    bm = 512
    tm = _pick(math.gcd(n, nl), (256, 128))
    n_blocks = -(-(n * TOP_K) // bm) + N_EXPERTS
    cnt = counts[0, :N_EXPERTS].astype(jnp.int32)
    padded = (cnt + bm - 1) // bm * bm
    pad_end = jnp.cumsum(padded)
    pad_start = pad_end - padded
    n_used = (pad_end[-1] // bm).astype(jnp.int32)
    blk = jnp.arange(n_blocks, dtype=jnp.int32)
    first_row = jnp.minimum(blk, n_used - 1) * bm
    block_e = jnp.minimum(jnp.sum(pad_end[None, :] <= first_row[:, None], axis=1),
                          N_EXPERTS - 1).astype(jnp.int32)
    start_row = jnp.zeros((1, LANES), F32).at[0, :N_EXPERTS].set(pad_start.astype(F32))
    slots = _slot_call(idx, rank, start_row)
    slots3 = slots[:, :TOP_K].reshape(n // tm, 1, tm * TOP_K)
    x_sorted = _dispatch_call(slots3, xn, jnp.zeros((n_blocks * bm, d), F32))
    y_sorted = _experts_call(li, block_e, n_used.reshape(1), x_sorted, w_gu, bgu_r, w_down, bd, bm)
    return _combine_call(slots3, wgt, xs_mid, mod, fin, y_sorted, nl, nb, final)


def _rope_tables(nl, nc):
    rows = nl // GRID_W
    r = np.broadcast_to(np.arange(rows)[:, None], (rows, GRID_W)).reshape(-1).astype(np.float32)
    c = np.broadcast_to(np.arange(GRID_W)[None, :], (rows, GRID_W)).reshape(-1).astype(np.float32)
    n_freq = DK // 4
    inv = (ROPE_BASE ** (-jnp.arange(n_freq, dtype=F32) / n_freq))
    ang = jnp.concatenate([jnp.asarray(r)[:, None] * inv, jnp.asarray(c)[:, None] * inv], axis=-1)
    cos, sin = jnp.cos(ang), jnp.sin(ang)
    cos2 = jnp.concatenate([cos, cos], axis=-1)
    sin2 = jnp.concatenate([-sin, sin], axis=-1)
    cos_t = jnp.concatenate([cos2, jnp.ones((nc, DK), F32)], axis=0)
    sin_t = jnp.concatenate([sin2, jnp.zeros((nc, DK), F32)], axis=0)
    return cos_t, sin_t


def kernel(x, c, ctx, c_ctx, w_ada, b_ada, norm1, norm2, w_in, gla_wa2, gla_ba, gla_norm, lru_conv_w, lru_conv_b, lru_wa, lru_ba, lru_wi, lru_bi, lru_lam, ret_norm, w_branch, w_out, w_router, b_router, w_gu, b_gu, w_down, b_down, final_norm):
    nb, nl, d = x.shape
    nc = ctx.shape[1]
    depth = w_ada.shape[0]
    n_lat = nb * nl
    f = w_down.shape[2]

    xs = jnp.concatenate([x.reshape(n_lat, d), ctx.reshape(nb * nc, d)], axis=0)
    mod_rows = -(-(nb + 1) // SUBLANES) * SUBLANES
    cc = jnp.zeros((mod_rows, d), F32).at[:nb].set(c).at[nb].set(c_ctx)
    mod_all = _ada_call(cc, w_ada.astype(_MXU), b_ada).reshape(depth, mod_rows, N_MOD, d)
    cos_t, sin_t = _rope_tables(nl, nc)
    bgu_r = b_gu.reshape(depth, N_EXPERTS, 2 * f // (2 * LANES), LANES, 2)
    bgu_r = jnp.swapaxes(bgu_r, -1, -2).reshape(depth, N_EXPERTS, 1, 2 * f)
    bd_r = b_down.reshape(depth, N_EXPERTS, 1, d)

    out = None
    for li in range(depth):
        last = li == depth - 1
        mod = mod_all[li]
        w = w_in[li]
        w_in_r = jnp.concatenate(
            [w[:, :3072], w[:, 3104:], w[:, 3072:3104],
             jnp.zeros((d, N_PROJ - w.shape[1]), w.dtype)], axis=1).astype(_MXU)
        p = _proj_call(xs, mod, norm1[li].reshape(1, d), w_in_r, n_lat, nl, nb)

        wa2p = jnp.zeros((2, LANES, HEADS * DK), F32)
        wa2p = wa2p.at[0, :GLA_RANK].set(gla_wa2[li, 0]).at[1, GLA_RANK:2 * GLA_RANK].set(gla_wa2[li, 1])
        gla = _la_call("gla", p, (wa2p.astype(_MXU), gla_ba[li].reshape(2, 1, HEADS * DK)),
                       gla_norm[li].reshape(1, HEADS * DV), nb, nl, nc)
        ret = _la_call("ret", p, (cos_t, sin_t), ret_norm[li].reshape(1, HEADS * DV), nb, nl, nc)

        lw = (lru_conv_w[li], lru_conv_b[li].reshape(1, -1), lru_wa[li].astype(_MXU), lru_ba[li],
              lru_wi[li].astype(_MXU), lru_bi[li], lru_lam[li])
        width = lru_conv_w.shape[2]
        lru_c, h_ctx = _lru_call(p, jnp.zeros((xs.shape[0], width), _MXU),
                                 jnp.zeros((nb, 2, width), F32), lw, nb, nc, n_lat // nc)
        lru, _ = _lru_call(p, lru_c, h_ctx, lw, nb, nl, 0)

        n_rows = n_lat if last else xs.shape[0]
        wr = jnp.zeros((d, LANES), F32).at[:, :N_EXPERTS].set(w_router[li]).astype(_MXU)
        br = jnp.full((1, LANES), NEG_BIG, F32).at[0, :N_EXPERTS].set(b_router[li])
        xs_mid, xn2, idx, wgt, rank, counts = _finish_call(
            gla, lru, ret, p, w_branch[li].astype(_MXU), w_out[li].astype(_MXU), xs, mod,
            norm2[li].reshape(1, d), wr, br, n_rows, nl, nb)

        out = _moe(li, xn2, idx, wgt, rank, counts, xs_mid, mod, final_norm.reshape(1, d), w_gu,
                   bgu_r, w_down, bd_r, nl, nb, last)
        xs = out
    return out.reshape(nb, nl, d)
```

```python
import functools
import math

import numpy as np
import jax
import jax.numpy as jnp
from jax import lax
from jax.experimental import pallas as pl
from jax.experimental.pallas import tpu as pltpu

F32 = jnp.float32
_MXU = jnp.bfloat16

EPS = 1e-6
N_MOD = 6
GRID_W = 64
CHUNK = 64
HEADS = 4
DK = 128
DV = 256
GLA_RANK = 16
GLA_GATE_NORM = 16.0
LRU_BLOCKS = 4
LRU_C = 8.0
ROPE_BASE = 10000.0
N_EXPERTS = 32
TOP_K = 4
SWIGLU_LIMIT = 7.0
SWIGLU_ALPHA = 1.702
LANES = 128
SUBLANES = 8
NEG_BIG = -1e30

C_GLA_Q, C_GLA_K, C_GLA_V, C_GLA_G = 0, 512, 1024, 2048
C_LRU_X, C_LRU_G = 3072, 4096
C_RET_Q, C_RET_K, C_RET_V, C_RET_G = 5120, 5632, 6144, 7168
C_MERGE = 8192
C_LR = 11264
N_PROJ = 11520
PROJ_TN = 1280


def _pick(n, prefs):
    for p in prefs:
        if n % p == 0:
            return p
    raise ValueError(f"no tile for {n} in {prefs}")


def _cparams(sem, vmem_mb):
    return pltpu.CompilerParams(dimension_semantics=sem, vmem_limit_bytes=vmem_mb * 1024 * 1024)


def _dot(a, b):
    return jnp.dot(a, b, preferred_element_type=F32)


def _dot_nt(a, b):
    return lax.dot_general(a, b, (((1,), (1,)), ((), ())), preferred_element_type=F32)


def _dot_tn(a, b):
    return lax.dot_general(a, b, (((0,), (0,)), ((), ())), preferred_element_type=F32)


def _sigmoid(x):
    return 0.5 * jnp.tanh(0.5 * x) + 0.5


def _log_sigmoid(x):
    return jnp.minimum(x, 0.0) - jnp.log1p(jnp.exp(-jnp.abs(x)))


def _silu(x):
    return x * _sigmoid(x)


def _gelu_tanh(x):
    return 0.5 * x * (1.0 + jnp.tanh(math.sqrt(2.0 / math.pi) * (x + 0.044715 * (x * x * x))))


def _rms(x, gain):
    return x * lax.rsqrt(jnp.mean(x * x, axis=-1, keepdims=True) + EPS) * gain


def _dot01_exact(tri, g):
    hi = g.astype(_MXU)
    r1 = g - hi.astype(F32)
    mid = r1.astype(_MXU)
    lo = (r1 - mid.astype(F32)).astype(_MXU)
    return _dot(tri, hi) + _dot(tri, mid) + _dot(tri, lo)


def _ada_kernel(c_ref, w_ref, b_ref, o_ref):
    s = _silu(c_ref[...])
    o_ref[...] = _dot(s.astype(_MXU), w_ref[...]) + b_ref[...]


def _ada_call(cc, w_ada, b_ada):
    depth, d, n = w_ada.shape
    rows = cc.shape[0]
    tn = _pick(n, (1536, 1024, 512, 128))
    return pl.pallas_call(
        _ada_kernel,
        grid=(depth, n // tn),
        in_specs=[pl.BlockSpec((rows, d), lambda l, j: (0, 0)),
                  pl.BlockSpec((None, d, tn), lambda l, j: (l, 0, j)),
                  pl.BlockSpec((None, 1, tn), lambda l, j: (l, 0, j))],
        out_specs=pl.BlockSpec((None, rows, tn), lambda l, j: (l, 0, j)),
        out_shape=jax.ShapeDtypeStruct((depth, rows, n), F32),
        compiler_params=_cparams(("arbitrary", "arbitrary"), 24),
        name="ada",
    )(cc, w_ada, b_ada.reshape(depth, 1, n))


def _proj_kernel(x_ref, mod_ref, g_ref, w_ref, o_ref, xn_ref):
    @pl.when(pl.program_id(1) == 0)
    def _():
        y = _rms(x_ref[...], g_ref[...])
        xn_ref[...] = (y * (1.0 + mod_ref[1:2, :]) + mod_ref[0:1, :]).astype(xn_ref.dtype)

    o_ref[...] = _dot(xn_ref[...], w_ref[...]).astype(o_ref.dtype)


def _proj_call(xs, mod, gain, w_in_r, n_lat_rows, nl, nb):
    t, d = xs.shape
    tm = _pick(math.gcd(t, nl), (1024, 512, 256))
    return pl.pallas_call(
        _proj_kernel,
        grid=(t // tm, N_PROJ // PROJ_TN),
        in_specs=[pl.BlockSpec((tm, d), lambda i, j: (i, 0)),
                  pl.BlockSpec((None, N_MOD, d), lambda i, j: (jnp.minimum(i * tm // nl, nb), 0, 0)),
                  pl.BlockSpec((1, d), lambda i, j: (0, 0)),
                  pl.BlockSpec((d, PROJ_TN), lambda i, j: (0, j))],
        out_specs=pl.BlockSpec((tm, PROJ_TN), lambda i, j: (i, j)),
        out_shape=jax.ShapeDtypeStruct((t, N_PROJ), _MXU),
        scratch_shapes=[pltpu.VMEM((tm, d), _MXU)],
        compiler_params=_cparams(("arbitrary", "arbitrary"), 40),
        name="proj",
    )(xs, mod, gain, w_in_r)


def _la_kernel(*refs, kind, tt, n_ct, n_lt):
    if kind == "gla":
        (q_ref, k_ref, v_ref, gate_ref, lr_ref, wa2_ref, ba_ref, gain_ref,
         out_ref, st_ref, of_ref) = refs
    else:
        (q_ref, k_ref, v_ref, gate_ref, cos_ref, sin_ref, gain_ref,
         out_ref, st_ref, of_ref, dm_ref) = refs
    ph = pl.program_id(1)
    s = pl.program_id(2)
    ck = CHUNK if kind == "gla" else tt
    n_chunks = tt // ck

    @pl.when(s == 0)
    def _():
        st_ref[...] = jnp.zeros_like(st_ref)

    row = lax.broadcasted_iota(jnp.int32, (tt, tt), 0)
    col = lax.broadcasted_iota(jnp.int32, (tt, tt), 1)
    shift = ck.bit_length() - 1
    same_chunk = (row >> shift) == (col >> shift)
    tpos = lax.broadcasted_iota(jnp.int32, (tt, 1), 0).astype(F32)

    if kind == "ret":
        first = jnp.logical_and(pl.program_id(0) == 0, jnp.logical_and(ph == 0, s == 0))

        @pl.when(first)
        def _():
            for h in range(HEADS):
                lg = math.log(1.0 - 2.0 ** (-5.0 - h))
                dm_ref[h] = jnp.where(col <= row, jnp.exp((row - col).astype(F32) * lg), 0.0)
                dm_ref[HEADS + h] = jnp.where(col > row, jnp.exp((col - row).astype(F32) * lg), 0.0)

    def tile_outputs(backward):
        if backward:
            mask = jnp.logical_and(same_chunk, col > row)
            tri = jnp.logical_and(same_chunk, col >= row).astype(_MXU)
        else:
            mask = jnp.logical_and(same_chunk, col <= row)
            tri = jnp.logical_and(same_chunk, col <= row).astype(_MXU)
        order = list(reversed(range(n_chunks))) if backward else list(range(n_chunks))
        if kind == "gla":
            d = 1 if backward else 0
            z = _dot(lr_ref[...], wa2_ref[d]) + ba_ref[d]
            g = _log_sigmoid(z) * (1.0 / GLA_GATE_NORM)
            big_g = _dot01_exact(tri, g)
            lasts = [big_g[c * ck:c * ck + 1, :] if backward else big_g[(c + 1) * ck - 1:(c + 1) * ck, :]
                     for c in range(n_chunks)]
            g_last = jnp.concatenate([jnp.broadcast_to(r, (ck, r.shape[1])) for r in lasts], axis=0)
            e_pos = jnp.exp(big_g)
            e_neg = jnp.exp(-big_g)
            e_end = jnp.exp(g_last - big_g)
            decs = [jnp.exp(r) for r in lasts]
        outs = []
        for h in range(HEADS):
            ks = slice(h * DK, (h + 1) * DK)
            vs = slice(h * DV, (h + 1) * DV)
            qh = q_ref[:, ks].astype(F32)
            kh = k_ref[:, ks].astype(F32)
            vh = v_ref[:, vs]
            if kind == "gla":
                qh = qh * DK ** -0.5
                q_dec = (qh * e_pos[:, ks]).astype(_MXU)
                k_inv = (kh * e_neg[:, ks]).astype(_MXU)
                k_end = (kh * e_end[:, ks]).astype(_MXU)
                sc = jnp.where(mask, _dot_nt(q_dec, k_inv), 0.0)
                dec_h = [dcy[:, ks] for dcy in decs]
            else:
                kh = kh * DK ** -0.5
                cos = cos_ref[...]
                sin = sin_ref[...]
                qh = qh * cos + pltpu.roll(qh, DK // 2, 1) * sin
                kh = kh * cos + pltpu.roll(kh, DK // 2, 1) * sin
                lg = math.log(1.0 - 2.0 ** (-5.0 - h))
                steps = (ck - tpos) if backward else (tpos + 1.0)
                q_dec = (qh * jnp.exp(steps * lg)).astype(_MXU)
                k_end = (kh * jnp.exp((ck - steps) * lg)).astype(_MXU)
                sc = _dot_nt(qh.astype(_MXU), kh.astype(_MXU)) * dm_ref[(HEADS if backward else 0) + h]
                dec_h = [math.exp(ck * lg)] * n_chunks
            o_intra = _dot(sc.astype(_MXU), vh)
            st = st_ref[h]
            parts = [None] * n_chunks
            for c in order:
                rs = slice(c * ck, (c + 1) * ck)
                parts[c] = o_intra[rs] + _dot_nt(q_dec[rs], st.astype(_MXU))
                st = st * dec_h[c] + _dot_tn(vh[rs], k_end[rs])
            st_ref[h] = st
            outs.append(parts[0] if n_chunks == 1 else jnp.concatenate(parts, axis=0))
        return jnp.concatenate(outs, axis=1)

    @pl.when(ph == 0)
    def _():
        base = pl.multiple_of(s * tt, tt)
        of_ref[pl.ds(base, tt), :] = tile_outputs(False)

    @pl.when(ph == 1)
    def _():
        loc = jnp.where(s < n_ct, n_ct - 1 - s, n_ct + n_lt - 1 - (s - n_ct))
        base = pl.multiple_of(loc * tt, tt)
        o = of_ref[pl.ds(base, tt), :] + tile_outputs(True)
        gate = gate_ref[...].astype(F32)
        parts = []
        for h in range(HEADS):
            oh = o[:, h * DV:(h + 1) * DV]
            if kind == "ret":
                oh = oh - jnp.mean(oh, axis=-1, keepdims=True)
            parts.append(oh * lax.rsqrt(jnp.mean(oh * oh, axis=-1, keepdims=True) + EPS))
        normed = jnp.concatenate(parts, axis=1) * gain_ref[...]
        out_ref[...] = (normed * _silu(gate)).astype(out_ref.dtype)


def _la_call(kind, p, extra, gain, nb, nl, nc):
    t = p.shape[0]
    tt = _pick(math.gcd(nl, nc), (256, 128, 64))
    n_ct, n_lt = nc // tt, nl // tt
    lat_tiles = nb * n_lt
    cq, ck, cv, cg = ((C_GLA_Q, C_GLA_K, C_GLA_V, C_GLA_G) if kind == "gla"
                      else (C_RET_Q, C_RET_K, C_RET_V, C_RET_G))
    hd = HEADS * DK
    hv = HEADS * DV

    def loc_of(ph, s):
        back = jnp.where(s < n_ct, n_ct - 1 - s, n_ct + n_lt - 1 - (s - n_ct))
        return jnp.where(ph == 0, s, back)

    def row_blk(b, loc):
        return jnp.where(loc < n_ct, lat_tiles + b * n_ct + loc, b * n_lt + (loc - n_ct))

    def in_map(cblk):
        return lambda b, ph, s: (row_blk(b, loc_of(ph, s)), cblk)

    def second_pass_map(cblk):
        return lambda b, ph, s: (row_blk(b, loc_of(1, jnp.where(ph == 0, 0, s))), cblk)

    in_specs = [pl.BlockSpec((tt, hd), in_map(cq // hd)),
                pl.BlockSpec((tt, hd), in_map(ck // hd)),
                pl.BlockSpec((tt, hv), in_map(cv // hv)),
                pl.BlockSpec((tt, hv), second_pass_map(cg // hv))]
    args = [p, p, p, p]
    if kind == "gla":
        wa2p, ba = extra
        in_specs += [pl.BlockSpec((tt, LANES), in_map(C_LR // LANES)),
                     pl.BlockSpec((2, LANES, hd), lambda b, ph, s: (0, 0, 0)),
                     pl.BlockSpec((2, 1, hd), lambda b, ph, s: (0, 0, 0))]
        args += [p, wa2p, ba]
    else:
        cos_t, sin_t = extra

        def rope_map(b, ph, s):
            loc = loc_of(ph, s)
            return (jnp.where(loc < n_ct, n_lt + loc, loc - n_ct), 0)
        in_specs += [pl.BlockSpec((tt, DK), rope_map), pl.BlockSpec((tt, DK), rope_map)]
        args += [cos_t, sin_t]
    in_specs.append(pl.BlockSpec((1, hv), lambda b, ph, s: (0, 0)))
    args.append(gain)
    return pl.pallas_call(
        functools.partial(_la_kernel, kind=kind, tt=tt, n_ct=n_ct, n_lt=n_lt),
        grid=(nb, 2, n_ct + n_lt),
        in_specs=in_specs,
        out_specs=pl.BlockSpec((tt, hv), second_pass_map(0)),
        out_shape=jax.ShapeDtypeStruct((t, hv), _MXU),
        scratch_shapes=[pltpu.VMEM((HEADS, DV, DK), F32),
                        pltpu.VMEM((nc + nl, hv), F32)]
        + ([pltpu.VMEM((2 * HEADS, tt, tt), F32)] if kind == "ret" else []),
        compiler_params=_cparams(("arbitrary", "arbitrary", "arbitrary"), 48),
        name=kind,
    )(*args)


def _scan_group(a, b, h, reverse):
    row = lax.broadcasted_iota(jnp.int32, a.shape, 0)
    for sft in (1, 2, 4):
        if reverse:
            a_sh = pltpu.roll(a, SUBLANES - sft, 0)
            b_sh = pltpu.roll(b, SUBLANES - sft, 0)
            m = row < SUBLANES - sft
        else:
            a_sh = pltpu.roll(a, sft, 0)
            b_sh = pltpu.roll(b, sft, 0)
            m = row >= sft
        b = jnp.where(m, a * b_sh + b, b)
        a = jnp.where(m, a * a_sh, a)
    hh = a * h + b
    return hh, (hh[0:1, :] if reverse else hh[SUBLANES - 1:SUBLANES, :])


def _lru_kernel(x_ref, gel_ref, cw_ref, cb_ref, wa_ref, ba_ref, wi_ref, bi_ref, lam_ref, h0_ref,
                out_ref, hn_ref, xpad_ref, xc_ref, hf_ref, *, seg, tile):
    n_tiles = seg // tile
    groups = tile // SUBLANES
    cw = cw_ref[...]
    cb = cb_ref[...]
    zeros8 = jnp.zeros((SUBLANES, xpad_ref.shape[1]), F32)
    xpad_ref[0:SUBLANES, :] = zeros8
    xpad_ref[SUBLANES + seg:2 * SUBLANES + seg, :] = zeros8

    def copy_in(i, carry):
        t0 = pl.multiple_of(i * tile, tile)
        xpad_ref[pl.ds(t0 + SUBLANES, tile), :] = x_ref[pl.ds(t0, tile), :].astype(F32)
        return carry
    lax.fori_loop(0, n_tiles, copy_in, 0)

    half_c = [0.5 * LRU_C * _log_sigmoid(lam_ref[d:d + 1, :]) for d in range(2)]

    def gates(xc, d):
        xb = xc.astype(_MXU)
        tr = jnp.tanh(0.5 * (_dot(xb, wa_ref[d]) + ba_ref[d:d + 1, :]))
        ig = _sigmoid(_dot(xb, wi_ref[d]) + bi_ref[d:d + 1, :])
        log_a = half_c[d] * tr + half_c[d]
        a = jnp.exp(log_a)
        b = jnp.sqrt(-jnp.tanh(log_a) * (a * a + 1.0)) * (ig * xc)
        return a, b

    def fwd_tile(i, h):
        t0 = pl.multiple_of(i * tile, tile)
        win = xpad_ref[pl.ds(t0, tile + 2 * SUBLANES), :]
        xc = (cw[0:1, :] * win[6:6 + tile] + cw[1:2, :] * win[7:7 + tile]
              + cw[2:3, :] * win[8:8 + tile] + cw[3:4, :] * win[9:9 + tile] + cb)
        xc_ref[pl.ds(t0, tile), :] = xc
        a, b = gates(xc, 0)
        for g in range(groups):
            rs = slice(g * SUBLANES, (g + 1) * SUBLANES)
            hh, h = _scan_group(a[rs], b[rs], h, False)
            hf_ref[pl.ds(t0 + g * SUBLANES, SUBLANES), :] = hh
        return h
    h_f = lax.fori_loop(0, n_tiles, fwd_tile, h0_ref[0:1, :])

    def bwd_tile(j, h):
        i = n_tiles - 1 - j
        t0 = pl.multiple_of(i * tile, tile)
        xc = xc_ref[pl.ds(t0, tile), :]
        a, b = gates(xc, 1)
        for g in reversed(range(groups)):
            rs = slice(g * SUBLANES, (g + 1) * SUBLANES)
            hh, h = _scan_group(a[rs], b[rs], h, True)
            rows = pl.ds(t0 + g * SUBLANES, SUBLANES)
            hsum = hf_ref[rows, :] + hh
            hf_ref[rows, :] = hsum
        gel = gel_ref[pl.ds(t0, tile), :].astype(F32)
        out_ref[pl.ds(t0, tile), :] = (hf_ref[pl.ds(t0, tile), :] * _gelu_tanh(gel)).astype(out_ref.dtype)
        return h
    h_b = lax.fori_loop(0, n_tiles, bwd_tile, h0_ref[1:2, :])
    hn_ref[0:1, :] = h_f
    hn_ref[1:2, :] = h_b


def _lru_call(p, prev_out, h0, lw, nb, seg, row_blk0):
    cw, cb, wa, ba, wi, bi, lam = lw
    t = p.shape[0]
    width = LRU_BLOCKS * DV
    cbw = width // LRU_BLOCKS
    tile = _pick(seg, (128, 64))
    xblk = C_LRU_X // cbw
    gblk = C_LRU_G // cbw
    in_specs = [pl.BlockSpec((seg, cbw), lambda b, c: (row_blk0 + b, xblk + c)),
                pl.BlockSpec((seg, cbw), lambda b, c: (row_blk0 + b, gblk + c)),
                pl.BlockSpec((4, cbw), lambda b, c: (0, c)),
                pl.BlockSpec((1, cbw), lambda b, c: (0, c)),
                pl.BlockSpec((2, None, cbw, cbw), lambda b, c: (0, c, 0, 0)),
                pl.BlockSpec((2, cbw), lambda b, c: (0, c)),
                pl.BlockSpec((2, None, cbw, cbw), lambda b, c: (0, c, 0, 0)),
                pl.BlockSpec((2, cbw), lambda b, c: (0, c)),
                pl.BlockSpec((2, cbw), lambda b, c: (0, c)),
                pl.BlockSpec((None, 2, cbw), lambda b, c: (b, 0, c))]
    in_specs.append(pl.BlockSpec(memory_space=pl.ANY))
    args = [p, p, cw, cb, wa, ba, wi, bi, lam, h0, prev_out]

    def body(*refs):
        _lru_kernel(*(refs[:10] + refs[11:]), seg=seg, tile=tile)

    return pl.pallas_call(
        body,
        grid=(nb, LRU_BLOCKS),
        in_specs=in_specs,
        out_specs=[pl.BlockSpec((seg, cbw), lambda b, c: (row_blk0 + b, c)),
                   pl.BlockSpec((None, 2, cbw), lambda b, c: (b, 0, c))],
        out_shape=[jax.ShapeDtypeStruct((t, width), _MXU),
                   jax.ShapeDtypeStruct((nb, 2, width), F32)],
        scratch_shapes=[pltpu.VMEM((seg + 2 * SUBLANES, cbw), F32),
                        pltpu.VMEM((seg, cbw), F32),
                        pltpu.VMEM((seg, cbw), F32)],
        input_output_aliases={10: 0},
        compiler_params=_cparams(("arbitrary", "arbitrary"), 48),
        name="lru",
    )(*args)


def _finish_kernel(gla_ref, lru_ref, ret_ref, m0_ref, m1_ref, m2_ref, wb_ref, wo_ref, xs_ref,
                   mod_ref, g2_ref, wr_ref, br_ref, tril_ref,
                   xo_ref, xn_ref, idx_ref, wgt_ref, rank_ref, cnt_ref, carry_ref):
    @pl.when(pl.program_id(0) == 0)
    def _():
        carry_ref[...] = jnp.zeros_like(carry_ref)

    hm = tril_ref.shape[0]
    total = carry_ref[0:1, :]
    for sub in range(xs_ref.shape[0] // hm):
        rs = slice(sub * hm, (sub + 1) * hm)
        merged = (_sigmoid(m0_ref[rs, :].astype(F32)) * _dot(gla_ref[rs, :], wb_ref[0])
                  + _sigmoid(m1_ref[rs, :].astype(F32)) * _dot(lru_ref[rs, :], wb_ref[1])
                  + _sigmoid(m2_ref[rs, :].astype(F32)) * _dot(ret_ref[rs, :], wb_ref[2]))
        x = xs_ref[rs, :] + mod_ref[2:3, :] * _dot(merged.astype(_MXU), wo_ref[...])
        xo_ref[rs, :] = x
        xn = _rms(x, g2_ref[...]) * (1.0 + mod_ref[4:5, :]) + mod_ref[3:4, :]
        xn_ref[rs, :] = xn

        logits = _dot(xn.astype(_MXU), wr_ref[...]) + br_ref[...]
        lane = lax.broadcasted_iota(jnp.int32, logits.shape, 1)
        ids, vals = [], []
        for _ in range(TOP_K):
            m = jnp.max(logits, axis=1, keepdims=True)
            sel = jnp.min(jnp.where(logits == m, lane.astype(F32), float(LANES)), axis=1,
                          keepdims=True).astype(jnp.int32)
            ids.append(sel)
            vals.append(m)
            logits = jnp.where(lane == sel, -jnp.inf, logits)
        ex = [jnp.exp(v - vals[0]) for v in vals]
        denom = ex[0] + ex[1] + ex[2] + ex[3]
        onehot = jnp.zeros(logits.shape, F32)
        for sel in ids:
            onehot = onehot + (lane == sel).astype(F32)
        before = _dot(tril_ref[...], onehot.astype(_MXU)) + total
        idx_o = jnp.zeros(logits.shape, jnp.int32)
        rank_o = jnp.zeros(logits.shape, jnp.int32)
        wgt_o = jnp.zeros(logits.shape, F32)
        for k in range(TOP_K):
            rk = jnp.sum(jnp.where(lane == ids[k], before, 0.0), axis=1, keepdims=True)
            idx_o = jnp.where(lane == k, ids[k], idx_o)
            rank_o = jnp.where(lane == k, rk.astype(jnp.int32), rank_o)
            wgt_o = jnp.where(lane == k, ex[k] / denom, wgt_o)
        idx_ref[rs, :] = idx_o
        rank_ref[rs, :] = rank_o
        wgt_ref[rs, :] = wgt_o
        total = total + jnp.sum(onehot, axis=0, keepdims=True)
    carry_ref[...] = jnp.broadcast_to(total, carry_ref.shape)
    cnt_ref[...] = jnp.broadcast_to(total, cnt_ref.shape)


def _finish_call(gla, lru, ret, p, wb, wo, xs, mod, g2, wr, br, n_rows, nl, nb):
    d = xs.shape[1]
    tm = _pick(math.gcd(n_rows, nl), (512, 256))
    hm = tm
    tril = jnp.asarray(np.tril(np.ones((hm, hm), np.float32), -1), _MXU)
    row = lambda i: (i, 0)
    const2 = lambda i: (0, 0)
    mblk = C_MERGE // d
    in_specs = [pl.BlockSpec((tm, d), row), pl.BlockSpec((tm, d), row), pl.BlockSpec((tm, d), row),
                pl.BlockSpec((tm, d), lambda i: (i, mblk)),
                pl.BlockSpec((tm, d), lambda i: (i, mblk + 1)),
                pl.BlockSpec((tm, d), lambda i: (i, mblk + 2)),
                pl.BlockSpec((3, d, d), lambda i: (0, 0, 0)),
                pl.BlockSpec((d, d), const2),
                pl.BlockSpec((tm, d), row),
                pl.BlockSpec((None, N_MOD, d), lambda i: (jnp.minimum(i * tm // nl, nb), 0, 0)),
                pl.BlockSpec((1, d), const2),
                pl.BlockSpec((d, LANES), const2),
                pl.BlockSpec((1, LANES), const2),
                pl.BlockSpec((hm, hm), const2)]
    out_specs = [pl.BlockSpec((tm, d), row), pl.BlockSpec((tm, d), row),
                 pl.BlockSpec((tm, LANES), row), pl.BlockSpec((tm, LANES), row),
                 pl.BlockSpec((tm, LANES), row), pl.BlockSpec((SUBLANES, LANES), const2)]
    out_shape = [jax.ShapeDtypeStruct((n_rows, d), F32), jax.ShapeDtypeStruct((n_rows, d), F32),
                 jax.ShapeDtypeStruct((n_rows, LANES), jnp.int32),
                 jax.ShapeDtypeStruct((n_rows, LANES), F32),
                 jax.ShapeDtypeStruct((n_rows, LANES), jnp.int32),
                 jax.ShapeDtypeStruct((SUBLANES, LANES), F32)]
    return pl.pallas_call(
        _finish_kernel,
        grid=(n_rows // tm,),
        in_specs=in_specs,
        out_specs=out_specs,
        out_shape=out_shape,
        scratch_shapes=[pltpu.VMEM((SUBLANES, LANES), F32)],
        compiler_params=_cparams(("arbitrary",), 56),
        name="finish",
    )(gla, lru, ret, p, p, p, wb, wo, xs, mod, g2, wr, br, tril)


def _slot_kernel(idx_ref, rank_ref, start_ref, slot_ref):
    idx = idx_ref[...].astype(F32)
    lane = lax.broadcasted_iota(jnp.int32, idx.shape, 1)
    start = start_ref[...]
    out = jnp.zeros(idx.shape, F32)
    for k in range(TOP_K):
        sel = jnp.sum(jnp.where(lane == k, idx, 0.0), axis=1, keepdims=True).astype(jnp.int32)
        st = jnp.sum(jnp.where(lane == sel, start, 0.0), axis=1, keepdims=True)
        out = jnp.where(lane == k, st, out)
    slot_ref[...] = out.astype(jnp.int32) + rank_ref[...]


def _slot_call(idx, rank, pad_start):
    n = idx.shape[0]
    tm = _pick(n, (2048, 1024, 512, 256))
    row = lambda i: (i, 0)
    return pl.pallas_call(
        _slot_kernel,
        grid=(n // tm,),
        in_specs=[pl.BlockSpec((tm, LANES), row), pl.BlockSpec((tm, LANES), row),
                  pl.BlockSpec((1, LANES), lambda i: (0, 0))],
        out_specs=pl.BlockSpec((tm, LANES), row),
        out_shape=jax.ShapeDtypeStruct((n, LANES), jnp.int32),
        compiler_params=_cparams(("arbitrary",), 24),
        name="slots",
    )(idx, rank, pad_start)


def _dispatch_kernel(slot_ref, x_ref, dst_in_ref, dst_ref, sem, *, tm):
    del dst_in_ref

    def issue(g, carry):
        t0 = pl.multiple_of(g * SUBLANES, SUBLANES)
        for u in range(SUBLANES):
            for k in range(TOP_K):
                pltpu.make_async_copy(
                    x_ref.at[pl.ds(t0 + u, 1)],
                    dst_ref.at[pl.ds(slot_ref[0, (t0 + u) * TOP_K + k], 1)],
                    sem).start(priority=k % 2)
        return carry
    lax.fori_loop(0, tm // SUBLANES, issue, 0)
    for k in range(TOP_K):
        pltpu.make_async_copy(x_ref, dst_ref.at[pl.ds(0, tm)], sem).wait()


def _dispatch_call(slots3, xn, zeros_sorted):
    n, d = xn.shape
    tm = slots3.shape[2] // TOP_K
    return pl.pallas_call(
        functools.partial(_dispatch_kernel, tm=tm),
        grid=(n // tm,),
        in_specs=[pl.BlockSpec((None, 1, tm * TOP_K), lambda i: (i, 0, 0), memory_space=pltpu.SMEM),
                  pl.BlockSpec((tm, d), lambda i: (i, 0)),
                  pl.BlockSpec(memory_space=pl.ANY)],
        out_specs=pl.BlockSpec(memory_space=pl.ANY),
        out_shape=jax.ShapeDtypeStruct(zeros_sorted.shape, zeros_sorted.dtype),
        scratch_shapes=[pltpu.SemaphoreType.DMA(())],
        input_output_aliases={2: 0},
        compiler_params=_cparams(("arbitrary",), 24),
        name="dispatch",
    )(slots3, xn, zeros_sorted)


def _experts_kernel(be_ref, nu_ref, x_ref, wgu_ref, bgu_ref, wd_ref, bd_ref, perm_ref, y_ref,
                    wgu_s, wd_s):
    j = pl.program_id(0)
    f = wd_ref.shape[0]
    grp = perm_ref.shape[0]
    half = grp // 2
    active = j < nu_ref[0]
    changed = jnp.logical_or(j == 0, be_ref[j] != be_ref[jnp.maximum(j - 1, 0)])

    @pl.when(jnp.logical_and(active, changed))
    def _():
        for g in range(2 * f // grp):
            cols = slice(g * grp, (g + 1) * grp)
            wgu_s[:, cols] = _dot(wgu_ref[:, cols].astype(_MXU), perm_ref[...]).astype(_MXU)
        wd_s[...] = wd_ref[...].astype(_MXU)

    @pl.when(active)
    def _():
        gu = _dot(x_ref[...].astype(_MXU), wgu_s[...]) + bgu_ref[...]
        acts = []
        for g in range(2 * f // grp):
            gate = jnp.minimum(gu[:, g * grp:g * grp + half], SWIGLU_LIMIT)
            up = jnp.clip(gu[:, g * grp + half:(g + 1) * grp], -SWIGLU_LIMIT, SWIGLU_LIMIT)
            acts.append((gate * _sigmoid(SWIGLU_ALPHA * gate) * (up + 1.0)).astype(_MXU))
        y_ref[...] = _dot(jnp.concatenate(acts, axis=1), wd_s[...]) + bd_ref[...]

    @pl.when(jnp.logical_not(active))
    def _():
        y_ref[...] = jnp.zeros_like(y_ref)


def _experts_call(li, block_e, n_used, x_sorted, w_gu, bgu_r, w_down, bd, bm):
    n_slots, d = x_sorted.shape
    f = w_down.shape[2]
    grp = 2 * LANES
    src = np.concatenate([np.arange(0, grp, 2), np.arange(1, grp, 2)])
    perm_np = np.zeros((grp, grp), np.float32)
    perm_np[src, np.arange(grp)] = 1.0
    perm = jnp.asarray(perm_np, _MXU)

    def xmap(j, be, nu):
        return (jnp.minimum(j, nu[0] - 1), 0)

    def wmap(j, be, nu):
        return (li, be[j], 0, 0)

    grid_spec = pltpu.PrefetchScalarGridSpec(
        num_scalar_prefetch=2,
        grid=(n_slots // bm,),
        in_specs=[pl.BlockSpec((bm, d), xmap),
                  pl.BlockSpec((None, None, d, 2 * f), wmap),
                  pl.BlockSpec((None, None, 1, 2 * f), wmap),
                  pl.BlockSpec((None, None, f, d), wmap),
                  pl.BlockSpec((None, None, 1, d), wmap),
                  pl.BlockSpec((grp, grp), lambda j, be, nu: (0, 0))],
        out_specs=pl.BlockSpec((bm, d), lambda j, be, nu: (j, 0)),
        scratch_shapes=[pltpu.VMEM((d, 2 * f), _MXU), pltpu.VMEM((f, d), _MXU)])
    return pl.pallas_call(
        _experts_kernel,
        grid_spec=grid_spec,
        out_shape=jax.ShapeDtypeStruct((n_slots, d), F32),
        compiler_params=_cparams(("arbitrary",), 56),
        name="experts",
    )(block_e, n_used, x_sorted, w_gu, bgu_r, w_down, bd, perm)


def _combine_kernel(slot_ref, wgt_ref, xs_ref, mod_ref, fin_ref, y_ref, out_ref, buf_ref, sem,
                    *, tm, final):
    def issue(g, carry):
        t0 = pl.multiple_of(g * SUBLANES, SUBLANES)
        for u in range(SUBLANES):
            for k in range(TOP_K):
                pltpu.make_async_copy(
                    y_ref.at[pl.ds(slot_ref[0, (t0 + u) * TOP_K + k], 1)],
                    buf_ref.at[k, pl.ds(t0 + u, 1)], sem).start(priority=k % 2)
        return carry
    lax.fori_loop(0, tm // SUBLANES, issue, 0)
    for k in range(TOP_K):
        pltpu.make_async_copy(y_ref.at[pl.ds(0, tm)], buf_ref.at[k], sem).wait()

    wgt = wgt_ref[...]
    acc = wgt[:, 0:1] * buf_ref[0]
    for k in range(1, TOP_K):
        acc = acc + wgt[:, k:k + 1] * buf_ref[k]
    x = xs_ref[...] + mod_ref[5:6, :] * acc
    out_ref[...] = _rms(x, fin_ref[...]) if final else x


def _combine_call(slots3, wgt, xs, mod, fin, y_sorted, nl, nb, final):
    n, d = xs.shape
    tm = slots3.shape[2] // TOP_K
    return pl.pallas_call(
        functools.partial(_combine_kernel, tm=tm, final=final),
        grid=(n // tm,),
        in_specs=[pl.BlockSpec((None, 1, tm * TOP_K), lambda i: (i, 0, 0), memory_space=pltpu.SMEM),
                  pl.BlockSpec((tm, LANES), lambda i: (i, 0)),
                  pl.BlockSpec((tm, d), lambda i: (i, 0)),
                  pl.BlockSpec((None, N_MOD, d), lambda i: (jnp.minimum(i * tm // nl, nb), 0, 0)),
                  pl.BlockSpec((1, d), lambda i: (0, 0)),
                  pl.BlockSpec(memory_space=pl.ANY)],
        out_specs=pl.BlockSpec((tm, d), lambda i: (i, 0)),
        out_shape=jax.ShapeDtypeStruct((n, d), F32),
        scratch_shapes=[pltpu.VMEM((TOP_K, tm, d), F32), pltpu.SemaphoreType.DMA(())],
        compiler_params=_cparams(("arbitrary",), 32),
        name="combine",
    )(slots3, wgt, xs, mod, fin, y_sorted)


MOE_BM = 512


def _sorted_rows(n_tokens):
    return (-(-(n_tokens * TOP_K) // MOE_BM) + N_EXPERTS) * MOE_BM


def _moe(li, xn, idx, wgt, rank, counts, xs_mid, mod, fin, w_gu, bgu_r, w_down, bd, nl, nb, final,
         sorted_buf):
    n, d = xn.shape
    bm = MOE_BM
    tm = _pick(math.gcd(n, nl), (256, 128))
    n_blocks = sorted_buf.shape[0] // bm
    cnt = counts[0, :N_EXPERTS].astype(jnp.int32)
    padded = (cnt + bm - 1) // bm * bm
    pad_end = jnp.cumsum(padded)
    pad_start = pad_end - padded
    n_used = (pad_end[-1] // bm).astype(jnp.int32)
    blk = jnp.arange(n_blocks, dtype=jnp.int32)
    first_row = jnp.minimum(blk, n_used - 1) * bm
    block_e = jnp.minimum(jnp.sum(pad_end[None, :] <= first_row[:, None], axis=1),
                          N_EXPERTS - 1).astype(jnp.int32)
    start_row = jnp.zeros((1, LANES), F32).at[0, :N_EXPERTS].set(pad_start.astype(F32))
    slots = _slot_call(idx, rank, start_row)
    slots3 = slots[:, :TOP_K].reshape(n // tm, 1, tm * TOP_K)
    x_sorted = _dispatch_call(slots3, xn, sorted_buf)
    y_sorted = _experts_call(li, block_e, n_used.reshape(1), x_sorted, w_gu, bgu_r, w_down, bd, bm)
    return _combine_call(slots3, wgt, xs_mid, mod, fin, y_sorted, nl, nb, final), x_sorted


def _rope_tables(nl, nc):
    rows = nl // GRID_W
    r = np.broadcast_to(np.arange(rows)[:, None], (rows, GRID_W)).reshape(-1).astype(np.float32)
    c = np.broadcast_to(np.arange(GRID_W)[None, :], (rows, GRID_W)).reshape(-1).astype(np.float32)
    n_freq = DK // 4
    inv = (ROPE_BASE ** (-jnp.arange(n_freq, dtype=F32) / n_freq))
    ang = jnp.concatenate([jnp.asarray(r)[:, None] * inv, jnp.asarray(c)[:, None] * inv], axis=-1)
    cos, sin = jnp.cos(ang), jnp.sin(ang)
    cos2 = jnp.concatenate([cos, cos], axis=-1)
    sin2 = jnp.concatenate([-sin, sin], axis=-1)
    cos_t = jnp.concatenate([cos2, jnp.ones((nc, DK), F32)], axis=0)
    sin_t = jnp.concatenate([sin2, jnp.zeros((nc, DK), F32)], axis=0)
    return cos_t, sin_t


def kernel(x, c, ctx, c_ctx, w_ada, b_ada, norm1, norm2, w_in, gla_wa2, gla_ba, gla_norm, lru_conv_w, lru_conv_b, lru_wa, lru_ba, lru_wi, lru_bi, lru_lam, ret_norm, w_branch, w_out, w_router, b_router, w_gu, b_gu, w_down, b_down, final_norm):
    nb, nl, d = x.shape
    nc = ctx.shape[1]
    depth = w_ada.shape[0]
    n_lat = nb * nl
    f = w_down.shape[2]

    xs = jnp.concatenate([x.reshape(n_lat, d), ctx.reshape(nb * nc, d)], axis=0)
    mod_rows = -(-(nb + 1) // SUBLANES) * SUBLANES
    cc = jnp.zeros((mod_rows, d), F32).at[:nb].set(c).at[nb].set(c_ctx)
    mod_all = _ada_call(cc, w_ada.astype(_MXU), b_ada).reshape(depth, mod_rows, N_MOD, d)
    cos_t, sin_t = _rope_tables(nl, nc)
    bgu_r = b_gu.reshape(depth, N_EXPERTS, 2 * f // (2 * LANES), LANES, 2)
    bgu_r = jnp.swapaxes(bgu_r, -1, -2).reshape(depth, N_EXPERTS, 1, 2 * f)
    bd_r = b_down.reshape(depth, N_EXPERTS, 1, d)
    sorted_buf = jnp.zeros((_sorted_rows(xs.shape[0]), d), F32)
    lru_buf = jnp.zeros((xs.shape[0], lru_conv_w.shape[2]), _MXU)

    out = None
    for li in range(depth):
        last = li == depth - 1
        mod = mod_all[li]
        w = w_in[li]
        w_in_r = jnp.concatenate(
            [w[:, :3072], w[:, 3104:], w[:, 3072:3104],
             jnp.zeros((d, N_PROJ - w.shape[1]), w.dtype)], axis=1).astype(_MXU)
        p = _proj_call(xs, mod, norm1[li].reshape(1, d), w_in_r, n_lat, nl, nb)

        wa2p = jnp.zeros((2, LANES, HEADS * DK), F32)
        wa2p = wa2p.at[0, :GLA_RANK].set(gla_wa2[li, 0]).at[1, GLA_RANK:2 * GLA_RANK].set(gla_wa2[li, 1])
        gla = _la_call("gla", p, (wa2p.astype(_MXU), gla_ba[li].reshape(2, 1, HEADS * DK)),
                       gla_norm[li].reshape(1, HEADS * DV), nb, nl, nc)
        ret = _la_call("ret", p, (cos_t, sin_t), ret_norm[li].reshape(1, HEADS * DV), nb, nl, nc)

        lw = (lru_conv_w[li], lru_conv_b[li].reshape(1, -1), lru_wa[li].astype(_MXU), lru_ba[li],
              lru_wi[li].astype(_MXU), lru_bi[li], lru_lam[li])
        width = lru_conv_w.shape[2]
        lru_c, h_ctx = _lru_call(p, lru_buf, jnp.zeros((nb, 2, width), F32), lw, nb, nc, n_lat // nc)
        lru, _ = _lru_call(p, lru_c, h_ctx, lw, nb, nl, 0)
        lru_buf = lru

        n_rows = n_lat if last else xs.shape[0]
        wr = jnp.zeros((d, LANES), F32).at[:, :N_EXPERTS].set(w_router[li]).astype(_MXU)
        br = jnp.full((1, LANES), NEG_BIG, F32).at[0, :N_EXPERTS].set(b_router[li])
        xs_mid, xn2, idx, wgt, rank, counts = _finish_call(
            gla, lru, ret, p, w_branch[li].astype(_MXU), w_out[li].astype(_MXU), xs, mod,
            norm2[li].reshape(1, d), wr, br, n_rows, nl, nb)

        out, sorted_buf = _moe(li, xn2, idx, wgt, rank, counts, xs_mid, mod, final_norm.reshape(1, d),
                               w_gu, bgu_r, w_down, bd_r, nl, nb, last, sorted_buf)
        xs = out
    return out.reshape(nb, nl, d)
```

```python
import functools
import math

import numpy as np
import jax
import jax.numpy as jnp
from jax import lax
from jax.experimental import pallas as pl
from jax.experimental.pallas import tpu as pltpu

F32 = jnp.float32
_MXU = jnp.bfloat16

EPS = 1e-6
N_MOD = 6
GRID_W = 64
CHUNK = 64
HEADS = 4
DK = 128
DV = 256
GLA_RANK = 16
GLA_GATE_NORM = 16.0
LRU_BLOCKS = 4
LRU_C = 8.0
ROPE_BASE = 10000.0
N_EXPERTS = 32
TOP_K = 4
SWIGLU_LIMIT = 7.0
SWIGLU_ALPHA = 1.702
LANES = 128
SUBLANES = 8
NEG_BIG = -1e30

C_GLA_Q, C_GLA_K, C_GLA_V, C_GLA_G = 0, 512, 1024, 2048
C_LRU_X, C_LRU_G = 3072, 4096
C_RET_Q, C_RET_K, C_RET_V, C_RET_G = 5120, 5632, 6144, 7168
C_MERGE = 8192
C_LR = 11264
N_PROJ = 11520
PROJ_TN = 2304


def _pick(n, prefs):
    for p in prefs:
        if n % p == 0:
            return p
    raise ValueError(f"no tile for {n} in {prefs}")


def _cparams(sem, vmem_mb):
    return pltpu.CompilerParams(dimension_semantics=sem, vmem_limit_bytes=vmem_mb * 1024 * 1024)


def _dot(a, b):
    return jnp.dot(a, b, preferred_element_type=F32)


def _dot_nt(a, b):
    return lax.dot_general(a, b, (((1,), (1,)), ((), ())), preferred_element_type=F32)


def _dot_tn(a, b):
    return lax.dot_general(a, b, (((0,), (0,)), ((), ())), preferred_element_type=F32)


def _sigmoid(x):
    return 0.5 * jnp.tanh(0.5 * x) + 0.5


def _log_sigmoid(x):
    return jnp.minimum(x, 0.0) - jnp.log(1.0 + jnp.exp(-jnp.abs(x)))


def _silu(x):
    return x * _sigmoid(x)


def _gelu_tanh(x):
    return 0.5 * x * (1.0 + jnp.tanh(math.sqrt(2.0 / math.pi) * (x + 0.044715 * (x * x * x))))


def _rms(x, gain):
    return x * lax.rsqrt(jnp.mean(x * x, axis=-1, keepdims=True) + EPS) * gain


def _dot01_exact(tri, g):
    hi = g.astype(_MXU)
    r1 = g - hi.astype(F32)
    mid = r1.astype(_MXU)
    lo = (r1 - mid.astype(F32)).astype(_MXU)
    return _dot(tri, hi) + _dot(tri, mid) + _dot(tri, lo)


def _ada_kernel(c_ref, w_ref, b_ref, o_ref):
    s = _silu(c_ref[...])
    o_ref[...] = _dot(s.astype(_MXU), w_ref[...]) + b_ref[...]


def _ada_call(cc, w_ada, b_ada):
    depth, d, n = w_ada.shape
    rows = cc.shape[0]
    tn = _pick(n, (1536, 1024, 512, 128))
    return pl.pallas_call(
        _ada_kernel,
        grid=(depth, n // tn),
        in_specs=[pl.BlockSpec((rows, d), lambda l, j: (0, 0)),
                  pl.BlockSpec((None, d, tn), lambda l, j: (l, 0, j)),
                  pl.BlockSpec((None, 1, tn), lambda l, j: (l, 0, j))],
        out_specs=pl.BlockSpec((None, rows, tn), lambda l, j: (l, 0, j)),
        out_shape=jax.ShapeDtypeStruct((depth, rows, n), F32),
        compiler_params=_cparams(("arbitrary", "arbitrary"), 24),
        name="ada",
    )(cc, w_ada, b_ada.reshape(depth, 1, n))


def _proj_kernel(x_ref, mod_ref, g_ref, w_ref, o_ref, xn_ref):
    @pl.when(pl.program_id(1) == 0)
    def _():
        y = _rms(x_ref[...], g_ref[...])
        xn_ref[...] = (y * (1.0 + mod_ref[1:2, :]) + mod_ref[0:1, :]).astype(xn_ref.dtype)

    o_ref[...] = _dot(xn_ref[...], w_ref[...]).astype(o_ref.dtype)


def _proj_call(xs, mod, gain, w_in_r, n_lat_rows, nl, nb):
    t, d = xs.shape
    tm = _pick(math.gcd(t, nl), (1024, 512, 256))
    return pl.pallas_call(
        _proj_kernel,
        grid=(t // tm, N_PROJ // PROJ_TN),
        in_specs=[pl.BlockSpec((tm, d), lambda i, j: (i, 0)),
                  pl.BlockSpec((None, N_MOD, d), lambda i, j: (jnp.minimum(i * tm // nl, nb), 0, 0)),
                  pl.BlockSpec((1, d), lambda i, j: (0, 0)),
                  pl.BlockSpec((d, PROJ_TN), lambda i, j: (0, j))],
        out_specs=pl.BlockSpec((tm, PROJ_TN), lambda i, j: (i, j)),
        out_shape=jax.ShapeDtypeStruct((t, N_PROJ), _MXU),
        scratch_shapes=[pltpu.VMEM((tm, d), _MXU)],
        compiler_params=_cparams(("arbitrary", "arbitrary"), 40),
        name="proj",
    )(xs, mod, gain, w_in_r)


def _la_kernel(*refs, kind, tt, n_ct, n_lt):
    if kind == "gla":
        (q_ref, k_ref, v_ref, gate_ref, lr_ref, wa2_ref, ba_ref, gain_ref,
         out_ref, st_ref, of_ref) = refs
    else:
        (q_ref, k_ref, v_ref, gate_ref, cos_ref, sin_ref, gain_ref,
         out_ref, st_ref, of_ref, dm_ref) = refs
    ph = pl.program_id(1)
    s = pl.program_id(2)
    ck = CHUNK if kind == "gla" else tt
    n_chunks = tt // ck

    @pl.when(s == 0)
    def _():
        st_ref[...] = jnp.zeros_like(st_ref)

    row = lax.broadcasted_iota(jnp.int32, (tt, tt), 0)
    col = lax.broadcasted_iota(jnp.int32, (tt, tt), 1)
    shift = ck.bit_length() - 1
    same_chunk = (row >> shift) == (col >> shift)
    tpos = lax.broadcasted_iota(jnp.int32, (tt, 1), 0).astype(F32)

    if kind == "ret":
        first = jnp.logical_and(pl.program_id(0) == 0, jnp.logical_and(ph == 0, s == 0))

        @pl.when(first)
        def _():
            for h in range(HEADS):
                lg = math.log(1.0 - 2.0 ** (-5.0 - h))
                dm_ref[h] = jnp.where(col <= row, jnp.exp((row - col).astype(F32) * lg), 0.0)
                dm_ref[HEADS + h] = jnp.where(col > row, jnp.exp((col - row).astype(F32) * lg), 0.0)

    def tile_outputs(backward):
        if backward:
            mask = jnp.logical_and(same_chunk, col > row)
            tri = jnp.logical_and(same_chunk, col >= row).astype(_MXU)
        else:
            mask = jnp.logical_and(same_chunk, col <= row)
            tri = jnp.logical_and(same_chunk, col <= row).astype(_MXU)
        order = list(reversed(range(n_chunks))) if backward else list(range(n_chunks))
        if kind == "gla":
            d = 1 if backward else 0
            z = _dot(lr_ref[...], wa2_ref[d]) + ba_ref[d]
            g = _log_sigmoid(z) * (1.0 / GLA_GATE_NORM)
            big_g = _dot01_exact(tri, g)
            lasts = [big_g[c * ck:c * ck + 1, :] if backward else big_g[(c + 1) * ck - 1:(c + 1) * ck, :]
                     for c in range(n_chunks)]
            g_last = jnp.concatenate([jnp.broadcast_to(r, (ck, r.shape[1])) for r in lasts], axis=0)
            e_pos = jnp.exp(big_g)
            e_neg = jnp.exp(-big_g)
            e_end = jnp.exp(g_last - big_g)
            decs = [jnp.exp(r) for r in lasts]
        outs = []
        for h in range(HEADS):
            ks = slice(h * DK, (h + 1) * DK)
            vs = slice(h * DV, (h + 1) * DV)
            qh = q_ref[:, ks].astype(F32)
            kh = k_ref[:, ks].astype(F32)
            vh = v_ref[:, vs]
            if kind == "gla":
                qh = qh * DK ** -0.5
                q_dec = (qh * e_pos[:, ks]).astype(_MXU)
                k_inv = (kh * e_neg[:, ks]).astype(_MXU)
                k_end = (kh * e_end[:, ks]).astype(_MXU)
                sc = jnp.where(mask, _dot_nt(q_dec, k_inv), 0.0)
                dec_h = [dcy[:, ks] for dcy in decs]
            else:
                kh = kh * DK ** -0.5
                cos = cos_ref[...]
                sin = sin_ref[...]
                qh = qh * cos + pltpu.roll(qh, DK // 2, 1) * sin
                kh = kh * cos + pltpu.roll(kh, DK // 2, 1) * sin
                lg = math.log(1.0 - 2.0 ** (-5.0 - h))
                steps = (ck - tpos) if backward else (tpos + 1.0)
                q_dec = (qh * jnp.exp(steps * lg)).astype(_MXU)
                k_end = (kh * jnp.exp((ck - steps) * lg)).astype(_MXU)
                sc = _dot_nt(qh.astype(_MXU), kh.astype(_MXU)) * dm_ref[(HEADS if backward else 0) + h]
                dec_h = [math.exp(ck * lg)] * n_chunks
            o_intra = _dot(sc.astype(_MXU), vh)
            st = st_ref[h]
            parts = [None] * n_chunks
            for c in order:
                rs = slice(c * ck, (c + 1) * ck)
                parts[c] = o_intra[rs] + _dot_nt(q_dec[rs], st.astype(_MXU))
                st = st * dec_h[c] + _dot_tn(vh[rs], k_end[rs])
            st_ref[h] = st
            outs.append(parts[0] if n_chunks == 1 else jnp.concatenate(parts, axis=0))
        return jnp.concatenate(outs, axis=1)

    @pl.when(ph == 0)
    def _():
        base = pl.multiple_of(s * tt, tt)
        of_ref[pl.ds(base, tt), :] = tile_outputs(False)

    @pl.when(ph == 1)
    def _():
        loc = jnp.where(s < n_ct, n_ct - 1 - s, n_ct + n_lt - 1 - (s - n_ct))
        base = pl.multiple_of(loc * tt, tt)
        o = of_ref[pl.ds(base, tt), :] + tile_outputs(True)
        gate = gate_ref[...].astype(F32)
        parts = []
        for h in range(HEADS):
            oh = o[:, h * DV:(h + 1) * DV]
            if kind == "ret":
                oh = oh - jnp.mean(oh, axis=-1, keepdims=True)
            parts.append(oh * lax.rsqrt(jnp.mean(oh * oh, axis=-1, keepdims=True) + EPS))
        normed = jnp.concatenate(parts, axis=1) * gain_ref[...]
        out_ref[...] = (normed * _silu(gate)).astype(out_ref.dtype)


def _la_call(kind, p, extra, gain, nb, nl, nc):
    t = p.shape[0]
    tt = _pick(math.gcd(nl, nc), (256, 128, 64))
    n_ct, n_lt = nc // tt, nl // tt
    lat_tiles = nb * n_lt
    cq, ck, cv, cg = ((C_GLA_Q, C_GLA_K, C_GLA_V, C_GLA_G) if kind == "gla"
                      else (C_RET_Q, C_RET_K, C_RET_V, C_RET_G))
    hd = HEADS * DK
    hv = HEADS * DV

    def loc_of(ph, s):
        back = jnp.where(s < n_ct, n_ct - 1 - s, n_ct + n_lt - 1 - (s - n_ct))
        return jnp.where(ph == 0, s, back)

    def row_blk(b, loc):
        return jnp.where(loc < n_ct, lat_tiles + b * n_ct + loc, b * n_lt + (loc - n_ct))

    def in_map(cblk):
        return lambda b, ph, s: (row_blk(b, loc_of(ph, s)), cblk)

    def second_pass_map(cblk):
        return lambda b, ph, s: (row_blk(b, loc_of(1, jnp.where(ph == 0, 0, s))), cblk)

    in_specs = [pl.BlockSpec((tt, hd), in_map(cq // hd)),
                pl.BlockSpec((tt, hd), in_map(ck // hd)),
                pl.BlockSpec((tt, hv), in_map(cv // hv)),
                pl.BlockSpec((tt, hv), second_pass_map(cg // hv))]
    args = [p, p, p, p]
    if kind == "gla":
        wa2p, ba = extra
        in_specs += [pl.BlockSpec((tt, LANES), in_map(C_LR // LANES)),
                     pl.BlockSpec((2, LANES, hd), lambda b, ph, s: (0, 0, 0)),
                     pl.BlockSpec((2, 1, hd), lambda b, ph, s: (0, 0, 0))]
        args += [p, wa2p, ba]
    else:
        cos_t, sin_t = extra

        def rope_map(b, ph, s):
            loc = loc_of(ph, s)
            return (jnp.where(loc < n_ct, n_lt + loc, loc - n_ct), 0)
        in_specs += [pl.BlockSpec((tt, DK), rope_map), pl.BlockSpec((tt, DK), rope_map)]
        args += [cos_t, sin_t]
    in_specs.append(pl.BlockSpec((1, hv), lambda b, ph, s: (0, 0)))
    args.append(gain)
    return pl.pallas_call(
        functools.partial(_la_kernel, kind=kind, tt=tt, n_ct=n_ct, n_lt=n_lt),
        grid=(nb, 2, n_ct + n_lt),
        in_specs=in_specs,
        out_specs=pl.BlockSpec((tt, hv), second_pass_map(0)),
        out_shape=jax.ShapeDtypeStruct((t, hv), _MXU),
        scratch_shapes=[pltpu.VMEM((HEADS, DV, DK), F32),
                        pltpu.VMEM((nc + nl, hv), F32)]
        + ([pltpu.VMEM((2 * HEADS, tt, tt), F32)] if kind == "ret" else []),
        compiler_params=_cparams(("arbitrary", "arbitrary", "arbitrary"), 48),
        name=kind,
    )(*args)


def _scan_group(a, b, h, reverse):
    row = lax.broadcasted_iota(jnp.int32, a.shape, 0)
    for sft in (1, 2, 4):
        if reverse:
            a_sh = pltpu.roll(a, SUBLANES - sft, 0)
            b_sh = pltpu.roll(b, SUBLANES - sft, 0)
            m = row < SUBLANES - sft
        else:
            a_sh = pltpu.roll(a, sft, 0)
            b_sh = pltpu.roll(b, sft, 0)
            m = row >= sft
        b = jnp.where(m, a * b_sh + b, b)
        a = jnp.where(m, a * a_sh, a)
    hh = a * h + b
    return hh, (hh[0:1, :] if reverse else hh[SUBLANES - 1:SUBLANES, :])


def _lru_kernel(x_ref, gel_ref, cw_ref, cb_ref, wa_ref, ba_ref, wi_ref, bi_ref, lam_ref, h0_ref,
                out_ref, hn_ref, xpad_ref, xc_ref, hf_ref, *, seg, tile):
    n_tiles = seg // tile
    groups = tile // SUBLANES
    cw = cw_ref[...]
    cb = cb_ref[...]
    zeros8 = jnp.zeros((SUBLANES, xpad_ref.shape[1]), F32)
    xpad_ref[0:SUBLANES, :] = zeros8
    xpad_ref[SUBLANES + seg:2 * SUBLANES + seg, :] = zeros8

    def copy_in(i, carry):
        t0 = pl.multiple_of(i * tile, tile)
        xpad_ref[pl.ds(t0 + SUBLANES, tile), :] = x_ref[pl.ds(t0, tile), :].astype(F32)
        return carry
    lax.fori_loop(0, n_tiles, copy_in, 0)

    neg_c = [(-0.5 * LRU_C) * _log_sigmoid(lam_ref[d:d + 1, :]) for d in range(2)]

    def gates(xc, d):
        xb = xc.astype(_MXU)
        tr = jnp.tanh(_dot(xb, wa_ref[d]) + ba_ref[d:d + 1, :])
        ig = 0.5 * jnp.tanh(_dot(xb, wi_ref[d]) + bi_ref[d:d + 1, :]) + 0.5
        nla = neg_c[d] * tr + neg_c[d]
        a = jnp.exp(-nla)
        b = jnp.sqrt(jnp.tanh(nla) * (a * a + 1.0)) * (ig * xc)
        return a, b

    def fwd_tile(i, h):
        t0 = pl.multiple_of(i * tile, tile)
        win = xpad_ref[pl.ds(t0, tile + 2 * SUBLANES), :]
        xc = (cw[0:1, :] * win[6:6 + tile] + cw[1:2, :] * win[7:7 + tile]
              + cw[2:3, :] * win[8:8 + tile] + cw[3:4, :] * win[9:9 + tile] + cb)
        xc_ref[pl.ds(t0, tile), :] = xc
        a, b = gates(xc, 0)
        for g in range(groups):
            rs = slice(g * SUBLANES, (g + 1) * SUBLANES)
            hh, h = _scan_group(a[rs], b[rs], h, False)
            hf_ref[pl.ds(t0 + g * SUBLANES, SUBLANES), :] = hh
        return h
    h_f = lax.fori_loop(0, n_tiles, fwd_tile, h0_ref[0:1, :])

    def bwd_tile(j, h):
        i = n_tiles - 1 - j
        t0 = pl.multiple_of(i * tile, tile)
        xc = xc_ref[pl.ds(t0, tile), :]
        a, b = gates(xc, 1)
        for g in reversed(range(groups)):
            rs = slice(g * SUBLANES, (g + 1) * SUBLANES)
            hh, h = _scan_group(a[rs], b[rs], h, True)
            rows = pl.ds(t0 + g * SUBLANES, SUBLANES)
            hsum = hf_ref[rows, :] + hh
            hf_ref[rows, :] = hsum
        gel = gel_ref[pl.ds(t0, tile), :].astype(F32)
        out_ref[pl.ds(t0, tile), :] = (hf_ref[pl.ds(t0, tile), :] * _gelu_tanh(gel)).astype(out_ref.dtype)
        return h
    h_b = lax.fori_loop(0, n_tiles, bwd_tile, h0_ref[1:2, :])
    hn_ref[0:1, :] = h_f
    hn_ref[1:2, :] = h_b


def _lru_call(p, prev_out, h0, lw, nb, seg, row_blk0):
    cw, cb, wa, ba, wi, bi, lam = lw
    t = p.shape[0]
    width = LRU_BLOCKS * DV
    cbw = width // LRU_BLOCKS
    tile = _pick(seg, (256, 128, 64))
    xblk = C_LRU_X // cbw
    gblk = C_LRU_G // cbw
    in_specs = [pl.BlockSpec((seg, cbw), lambda b, c: (row_blk0 + b, xblk + c)),
                pl.BlockSpec((seg, cbw), lambda b, c: (row_blk0 + b, gblk + c)),
                pl.BlockSpec((4, cbw), lambda b, c: (0, c)),
                pl.BlockSpec((1, cbw), lambda b, c: (0, c)),
                pl.BlockSpec((2, None, cbw, cbw), lambda b, c: (0, c, 0, 0)),
                pl.BlockSpec((2, cbw), lambda b, c: (0, c)),
                pl.BlockSpec((2, None, cbw, cbw), lambda b, c: (0, c, 0, 0)),
                pl.BlockSpec((2, cbw), lambda b, c: (0, c)),
                pl.BlockSpec((2, cbw), lambda b, c: (0, c)),
                pl.BlockSpec((None, 2, cbw), lambda b, c: (b, 0, c))]
    in_specs.append(pl.BlockSpec(memory_space=pl.ANY))
    args = [p, p, cw, cb, wa, ba, wi, bi, lam, h0, prev_out]

    def body(*refs):
        _lru_kernel(*(refs[:10] + refs[11:]), seg=seg, tile=tile)

    return pl.pallas_call(
        body,
        grid=(nb, LRU_BLOCKS),
        in_specs=in_specs,
        out_specs=[pl.BlockSpec((seg, cbw), lambda b, c: (row_blk0 + b, c)),
                   pl.BlockSpec((None, 2, cbw), lambda b, c: (b, 0, c))],
        out_shape=[jax.ShapeDtypeStruct((t, width), _MXU),
                   jax.ShapeDtypeStruct((nb, 2, width), F32)],
        scratch_shapes=[pltpu.VMEM((seg + 2 * SUBLANES, cbw), F32),
                        pltpu.VMEM((seg, cbw), F32),
                        pltpu.VMEM((seg, cbw), F32)],
        input_output_aliases={10: 0},
        compiler_params=_cparams(("arbitrary", "arbitrary"), 48),
        name="lru",
    )(*args)


def _finish_kernel(gla_ref, lru_ref, ret_ref, m0_ref, m1_ref, m2_ref, wb_ref, wo_ref, xs_ref,
                   mod_ref, g2_ref, wr_ref, br_ref, tril_ref,
                   xo_ref, xn_ref, idx_ref, wgt_ref, rank_ref, cnt_ref, carry_ref):
    @pl.when(pl.program_id(0) == 0)
    def _():
        carry_ref[...] = jnp.zeros_like(carry_ref)

    hm = tril_ref.shape[0]
    total = carry_ref[0:1, :]
    for sub in range(xs_ref.shape[0] // hm):
        rs = slice(sub * hm, (sub + 1) * hm)
        merged = (_sigmoid(m0_ref[rs, :].astype(F32)) * _dot(gla_ref[rs, :], wb_ref[0])
                  + _sigmoid(m1_ref[rs, :].astype(F32)) * _dot(lru_ref[rs, :], wb_ref[1])
                  + _sigmoid(m2_ref[rs, :].astype(F32)) * _dot(ret_ref[rs, :], wb_ref[2]))
        x = xs_ref[rs, :] + mod_ref[2:3, :] * _dot(merged.astype(_MXU), wo_ref[...])
        xo_ref[rs, :] = x
        xn = _rms(x, g2_ref[...]) * (1.0 + mod_ref[4:5, :]) + mod_ref[3:4, :]
        xn_ref[rs, :] = xn

        logits = _dot(xn.astype(_MXU), wr_ref[...]) + br_ref[...]
        lane = lax.broadcasted_iota(jnp.int32, logits.shape, 1)
        ids, vals = [], []
        for _ in range(TOP_K):
            m = jnp.max(logits, axis=1, keepdims=True)
            sel = jnp.min(jnp.where(logits == m, lane.astype(F32), float(LANES)), axis=1,
                          keepdims=True).astype(jnp.int32)
            ids.append(sel)
            vals.append(m)
            logits = jnp.where(lane == sel, -jnp.inf, logits)
        ex = [jnp.exp(v - vals[0]) for v in vals]
        denom = ex[0] + ex[1] + ex[2] + ex[3]
        onehot = jnp.zeros(logits.shape, F32)
        for sel in ids:
            onehot = onehot + (lane == sel).astype(F32)
        before = _dot(tril_ref[...], onehot.astype(_MXU)) + total
        idx_o = jnp.zeros(logits.shape, jnp.int32)
        rank_o = jnp.zeros(logits.shape, jnp.int32)
        wgt_o = jnp.zeros(logits.shape, F32)
        for k in range(TOP_K):
            rk = jnp.sum(jnp.where(lane == ids[k], before, 0.0), axis=1, keepdims=True)
            idx_o = jnp.where(lane == k, ids[k], idx_o)
            rank_o = jnp.where(lane == k, rk.astype(jnp.int32), rank_o)
            wgt_o = jnp.where(lane == k, ex[k] / denom, wgt_o)
        idx_ref[rs, :] = idx_o
        rank_ref[rs, :] = rank_o
        wgt_ref[rs, :] = wgt_o
        total = total + jnp.sum(onehot, axis=0, keepdims=True)
    carry_ref[...] = jnp.broadcast_to(total, carry_ref.shape)
    cnt_ref[...] = jnp.broadcast_to(total, cnt_ref.shape)


def _finish_call(gla, lru, ret, p, wb, wo, xs, mod, g2, wr, br, n_rows, nl, nb):
    d = xs.shape[1]
    tm = _pick(math.gcd(n_rows, nl), (512, 256))
    hm = tm
    tril = jnp.asarray(np.tril(np.ones((hm, hm), np.float32), -1), _MXU)
    row = lambda i: (i, 0)
    const2 = lambda i: (0, 0)
    mblk = C_MERGE // d
    in_specs = [pl.BlockSpec((tm, d), row), pl.BlockSpec((tm, d), row), pl.BlockSpec((tm, d), row),
                pl.BlockSpec((tm, d), lambda i: (i, mblk)),
                pl.BlockSpec((tm, d), lambda i: (i, mblk + 1)),
                pl.BlockSpec((tm, d), lambda i: (i, mblk + 2)),
                pl.BlockSpec((3, d, d), lambda i: (0, 0, 0)),
                pl.BlockSpec((d, d), const2),
                pl.BlockSpec((tm, d), row),
                pl.BlockSpec((None, N_MOD, d), lambda i: (jnp.minimum(i * tm // nl, nb), 0, 0)),
                pl.BlockSpec((1, d), const2),
                pl.BlockSpec((d, LANES), const2),
                pl.BlockSpec((1, LANES), const2),
                pl.BlockSpec((hm, hm), const2)]
    out_specs = [pl.BlockSpec((tm, d), row), pl.BlockSpec((tm, d), row),
                 pl.BlockSpec((tm, LANES), row), pl.BlockSpec((tm, LANES), row),
                 pl.BlockSpec((tm, LANES), row), pl.BlockSpec((SUBLANES, LANES), const2)]
    out_shape = [jax.ShapeDtypeStruct((n_rows, d), F32), jax.ShapeDtypeStruct((n_rows, d), F32),
                 jax.ShapeDtypeStruct((n_rows, LANES), jnp.int32),
                 jax.ShapeDtypeStruct((n_rows, LANES), F32),
                 jax.ShapeDtypeStruct((n_rows, LANES), jnp.int32),
                 jax.ShapeDtypeStruct((SUBLANES, LANES), F32)]
    return pl.pallas_call(
        _finish_kernel,
        grid=(n_rows // tm,),
        in_specs=in_specs,
        out_specs=out_specs,
        out_shape=out_shape,
        scratch_shapes=[pltpu.VMEM((SUBLANES, LANES), F32)],
        compiler_params=_cparams(("arbitrary",), 56),
        name="finish",
    )(gla, lru, ret, p, p, p, wb, wo, xs, mod, g2, wr, br, tril)


def _slot_kernel(idx_ref, rank_ref, start_ref, slot_ref):
    idx = idx_ref[...].astype(F32)
    lane = lax.broadcasted_iota(jnp.int32, idx.shape, 1)
    start = start_ref[...]
    out = jnp.zeros(idx.shape, F32)
    for k in range(TOP_K):
        sel = jnp.sum(jnp.where(lane == k, idx, 0.0), axis=1, keepdims=True).astype(jnp.int32)
        st = jnp.sum(jnp.where(lane == sel, start, 0.0), axis=1, keepdims=True)
        out = jnp.where(lane == k, st, out)
    slot_ref[...] = out.astype(jnp.int32) + rank_ref[...]


def _slot_call(idx, rank, pad_start):
    n = idx.shape[0]
    tm = _pick(n, (2048, 1024, 512, 256))
    row = lambda i: (i, 0)
    return pl.pallas_call(
        _slot_kernel,
        grid=(n // tm,),
        in_specs=[pl.BlockSpec((tm, LANES), row), pl.BlockSpec((tm, LANES), row),
                  pl.BlockSpec((1, LANES), lambda i: (0, 0))],
        out_specs=pl.BlockSpec((tm, LANES), row),
        out_shape=jax.ShapeDtypeStruct((n, LANES), jnp.int32),
        compiler_params=_cparams(("arbitrary",), 24),
        name="slots",
    )(idx, rank, pad_start)


def _dispatch_kernel(slot_ref, x_ref, dst_in_ref, dst_ref, sem, *, tm):
    del dst_in_ref

    def issue(g, carry):
        t0 = pl.multiple_of(g * SUBLANES, SUBLANES)
        for u in range(SUBLANES):
            for k in range(TOP_K):
                pltpu.make_async_copy(
                    x_ref.at[pl.ds(t0 + u, 1)],
                    dst_ref.at[pl.ds(slot_ref[0, (t0 + u) * TOP_K + k], 1)],
                    sem).start(priority=k % 2)
        return carry
    lax.fori_loop(0, tm // SUBLANES, issue, 0)
    for k in range(TOP_K):
        pltpu.make_async_copy(x_ref, dst_ref.at[pl.ds(0, tm)], sem).wait()


def _dispatch_call(slots3, xn, zeros_sorted):
    n, d = xn.shape
    tm = slots3.shape[2] // TOP_K
    return pl.pallas_call(
        functools.partial(_dispatch_kernel, tm=tm),
        grid=(n // tm,),
        in_specs=[pl.BlockSpec((None, 1, tm * TOP_K), lambda i: (i, 0, 0), memory_space=pltpu.SMEM),
                  pl.BlockSpec((tm, d), lambda i: (i, 0)),
                  pl.BlockSpec(memory_space=pl.ANY)],
        out_specs=pl.BlockSpec(memory_space=pl.ANY),
        out_shape=jax.ShapeDtypeStruct(zeros_sorted.shape, zeros_sorted.dtype),
        scratch_shapes=[pltpu.SemaphoreType.DMA(())],
        input_output_aliases={2: 0},
        compiler_params=_cparams(("arbitrary",), 24),
        name="dispatch",
    )(slots3, xn, zeros_sorted)


def _experts_kernel(be_ref, nu_ref, x_ref, wgu_ref, bgu_ref, wd_ref, bd_ref, perm_ref, y_ref,
                    wgu_s, wd_s):
    j = pl.program_id(0)
    f = wd_ref.shape[0]
    grp = perm_ref.shape[0]
    half = grp // 2
    active = j < nu_ref[0]
    changed = jnp.logical_or(j == 0, be_ref[j] != be_ref[jnp.maximum(j - 1, 0)])

    @pl.when(jnp.logical_and(active, changed))
    def _():
        for g in range(2 * f // grp):
            cols = slice(g * grp, (g + 1) * grp)
            wgu_s[:, cols] = _dot(wgu_ref[:, cols].astype(_MXU), perm_ref[...]).astype(_MXU)
        wd_s[...] = wd_ref[...].astype(_MXU)

    @pl.when(active)
    def _():
        gu = _dot(x_ref[...].astype(_MXU), wgu_s[...]) + bgu_ref[...]
        acts = []
        for g in range(2 * f // grp):
            gate = jnp.minimum(gu[:, g * grp:g * grp + half], SWIGLU_LIMIT)
            up = jnp.clip(gu[:, g * grp + half:(g + 1) * grp], -SWIGLU_LIMIT, SWIGLU_LIMIT)
            acts.append((gate * _sigmoid(SWIGLU_ALPHA * gate) * (up + 1.0)).astype(_MXU))
        y_ref[...] = _dot(jnp.concatenate(acts, axis=1), wd_s[...]) + bd_ref[...]

    @pl.when(jnp.logical_not(active))
    def _():
        y_ref[...] = jnp.zeros_like(y_ref)


def _experts_call(li, block_e, n_used, x_sorted, w_gu, bgu_r, w_down, bd, bm):
    n_slots, d = x_sorted.shape
    f = w_down.shape[2]
    grp = 2 * LANES
    src = np.concatenate([np.arange(0, grp, 2), np.arange(1, grp, 2)])
    perm_np = np.zeros((grp, grp), np.float32)
    perm_np[src, np.arange(grp)] = 1.0
    perm = jnp.asarray(perm_np, _MXU)

    def xmap(j, be, nu):
        return (jnp.minimum(j, nu[0] - 1), 0)

    def wmap(j, be, nu):
        return (li, be[j], 0, 0)

    grid_spec = pltpu.PrefetchScalarGridSpec(
        num_scalar_prefetch=2,
        grid=(n_slots // bm,),
        in_specs=[pl.BlockSpec((bm, d), xmap),
                  pl.BlockSpec((None, None, d, 2 * f), wmap),
                  pl.BlockSpec((None, None, 1, 2 * f), wmap),
                  pl.BlockSpec((None, None, f, d), wmap),
                  pl.BlockSpec((None, None, 1, d), wmap),
                  pl.BlockSpec((grp, grp), lambda j, be, nu: (0, 0))],
        out_specs=pl.BlockSpec((bm, d), lambda j, be, nu: (j, 0)),
        scratch_shapes=[pltpu.VMEM((d, 2 * f), _MXU), pltpu.VMEM((f, d), _MXU)])
    return pl.pallas_call(
        _experts_kernel,
        grid_spec=grid_spec,
        out_shape=jax.ShapeDtypeStruct((n_slots, d), F32),
        compiler_params=_cparams(("arbitrary",), 56),
        name="experts",
    )(block_e, n_used, x_sorted, w_gu, bgu_r, w_down, bd, perm)


def _combine_kernel(slot_ref, wgt_ref, xs_ref, mod_ref, fin_ref, y_ref, out_ref, buf_ref, sem,
                    *, tm, final):
    def issue(g, carry):
        t0 = pl.multiple_of(g * SUBLANES, SUBLANES)
        for u in range(SUBLANES):
            for k in range(TOP_K):
                pltpu.make_async_copy(
                    y_ref.at[pl.ds(slot_ref[0, (t0 + u) * TOP_K + k], 1)],
                    buf_ref.at[k, pl.ds(t0 + u, 1)], sem).start(priority=k % 2)
        return carry
    lax.fori_loop(0, tm // SUBLANES, issue, 0)
    for k in range(TOP_K):
        pltpu.make_async_copy(y_ref.at[pl.ds(0, tm)], buf_ref.at[k], sem).wait()

    wgt = wgt_ref[...]
    acc = wgt[:, 0:1] * buf_ref[0]
    for k in range(1, TOP_K):
        acc = acc + wgt[:, k:k + 1] * buf_ref[k]
    x = xs_ref[...] + mod_ref[5:6, :] * acc
    out_ref[...] = _rms(x, fin_ref[...]) if final else x


def _combine_call(slots3, wgt, xs, mod, fin, y_sorted, nl, nb, final):
    n, d = xs.shape
    tm = slots3.shape[2] // TOP_K
    return pl.pallas_call(
        functools.partial(_combine_kernel, tm=tm, final=final),
        grid=(n // tm,),
        in_specs=[pl.BlockSpec((None, 1, tm * TOP_K), lambda i: (i, 0, 0), memory_space=pltpu.SMEM),
                  pl.BlockSpec((tm, LANES), lambda i: (i, 0)),
                  pl.BlockSpec((tm, d), lambda i: (i, 0)),
                  pl.BlockSpec((None, N_MOD, d), lambda i: (jnp.minimum(i * tm // nl, nb), 0, 0)),
                  pl.BlockSpec((1, d), lambda i: (0, 0)),
                  pl.BlockSpec(memory_space=pl.ANY)],
        out_specs=pl.BlockSpec((tm, d), lambda i: (i, 0)),
        out_shape=jax.ShapeDtypeStruct((n, d), F32),
        scratch_shapes=[pltpu.VMEM((TOP_K, tm, d), F32), pltpu.SemaphoreType.DMA(())],
        compiler_params=_cparams(("arbitrary",), 32),
        name="combine",
    )(slots3, wgt, xs, mod, fin, y_sorted)


MOE_BM = 512


def _sorted_rows(n_tokens):
    return (-(-(n_tokens * TOP_K) // MOE_BM) + N_EXPERTS) * MOE_BM


def _moe(li, xn, idx, wgt, rank, counts, xs_mid, mod, fin, w_gu, bgu_r, w_down, bd, nl, nb, final,
         sorted_buf):
    n, d = xn.shape
    bm = MOE_BM
    tm = _pick(math.gcd(n, nl), (256, 128))
    n_blocks = sorted_buf.shape[0] // bm
    cnt = counts[0, :N_EXPERTS].astype(jnp.int32)
    padded = (cnt + bm - 1) // bm * bm
    pad_end = jnp.cumsum(padded)
    pad_start = pad_end - padded
    n_used = (pad_end[-1] // bm).astype(jnp.int32)
    blk = jnp.arange(n_blocks, dtype=jnp.int32)
    first_row = jnp.minimum(blk, n_used - 1) * bm
    block_e = jnp.minimum(jnp.sum(pad_end[None, :] <= first_row[:, None], axis=1),
                          N_EXPERTS - 1).astype(jnp.int32)
    start_row = jnp.zeros((1, LANES), F32).at[0, :N_EXPERTS].set(pad_start.astype(F32))
    slots = _slot_call(idx, rank, start_row)
    slots3 = slots[:, :TOP_K].reshape(n // tm, 1, tm * TOP_K)
    x_sorted = _dispatch_call(slots3, xn, sorted_buf)
    y_sorted = _experts_call(li, block_e, n_used.reshape(1), x_sorted, w_gu, bgu_r, w_down, bd, bm)
    return _combine_call(slots3, wgt, xs_mid, mod, fin, y_sorted, nl, nb, final), x_sorted


def _rope_tables(nl, nc):
    rows = nl // GRID_W
    r = np.broadcast_to(np.arange(rows)[:, None], (rows, GRID_W)).reshape(-1).astype(np.float32)
    c = np.broadcast_to(np.arange(GRID_W)[None, :], (rows, GRID_W)).reshape(-1).astype(np.float32)
    n_freq = DK // 4
    inv = (ROPE_BASE ** (-jnp.arange(n_freq, dtype=F32) / n_freq))
    ang = jnp.concatenate([jnp.asarray(r)[:, None] * inv, jnp.asarray(c)[:, None] * inv], axis=-1)
    cos, sin = jnp.cos(ang), jnp.sin(ang)
    cos2 = jnp.concatenate([cos, cos], axis=-1)
    sin2 = jnp.concatenate([-sin, sin], axis=-1)
    cos_t = jnp.concatenate([cos2, jnp.ones((nc, DK), F32)], axis=0)
    sin_t = jnp.concatenate([sin2, jnp.zeros((nc, DK), F32)], axis=0)
    return cos_t, sin_t


def kernel(x, c, ctx, c_ctx, w_ada, b_ada, norm1, norm2, w_in, gla_wa2, gla_ba, gla_norm, lru_conv_w, lru_conv_b, lru_wa, lru_ba, lru_wi, lru_bi, lru_lam, ret_norm, w_branch, w_out, w_router, b_router, w_gu, b_gu, w_down, b_down, final_norm):
    nb, nl, d = x.shape
    nc = ctx.shape[1]
    depth = w_ada.shape[0]
    n_lat = nb * nl
    f = w_down.shape[2]

    xs = jnp.concatenate([x.reshape(n_lat, d), ctx.reshape(nb * nc, d)], axis=0)
    mod_rows = -(-(nb + 1) // SUBLANES) * SUBLANES
    cc = jnp.zeros((mod_rows, d), F32).at[:nb].set(c).at[nb].set(c_ctx)
    mod_all = _ada_call(cc, w_ada.astype(_MXU), b_ada).reshape(depth, mod_rows, N_MOD, d)
    cos_t, sin_t = _rope_tables(nl, nc)
    bgu_r = b_gu.reshape(depth, N_EXPERTS, 2 * f // (2 * LANES), LANES, 2)
    bgu_r = jnp.swapaxes(bgu_r, -1, -2).reshape(depth, N_EXPERTS, 1, 2 * f)
    bd_r = b_down.reshape(depth, N_EXPERTS, 1, d)
    sorted_buf = jnp.zeros((_sorted_rows(xs.shape[0]), d), F32)
    lru_buf = jnp.zeros((xs.shape[0], lru_conv_w.shape[2]), _MXU)

    out = None
    for li in range(depth):
        last = li == depth - 1
        mod = mod_all[li]
        w = w_in[li]
        w_in_r = jnp.concatenate(
            [w[:, :3072], w[:, 3104:], w[:, 3072:3104],
             jnp.zeros((d, N_PROJ - w.shape[1]), w.dtype)], axis=1).astype(_MXU)
        p = _proj_call(xs, mod, norm1[li].reshape(1, d), w_in_r, n_lat, nl, nb)

        wa2p = jnp.zeros((2, LANES, HEADS * DK), F32)
        wa2p = wa2p.at[0, :GLA_RANK].set(gla_wa2[li, 0]).at[1, GLA_RANK:2 * GLA_RANK].set(gla_wa2[li, 1])
        gla = _la_call("gla", p, (wa2p.astype(_MXU), gla_ba[li].reshape(2, 1, HEADS * DK)),
                       gla_norm[li].reshape(1, HEADS * DV), nb, nl, nc)
        ret = _la_call("ret", p, (cos_t, sin_t), ret_norm[li].reshape(1, HEADS * DV), nb, nl, nc)

        lw = (lru_conv_w[li], lru_conv_b[li].reshape(1, -1), (0.5 * lru_wa[li]).astype(_MXU),
              0.5 * lru_ba[li], (0.5 * lru_wi[li]).astype(_MXU), 0.5 * lru_bi[li], lru_lam[li])
        width = lru_conv_w.shape[2]
        lru_c, h_ctx = _lru_call(p, lru_buf, jnp.zeros((nb, 2, width), F32), lw, nb, nc, n_lat // nc)
        lru, _ = _lru_call(p, lru_c, h_ctx, lw, nb, nl, 0)
        lru_buf = lru

        n_rows = n_lat if last else xs.shape[0]
        wr = jnp.zeros((d, LANES), F32).at[:, :N_EXPERTS].set(w_router[li]).astype(_MXU)
        br = jnp.full((1, LANES), NEG_BIG, F32).at[0, :N_EXPERTS].set(b_router[li])
        xs_mid, xn2, idx, wgt, rank, counts = _finish_call(
            gla, lru, ret, p, w_branch[li].astype(_MXU), w_out[li].astype(_MXU), xs, mod,
            norm2[li].reshape(1, d), wr, br, n_rows, nl, nb)

        out, sorted_buf = _moe(li, xn2, idx, wgt, rank, counts, xs_mid, mod, final_norm.reshape(1, d),
                               w_gu, bgu_r, w_down, bd_r, nl, nb, last, sorted_buf)
        xs = out
    return out.reshape(nb, nl, d)
```

```python
import functools
import math

import numpy as np
import jax
import jax.numpy as jnp
from jax import lax
from jax.experimental import pallas as pl
from jax.experimental.pallas import tpu as pltpu

F32 = jnp.float32
_MXU = jnp.bfloat16

EPS = 1e-6
N_MOD = 6
GRID_W = 64
CHUNK = 64
HEADS = 4
DK = 128
DV = 256
GLA_RANK = 16
GLA_GATE_NORM = 16.0
LRU_BLOCKS = 4
LRU_C = 8.0
ROPE_BASE = 10000.0
N_EXPERTS = 32
TOP_K = 4
SWIGLU_LIMIT = 7.0
SWIGLU_ALPHA = 1.702
LANES = 128
SUBLANES = 8
NEG_BIG = -1e30

C_GLA_Q, C_GLA_K, C_GLA_V, C_GLA_G = 0, 512, 1024, 2048
C_LRU_X, C_LRU_G = 3072, 4096
C_RET_Q, C_RET_K, C_RET_V, C_RET_G = 5120, 5632, 6144, 7168
C_MERGE = 8192
C_LR = 11264
N_PROJ = 11520
PROJ_TN = 3840


def _pick(n, prefs):
    for p in prefs:
        if n % p == 0:
            return p
    raise ValueError(f"no tile for {n} in {prefs}")


def _cparams(sem, vmem_mb):
    return pltpu.CompilerParams(dimension_semantics=sem, vmem_limit_bytes=vmem_mb * 1024 * 1024)


def _dot(a, b):
    return jnp.dot(a, b, preferred_element_type=F32)


def _dot_nt(a, b):
    return lax.dot_general(a, b, (((1,), (1,)), ((), ())), preferred_element_type=F32)


def _dot_tn(a, b):
    return lax.dot_general(a, b, (((0,), (0,)), ((), ())), preferred_element_type=F32)


def _sigmoid(x):
    return 0.5 * jnp.tanh(0.5 * x) + 0.5


def _log_sigmoid(x):
    return jnp.minimum(x, 0.0) - jnp.log(1.0 + jnp.exp(-jnp.abs(x)))


def _silu(x):
    return x * _sigmoid(x)


def _gelu_tanh(x):
    return 0.5 * x * (1.0 + jnp.tanh(math.sqrt(2.0 / math.pi) * (x + 0.044715 * (x * x * x))))


def _rms(x, gain):
    return x * lax.rsqrt(jnp.mean(x * x, axis=-1, keepdims=True) + EPS) * gain


def _dot01_exact(tri, g):
    hi = g.astype(_MXU)
    r1 = g - hi.astype(F32)
    mid = r1.astype(_MXU)
    lo = (r1 - mid.astype(F32)).astype(_MXU)
    return _dot(tri, hi) + _dot(tri, mid) + _dot(tri, lo)


def _ada_kernel(c_ref, w_ref, b_ref, o_ref):
    s = _silu(c_ref[...])
    o_ref[...] = _dot(s.astype(_MXU), w_ref[...]) + b_ref[...]


def _ada_call(cc, w_ada, b_ada):
    depth, d, n = w_ada.shape
    rows = cc.shape[0]
    tn = _pick(n, (1536, 1024, 512, 128))
    return pl.pallas_call(
        _ada_kernel,
        grid=(depth, n // tn),
        in_specs=[pl.BlockSpec((rows, d), lambda l, j: (0, 0)),
                  pl.BlockSpec((None, d, tn), lambda l, j: (l, 0, j)),
                  pl.BlockSpec((None, 1, tn), lambda l, j: (l, 0, j))],
        out_specs=pl.BlockSpec((None, rows, tn), lambda l, j: (l, 0, j)),
        out_shape=jax.ShapeDtypeStruct((depth, rows, n), F32),
        compiler_params=_cparams(("arbitrary", "arbitrary"), 24),
        name="ada",
    )(cc, w_ada, b_ada.reshape(depth, 1, n))


def _proj_kernel(x_ref, mod_ref, g_ref, w_ref, o_ref, xn_ref):
    @pl.when(pl.program_id(1) == 0)
    def _():
        y = _rms(x_ref[...], g_ref[...])
        xn_ref[...] = (y * (1.0 + mod_ref[1:2, :]) + mod_ref[0:1, :]).astype(xn_ref.dtype)

    o_ref[...] = _dot(xn_ref[...], w_ref[...]).astype(o_ref.dtype)


def _proj_call(xs, mod, gain, w_in_r, n_lat_rows, nl, nb):
    t, d = xs.shape
    tm = _pick(math.gcd(t, nl), (1024, 512, 256))
    return pl.pallas_call(
        _proj_kernel,
        grid=(t // tm, N_PROJ // PROJ_TN),
        in_specs=[pl.BlockSpec((tm, d), lambda i, j: (i, 0)),
                  pl.BlockSpec((None, N_MOD, d), lambda i, j: (jnp.minimum(i * tm // nl, nb), 0, 0)),
                  pl.BlockSpec((1, d), lambda i, j: (0, 0)),
                  pl.BlockSpec((d, PROJ_TN), lambda i, j: (0, j))],
        out_specs=pl.BlockSpec((tm, PROJ_TN), lambda i, j: (i, j)),
        out_shape=jax.ShapeDtypeStruct((t, N_PROJ), _MXU),
        scratch_shapes=[pltpu.VMEM((tm, d), _MXU)],
        compiler_params=_cparams(("arbitrary", "arbitrary"), 52),
        name="proj",
    )(xs, mod, gain, w_in_r)


def _la_kernel(*refs, kind, tt, n_ct, n_lt):
    if kind == "gla":
        (q_ref, k_ref, v_ref, gate_ref, lr_ref, wa2_ref, ba_ref, gain_ref,
         out_ref, st_ref, of_ref) = refs
    else:
        (q_ref, k_ref, v_ref, gate_ref, cos_ref, sin_ref, gain_ref,
         out_ref, st_ref, of_ref, dm_ref) = refs
    ph = pl.program_id(1)
    s = pl.program_id(2)
    ck = CHUNK if kind == "gla" else tt
    n_chunks = tt // ck

    @pl.when(s == 0)
    def _():
        st_ref[...] = jnp.zeros_like(st_ref)

    row = lax.broadcasted_iota(jnp.int32, (tt, tt), 0)
    col = lax.broadcasted_iota(jnp.int32, (tt, tt), 1)
    shift = ck.bit_length() - 1
    same_chunk = (row >> shift) == (col >> shift)
    tpos = lax.broadcasted_iota(jnp.int32, (tt, 1), 0).astype(F32)

    if kind == "ret":
        first = jnp.logical_and(pl.program_id(0) == 0, jnp.logical_and(ph == 0, s == 0))

        @pl.when(first)
        def _():
            for h in range(HEADS):
                lg = math.log(1.0 - 2.0 ** (-5.0 - h))
                dm_ref[h] = jnp.where(col <= row, jnp.exp((row - col).astype(F32) * lg), 0.0)
                dm_ref[HEADS + h] = jnp.where(col > row, jnp.exp((col - row).astype(F32) * lg), 0.0)

    def tile_outputs(backward):
        if backward:
            mask = jnp.logical_and(same_chunk, col > row)
            tri = jnp.logical_and(same_chunk, col >= row).astype(_MXU)
        else:
            mask = jnp.logical_and(same_chunk, col <= row)
            tri = jnp.logical_and(same_chunk, col <= row).astype(_MXU)
        order = list(reversed(range(n_chunks))) if backward else list(range(n_chunks))
        if kind == "gla":
            d = 1 if backward else 0
            z = _dot(lr_ref[...], wa2_ref[d]) + ba_ref[d]
            g = _log_sigmoid(z) * (1.0 / GLA_GATE_NORM)
            big_g = _dot01_exact(tri, g)
            lasts = [big_g[c * ck:c * ck + 1, :] if backward else big_g[(c + 1) * ck - 1:(c + 1) * ck, :]
                     for c in range(n_chunks)]
            g_last = jnp.concatenate([jnp.broadcast_to(r, (ck, r.shape[1])) for r in lasts], axis=0)
            e_pos = jnp.exp(big_g)
            e_neg = jnp.exp(-big_g)
            e_end = jnp.exp(g_last - big_g)
            decs = [jnp.exp(r) for r in lasts]
        outs = []
        for h in range(HEADS):
            ks = slice(h * DK, (h + 1) * DK)
            vs = slice(h * DV, (h + 1) * DV)
            qh = q_ref[:, ks].astype(F32)
            kh = k_ref[:, ks].astype(F32)
            vh = v_ref[:, vs]
            if kind == "gla":
                qh = qh * DK ** -0.5
                q_dec = (qh * e_pos[:, ks]).astype(_MXU)
                k_inv = (kh * e_neg[:, ks]).astype(_MXU)
                k_end = (kh * e_end[:, ks]).astype(_MXU)
                sc = jnp.where(mask, _dot_nt(q_dec, k_inv), 0.0)
                dec_h = [dcy[:, ks] for dcy in decs]
            else:
                kh = kh * DK ** -0.5
                cos = cos_ref[...]
                sin = sin_ref[...]
                qh = qh * cos + pltpu.roll(qh, DK // 2, 1) * sin
                kh = kh * cos + pltpu.roll(kh, DK // 2, 1) * sin
                lg = math.log(1.0 - 2.0 ** (-5.0 - h))
                steps = (ck - tpos) if backward else (tpos + 1.0)
                q_dec = (qh * jnp.exp(steps * lg)).astype(_MXU)
                k_end = (kh * jnp.exp((ck - steps) * lg)).astype(_MXU)
                sc = _dot_nt(qh.astype(_MXU), kh.astype(_MXU)) * dm_ref[(HEADS if backward else 0) + h]
                dec_h = [math.exp(ck * lg)] * n_chunks
            o_intra = _dot(sc.astype(_MXU), vh)
            st = st_ref[h]
            parts = [None] * n_chunks
            for c in order:
                rs = slice(c * ck, (c + 1) * ck)
                parts[c] = o_intra[rs] + _dot_nt(q_dec[rs], st.astype(_MXU))
                st = st * dec_h[c] + _dot_tn(vh[rs], k_end[rs])
            st_ref[h] = st
            outs.append(parts[0] if n_chunks == 1 else jnp.concatenate(parts, axis=0))
        return jnp.concatenate(outs, axis=1)

    @pl.when(ph == 0)
    def _():
        base = pl.multiple_of(s * tt, tt)
        of_ref[pl.ds(base, tt), :] = tile_outputs(False)

    @pl.when(ph == 1)
    def _():
        loc = jnp.where(s < n_ct, n_ct - 1 - s, n_ct + n_lt - 1 - (s - n_ct))
        base = pl.multiple_of(loc * tt, tt)
        o = of_ref[pl.ds(base, tt), :] + tile_outputs(True)
        gate = gate_ref[...].astype(F32)
        parts = []
        for h in range(HEADS):
            oh = o[:, h * DV:(h + 1) * DV]
            if kind == "ret":
                oh = oh - jnp.mean(oh, axis=-1, keepdims=True)
            parts.append(oh * lax.rsqrt(jnp.mean(oh * oh, axis=-1, keepdims=True) + EPS))
        normed = jnp.concatenate(parts, axis=1) * gain_ref[...]
        out_ref[...] = (normed * _silu(gate)).astype(out_ref.dtype)


def _la_call(kind, p, extra, gain, nb, nl, nc):
    t = p.shape[0]
    tt = _pick(math.gcd(nl, nc), (256, 128, 64))
    n_ct, n_lt = nc // tt, nl // tt
    lat_tiles = nb * n_lt
    cq, ck, cv, cg = ((C_GLA_Q, C_GLA_K, C_GLA_V, C_GLA_G) if kind == "gla"
                      else (C_RET_Q, C_RET_K, C_RET_V, C_RET_G))
    hd = HEADS * DK
    hv = HEADS * DV

    def loc_of(ph, s):
        back = jnp.where(s < n_ct, n_ct - 1 - s, n_ct + n_lt - 1 - (s - n_ct))
        return jnp.where(ph == 0, s, back)

    def row_blk(b, loc):
        return jnp.where(loc < n_ct, lat_tiles + b * n_ct + loc, b * n_lt + (loc - n_ct))

    def in_map(cblk):
        return lambda b, ph, s: (row_blk(b, loc_of(ph, s)), cblk)

    def second_pass_map(cblk):
        return lambda b, ph, s: (row_blk(b, loc_of(1, jnp.where(ph == 0, 0, s))), cblk)

    in_specs = [pl.BlockSpec((tt, hd), in_map(cq // hd)),
                pl.BlockSpec((tt, hd), in_map(ck // hd)),
                pl.BlockSpec((tt, hv), in_map(cv // hv)),
                pl.BlockSpec((tt, hv), second_pass_map(cg // hv))]
    args = [p, p, p, p]
    if kind == "gla":
        wa2p, ba = extra
        in_specs += [pl.BlockSpec((tt, LANES), in_map(C_LR // LANES)),
                     pl.BlockSpec((2, LANES, hd), lambda b, ph, s: (0, 0, 0)),
                     pl.BlockSpec((2, 1, hd), lambda b, ph, s: (0, 0, 0))]
        args += [p, wa2p, ba]
    else:
        cos_t, sin_t = extra

        def rope_map(b, ph, s):
            loc = loc_of(ph, s)
            return (jnp.where(loc < n_ct, n_lt + loc, loc - n_ct), 0)
        in_specs += [pl.BlockSpec((tt, DK), rope_map), pl.BlockSpec((tt, DK), rope_map)]
        args += [cos_t, sin_t]
    in_specs.append(pl.BlockSpec((1, hv), lambda b, ph, s: (0, 0)))
    args.append(gain)
    return pl.pallas_call(
        functools.partial(_la_kernel, kind=kind, tt=tt, n_ct=n_ct, n_lt=n_lt),
        grid=(nb, 2, n_ct + n_lt),
        in_specs=in_specs,
        out_specs=pl.BlockSpec((tt, hv), second_pass_map(0)),
        out_shape=jax.ShapeDtypeStruct((t, hv), _MXU),
        scratch_shapes=[pltpu.VMEM((HEADS, DV, DK), F32),
                        pltpu.VMEM((nc + nl, hv), F32)]
        + ([pltpu.VMEM((2 * HEADS, tt, tt), F32)] if kind == "ret" else []),
        compiler_params=_cparams(("arbitrary", "arbitrary", "arbitrary"), 48),
        name=kind,
    )(*args)


def _scan_group(a, b, h, reverse):
    row = lax.broadcasted_iota(jnp.int32, a.shape, 0)
    for sft in (1, 2, 4):
        if reverse:
            a_sh = pltpu.roll(a, SUBLANES - sft, 0)
            b_sh = pltpu.roll(b, SUBLANES - sft, 0)
            m = row < SUBLANES - sft
        else:
            a_sh = pltpu.roll(a, sft, 0)
            b_sh = pltpu.roll(b, sft, 0)
            m = row >= sft
        b = jnp.where(m, a * b_sh + b, b)
        a = jnp.where(m, a * a_sh, a)
    hh = a * h + b
    return hh, (hh[0:1, :] if reverse else hh[SUBLANES - 1:SUBLANES, :])


def _lru_kernel(x_ref, gel_ref, cw_ref, cb_ref, wa_ref, ba_ref, wi_ref, bi_ref, lam_ref, h0_ref,
                out_ref, hn_ref, xpad_ref, xc_ref, hf_ref, *, seg, tile):
    n_tiles = seg // tile
    groups = tile // SUBLANES
    cw = cw_ref[...]
    cb = cb_ref[...]
    zeros8 = jnp.zeros((SUBLANES, xpad_ref.shape[1]), F32)
    xpad_ref[0:SUBLANES, :] = zeros8
    xpad_ref[SUBLANES + seg:2 * SUBLANES + seg, :] = zeros8

    def copy_in(i, carry):
        t0 = pl.multiple_of(i * tile, tile)
        xpad_ref[pl.ds(t0 + SUBLANES, tile), :] = x_ref[pl.ds(t0, tile), :].astype(F32)
        return carry
    lax.fori_loop(0, n_tiles, copy_in, 0)

    neg_c = [(-0.5 * LRU_C) * _log_sigmoid(lam_ref[d:d + 1, :]) for d in range(2)]

    def gates(xc, d):
        xb = xc.astype(_MXU)
        tr = jnp.tanh(_dot(xb, wa_ref[d]) + ba_ref[d:d + 1, :])
        ig = 0.5 * jnp.tanh(_dot(xb, wi_ref[d]) + bi_ref[d:d + 1, :]) + 0.5
        nla = neg_c[d] * tr + neg_c[d]
        a = jnp.exp(-nla)
        b = jnp.sqrt(jnp.tanh(nla) * (a * a + 1.0)) * (ig * xc)
        return a, b

    def fwd_tile(i, h):
        t0 = pl.multiple_of(i * tile, tile)
        win = xpad_ref[pl.ds(t0, tile + 2 * SUBLANES), :]
        xc = (cw[0:1, :] * win[6:6 + tile] + cw[1:2, :] * win[7:7 + tile]
              + cw[2:3, :] * win[8:8 + tile] + cw[3:4, :] * win[9:9 + tile] + cb)
        xc_ref[pl.ds(t0, tile), :] = xc
        a, b = gates(xc, 0)
        for g in range(groups):
            rs = slice(g * SUBLANES, (g + 1) * SUBLANES)
            hh, h = _scan_group(a[rs], b[rs], h, False)
            hf_ref[pl.ds(t0 + g * SUBLANES, SUBLANES), :] = hh
        return h
    h_f = lax.fori_loop(0, n_tiles, fwd_tile, h0_ref[0:1, :])

    def bwd_tile(j, h):
        i = n_tiles - 1 - j
        t0 = pl.multiple_of(i * tile, tile)
        xc = xc_ref[pl.ds(t0, tile), :]
        a, b = gates(xc, 1)
        for g in reversed(range(groups)):
            rs = slice(g * SUBLANES, (g + 1) * SUBLANES)
            hh, h = _scan_group(a[rs], b[rs], h, True)
            rows = pl.ds(t0 + g * SUBLANES, SUBLANES)
            hsum = hf_ref[rows, :] + hh
            hf_ref[rows, :] = hsum
        gel = gel_ref[pl.ds(t0, tile), :].astype(F32)
        out_ref[pl.ds(t0, tile), :] = (hf_ref[pl.ds(t0, tile), :] * _gelu_tanh(gel)).astype(out_ref.dtype)
        return h
    h_b = lax.fori_loop(0, n_tiles, bwd_tile, h0_ref[1:2, :])
    hn_ref[0:1, :] = h_f
    hn_ref[1:2, :] = h_b


def _lru_call(p, prev_out, h0, lw, nb, seg, row_blk0):
    cw, cb, wa, ba, wi, bi, lam = lw
    t = p.shape[0]
    width = LRU_BLOCKS * DV
    cbw = width // LRU_BLOCKS
    tile = _pick(seg, (256, 128, 64))
    xblk = C_LRU_X // cbw
    gblk = C_LRU_G // cbw
    in_specs = [pl.BlockSpec((seg, cbw), lambda b, c: (row_blk0 + b, xblk + c)),
                pl.BlockSpec((seg, cbw), lambda b, c: (row_blk0 + b, gblk + c)),
                pl.BlockSpec((4, cbw), lambda b, c: (0, c)),
                pl.BlockSpec((1, cbw), lambda b, c: (0, c)),
                pl.BlockSpec((2, None, cbw, cbw), lambda b, c: (0, c, 0, 0)),
                pl.BlockSpec((2, cbw), lambda b, c: (0, c)),
                pl.BlockSpec((2, None, cbw, cbw), lambda b, c: (0, c, 0, 0)),
                pl.BlockSpec((2, cbw), lambda b, c: (0, c)),
                pl.BlockSpec((2, cbw), lambda b, c: (0, c)),
                pl.BlockSpec((None, 2, cbw), lambda b, c: (b, 0, c))]
    in_specs.append(pl.BlockSpec(memory_space=pl.ANY))
    args = [p, p, cw, cb, wa, ba, wi, bi, lam, h0, prev_out]

    def body(*refs):
        _lru_kernel(*(refs[:10] + refs[11:]), seg=seg, tile=tile)

    return pl.pallas_call(
        body,
        grid=(nb, LRU_BLOCKS),
        in_specs=in_specs,
        out_specs=[pl.BlockSpec((seg, cbw), lambda b, c: (row_blk0 + b, c)),
                   pl.BlockSpec((None, 2, cbw), lambda b, c: (b, 0, c))],
        out_shape=[jax.ShapeDtypeStruct((t, width), _MXU),
                   jax.ShapeDtypeStruct((nb, 2, width), F32)],
        scratch_shapes=[pltpu.VMEM((seg + 2 * SUBLANES, cbw), F32),
                        pltpu.VMEM((seg, cbw), F32),
                        pltpu.VMEM((seg, cbw), F32)],
        input_output_aliases={10: 0},
        compiler_params=_cparams(("arbitrary", "arbitrary"), 48),
        name="lru",
    )(*args)


def _finish_kernel(gla_ref, lru_ref, ret_ref, m0_ref, m1_ref, m2_ref, wb_ref, wo_ref, xs_ref,
                   mod_ref, g2_ref, wr_ref, br_ref, tril_ref,
                   xo_ref, xn_ref, idx_ref, wgt_ref, rank_ref, cnt_ref, carry_ref):
    @pl.when(pl.program_id(0) == 0)
    def _():
        carry_ref[...] = jnp.zeros_like(carry_ref)

    hm = tril_ref.shape[0]
    total = carry_ref[0:1, :]
    for sub in range(xs_ref.shape[0] // hm):
        rs = slice(sub * hm, (sub + 1) * hm)
        merged = (_sigmoid(m0_ref[rs, :].astype(F32)) * _dot(gla_ref[rs, :], wb_ref[0])
                  + _sigmoid(m1_ref[rs, :].astype(F32)) * _dot(lru_ref[rs, :], wb_ref[1])
                  + _sigmoid(m2_ref[rs, :].astype(F32)) * _dot(ret_ref[rs, :], wb_ref[2]))
        x = xs_ref[rs, :] + mod_ref[2:3, :] * _dot(merged.astype(_MXU), wo_ref[...])
        xo_ref[rs, :] = x
        xn = _rms(x, g2_ref[...]) * (1.0 + mod_ref[4:5, :]) + mod_ref[3:4, :]
        xn_ref[rs, :] = xn

        logits = _dot(xn.astype(_MXU), wr_ref[...]) + br_ref[...]
        lane = lax.broadcasted_iota(jnp.int32, logits.shape, 1)
        ids, vals = [], []
        for _ in range(TOP_K):
            m = jnp.max(logits, axis=1, keepdims=True)
            sel = jnp.min(jnp.where(logits == m, lane.astype(F32), float(LANES)), axis=1,
                          keepdims=True).astype(jnp.int32)
            ids.append(sel)
            vals.append(m)
            logits = jnp.where(lane == sel, -jnp.inf, logits)
        ex = [jnp.exp(v - vals[0]) for v in vals]
        denom = ex[0] + ex[1] + ex[2] + ex[3]
        onehot = jnp.zeros(logits.shape, F32)
        for sel in ids:
            onehot = onehot + (lane == sel).astype(F32)
        before = _dot(tril_ref[...], onehot.astype(_MXU)) + total
        idx_o = jnp.zeros(logits.shape, jnp.int32)
        rank_o = jnp.zeros(logits.shape, jnp.int32)
        wgt_o = jnp.zeros(logits.shape, F32)
        for k in range(TOP_K):
            rk = jnp.sum(jnp.where(lane == ids[k], before, 0.0), axis=1, keepdims=True)
            idx_o = jnp.where(lane == k, ids[k], idx_o)
            rank_o = jnp.where(lane == k, rk.astype(jnp.int32), rank_o)
            wgt_o = jnp.where(lane == k, ex[k] / denom, wgt_o)
        idx_ref[rs, :] = idx_o
        rank_ref[rs, :] = rank_o
        wgt_ref[rs, :] = wgt_o
        total = total + jnp.sum(onehot, axis=0, keepdims=True)
    carry_ref[...] = jnp.broadcast_to(total, carry_ref.shape)
    cnt_ref[...] = jnp.broadcast_to(total, cnt_ref.shape)


def _finish_call(gla, lru, ret, p, wb, wo, xs, mod, g2, wr, br, n_rows, nl, nb):
    d = xs.shape[1]
    tm = _pick(math.gcd(n_rows, nl), (512, 256))
    hm = tm
    tril = jnp.asarray(np.tril(np.ones((hm, hm), np.float32), -1), _MXU)
    row = lambda i: (i, 0)
    const2 = lambda i: (0, 0)
    mblk = C_MERGE // d
    in_specs = [pl.BlockSpec((tm, d), row), pl.BlockSpec((tm, d), row), pl.BlockSpec((tm, d), row),
                pl.BlockSpec((tm, d), lambda i: (i, mblk)),
                pl.BlockSpec((tm, d), lambda i: (i, mblk + 1)),
                pl.BlockSpec((tm, d), lambda i: (i, mblk + 2)),
                pl.BlockSpec((3, d, d), lambda i: (0, 0, 0)),
                pl.BlockSpec((d, d), const2),
                pl.BlockSpec((tm, d), row),
                pl.BlockSpec((None, N_MOD, d), lambda i: (jnp.minimum(i * tm // nl, nb), 0, 0)),
                pl.BlockSpec((1, d), const2),
                pl.BlockSpec((d, LANES), const2),
                pl.BlockSpec((1, LANES), const2),
                pl.BlockSpec((hm, hm), const2)]
    out_specs = [pl.BlockSpec((tm, d), row), pl.BlockSpec((tm, d), row),
                 pl.BlockSpec((tm, LANES), row), pl.BlockSpec((tm, LANES), row),
                 pl.BlockSpec((tm, LANES), row), pl.BlockSpec((SUBLANES, LANES), const2)]
    out_shape = [jax.ShapeDtypeStruct((n_rows, d), F32), jax.ShapeDtypeStruct((n_rows, d), F32),
                 jax.ShapeDtypeStruct((n_rows, LANES), jnp.int32),
                 jax.ShapeDtypeStruct((n_rows, LANES), F32),
                 jax.ShapeDtypeStruct((n_rows, LANES), jnp.int32),
                 jax.ShapeDtypeStruct((SUBLANES, LANES), F32)]
    return pl.pallas_call(
        _finish_kernel,
        grid=(n_rows // tm,),
        in_specs=in_specs,
        out_specs=out_specs,
        out_shape=out_shape,
        scratch_shapes=[pltpu.VMEM((SUBLANES, LANES), F32)],
        compiler_params=_cparams(("arbitrary",), 56),
        name="finish",
    )(gla, lru, ret, p, p, p, wb, wo, xs, mod, g2, wr, br, tril)


def _slot_kernel(idx_ref, rank_ref, start_ref, slot_ref):
    idx = idx_ref[...].astype(F32)
    lane = lax.broadcasted_iota(jnp.int32, idx.shape, 1)
    start = start_ref[...]
    out = jnp.zeros(idx.shape, F32)
    for k in range(TOP_K):
        sel = jnp.sum(jnp.where(lane == k, idx, 0.0), axis=1, keepdims=True).astype(jnp.int32)
        st = jnp.sum(jnp.where(lane == sel, start, 0.0), axis=1, keepdims=True)
        out = jnp.where(lane == k, st, out)
    slot_ref[...] = out.astype(jnp.int32) + rank_ref[...]


def _slot_call(idx, rank, pad_start):
    n = idx.shape[0]
    tm = _pick(n, (2048, 1024, 512, 256))
    row = lambda i: (i, 0)
    return pl.pallas_call(
        _slot_kernel,
        grid=(n // tm,),
        in_specs=[pl.BlockSpec((tm, LANES), row), pl.BlockSpec((tm, LANES), row),
                  pl.BlockSpec((1, LANES), lambda i: (0, 0))],
        out_specs=pl.BlockSpec((tm, LANES), row),
        out_shape=jax.ShapeDtypeStruct((n, LANES), jnp.int32),
        compiler_params=_cparams(("arbitrary",), 24),
        name="slots",
    )(idx, rank, pad_start)


def _dispatch_kernel(slot_ref, x_ref, dst_in_ref, dst_ref, sem, *, tm):
    del dst_in_ref

    def issue(g, carry):
        t0 = pl.multiple_of(g * SUBLANES, SUBLANES)
        for u in range(SUBLANES):
            for k in range(TOP_K):
                pltpu.make_async_copy(
                    x_ref.at[pl.ds(t0 + u, 1)],
                    dst_ref.at[pl.ds(slot_ref[0, (t0 + u) * TOP_K + k], 1)],
                    sem).start(priority=k % 2)
        return carry
    lax.fori_loop(0, tm // SUBLANES, issue, 0)
    for k in range(TOP_K):
        pltpu.make_async_copy(x_ref, dst_ref.at[pl.ds(0, tm)], sem).wait()


def _dispatch_call(slots3, xn, zeros_sorted):
    n, d = xn.shape
    tm = slots3.shape[2] // TOP_K
    return pl.pallas_call(
        functools.partial(_dispatch_kernel, tm=tm),
        grid=(n // tm,),
        in_specs=[pl.BlockSpec((None, 1, tm * TOP_K), lambda i: (i, 0, 0), memory_space=pltpu.SMEM),
                  pl.BlockSpec((tm, d), lambda i: (i, 0)),
                  pl.BlockSpec(memory_space=pl.ANY)],
        out_specs=pl.BlockSpec(memory_space=pl.ANY),
        out_shape=jax.ShapeDtypeStruct(zeros_sorted.shape, zeros_sorted.dtype),
        scratch_shapes=[pltpu.SemaphoreType.DMA(())],
        input_output_aliases={2: 0},
        compiler_params=_cparams(("arbitrary",), 24),
        name="dispatch",
    )(slots3, xn, zeros_sorted)


def _experts_kernel(be_ref, nu_ref, x_ref, wgu_ref, bgu_ref, wd_ref, bd_ref, perm_ref, y_ref,
                    wgu_s, wd_s):
    j = pl.program_id(0)
    f = wd_ref.shape[0]
    grp = perm_ref.shape[0]
    half = grp // 2
    active = j < nu_ref[0]
    changed = jnp.logical_or(j == 0, be_ref[j] != be_ref[jnp.maximum(j - 1, 0)])

    @pl.when(jnp.logical_and(active, changed))
    def _():
        for g in range(2 * f // grp):
            cols = slice(g * grp, (g + 1) * grp)
            wgu_s[:, cols] = _dot(wgu_ref[:, cols].astype(_MXU), perm_ref[...]).astype(_MXU)
        wd_s[...] = wd_ref[...].astype(_MXU)

    @pl.when(active)
    def _():
        gu = _dot(x_ref[...].astype(_MXU), wgu_s[...]) + bgu_ref[...]
        acts = []
        for g in range(2 * f // grp):
            gate = jnp.minimum(gu[:, g * grp:g * grp + half], SWIGLU_LIMIT)
            up = jnp.clip(gu[:, g * grp + half:(g + 1) * grp], -SWIGLU_LIMIT, SWIGLU_LIMIT)
            acts.append((gate * _sigmoid(SWIGLU_ALPHA * gate) * (up + 1.0)).astype(_MXU))
        y_ref[...] = _dot(jnp.concatenate(acts, axis=1), wd_s[...]) + bd_ref[...]

    @pl.when(jnp.logical_not(active))
    def _():
        y_ref[...] = jnp.zeros_like(y_ref)


def _experts_call(li, block_e, n_used, x_sorted, w_gu, bgu_r, w_down, bd, bm):
    n_slots, d = x_sorted.shape
    f = w_down.shape[2]
    grp = 2 * LANES
    src = np.concatenate([np.arange(0, grp, 2), np.arange(1, grp, 2)])
    perm_np = np.zeros((grp, grp), np.float32)
    perm_np[src, np.arange(grp)] = 1.0
    perm = jnp.asarray(perm_np, _MXU)

    def xmap(j, be, nu):
        return (jnp.minimum(j, nu[0] - 1), 0)

    def wmap(j, be, nu):
        return (li, be[j], 0, 0)

    grid_spec = pltpu.PrefetchScalarGridSpec(
        num_scalar_prefetch=2,
        grid=(n_slots // bm,),
        in_specs=[pl.BlockSpec((bm, d), xmap),
                  pl.BlockSpec((None, None, d, 2 * f), wmap),
                  pl.BlockSpec((None, None, 1, 2 * f), wmap),
                  pl.BlockSpec((None, None, f, d), wmap),
                  pl.BlockSpec((None, None, 1, d), wmap),
                  pl.BlockSpec((grp, grp), lambda j, be, nu: (0, 0))],
        out_specs=pl.BlockSpec((bm, d), lambda j, be, nu: (j, 0)),
        scratch_shapes=[pltpu.VMEM((d, 2 * f), _MXU), pltpu.VMEM((f, d), _MXU)])
    return pl.pallas_call(
        _experts_kernel,
        grid_spec=grid_spec,
        out_shape=jax.ShapeDtypeStruct((n_slots, d), F32),
        compiler_params=_cparams(("arbitrary",), 56),
        name="experts",
    )(block_e, n_used, x_sorted, w_gu, bgu_r, w_down, bd, perm)


def _combine_kernel(slot_ref, wgt_ref, xs_ref, mod_ref, fin_ref, y_ref, out_ref, buf_ref, sem,
                    *, tm, final):
    def issue(g, carry):
        t0 = pl.multiple_of(g * SUBLANES, SUBLANES)
        for u in range(SUBLANES):
            for k in range(TOP_K):
                pltpu.make_async_copy(
                    y_ref.at[pl.ds(slot_ref[0, (t0 + u) * TOP_K + k], 1)],
                    buf_ref.at[k, pl.ds(t0 + u, 1)], sem).start(priority=k % 2)
        return carry
    lax.fori_loop(0, tm // SUBLANES, issue, 0)
    for k in range(TOP_K):
        pltpu.make_async_copy(y_ref.at[pl.ds(0, tm)], buf_ref.at[k], sem).wait()

    wgt = wgt_ref[...]
    acc = wgt[:, 0:1] * buf_ref[0]
    for k in range(1, TOP_K):
        acc = acc + wgt[:, k:k + 1] * buf_ref[k]
    x = xs_ref[...] + mod_ref[5:6, :] * acc
    out_ref[...] = _rms(x, fin_ref[...]) if final else x


def _combine_call(slots3, wgt, xs, mod, fin, y_sorted, nl, nb, final):
    n, d = xs.shape
    tm = slots3.shape[2] // TOP_K
    return pl.pallas_call(
        functools.partial(_combine_kernel, tm=tm, final=final),
        grid=(n // tm,),
        in_specs=[pl.BlockSpec((None, 1, tm * TOP_K), lambda i: (i, 0, 0), memory_space=pltpu.SMEM),
                  pl.BlockSpec((tm, LANES), lambda i: (i, 0)),
                  pl.BlockSpec((tm, d), lambda i: (i, 0)),
                  pl.BlockSpec((None, N_MOD, d), lambda i: (jnp.minimum(i * tm // nl, nb), 0, 0)),
                  pl.BlockSpec((1, d), lambda i: (0, 0)),
                  pl.BlockSpec(memory_space=pl.ANY)],
        out_specs=pl.BlockSpec((tm, d), lambda i: (i, 0)),
        out_shape=jax.ShapeDtypeStruct((n, d), F32),
        scratch_shapes=[pltpu.VMEM((TOP_K, tm, d), F32), pltpu.SemaphoreType.DMA(())],
        compiler_params=_cparams(("arbitrary",), 32),
        name="combine",
    )(slots3, wgt, xs, mod, fin, y_sorted)


MOE_BM = 512


def _sorted_rows(n_tokens):
    return (-(-(n_tokens * TOP_K) // MOE_BM) + N_EXPERTS) * MOE_BM


def _moe(li, xn, idx, wgt, rank, counts, xs_mid, mod, fin, w_gu, bgu_r, w_down, bd, nl, nb, final,
         sorted_buf):
    n, d = xn.shape
    bm = MOE_BM
    tm = _pick(math.gcd(n, nl), (512, 256, 128))
    n_blocks = sorted_buf.shape[0] // bm
    cnt = counts[0, :N_EXPERTS].astype(jnp.int32)
    padded = (cnt + bm - 1) // bm * bm
    pad_end = jnp.cumsum(padded)
    pad_start = pad_end - padded
    n_used = (pad_end[-1] // bm).astype(jnp.int32)
    blk = jnp.arange(n_blocks, dtype=jnp.int32)
    first_row = jnp.minimum(blk, n_used - 1) * bm
    block_e = jnp.minimum(jnp.sum(pad_end[None, :] <= first_row[:, None], axis=1),
                          N_EXPERTS - 1).astype(jnp.int32)
    start_row = jnp.zeros((1, LANES), F32).at[0, :N_EXPERTS].set(pad_start.astype(F32))
    slots = _slot_call(idx, rank, start_row)
    slots3 = slots[:, :TOP_K].reshape(n // tm, 1, tm * TOP_K)
    x_sorted = _dispatch_call(slots3, xn, sorted_buf)
    y_sorted = _experts_call(li, block_e, n_used.reshape(1), x_sorted, w_gu, bgu_r, w_down, bd, bm)
    return _combine_call(slots3, wgt, xs_mid, mod, fin, y_sorted, nl, nb, final), x_sorted


def _rope_tables(nl, nc):
    rows = nl // GRID_W
    r = np.broadcast_to(np.arange(rows)[:, None], (rows, GRID_W)).reshape(-1).astype(np.float32)
    c = np.broadcast_to(np.arange(GRID_W)[None, :], (rows, GRID_W)).reshape(-1).astype(np.float32)
    n_freq = DK // 4
    inv = (ROPE_BASE ** (-jnp.arange(n_freq, dtype=F32) / n_freq))
    ang = jnp.concatenate([jnp.asarray(r)[:, None] * inv, jnp.asarray(c)[:, None] * inv], axis=-1)
    cos, sin = jnp.cos(ang), jnp.sin(ang)
    cos2 = jnp.concatenate([cos, cos], axis=-1)
    sin2 = jnp.concatenate([-sin, sin], axis=-1)
    cos_t = jnp.concatenate([cos2, jnp.ones((nc, DK), F32)], axis=0)
    sin_t = jnp.concatenate([sin2, jnp.zeros((nc, DK), F32)], axis=0)
    return cos_t, sin_t


def kernel(x, c, ctx, c_ctx, w_ada, b_ada, norm1, norm2, w_in, gla_wa2, gla_ba, gla_norm, lru_conv_w, lru_conv_b, lru_wa, lru_ba, lru_wi, lru_bi, lru_lam, ret_norm, w_branch, w_out, w_router, b_router, w_gu, b_gu, w_down, b_down, final_norm):
    nb, nl, d = x.shape
    nc = ctx.shape[1]
    depth = w_ada.shape[0]
    n_lat = nb * nl
    f = w_down.shape[2]

    xs = jnp.concatenate([x.reshape(n_lat, d), ctx.reshape(nb * nc, d)], axis=0)
    mod_rows = -(-(nb + 1) // SUBLANES) * SUBLANES
    cc = jnp.zeros((mod_rows, d), F32).at[:nb].set(c).at[nb].set(c_ctx)
    mod_all = _ada_call(cc, w_ada.astype(_MXU), b_ada).reshape(depth, mod_rows, N_MOD, d)
    cos_t, sin_t = _rope_tables(nl, nc)
    bgu_r = b_gu.reshape(depth, N_EXPERTS, 2 * f // (2 * LANES), LANES, 2)
    bgu_r = jnp.swapaxes(bgu_r, -1, -2).reshape(depth, N_EXPERTS, 1, 2 * f)
    bd_r = b_down.reshape(depth, N_EXPERTS, 1, d)
    sorted_buf = jnp.zeros((_sorted_rows(xs.shape[0]), d), F32)
    lru_buf = jnp.zeros((xs.shape[0], lru_conv_w.shape[2]), _MXU)

    out = None
    for li in range(depth):
        last = li == depth - 1
        mod = mod_all[li]
        w = w_in[li]
        w_in_r = jnp.concatenate(
            [w[:, :3072], w[:, 3104:], w[:, 3072:3104],
             jnp.zeros((d, N_PROJ - w.shape[1]), w.dtype)], axis=1).astype(_MXU)
        p = _proj_call(xs, mod, norm1[li].reshape(1, d), w_in_r, n_lat, nl, nb)

        wa2p = jnp.zeros((2, LANES, HEADS * DK), F32)
        wa2p = wa2p.at[0, :GLA_RANK].set(gla_wa2[li, 0]).at[1, GLA_RANK:2 * GLA_RANK].set(gla_wa2[li, 1])
        gla = _la_call("gla", p, (wa2p.astype(_MXU), gla_ba[li].reshape(2, 1, HEADS * DK)),
                       gla_norm[li].reshape(1, HEADS * DV), nb, nl, nc)
        ret = _la_call("ret", p, (cos_t, sin_t), ret_norm[li].reshape(1, HEADS * DV), nb, nl, nc)

        lw = (lru_conv_w[li], lru_conv_b[li].reshape(1, -1), (0.5 * lru_wa[li]).astype(_MXU),
              0.5 * lru_ba[li], (0.5 * lru_wi[li]).astype(_MXU), 0.5 * lru_bi[li], lru_lam[li])
        width = lru_conv_w.shape[2]
        lru_c, h_ctx = _lru_call(p, lru_buf, jnp.zeros((nb, 2, width), F32), lw, nb, nc, n_lat // nc)
        lru, _ = _lru_call(p, lru_c, h_ctx, lw, nb, nl, 0)
        lru_buf = lru

        n_rows = n_lat if last else xs.shape[0]
        wr = jnp.zeros((d, LANES), F32).at[:, :N_EXPERTS].set(w_router[li]).astype(_MXU)
        br = jnp.full((1, LANES), NEG_BIG, F32).at[0, :N_EXPERTS].set(b_router[li])
        xs_mid, xn2, idx, wgt, rank, counts = _finish_call(
            gla, lru, ret, p, w_branch[li].astype(_MXU), w_out[li].astype(_MXU), xs, mod,
            norm2[li].reshape(1, d), wr, br, n_rows, nl, nb)

        out, sorted_buf = _moe(li, xn2, idx, wgt, rank, counts, xs_mid, mod, final_norm.reshape(1, d),
                               w_gu, bgu_r, w_down, bd_r, nl, nb, last, sorted_buf)
        xs = out
    return out.reshape(nb, nl, d)
```

```python
import functools
import math

import numpy as np
import jax
import jax.numpy as jnp
from jax import lax
from jax.experimental import pallas as pl
from jax.experimental.pallas import tpu as pltpu

F32 = jnp.float32
_MXU = jnp.bfloat16

EPS = 1e-6
N_MOD = 6
GRID_W = 64
CHUNK = 64
HEADS = 4
DK = 128
DV = 256
GLA_RANK = 16
GLA_GATE_NORM = 16.0
LRU_BLOCKS = 4
LRU_C = 8.0
ROPE_BASE = 10000.0
N_EXPERTS = 32
TOP_K = 4
SWIGLU_LIMIT = 7.0
SWIGLU_ALPHA = 1.702
LANES = 128
SUBLANES = 8
NEG_BIG = -1e30

C_GLA_Q, C_GLA_K, C_GLA_V, C_GLA_G = 0, 512, 1024, 2048
C_LRU_X, C_LRU_G = 3072, 4096
C_RET_Q, C_RET_K, C_RET_V, C_RET_G = 5120, 5632, 6144, 7168
C_MERGE = 8192
C_LR = 11264
N_PROJ = 11520
PROJ_TN = 3840


def _pick(n, prefs):
    for p in prefs:
        if n % p == 0:
            return p
    raise ValueError(f"no tile for {n} in {prefs}")


def _cparams(sem, vmem_mb):
    return pltpu.CompilerParams(dimension_semantics=sem, vmem_limit_bytes=vmem_mb * 1024 * 1024)


def _dot(a, b):
    return jnp.dot(a, b, preferred_element_type=F32)


def _dot_nt(a, b):
    return lax.dot_general(a, b, (((1,), (1,)), ((), ())), preferred_element_type=F32)


def _dot_tn(a, b):
    return lax.dot_general(a, b, (((0,), (0,)), ((), ())), preferred_element_type=F32)


def _sigmoid(x):
    return 0.5 * jnp.tanh(0.5 * x) + 0.5


def _log_sigmoid(x):
    return jnp.minimum(x, 0.0) - jnp.log(1.0 + jnp.exp(-jnp.abs(x)))


def _silu(x):
    return x * _sigmoid(x)


def _gelu_tanh(x):
    return 0.5 * x * (1.0 + jnp.tanh(math.sqrt(2.0 / math.pi) * (x + 0.044715 * (x * x * x))))


def _rms(x, gain):
    return x * lax.rsqrt(jnp.mean(x * x, axis=-1, keepdims=True) + EPS) * gain


def _dot01_exact(tri, g):
    hi = g.astype(_MXU)
    r1 = g - hi.astype(F32)
    mid = r1.astype(_MXU)
    lo = (r1 - mid.astype(F32)).astype(_MXU)
    return _dot(tri, hi) + _dot(tri, mid) + _dot(tri, lo)


def _ada_kernel(c_ref, w_ref, b_ref, o_ref):
    s = _silu(c_ref[...])
    o_ref[...] = _dot(s.astype(_MXU), w_ref[...].astype(_MXU)) + b_ref[...]


def _ada_call(cc, w_ada, b_ada):
    depth, d, n = w_ada.shape
    rows = cc.shape[0]
    tn = _pick(n, (1536, 1024, 512, 128))
    return pl.pallas_call(
        _ada_kernel,
        grid=(depth, n // tn),
        in_specs=[pl.BlockSpec((rows, d), lambda l, j: (0, 0)),
                  pl.BlockSpec((None, d, tn), lambda l, j: (l, 0, j)),
                  pl.BlockSpec((None, 1, tn), lambda l, j: (l, 0, j))],
        out_specs=pl.BlockSpec((None, rows, tn), lambda l, j: (l, 0, j)),
        out_shape=jax.ShapeDtypeStruct((depth, rows, n), F32),
        compiler_params=_cparams(("arbitrary", "arbitrary"), 24),
        name="ada",
    )(cc, w_ada, b_ada.reshape(depth, 1, n))


def _proj_kernel(x_ref, mod_ref, g_ref, w_ref, o_ref, xn_ref):
    @pl.when(pl.program_id(1) == 0)
    def _():
        y = _rms(x_ref[...], g_ref[...])
        xn_ref[...] = (y * (1.0 + mod_ref[1:2, :]) + mod_ref[0:1, :]).astype(xn_ref.dtype)

    o_ref[...] = _dot(xn_ref[...], w_ref[...]).astype(o_ref.dtype)


def _proj_call(xs, mod, gain, w_in_r, nl, nb):
    t, d = xs.shape
    tm = _pick(math.gcd(t, nl), (1024, 512, 256))
    return pl.pallas_call(
        _proj_kernel,
        grid=(t // tm, N_PROJ // PROJ_TN),
        in_specs=[pl.BlockSpec((tm, d), lambda i, j: (i, 0)),
                  pl.BlockSpec((None, N_MOD, d), lambda i, j: (jnp.minimum(i * tm // nl, nb), 0, 0)),
                  pl.BlockSpec((1, d), lambda i, j: (0, 0)),
                  pl.BlockSpec((d, PROJ_TN), lambda i, j: (0, j))],
        out_specs=pl.BlockSpec((tm, PROJ_TN), lambda i, j: (i, j)),
        out_shape=jax.ShapeDtypeStruct((t, N_PROJ), _MXU),
        scratch_shapes=[pltpu.VMEM((tm, d), _MXU)],
        compiler_params=_cparams(("arbitrary", "arbitrary"), 52),
        name="proj",
    )(xs, mod, gain, w_in_r)


def _la_kernel(*refs, kind, tt, n_ct, n_lt):
    if kind == "gla":
        (q_ref, k_ref, v_ref, gate_ref, lr_ref, wa2_ref, ba_ref, gain_ref,
         out_ref, st_ref, of_ref) = refs
    else:
        (q_ref, k_ref, v_ref, gate_ref, cos_ref, sin_ref, gain_ref,
         out_ref, st_ref, of_ref, dm_ref) = refs
    ph = pl.program_id(1)
    s = pl.program_id(2)
    ck = CHUNK if kind == "gla" else tt
    n_chunks = tt // ck

    @pl.when(s == 0)
    def _():
        st_ref[...] = jnp.zeros_like(st_ref)

    row = lax.broadcasted_iota(jnp.int32, (tt, tt), 0)
    col = lax.broadcasted_iota(jnp.int32, (tt, tt), 1)
    shift = ck.bit_length() - 1
    same_chunk = (row >> shift) == (col >> shift)
    tpos = lax.broadcasted_iota(jnp.int32, (tt, 1), 0).astype(F32)

    if kind == "ret":
        first = jnp.logical_and(pl.program_id(0) == 0, jnp.logical_and(ph == 0, s == 0))

        @pl.when(first)
        def _():
            for h in range(HEADS):
                lg = math.log(1.0 - 2.0 ** (-5.0 - h))
                dm_ref[h] = jnp.where(col <= row, jnp.exp((row - col).astype(F32) * lg), 0.0)
                dm_ref[HEADS + h] = jnp.where(col > row, jnp.exp((col - row).astype(F32) * lg), 0.0)

    def tile_outputs(backward):
        if backward:
            mask = jnp.logical_and(same_chunk, col > row)
            tri = jnp.logical_and(same_chunk, col >= row).astype(_MXU)
        else:
            mask = jnp.logical_and(same_chunk, col <= row)
            tri = jnp.logical_and(same_chunk, col <= row).astype(_MXU)
        order = list(reversed(range(n_chunks))) if backward else list(range(n_chunks))
        if kind == "gla":
            d = 1 if backward else 0
            z = _dot(lr_ref[...], wa2_ref[d]) + ba_ref[d]
            g = _log_sigmoid(z) * (1.0 / GLA_GATE_NORM)
            big_g = _dot01_exact(tri, g)
            lasts = [big_g[c * ck:c * ck + 1, :] if backward else big_g[(c + 1) * ck - 1:(c + 1) * ck, :]
                     for c in range(n_chunks)]
            g_last = jnp.concatenate([jnp.broadcast_to(r, (ck, r.shape[1])) for r in lasts], axis=0)
            e_pos = jnp.exp(big_g)
            e_neg = jnp.exp(-big_g)
            e_end = jnp.exp(g_last - big_g)
            decs = [jnp.exp(r) for r in lasts]
        outs = []
        for h in range(HEADS):
            ks = slice(h * DK, (h + 1) * DK)
            vs = slice(h * DV, (h + 1) * DV)
            qh = q_ref[:, ks].astype(F32)
            kh = k_ref[:, ks].astype(F32)
            vh = v_ref[:, vs]
            if kind == "gla":
                qh = qh * DK ** -0.5
                q_dec = (qh * e_pos[:, ks]).astype(_MXU)
                k_inv = (kh * e_neg[:, ks]).astype(_MXU)
                k_end = (kh * e_end[:, ks]).astype(_MXU)
                sc = jnp.where(mask, _dot_nt(q_dec, k_inv), 0.0)
                dec_h = [dcy[:, ks] for dcy in decs]
            else:
                kh = kh * DK ** -0.5
                cos = cos_ref[...]
                sin = sin_ref[...]
                qh = qh * cos + pltpu.roll(qh, DK // 2, 1) * sin
                kh = kh * cos + pltpu.roll(kh, DK // 2, 1) * sin
                lg = math.log(1.0 - 2.0 ** (-5.0 - h))
                steps = (ck - tpos) if backward else (tpos + 1.0)
                q_dec = (qh * jnp.exp(steps * lg)).astype(_MXU)
                k_end = (kh * jnp.exp((ck - steps) * lg)).astype(_MXU)
                sc = _dot_nt(qh.astype(_MXU), kh.astype(_MXU)) * dm_ref[(HEADS if backward else 0) + h]
                dec_h = [math.exp(ck * lg)] * n_chunks
            o_intra = _dot(sc.astype(_MXU), vh)
            st = st_ref[h]
            parts = [None] * n_chunks
            for c in order:
                rs = slice(c * ck, (c + 1) * ck)
                parts[c] = o_intra[rs] + _dot_nt(q_dec[rs], st.astype(_MXU))
                st = st * dec_h[c] + _dot_tn(vh[rs], k_end[rs])
            st_ref[h] = st
            outs.append(parts[0] if n_chunks == 1 else jnp.concatenate(parts, axis=0))
        return jnp.concatenate(outs, axis=1)

    @pl.when(ph == 0)
    def _():
        base = pl.multiple_of(s * tt, tt)
        of_ref[pl.ds(base, tt), :] = tile_outputs(False)

    @pl.when(ph == 1)
    def _():
        loc = jnp.where(s < n_ct, n_ct - 1 - s, n_ct + n_lt - 1 - (s - n_ct))
        base = pl.multiple_of(loc * tt, tt)
        o = of_ref[pl.ds(base, tt), :] + tile_outputs(True)
        gate = gate_ref[...].astype(F32)
        parts = []
        for h in range(HEADS):
            oh = o[:, h * DV:(h + 1) * DV]
            if kind == "ret":
                oh = oh - jnp.mean(oh, axis=-1, keepdims=True)
            parts.append(oh * lax.rsqrt(jnp.mean(oh * oh, axis=-1, keepdims=True) + EPS))
        normed = jnp.concatenate(parts, axis=1) * gain_ref[...]
        out_ref[...] = (normed * (gate * (jnp.tanh(gate) + 1.0))).astype(out_ref.dtype)


def _la_call(kind, p, extra, gain, nb, nl, nc):
    t = p.shape[0]
    tt = _pick(math.gcd(nl, nc), (256, 128, 64))
    n_ct, n_lt = nc // tt, nl // tt
    lat_tiles = nb * n_lt
    cq, ck, cv, cg = ((C_GLA_Q, C_GLA_K, C_GLA_V, C_GLA_G) if kind == "gla"
                      else (C_RET_Q, C_RET_K, C_RET_V, C_RET_G))
    hd = HEADS * DK
    hv = HEADS * DV

    def loc_of(ph, s):
        back = jnp.where(s < n_ct, n_ct - 1 - s, n_ct + n_lt - 1 - (s - n_ct))
        return jnp.where(ph == 0, s, back)

    def row_blk(b, loc):
        return jnp.where(loc < n_ct, lat_tiles + b * n_ct + loc, b * n_lt + (loc - n_ct))

    def in_map(cblk):
        return lambda b, ph, s: (row_blk(b, loc_of(ph, s)), cblk)

    def second_pass_map(cblk):
        return lambda b, ph, s: (row_blk(b, loc_of(1, jnp.where(ph == 0, 0, s))), cblk)

    in_specs = [pl.BlockSpec((tt, hd), in_map(cq // hd)),
                pl.BlockSpec((tt, hd), in_map(ck // hd)),
                pl.BlockSpec((tt, hv), in_map(cv // hv)),
                pl.BlockSpec((tt, hv), second_pass_map(cg // hv))]
    args = [p, p, p, p]
    if kind == "gla":
        wa2p, ba = extra
        in_specs += [pl.BlockSpec((tt, LANES), in_map(C_LR // LANES)),
                     pl.BlockSpec((2, LANES, hd), lambda b, ph, s: (0, 0, 0)),
                     pl.BlockSpec((2, 1, hd), lambda b, ph, s: (0, 0, 0))]
        args += [p, wa2p, ba]
    else:
        cos_t, sin_t = extra

        def rope_map(b, ph, s):
            loc = loc_of(ph, s)
            return (jnp.where(loc < n_ct, n_lt + loc, loc - n_ct), 0)
        in_specs += [pl.BlockSpec((tt, DK), rope_map), pl.BlockSpec((tt, DK), rope_map)]
        args += [cos_t, sin_t]
    in_specs.append(pl.BlockSpec((1, hv), lambda b, ph, s: (0, 0)))
    args.append(gain)
    return pl.pallas_call(
        functools.partial(_la_kernel, kind=kind, tt=tt, n_ct=n_ct, n_lt=n_lt),
        grid=(nb, 2, n_ct + n_lt),
        in_specs=in_specs,
        out_specs=pl.BlockSpec((tt, hv), second_pass_map(0)),
        out_shape=jax.ShapeDtypeStruct((t, hv), _MXU),
        scratch_shapes=[pltpu.VMEM((HEADS, DV, DK), F32),
                        pltpu.VMEM((nc + nl, hv), F32)]
        + ([pltpu.VMEM((2 * HEADS, tt, tt), F32)] if kind == "ret" else []),
        compiler_params=_cparams(("arbitrary", "arbitrary", "arbitrary"), 48),
        name=kind,
    )(*args)


def _scan_group(a, b, h, reverse):
    row = lax.broadcasted_iota(jnp.int32, a.shape, 0)
    for sft in (1, 2, 4):
        if reverse:
            a_sh = pltpu.roll(a, SUBLANES - sft, 0)
            b_sh = pltpu.roll(b, SUBLANES - sft, 0)
            m = row < SUBLANES - sft
        else:
            a_sh = pltpu.roll(a, sft, 0)
            b_sh = pltpu.roll(b, sft, 0)
            m = row >= sft
        b = jnp.where(m, a * b_sh + b, b)
        a = jnp.where(m, a * a_sh, a)
    hh = a * h + b
    return hh, (hh[0:1, :] if reverse else hh[SUBLANES - 1:SUBLANES, :])


def _lru_kernel(x_ref, gel_ref, cw_ref, cb_ref, wa_ref, ba_ref, wi_ref, bi_ref, lam_ref, h0_ref,
                out_ref, hn_ref, xpad_ref, xc_ref, hf_ref, *, seg, tile):
    n_tiles = seg // tile
    groups = tile // SUBLANES
    cw = cw_ref[...]
    cb = cb_ref[...]
    zeros8 = jnp.zeros((SUBLANES, xpad_ref.shape[1]), F32)
    xpad_ref[0:SUBLANES, :] = zeros8
    xpad_ref[SUBLANES + seg:2 * SUBLANES + seg, :] = zeros8

    def copy_in(i, carry):
        t0 = pl.multiple_of(i * tile, tile)
        xpad_ref[pl.ds(t0 + SUBLANES, tile), :] = x_ref[pl.ds(t0, tile), :].astype(F32)
        return carry
    lax.fori_loop(0, n_tiles, copy_in, 0)

    neg_c = [(-0.5 * LRU_C) * _log_sigmoid(lam_ref[d:d + 1, :]) for d in range(2)]

    def gates(xc, d):
        xb = xc.astype(_MXU)
        tr = jnp.tanh(_dot(xb, wa_ref[d]) + ba_ref[d:d + 1, :])
        ig = 0.5 * jnp.tanh(_dot(xb, wi_ref[d]) + bi_ref[d:d + 1, :]) + 0.5
        nla = neg_c[d] * tr + neg_c[d]
        a = jnp.exp(-nla)
        b = jnp.sqrt(jnp.tanh(nla) * (a * a + 1.0)) * (ig * xc)
        return a, b

    def fwd_tile(i, h):
        t0 = pl.multiple_of(i * tile, tile)
        win = xpad_ref[pl.ds(t0, tile + 2 * SUBLANES), :]
        xc = (cw[0:1, :] * win[6:6 + tile] + cw[1:2, :] * win[7:7 + tile]
              + cw[2:3, :] * win[8:8 + tile] + cw[3:4, :] * win[9:9 + tile] + cb)
        xc_ref[pl.ds(t0, tile), :] = xc
        a, b = gates(xc, 0)
        for g in range(groups):
            rs = slice(g * SUBLANES, (g + 1) * SUBLANES)
            hh, h = _scan_group(a[rs], b[rs], h, False)
            hf_ref[pl.ds(t0 + g * SUBLANES, SUBLANES), :] = hh
        return h
    h_f = lax.fori_loop(0, n_tiles, fwd_tile, h0_ref[0:1, :])

    def bwd_tile(j, h):
        i = n_tiles - 1 - j
        t0 = pl.multiple_of(i * tile, tile)
        xc = xc_ref[pl.ds(t0, tile), :]
        a, b = gates(xc, 1)
        for g in reversed(range(groups)):
            rs = slice(g * SUBLANES, (g + 1) * SUBLANES)
            hh, h = _scan_group(a[rs], b[rs], h, True)
            rows = pl.ds(t0 + g * SUBLANES, SUBLANES)
            hsum = hf_ref[rows, :] + hh
            hf_ref[rows, :] = hsum
        gel = gel_ref[pl.ds(t0, tile), :].astype(F32)
        out_ref[pl.ds(t0, tile), :] = (hf_ref[pl.ds(t0, tile), :] * _gelu_tanh(gel)).astype(out_ref.dtype)
        return h
    h_b = lax.fori_loop(0, n_tiles, bwd_tile, h0_ref[1:2, :])
    hn_ref[0:1, :] = h_f
    hn_ref[1:2, :] = h_b


def _lru_call(p, prev_out, h0, lw, nb, seg, row_blk0):
    cw, cb, wa, ba, wi, bi, lam = lw
    t = p.shape[0]
    width = LRU_BLOCKS * DV
    cbw = width // LRU_BLOCKS
    tile = _pick(seg, (512, 256, 128, 64))
    xblk = C_LRU_X // cbw
    gblk = C_LRU_G // cbw
    in_specs = [pl.BlockSpec((seg, cbw), lambda b, c: (row_blk0 + b, xblk + c)),
                pl.BlockSpec((seg, cbw), lambda b, c: (row_blk0 + b, gblk + c)),
                pl.BlockSpec((4, cbw), lambda b, c: (0, c)),
                pl.BlockSpec((1, cbw), lambda b, c: (0, c)),
                pl.BlockSpec((2, None, cbw, cbw), lambda b, c: (0, c, 0, 0)),
                pl.BlockSpec((2, cbw), lambda b, c: (0, c)),
                pl.BlockSpec((2, None, cbw, cbw), lambda b, c: (0, c, 0, 0)),
                pl.BlockSpec((2, cbw), lambda b, c: (0, c)),
                pl.BlockSpec((2, cbw), lambda b, c: (0, c)),
                pl.BlockSpec((None, 2, cbw), lambda b, c: (b, 0, c))]
    in_specs.append(pl.BlockSpec(memory_space=pl.ANY))
    args = [p, p, cw, cb, wa, ba, wi, bi, lam, h0, prev_out]

    def body(*refs):
        _lru_kernel(*(refs[:10] + refs[11:]), seg=seg, tile=tile)

    return pl.pallas_call(
        body,
        grid=(nb, LRU_BLOCKS),
        in_specs=in_specs,
        out_specs=[pl.BlockSpec((seg, cbw), lambda b, c: (row_blk0 + b, c)),
                   pl.BlockSpec((None, 2, cbw), lambda b, c: (b, 0, c))],
        out_shape=[jax.ShapeDtypeStruct((t, width), _MXU),
                   jax.ShapeDtypeStruct((nb, 2, width), F32)],
        scratch_shapes=[pltpu.VMEM((seg + 2 * SUBLANES, cbw), F32),
                        pltpu.VMEM((seg, cbw), F32),
                        pltpu.VMEM((seg, cbw), F32)],
        input_output_aliases={10: 0},
        compiler_params=_cparams(("arbitrary", "arbitrary"), 48),
        name="lru",
    )(*args)


def _finish_kernel(gla_ref, lru_ref, ret_ref, m0_ref, m1_ref, m2_ref, wb_ref, wo_ref, xs_ref,
                   mod_ref, g2_ref, wr_ref, br_ref, tril_ref,
                   xo_ref, xn_ref, idx_ref, wgt_ref, rank_ref, cnt_ref, carry_ref):
    @pl.when(pl.program_id(0) == 0)
    def _():
        carry_ref[...] = jnp.zeros_like(carry_ref)

    hm = tril_ref.shape[0]
    total = carry_ref[0:1, :]
    for sub in range(xs_ref.shape[0] // hm):
        rs = slice(sub * hm, (sub + 1) * hm)
        merged = ((0.5 * jnp.tanh(m0_ref[rs, :].astype(F32)) + 0.5) * _dot(gla_ref[rs, :], wb_ref[0])
                  + (0.5 * jnp.tanh(m1_ref[rs, :].astype(F32)) + 0.5) * _dot(lru_ref[rs, :], wb_ref[1])
                  + (0.5 * jnp.tanh(m2_ref[rs, :].astype(F32)) + 0.5) * _dot(ret_ref[rs, :], wb_ref[2]))
        x = xs_ref[rs, :] + mod_ref[2:3, :] * _dot(merged.astype(_MXU), wo_ref[...])
        xo_ref[rs, :] = x
        xn = _rms(x, g2_ref[...]) * (1.0 + mod_ref[4:5, :]) + mod_ref[3:4, :]
        xn_ref[rs, :] = xn

        logits = _dot(xn.astype(_MXU), wr_ref[...]) + br_ref[...]
        lane = lax.broadcasted_iota(jnp.int32, logits.shape, 1)
        ids, vals = [], []
        for _ in range(TOP_K):
            m = jnp.max(logits, axis=1, keepdims=True)
            sel = jnp.min(jnp.where(logits == m, lane.astype(F32), float(LANES)), axis=1,
                          keepdims=True).astype(jnp.int32)
            ids.append(sel)
            vals.append(m)
            logits = jnp.where(lane == sel, -jnp.inf, logits)
        ex = [jnp.exp(v - vals[0]) for v in vals]
        denom = ex[0] + ex[1] + ex[2] + ex[3]
        onehot = jnp.zeros(logits.shape, F32)
        for sel in ids:
            onehot = onehot + (lane == sel).astype(F32)
        before = _dot(tril_ref[...], onehot.astype(_MXU)) + total
        idx_o = jnp.zeros(logits.shape, jnp.int32)
        rank_o = jnp.zeros(logits.shape, jnp.int32)
        wgt_o = jnp.zeros(logits.shape, F32)
        for k in range(TOP_K):
            rk = jnp.sum(jnp.where(lane == ids[k], before, 0.0), axis=1, keepdims=True)
            idx_o = jnp.where(lane == k, ids[k], idx_o)
            rank_o = jnp.where(lane == k, rk.astype(jnp.int32), rank_o)
            wgt_o = jnp.where(lane == k, ex[k] / denom, wgt_o)
        idx_ref[rs, :] = idx_o
        rank_ref[rs, :] = rank_o
        wgt_ref[rs, :] = wgt_o
        total = total + jnp.sum(onehot, axis=0, keepdims=True)
    carry_ref[...] = jnp.broadcast_to(total, carry_ref.shape)
    cnt_ref[...] = jnp.broadcast_to(total, cnt_ref.shape)


def _finish_call(gla, lru, ret, p, wb, wo, xs, mod, g2, wr, br, n_rows, nl, nb):
    d = xs.shape[1]
    tm = _pick(math.gcd(n_rows, nl), (512, 256))
    hm = tm
    tril = jnp.asarray(np.tril(np.ones((hm, hm), np.float32), -1), _MXU)
    row = lambda i: (i, 0)
    const2 = lambda i: (0, 0)
    mblk = C_MERGE // d
    in_specs = [pl.BlockSpec((tm, d), row), pl.BlockSpec((tm, d), row), pl.BlockSpec((tm, d), row),
                pl.BlockSpec((tm, d), lambda i: (i, mblk)),
                pl.BlockSpec((tm, d), lambda i: (i, mblk + 1)),
                pl.BlockSpec((tm, d), lambda i: (i, mblk + 2)),
                pl.BlockSpec((3, d, d), lambda i: (0, 0, 0)),
                pl.BlockSpec((d, d), const2),
                pl.BlockSpec((tm, d), row),
                pl.BlockSpec((None, N_MOD, d), lambda i: (jnp.minimum(i * tm // nl, nb), 0, 0)),
                pl.BlockSpec((1, d), const2),
                pl.BlockSpec((d, LANES), const2),
                pl.BlockSpec((1, LANES), const2),
                pl.BlockSpec((hm, hm), const2)]
    out_specs = [pl.BlockSpec((tm, d), row), pl.BlockSpec((tm, d), row),
                 pl.BlockSpec((tm, LANES), row), pl.BlockSpec((tm, LANES), row),
                 pl.BlockSpec((tm, LANES), row), pl.BlockSpec((SUBLANES, LANES), const2)]
    out_shape = [jax.ShapeDtypeStruct((n_rows, d), F32), jax.ShapeDtypeStruct((n_rows, d), F32),
                 jax.ShapeDtypeStruct((n_rows, LANES), jnp.int32),
                 jax.ShapeDtypeStruct((n_rows, LANES), F32),
                 jax.ShapeDtypeStruct((n_rows, LANES), jnp.int32),
                 jax.ShapeDtypeStruct((SUBLANES, LANES), F32)]
    return pl.pallas_call(
        _finish_kernel,
        grid=(n_rows // tm,),
        in_specs=in_specs,
        out_specs=out_specs,
        out_shape=out_shape,
        scratch_shapes=[pltpu.VMEM((SUBLANES, LANES), F32)],
        compiler_params=_cparams(("arbitrary",), 56),
        name="finish",
    )(gla, lru, ret, p, p, p, wb, wo, xs, mod, g2, wr, br, tril)


def _slot_kernel(idx_ref, rank_ref, start_ref, slot_ref):
    idx = idx_ref[...].astype(F32)
    lane = lax.broadcasted_iota(jnp.int32, idx.shape, 1)
    start = start_ref[...]
    out = jnp.zeros(idx.shape, F32)
    for k in range(TOP_K):
        sel = jnp.sum(jnp.where(lane == k, idx, 0.0), axis=1, keepdims=True).astype(jnp.int32)
        st = jnp.sum(jnp.where(lane == sel, start, 0.0), axis=1, keepdims=True)
        out = jnp.where(lane == k, st, out)
    slot_ref[...] = out.astype(jnp.int32) + rank_ref[...]


def _slot_call(idx, rank, pad_start):
    n = idx.shape[0]
    tm = _pick(n, (2048, 1024, 512, 256))
    row = lambda i: (i, 0)
    return pl.pallas_call(
        _slot_kernel,
        grid=(n // tm,),
        in_specs=[pl.BlockSpec((tm, LANES), row), pl.BlockSpec((tm, LANES), row),
                  pl.BlockSpec((1, LANES), lambda i: (0, 0))],
        out_specs=pl.BlockSpec((tm, LANES), row),
        out_shape=jax.ShapeDtypeStruct((n, LANES), jnp.int32),
        compiler_params=_cparams(("arbitrary",), 24),
        name="slots",
    )(idx, rank, pad_start)


def _dispatch_kernel(slot_ref, x_ref, dst_in_ref, dst_ref, sem, *, tm):
    del dst_in_ref

    def issue(g, carry):
        t0 = pl.multiple_of(g * SUBLANES, SUBLANES)
        for u in range(SUBLANES):
            for k in range(TOP_K):
                pltpu.make_async_copy(
                    x_ref.at[pl.ds(t0 + u, 1)],
                    dst_ref.at[pl.ds(slot_ref[0, (t0 + u) * TOP_K + k], 1)],
                    sem).start(priority=k % 2)
        return carry
    lax.fori_loop(0, tm // SUBLANES, issue, 0)
    for k in range(TOP_K):
        pltpu.make_async_copy(x_ref, dst_ref.at[pl.ds(0, tm)], sem).wait()


def _dispatch_call(slots3, xn, zeros_sorted):
    n, d = xn.shape
    tm = slots3.shape[2] // TOP_K
    return pl.pallas_call(
        functools.partial(_dispatch_kernel, tm=tm),
        grid=(n // tm,),
        in_specs=[pl.BlockSpec((None, 1, tm * TOP_K), lambda i: (i, 0, 0), memory_space=pltpu.SMEM),
                  pl.BlockSpec((tm, d), lambda i: (i, 0)),
                  pl.BlockSpec(memory_space=pl.ANY)],
        out_specs=pl.BlockSpec(memory_space=pl.ANY),
        out_shape=jax.ShapeDtypeStruct(zeros_sorted.shape, zeros_sorted.dtype),
        scratch_shapes=[pltpu.SemaphoreType.DMA(())],
        input_output_aliases={2: 0},
        compiler_params=_cparams(("arbitrary",), 24),
        name="dispatch",
    )(slots3, xn, zeros_sorted)


def _experts_kernel(be_ref, nu_ref, x_ref, wgu_ref, bgu_ref, wd_ref, bd_ref, perm_ref, y_ref,
                    wgu_s, wd_s):
    j = pl.program_id(0)
    f = wd_ref.shape[0]
    grp = perm_ref.shape[0]
    half = grp // 2
    active = j < nu_ref[0]
    changed = jnp.logical_or(j == 0, be_ref[j] != be_ref[jnp.maximum(j - 1, 0)])

    @pl.when(jnp.logical_and(active, changed))
    def _():
        for g in range(2 * f // grp):
            cols = slice(g * grp, (g + 1) * grp)
            wgu_s[:, cols] = _dot(wgu_ref[:, cols].astype(_MXU), perm_ref[...]).astype(_MXU)
        wd_s[...] = wd_ref[...].astype(_MXU)

    @pl.when(active)
    def _():
        gu = _dot(x_ref[...].astype(_MXU), wgu_s[...]) + bgu_ref[...]
        acts = []
        for g in range(2 * f // grp):
            gate = jnp.minimum(gu[:, g * grp:g * grp + half], SWIGLU_LIMIT)
            up = jnp.clip(gu[:, g * grp + half:(g + 1) * grp], -SWIGLU_LIMIT, SWIGLU_LIMIT)
            acts.append((gate * _sigmoid(SWIGLU_ALPHA * gate) * (up + 1.0)).astype(_MXU))
        y_ref[...] = _dot(jnp.concatenate(acts, axis=1), wd_s[...]) + bd_ref[...]

    @pl.when(jnp.logical_not(active))
    def _():
        y_ref[...] = jnp.zeros_like(y_ref)


def _experts_call(li, block_e, n_used, x_sorted, w_gu, bgu_r, w_down, bd, bm):
    n_slots, d = x_sorted.shape
    f = w_down.shape[2]
    grp = 2 * LANES
    src = np.concatenate([np.arange(0, grp, 2), np.arange(1, grp, 2)])
    perm_np = np.zeros((grp, grp), np.float32)
    perm_np[src, np.arange(grp)] = 1.0
    perm = jnp.asarray(perm_np, _MXU)

    def xmap(j, be, nu):
        return (jnp.minimum(j, nu[0] - 1), 0)

    def wmap(j, be, nu):
        return (li, be[j], 0, 0)

    grid_spec = pltpu.PrefetchScalarGridSpec(
        num_scalar_prefetch=2,
        grid=(n_slots // bm,),
        in_specs=[pl.BlockSpec((bm, d), xmap),
                  pl.BlockSpec((None, None, d, 2 * f), wmap),
                  pl.BlockSpec((None, None, 1, 2 * f), wmap),
                  pl.BlockSpec((None, None, f, d), wmap),
                  pl.BlockSpec((None, None, 1, d), wmap),
                  pl.BlockSpec((grp, grp), lambda j, be, nu: (0, 0))],
        out_specs=pl.BlockSpec((bm, d), lambda j, be, nu: (j, 0)),
        scratch_shapes=[pltpu.VMEM((d, 2 * f), _MXU), pltpu.VMEM((f, d), _MXU)])
    return pl.pallas_call(
        _experts_kernel,
        grid_spec=grid_spec,
        out_shape=jax.ShapeDtypeStruct((n_slots, d), F32),
        compiler_params=_cparams(("arbitrary",), 56),
        name="experts",
    )(block_e, n_used, x_sorted, w_gu, bgu_r, w_down, bd, perm)


def _combine_kernel(slot_ref, wgt_ref, xs_ref, mod_ref, fin_ref, y_ref, out_ref, buf_ref, sem,
                    *, tm, final):
    def issue(g, carry):
        t0 = pl.multiple_of(g * SUBLANES, SUBLANES)
        for u in range(SUBLANES):
            for k in range(TOP_K):
                pltpu.make_async_copy(
                    y_ref.at[pl.ds(slot_ref[0, (t0 + u) * TOP_K + k], 1)],
                    buf_ref.at[k, pl.ds(t0 + u, 1)], sem).start(priority=k % 2)
        return carry
    lax.fori_loop(0, tm // SUBLANES, issue, 0)
    for k in range(TOP_K):
        pltpu.make_async_copy(y_ref.at[pl.ds(0, tm)], buf_ref.at[k], sem).wait()

    wgt = wgt_ref[...]
    acc = wgt[:, 0:1] * buf_ref[0]
    for k in range(1, TOP_K):
        acc = acc + wgt[:, k:k + 1] * buf_ref[k]
    x = xs_ref[...] + mod_ref[5:6, :] * acc
    out_ref[...] = _rms(x, fin_ref[...]) if final else x


def _combine_call(slots3, wgt, xs, mod, fin, y_sorted, nl, nb, final):
    n, d = xs.shape
    tm = slots3.shape[2] // TOP_K
    return pl.pallas_call(
        functools.partial(_combine_kernel, tm=tm, final=final),
        grid=(n // tm,),
        in_specs=[pl.BlockSpec((None, 1, tm * TOP_K), lambda i: (i, 0, 0), memory_space=pltpu.SMEM),
                  pl.BlockSpec((tm, LANES), lambda i: (i, 0)),
                  pl.BlockSpec((tm, d), lambda i: (i, 0)),
                  pl.BlockSpec((None, N_MOD, d), lambda i: (jnp.minimum(i * tm // nl, nb), 0, 0)),
                  pl.BlockSpec((1, d), lambda i: (0, 0)),
                  pl.BlockSpec(memory_space=pl.ANY)],
        out_specs=pl.BlockSpec((tm, d), lambda i: (i, 0)),
        out_shape=jax.ShapeDtypeStruct((n, d), F32),
        scratch_shapes=[pltpu.VMEM((TOP_K, tm, d), F32), pltpu.SemaphoreType.DMA(())],
        compiler_params=_cparams(("arbitrary",), 32),
        name="combine",
    )(slots3, wgt, xs, mod, fin, y_sorted)


MOE_BM = 512


def _sorted_rows(n_tokens):
    return (-(-(n_tokens * TOP_K) // MOE_BM) + N_EXPERTS) * MOE_BM


def _moe(li, xn, idx, wgt, rank, counts, xs_mid, mod, fin, w_gu, bgu_r, w_down, bd, nl, nb, final,
         sorted_buf):
    n = xn.shape[0]
    bm = MOE_BM
    tm = _pick(math.gcd(n, nl), (512, 256, 128))
    n_blocks = sorted_buf.shape[0] // bm
    cnt = counts[0, :N_EXPERTS].astype(jnp.int32)
    padded = (cnt + bm - 1) // bm * bm
    pad_end = jnp.cumsum(padded)
    pad_start = pad_end - padded
    n_used = (pad_end[-1] // bm).astype(jnp.int32)
    blk = jnp.arange(n_blocks, dtype=jnp.int32)
    first_row = jnp.minimum(blk, n_used - 1) * bm
    block_e = jnp.minimum(jnp.sum(pad_end[None, :] <= first_row[:, None], axis=1),
                          N_EXPERTS - 1).astype(jnp.int32)
    start_row = jnp.zeros((1, LANES), F32).at[0, :N_EXPERTS].set(pad_start.astype(F32))
    slots = _slot_call(idx, rank, start_row)
    slots3 = slots[:, :TOP_K].reshape(n // tm, 1, tm * TOP_K)
    x_sorted = _dispatch_call(slots3, xn, sorted_buf)
    y_sorted = _experts_call(li, block_e, n_used.reshape(1), x_sorted, w_gu, bgu_r, w_down, bd, bm)
    return _combine_call(slots3, wgt, xs_mid, mod, fin, y_sorted, nl, nb, final), x_sorted


def _rope_tables(nl, nc):
    rows = nl // GRID_W
    r = np.broadcast_to(np.arange(rows)[:, None], (rows, GRID_W)).reshape(-1).astype(np.float32)
    c = np.broadcast_to(np.arange(GRID_W)[None, :], (rows, GRID_W)).reshape(-1).astype(np.float32)
    n_freq = DK // 4
    inv = (ROPE_BASE ** (-jnp.arange(n_freq, dtype=F32) / n_freq))
    ang = jnp.concatenate([jnp.asarray(r)[:, None] * inv, jnp.asarray(c)[:, None] * inv], axis=-1)
    cos, sin = jnp.cos(ang), jnp.sin(ang)
    cos2 = jnp.concatenate([cos, cos], axis=-1)
    sin2 = jnp.concatenate([-sin, sin], axis=-1)
    cos_t = jnp.concatenate([cos2, jnp.ones((nc, DK), F32)], axis=0)
    sin_t = jnp.concatenate([sin2, jnp.zeros((nc, DK), F32)], axis=0)
    return cos_t, sin_t


def kernel(x, c, ctx, c_ctx, w_ada, b_ada, norm1, norm2, w_in, gla_wa2, gla_ba, gla_norm, lru_conv_w, lru_conv_b, lru_wa, lru_ba, lru_wi, lru_bi, lru_lam, ret_norm, w_branch, w_out, w_router, b_router, w_gu, b_gu, w_down, b_down, final_norm):
    nb, nl, d = x.shape
    nc = ctx.shape[1]
    depth = w_ada.shape[0]
    n_lat = nb * nl
    f = w_down.shape[2]

    xs = jnp.concatenate([x.reshape(n_lat, d), ctx.reshape(nb * nc, d)], axis=0)
    mod_rows = -(-(nb + 1) // SUBLANES) * SUBLANES
    cc = jnp.zeros((mod_rows, d), F32).at[:nb].set(c).at[nb].set(c_ctx)
    mod_all = _ada_call(cc, w_ada, b_ada).reshape(depth, mod_rows, N_MOD, d)
    cos_t, sin_t = _rope_tables(nl, nc)
    bgu_r = b_gu.reshape(depth, N_EXPERTS, 2 * f // (2 * LANES), LANES, 2)
    bgu_r = jnp.swapaxes(bgu_r, -1, -2).reshape(depth, N_EXPERTS, 1, 2 * f)
    bd_r = b_down.reshape(depth, N_EXPERTS, 1, d)
    half_cols = np.ones((1, N_PROJ), np.float32)
    for c0, width_c in ((C_GLA_G, HEADS * DV), (C_RET_G, HEADS * DV), (C_MERGE, 3 * d)):
        half_cols[0, c0:c0 + width_c] = 0.5
    col_scale = jnp.asarray(half_cols)
    sorted_buf = jnp.zeros((_sorted_rows(xs.shape[0]), d), F32)
    lru_buf = jnp.zeros((xs.shape[0], lru_conv_w.shape[2]), _MXU)

    out = None
    for li in range(depth):
        last = li == depth - 1
        mod = mod_all[li]
        w = w_in[li]
        w_in_r = (jnp.concatenate(
            [w[:, :3072], w[:, 3104:], w[:, 3072:3104],
             jnp.zeros((d, N_PROJ - w.shape[1]), w.dtype)], axis=1) * col_scale).astype(_MXU)
        p = _proj_call(xs, mod, norm1[li].reshape(1, d), w_in_r, nl, nb)

        wa2p = jnp.zeros((2, LANES, HEADS * DK), F32)
        wa2p = wa2p.at[0, :GLA_RANK].set(gla_wa2[li, 0]).at[1, GLA_RANK:2 * GLA_RANK].set(gla_wa2[li, 1])
        gla = _la_call("gla", p, (wa2p.astype(_MXU), gla_ba[li].reshape(2, 1, HEADS * DK)),
                       gla_norm[li].reshape(1, HEADS * DV), nb, nl, nc)
        ret = _la_call("ret", p, (cos_t, sin_t), ret_norm[li].reshape(1, HEADS * DV), nb, nl, nc)

        lw = (lru_conv_w[li], lru_conv_b[li].reshape(1, -1), (0.5 * lru_wa[li]).astype(_MXU),
              0.5 * lru_ba[li], (0.5 * lru_wi[li]).astype(_MXU), 0.5 * lru_bi[li], lru_lam[li])
        width = lru_conv_w.shape[2]
        lru_c, h_ctx = _lru_call(p, lru_buf, jnp.zeros((nb, 2, width), F32), lw, nb, nc, n_lat // nc)
        lru, _ = _lru_call(p, lru_c, h_ctx, lw, nb, nl, 0)
        lru_buf = lru

        n_rows = n_lat if last else xs.shape[0]
        wr = jnp.zeros((d, LANES), F32).at[:, :N_EXPERTS].set(w_router[li]).astype(_MXU)
        br = jnp.full((1, LANES), NEG_BIG, F32).at[0, :N_EXPERTS].set(b_router[li])
        xs_mid, xn2, idx, wgt, rank, counts = _finish_call(
            gla, lru, ret, p, w_branch[li].astype(_MXU), w_out[li].astype(_MXU), xs, mod,
            norm2[li].reshape(1, d), wr, br, n_rows, nl, nb)

        out, sorted_buf = _moe(li, xn2, idx, wgt, rank, counts, xs_mid, mod, final_norm.reshape(1, d),
                               w_gu, bgu_r, w_down, bd_r, nl, nb, last, sorted_buf)
        xs = out
    return out.reshape(nb, nl, d)
```

```python
import functools
import math

import numpy as np
import jax
import jax.numpy as jnp
from jax import lax
from jax.experimental import pallas as pl
from jax.experimental.pallas import tpu as pltpu

F32 = jnp.float32
_MXU = jnp.bfloat16

EPS = 1e-6
N_MOD = 6
GRID_W = 64
CHUNK = 64
HEADS = 4
DK = 128
DV = 256
GLA_RANK = 16
GLA_GATE_NORM = 16.0
LRU_BLOCKS = 4
LRU_C = 8.0
ROPE_BASE = 10000.0
N_EXPERTS = 32
TOP_K = 4
SWIGLU_LIMIT = 7.0
SWIGLU_ALPHA = 1.702
LANES = 128
SUBLANES = 8
NEG_BIG = -1e30

C_GLA_Q, C_GLA_K, C_GLA_V, C_GLA_G = 0, 512, 1024, 2048
C_LRU_X, C_LRU_G = 3072, 4096
C_RET_Q, C_RET_K, C_RET_V, C_RET_G = 5120, 5632, 6144, 7168
C_MERGE = 8192
C_LR = 11264
N_PROJ = 11520
PROJ_TN = 3840


def _pick(n, prefs):
    for p in prefs:
        if n % p == 0:
            return p
    raise ValueError(f"no tile for {n} in {prefs}")


def _cparams(sem, vmem_mb):
    return pltpu.CompilerParams(dimension_semantics=sem, vmem_limit_bytes=vmem_mb * 1024 * 1024)


def _dot(a, b):
    return jnp.dot(a, b, preferred_element_type=F32)


def _dot_nt(a, b):
    return lax.dot_general(a, b, (((1,), (1,)), ((), ())), preferred_element_type=F32)


def _dot_tn(a, b):
    return lax.dot_general(a, b, (((0,), (0,)), ((), ())), preferred_element_type=F32)


def _sigmoid(x):
    return 0.5 * jnp.tanh(0.5 * x) + 0.5


def _log_sigmoid(x):
    return jnp.minimum(x, 0.0) - jnp.log(1.0 + jnp.exp(-jnp.abs(x)))


def _silu(x):
    return x * _sigmoid(x)


def _gelu_tanh(x):
    return 0.5 * x * (1.0 + jnp.tanh(math.sqrt(2.0 / math.pi) * (x + 0.044715 * (x * x * x))))


def _rms(x, gain):
    return x * lax.rsqrt(jnp.mean(x * x, axis=-1, keepdims=True) + EPS) * gain


def _dot01_exact(tri, g):
    hi = g.astype(_MXU)
    r1 = g - hi.astype(F32)
    mid = r1.astype(_MXU)
    lo = (r1 - mid.astype(F32)).astype(_MXU)
    return _dot(tri, hi) + _dot(tri, mid) + _dot(tri, lo)


def _ada_kernel(c_ref, w_ref, b_ref, o_ref):
    s = _silu(c_ref[...])
    o_ref[...] = _dot(s.astype(_MXU), w_ref[...].astype(_MXU)) + b_ref[...]


def _ada_call(cc, w_ada, b_ada):
    depth, d, n = w_ada.shape
    rows = cc.shape[0]
    tn = _pick(n, (1536, 1024, 512, 128))
    return pl.pallas_call(
        _ada_kernel,
        grid=(depth, n // tn),
        in_specs=[pl.BlockSpec((rows, d), lambda l, j: (0, 0)),
                  pl.BlockSpec((None, d, tn), lambda l, j: (l, 0, j)),
                  pl.BlockSpec((None, 1, tn), lambda l, j: (l, 0, j))],
        out_specs=pl.BlockSpec((None, rows, tn), lambda l, j: (l, 0, j)),
        out_shape=jax.ShapeDtypeStruct((depth, rows, n), F32),
        compiler_params=_cparams(("arbitrary", "arbitrary"), 24),
        name="ada",
    )(cc, w_ada, b_ada.reshape(depth, 1, n))


def _proj_kernel(x_ref, mod_ref, g_ref, w_ref, o_ref, xn_ref):
    @pl.when(pl.program_id(1) == 0)
    def _():
        y = _rms(x_ref[...], g_ref[...])
        xn_ref[...] = (y * (1.0 + mod_ref[1:2, :]) + mod_ref[0:1, :]).astype(xn_ref.dtype)

    o_ref[...] = _dot(xn_ref[...], w_ref[...]).astype(o_ref.dtype)


def _proj_call(xs, mod, gain, w_in_r, nl, nb):
    t, d = xs.shape
    tm = _pick(math.gcd(t, nl), (1024, 512, 256))
    return pl.pallas_call(
        _proj_kernel,
        grid=(t // tm, N_PROJ // PROJ_TN),
        in_specs=[pl.BlockSpec((tm, d), lambda i, j: (i, 0)),
                  pl.BlockSpec((None, N_MOD, d), lambda i, j: (jnp.minimum(i * tm // nl, nb), 0, 0)),
                  pl.BlockSpec((1, d), lambda i, j: (0, 0)),
                  pl.BlockSpec((d, PROJ_TN), lambda i, j: (0, j))],
        out_specs=pl.BlockSpec((tm, PROJ_TN), lambda i, j: (i, j)),
        out_shape=jax.ShapeDtypeStruct((t, N_PROJ), _MXU),
        scratch_shapes=[pltpu.VMEM((tm, d), _MXU)],
        compiler_params=_cparams(("arbitrary", "arbitrary"), 52),
        name="proj",
    )(xs, mod, gain, w_in_r)


def _la_kernel(*refs, kind, tt, n_ct, n_lt):
    if kind == "gla":
        (q_ref, k_ref, v_ref, gate_ref, lr_ref, wa2_ref, ba_ref, gain_ref,
         out_ref, st_ref, of_ref) = refs
    else:
        (q_ref, k_ref, v_ref, gate_ref, cos_ref, sin_ref, gain_ref,
         out_ref, st_ref, of_ref, dm_ref) = refs
    ph = pl.program_id(1)
    s = pl.program_id(2)
    ck = CHUNK if kind == "gla" else tt
    n_chunks = tt // ck

    @pl.when(s == 0)
    def _():
        st_ref[...] = jnp.zeros_like(st_ref)

    row = lax.broadcasted_iota(jnp.int32, (tt, tt), 0)
    col = lax.broadcasted_iota(jnp.int32, (tt, tt), 1)
    shift = ck.bit_length() - 1
    same_chunk = (row >> shift) == (col >> shift)
    tpos = lax.broadcasted_iota(jnp.int32, (tt, 1), 0).astype(F32)

    if kind == "ret":
        first = jnp.logical_and(pl.program_id(0) == 0, jnp.logical_and(ph == 0, s == 0))

        @pl.when(first)
        def _():
            for h in range(HEADS):
                lg = math.log(1.0 - 2.0 ** (-5.0 - h))
                dm_ref[h] = jnp.where(col <= row, jnp.exp((row - col).astype(F32) * lg), 0.0)
                dm_ref[HEADS + h] = jnp.where(col > row, jnp.exp((col - row).astype(F32) * lg), 0.0)

    def tile_outputs(backward):
        if backward:
            mask = jnp.logical_and(same_chunk, col > row)
            tri = jnp.logical_and(same_chunk, col >= row).astype(_MXU)
        else:
            mask = jnp.logical_and(same_chunk, col <= row)
            tri = jnp.logical_and(same_chunk, col <= row).astype(_MXU)
        order = list(reversed(range(n_chunks))) if backward else list(range(n_chunks))
        if kind == "gla":
            d = 1 if backward else 0
            z = _dot(lr_ref[...], wa2_ref[d]) + ba_ref[d]
            g = _log_sigmoid(z) * (1.0 / GLA_GATE_NORM)
            big_g = _dot01_exact(tri, g)
            lasts = [big_g[c * ck:c * ck + 1, :] if backward else big_g[(c + 1) * ck - 1:(c + 1) * ck, :]
                     for c in range(n_chunks)]
            g_last = jnp.concatenate([jnp.broadcast_to(r, (ck, r.shape[1])) for r in lasts], axis=0)
            e_pos = jnp.exp(big_g)
            e_neg = jnp.exp(-big_g)
            e_end = jnp.exp(g_last - big_g)
            decs = [jnp.exp(r) for r in lasts]
        outs = []
        for h in range(HEADS):
            ks = slice(h * DK, (h + 1) * DK)
            vs = slice(h * DV, (h + 1) * DV)
            qh = q_ref[:, ks].astype(F32)
            kh = k_ref[:, ks].astype(F32)
            vh = v_ref[:, vs]
            if kind == "gla":
                qh = qh * DK ** -0.5
                q_dec = (qh * e_pos[:, ks]).astype(_MXU)
                k_inv = (kh * e_neg[:, ks]).astype(_MXU)
                k_end = (kh * e_end[:, ks]).astype(_MXU)
                sc = jnp.where(mask, _dot_nt(q_dec, k_inv), 0.0)
                dec_h = [dcy[:, ks] for dcy in decs]
            else:
                kh = kh * DK ** -0.5
                cos = cos_ref[...]
                sin = sin_ref[...]
                qh = qh * cos + pltpu.roll(qh, DK // 2, 1) * sin
                kh = kh * cos + pltpu.roll(kh, DK // 2, 1) * sin
                lg = math.log(1.0 - 2.0 ** (-5.0 - h))
                steps = (ck - tpos) if backward else (tpos + 1.0)
                q_dec = (qh * jnp.exp(steps * lg)).astype(_MXU)
                k_end = (kh * jnp.exp((ck - steps) * lg)).astype(_MXU)
                sc = _dot_nt(qh.astype(_MXU), kh.astype(_MXU)) * dm_ref[(HEADS if backward else 0) + h]
                dec_h = [math.exp(ck * lg)] * n_chunks
            o_intra = _dot(sc.astype(_MXU), vh)
            st = st_ref[h]
            parts = [None] * n_chunks
            for c in order:
                rs = slice(c * ck, (c + 1) * ck)
                parts[c] = o_intra[rs] + _dot_nt(q_dec[rs], st.astype(_MXU))
                st = st * dec_h[c] + _dot_tn(vh[rs], k_end[rs])
            st_ref[h] = st
            outs.append(parts[0] if n_chunks == 1 else jnp.concatenate(parts, axis=0))
        return jnp.concatenate(outs, axis=1)

    @pl.when(ph == 0)
    def _():
        base = pl.multiple_of(s * tt, tt)
        of_ref[pl.ds(base, tt), :] = tile_outputs(False)

    @pl.when(ph == 1)
    def _():
        loc = jnp.where(s < n_ct, n_ct - 1 - s, n_ct + n_lt - 1 - (s - n_ct))
        base = pl.multiple_of(loc * tt, tt)
        o = of_ref[pl.ds(base, tt), :] + tile_outputs(True)
        gate = gate_ref[...].astype(F32)
        parts = []
        for h in range(HEADS):
            oh = o[:, h * DV:(h + 1) * DV]
            if kind == "ret":
                oh = oh - jnp.mean(oh, axis=-1, keepdims=True)
            parts.append(oh * lax.rsqrt(jnp.mean(oh * oh, axis=-1, keepdims=True) + EPS))
        normed = jnp.concatenate(parts, axis=1) * gain_ref[...]
        out_ref[...] = (normed * _silu(gate)).astype(out_ref.dtype)


def _la_call(kind, p, extra, gain, nb, nl, nc):
    t = p.shape[0]
    tt = _pick(math.gcd(nl, nc), (256, 128, 64))
    n_ct, n_lt = nc // tt, nl // tt
    lat_tiles = nb * n_lt
    cq, ck, cv, cg = ((C_GLA_Q, C_GLA_K, C_GLA_V, C_GLA_G) if kind == "gla"
                      else (C_RET_Q, C_RET_K, C_RET_V, C_RET_G))
    hd = HEADS * DK
    hv = HEADS * DV

    def loc_of(ph, s):
        back = jnp.where(s < n_ct, n_ct - 1 - s, n_ct + n_lt - 1 - (s - n_ct))
        return jnp.where(ph == 0, s, back)

    def row_blk(b, loc):
        return jnp.where(loc < n_ct, lat_tiles + b * n_ct + loc, b * n_lt + (loc - n_ct))

    def in_map(cblk):
        return lambda b, ph, s: (row_blk(b, loc_of(ph, s)), cblk)

    def second_pass_map(cblk):
        return lambda b, ph, s: (row_blk(b, loc_of(1, jnp.where(ph == 0, 0, s))), cblk)

    in_specs = [pl.BlockSpec((tt, hd), in_map(cq // hd)),
                pl.BlockSpec((tt, hd), in_map(ck // hd)),
                pl.BlockSpec((tt, hv), in_map(cv // hv)),
                pl.BlockSpec((tt, hv), second_pass_map(cg // hv))]
    args = [p, p, p, p]
    if kind == "gla":
        wa2p, ba = extra
        in_specs += [pl.BlockSpec((tt, LANES), in_map(C_LR // LANES)),
                     pl.BlockSpec((2, LANES, hd), lambda b, ph, s: (0, 0, 0)),
                     pl.BlockSpec((2, 1, hd), lambda b, ph, s: (0, 0, 0))]
        args += [p, wa2p, ba]
    else:
        cos_t, sin_t = extra

        def rope_map(b, ph, s):
            loc = loc_of(ph, s)
            return (jnp.where(loc < n_ct, n_lt + loc, loc - n_ct), 0)
        in_specs += [pl.BlockSpec((tt, DK), rope_map), pl.BlockSpec((tt, DK), rope_map)]
        args += [cos_t, sin_t]
    in_specs.append(pl.BlockSpec((1, hv), lambda b, ph, s: (0, 0)))
    args.append(gain)
    return pl.pallas_call(
        functools.partial(_la_kernel, kind=kind, tt=tt, n_ct=n_ct, n_lt=n_lt),
        grid=(nb, 2, n_ct + n_lt),
        in_specs=in_specs,
        out_specs=pl.BlockSpec((tt, hv), second_pass_map(0)),
        out_shape=jax.ShapeDtypeStruct((t, hv), _MXU),
        scratch_shapes=[pltpu.VMEM((HEADS, DV, DK), F32),
                        pltpu.VMEM((nc + nl, hv), F32)]
        + ([pltpu.VMEM((2 * HEADS, tt, tt), F32)] if kind == "ret" else []),
        compiler_params=_cparams(("arbitrary", "arbitrary", "arbitrary"), 48),
        name=kind,
    )(*args)


def _scan_group(a, b, h, reverse):
    row = lax.broadcasted_iota(jnp.int32, a.shape, 0)
    for sft in (1, 2, 4):
        if reverse:
            a_sh = pltpu.roll(a, SUBLANES - sft, 0)
            b_sh = pltpu.roll(b, SUBLANES - sft, 0)
            m = row < SUBLANES - sft
        else:
            a_sh = pltpu.roll(a, sft, 0)
            b_sh = pltpu.roll(b, sft, 0)
            m = row >= sft
        b = jnp.where(m, a * b_sh + b, b)
        a = jnp.where(m, a * a_sh, a)
    hh = a * h + b
    return hh, (hh[0:1, :] if reverse else hh[SUBLANES - 1:SUBLANES, :])


def _lru_kernel(x_ref, gel_ref, cw_ref, cb_ref, wa_ref, ba_ref, wi_ref, bi_ref, lam_ref, h0_ref,
                out_ref, hn_ref, xpad_ref, xc_ref, hf_ref, *, seg, tile):
    n_tiles = seg // tile
    groups = tile // SUBLANES
    cw = cw_ref[...]
    cb = cb_ref[...]
    zeros8 = jnp.zeros((SUBLANES, xpad_ref.shape[1]), F32)
    xpad_ref[0:SUBLANES, :] = zeros8
    xpad_ref[SUBLANES + seg:2 * SUBLANES + seg, :] = zeros8

    def copy_in(i, carry):
        t0 = pl.multiple_of(i * tile, tile)
        xpad_ref[pl.ds(t0 + SUBLANES, tile), :] = x_ref[pl.ds(t0, tile), :].astype(F32)
        return carry
    lax.fori_loop(0, n_tiles, copy_in, 0)

    neg_c = [(-0.5 * LRU_C) * _log_sigmoid(lam_ref[d:d + 1, :]) for d in range(2)]

    def gates(xc, d):
        xb = xc.astype(_MXU)
        tr = jnp.tanh(_dot(xb, wa_ref[d]) + ba_ref[d:d + 1, :])
        ig = 0.5 * jnp.tanh(_dot(xb, wi_ref[d]) + bi_ref[d:d + 1, :]) + 0.5
        nla = neg_c[d] * tr + neg_c[d]
        a = jnp.exp(-nla)
        b = jnp.sqrt(jnp.tanh(nla) * (a * a + 1.0)) * (ig * xc)
        return a, b

    def fwd_tile(i, h):
        t0 = pl.multiple_of(i * tile, tile)
        win = xpad_ref[pl.ds(t0, tile + 2 * SUBLANES), :]
        xc = (cw[0:1, :] * win[6:6 + tile] + cw[1:2, :] * win[7:7 + tile]
              + cw[2:3, :] * win[8:8 + tile] + cw[3:4, :] * win[9:9 + tile] + cb)
        xc_ref[pl.ds(t0, tile), :] = xc
        a, b = gates(xc, 0)
        for g in range(groups):
            rs = slice(g * SUBLANES, (g + 1) * SUBLANES)
            hh, h = _scan_group(a[rs], b[rs], h, False)
            hf_ref[pl.ds(t0 + g * SUBLANES, SUBLANES), :] = hh
        return h
    h_f = lax.fori_loop(0, n_tiles, fwd_tile, h0_ref[0:1, :])

    def bwd_tile(j, h):
        i = n_tiles - 1 - j
        t0 = pl.multiple_of(i * tile, tile)
        xc = xc_ref[pl.ds(t0, tile), :]
        a, b = gates(xc, 1)
        for g in reversed(range(groups)):
            rs = slice(g * SUBLANES, (g + 1) * SUBLANES)
            hh, h = _scan_group(a[rs], b[rs], h, True)
            rows = pl.ds(t0 + g * SUBLANES, SUBLANES)
            hsum = hf_ref[rows, :] + hh
            hf_ref[rows, :] = hsum
        gel = gel_ref[pl.ds(t0, tile), :].astype(F32)
        out_ref[pl.ds(t0, tile), :] = (hf_ref[pl.ds(t0, tile), :] * _gelu_tanh(gel)).astype(out_ref.dtype)
        return h
    h_b = lax.fori_loop(0, n_tiles, bwd_tile, h0_ref[1:2, :])
    hn_ref[0:1, :] = h_f
    hn_ref[1:2, :] = h_b


def _lru_call(p, prev_out, h0, lw, nb, seg, row_blk0):
    cw, cb, wa, ba, wi, bi, lam = lw
    t = p.shape[0]
    width = LRU_BLOCKS * DV
    cbw = width // LRU_BLOCKS
    tile = _pick(seg, (512, 256, 128, 64))
    xblk = C_LRU_X // cbw
    gblk = C_LRU_G // cbw
    in_specs = [pl.BlockSpec((seg, cbw), lambda b, c: (row_blk0 + b, xblk + c)),
                pl.BlockSpec((seg, cbw), lambda b, c: (row_blk0 + b, gblk + c)),
                pl.BlockSpec((4, cbw), lambda b, c: (0, c)),
                pl.BlockSpec((1, cbw), lambda b, c: (0, c)),
                pl.BlockSpec((2, None, cbw, cbw), lambda b, c: (0, c, 0, 0)),
                pl.BlockSpec((2, cbw), lambda b, c: (0, c)),
                pl.BlockSpec((2, None, cbw, cbw), lambda b, c: (0, c, 0, 0)),
                pl.BlockSpec((2, cbw), lambda b, c: (0, c)),
                pl.BlockSpec((2, cbw), lambda b, c: (0, c)),
                pl.BlockSpec((None, 2, cbw), lambda b, c: (b, 0, c))]
    in_specs.append(pl.BlockSpec(memory_space=pl.ANY))
    args = [p, p, cw, cb, wa, ba, wi, bi, lam, h0, prev_out]

    def body(*refs):
        _lru_kernel(*(refs[:10] + refs[11:]), seg=seg, tile=tile)

    return pl.pallas_call(
        body,
        grid=(nb, LRU_BLOCKS),
        in_specs=in_specs,
        out_specs=[pl.BlockSpec((seg, cbw), lambda b, c: (row_blk0 + b, c)),
                   pl.BlockSpec((None, 2, cbw), lambda b, c: (b, 0, c))],
        out_shape=[jax.ShapeDtypeStruct((t, width), _MXU),
                   jax.ShapeDtypeStruct((nb, 2, width), F32)],
        scratch_shapes=[pltpu.VMEM((seg + 2 * SUBLANES, cbw), F32),
                        pltpu.VMEM((seg, cbw), F32),
                        pltpu.VMEM((seg, cbw), F32)],
        input_output_aliases={10: 0},
        compiler_params=_cparams(("arbitrary", "arbitrary"), 48),
        name="lru",
    )(*args)


def _finish_kernel(gla_ref, lru_ref, ret_ref, m0_ref, m1_ref, m2_ref, wb_ref, wo_ref, xs_ref,
                   mod_ref, g2_ref, wr_ref, br_ref, tril_ref,
                   xo_ref, xn_ref, idx_ref, wgt_ref, rank_ref, cnt_ref, carry_ref):
    @pl.when(pl.program_id(0) == 0)
    def _():
        carry_ref[...] = jnp.zeros_like(carry_ref)

    hm = tril_ref.shape[0]
    total = carry_ref[0:1, :]
    for sub in range(xs_ref.shape[0] // hm):
        rs = slice(sub * hm, (sub + 1) * hm)
        merged = (_sigmoid(m0_ref[rs, :].astype(F32)) * _dot(gla_ref[rs, :], wb_ref[0])
                  + _sigmoid(m1_ref[rs, :].astype(F32)) * _dot(lru_ref[rs, :], wb_ref[1])
                  + _sigmoid(m2_ref[rs, :].astype(F32)) * _dot(ret_ref[rs, :], wb_ref[2]))
        x = xs_ref[rs, :] + mod_ref[2:3, :] * _dot(merged.astype(_MXU), wo_ref[...])
        xo_ref[rs, :] = x
        xn = _rms(x, g2_ref[...]) * (1.0 + mod_ref[4:5, :]) + mod_ref[3:4, :]
        xn_ref[rs, :] = xn

        logits = _dot(xn.astype(_MXU), wr_ref[...]) + br_ref[...]
        lane = lax.broadcasted_iota(jnp.int32, logits.shape, 1)
        ids, vals = [], []
        for _ in range(TOP_K):
            m = jnp.max(logits, axis=1, keepdims=True)
            sel = jnp.min(jnp.where(logits == m, lane.astype(F32), float(LANES)), axis=1,
                          keepdims=True).astype(jnp.int32)
            ids.append(sel)
            vals.append(m)
            logits = jnp.where(lane == sel, -jnp.inf, logits)
        ex = [jnp.exp(v - vals[0]) for v in vals]
        denom = ex[0] + ex[1] + ex[2] + ex[3]
        onehot = jnp.zeros(logits.shape, F32)
        for sel in ids:
            onehot = onehot + (lane == sel).astype(F32)
        before = _dot(tril_ref[...], onehot.astype(_MXU)) + total
        idx_o = jnp.zeros(logits.shape, jnp.int32)
        rank_o = jnp.zeros(logits.shape, jnp.int32)
        wgt_o = jnp.zeros(logits.shape, F32)
        for k in range(TOP_K):
            rk = jnp.sum(jnp.where(lane == ids[k], before, 0.0), axis=1, keepdims=True)
            idx_o = jnp.where(lane == k, ids[k], idx_o)
            rank_o = jnp.where(lane == k, rk.astype(jnp.int32), rank_o)
            wgt_o = jnp.where(lane == k, ex[k] / denom, wgt_o)
        idx_ref[rs, :] = idx_o
        rank_ref[rs, :] = rank_o
        wgt_ref[rs, :] = wgt_o
        total = total + jnp.sum(onehot, axis=0, keepdims=True)
    carry_ref[...] = jnp.broadcast_to(total, carry_ref.shape)
    cnt_ref[...] = jnp.broadcast_to(total, cnt_ref.shape)


def _finish_call(gla, lru, ret, p, wb, wo, xs, mod, g2, wr, br, n_rows, nl, nb):
    d = xs.shape[1]
    tm = _pick(math.gcd(n_rows, nl), (512, 256))
    hm = tm
    tril = jnp.asarray(np.tril(np.ones((hm, hm), np.float32), -1), _MXU)
    row = lambda i: (i, 0)
    const2 = lambda i: (0, 0)
    mblk = C_MERGE // d
    in_specs = [pl.BlockSpec((tm, d), row), pl.BlockSpec((tm, d), row), pl.BlockSpec((tm, d), row),
                pl.BlockSpec((tm, d), lambda i: (i, mblk)),
                pl.BlockSpec((tm, d), lambda i: (i, mblk + 1)),
                pl.BlockSpec((tm, d), lambda i: (i, mblk + 2)),
                pl.BlockSpec((3, d, d), lambda i: (0, 0, 0)),
                pl.BlockSpec((d, d), const2),
                pl.BlockSpec((tm, d), row),
                pl.BlockSpec((None, N_MOD, d), lambda i: (jnp.minimum(i * tm // nl, nb), 0, 0)),
                pl.BlockSpec((1, d), const2),
                pl.BlockSpec((d, LANES), const2),
                pl.BlockSpec((1, LANES), const2),
                pl.BlockSpec((hm, hm), const2)]
    out_specs = [pl.BlockSpec((tm, d), row), pl.BlockSpec((tm, d), row),
                 pl.BlockSpec((tm, LANES), row), pl.BlockSpec((tm, LANES), row),
                 pl.BlockSpec((tm, LANES), row), pl.BlockSpec((SUBLANES, LANES), const2)]
    out_shape = [jax.ShapeDtypeStruct((n_rows, d), F32), jax.ShapeDtypeStruct((n_rows, d), F32),
                 jax.ShapeDtypeStruct((n_rows, LANES), jnp.int32),
                 jax.ShapeDtypeStruct((n_rows, LANES), F32),
                 jax.ShapeDtypeStruct((n_rows, LANES), jnp.int32),
                 jax.ShapeDtypeStruct((SUBLANES, LANES), F32)]
    return pl.pallas_call(
        _finish_kernel,
        grid=(n_rows // tm,),
        in_specs=in_specs,
        out_specs=out_specs,
        out_shape=out_shape,
        scratch_shapes=[pltpu.VMEM((SUBLANES, LANES), F32)],
        compiler_params=_cparams(("arbitrary",), 56),
        name="finish",
    )(gla, lru, ret, p, p, p, wb, wo, xs, mod, g2, wr, br, tril)


def _slot_kernel(idx_ref, rank_ref, start_ref, slot_ref):
    idx = idx_ref[...].astype(F32)
    lane = lax.broadcasted_iota(jnp.int32, idx.shape, 1)
    start = start_ref[...]
    out = jnp.zeros(idx.shape, F32)
    for k in range(TOP_K):
        sel = jnp.sum(jnp.where(lane == k, idx, 0.0), axis=1, keepdims=True).astype(jnp.int32)
        st = jnp.sum(jnp.where(lane == sel, start, 0.0), axis=1, keepdims=True)
        out = jnp.where(lane == k, st, out)
    slot_ref[...] = out.astype(jnp.int32) + rank_ref[...]


def _slot_call(idx, rank, pad_start):
    n = idx.shape[0]
    tm = _pick(n, (2048, 1024, 512, 256))
    row = lambda i: (i, 0)
    return pl.pallas_call(
        _slot_kernel,
        grid=(n // tm,),
        in_specs=[pl.BlockSpec((tm, LANES), row), pl.BlockSpec((tm, LANES), row),
                  pl.BlockSpec((1, LANES), lambda i: (0, 0))],
        out_specs=pl.BlockSpec((tm, LANES), row),
        out_shape=jax.ShapeDtypeStruct((n, LANES), jnp.int32),
        compiler_params=_cparams(("arbitrary",), 24),
        name="slots",
    )(idx, rank, pad_start)


def _dispatch_kernel(slot_ref, x_ref, dst_in_ref, dst_ref, sem, *, tm):
    del dst_in_ref

    def issue(g, carry):
        t0 = pl.multiple_of(g * SUBLANES, SUBLANES)
        for u in range(SUBLANES):
            for k in range(TOP_K):
                pltpu.make_async_copy(
                    x_ref.at[pl.ds(t0 + u, 1)],
                    dst_ref.at[pl.ds(slot_ref[0, (t0 + u) * TOP_K + k], 1)],
                    sem).start(priority=k % 2)
        return carry
    lax.fori_loop(0, tm // SUBLANES, issue, 0)
    for k in range(TOP_K):
        pltpu.make_async_copy(x_ref, dst_ref.at[pl.ds(0, tm)], sem).wait()


def _dispatch_call(slots3, xn, zeros_sorted):
    n, d = xn.shape
    tm = slots3.shape[2] // TOP_K
    return pl.pallas_call(
        functools.partial(_dispatch_kernel, tm=tm),
        grid=(n // tm,),
        in_specs=[pl.BlockSpec((None, 1, tm * TOP_K), lambda i: (i, 0, 0), memory_space=pltpu.SMEM),
                  pl.BlockSpec((tm, d), lambda i: (i, 0)),
                  pl.BlockSpec(memory_space=pl.ANY)],
        out_specs=pl.BlockSpec(memory_space=pl.ANY),
        out_shape=jax.ShapeDtypeStruct(zeros_sorted.shape, zeros_sorted.dtype),
        scratch_shapes=[pltpu.SemaphoreType.DMA(())],
        input_output_aliases={2: 0},
        compiler_params=_cparams(("arbitrary",), 24),
        name="dispatch",
    )(slots3, xn, zeros_sorted)


def _experts_kernel(be_ref, nu_ref, x_ref, wgu_ref, bgu_ref, wd_ref, bd_ref, perm_ref, y_ref,
                    wgu_s, wd_s):
    j = pl.program_id(0)
    f = wd_ref.shape[0]
    grp = perm_ref.shape[0]
    half = grp // 2
    active = j < nu_ref[0]
    changed = jnp.logical_or(j == 0, be_ref[j] != be_ref[jnp.maximum(j - 1, 0)])

    @pl.when(jnp.logical_and(active, changed))
    def _():
        for g in range(2 * f // grp):
            cols = slice(g * grp, (g + 1) * grp)
            wgu_s[:, cols] = _dot(wgu_ref[:, cols].astype(_MXU), perm_ref[...]).astype(_MXU)
        wd_s[...] = wd_ref[...].astype(_MXU)

    @pl.when(active)
    def _():
        gu = _dot(x_ref[...].astype(_MXU), wgu_s[...]) + bgu_ref[...]
        acts = []
        for g in range(2 * f // grp):
            gate = jnp.minimum(gu[:, g * grp:g * grp + half], SWIGLU_LIMIT)
            up = jnp.clip(gu[:, g * grp + half:(g + 1) * grp], -SWIGLU_LIMIT, SWIGLU_LIMIT)
            acts.append((gate * _sigmoid(SWIGLU_ALPHA * gate) * (up + 1.0)).astype(_MXU))
        y_ref[...] = _dot(jnp.concatenate(acts, axis=1), wd_s[...]) + bd_ref[...]

    @pl.when(jnp.logical_not(active))
    def _():
        y_ref[...] = jnp.zeros_like(y_ref)


def _experts_call(li, block_e, n_used, x_sorted, w_gu, bgu_r, w_down, bd, bm):
    n_slots, d = x_sorted.shape
    f = w_down.shape[2]
    grp = 2 * LANES
    src = np.concatenate([np.arange(0, grp, 2), np.arange(1, grp, 2)])
    perm_np = np.zeros((grp, grp), np.float32)
    perm_np[src, np.arange(grp)] = 1.0
    perm = jnp.asarray(perm_np, _MXU)

    def xmap(j, be, nu):
        return (jnp.minimum(j, nu[0] - 1), 0)

    def wmap(j, be, nu):
        return (li, be[j], 0, 0)

    grid_spec = pltpu.PrefetchScalarGridSpec(
        num_scalar_prefetch=2,
        grid=(n_slots // bm,),
        in_specs=[pl.BlockSpec((bm, d), xmap),
                  pl.BlockSpec((None, None, d, 2 * f), wmap),
                  pl.BlockSpec((None, None, 1, 2 * f), wmap),
                  pl.BlockSpec((None, None, f, d), wmap),
                  pl.BlockSpec((None, None, 1, d), wmap),
                  pl.BlockSpec((grp, grp), lambda j, be, nu: (0, 0))],
        out_specs=pl.BlockSpec((bm, d), lambda j, be, nu: (j, 0)),
        scratch_shapes=[pltpu.VMEM((d, 2 * f), _MXU), pltpu.VMEM((f, d), _MXU)])
    return pl.pallas_call(
        _experts_kernel,
        grid_spec=grid_spec,
        out_shape=jax.ShapeDtypeStruct((n_slots, d), F32),
        compiler_params=_cparams(("arbitrary",), 56),
        name="experts",
    )(block_e, n_used, x_sorted, w_gu, bgu_r, w_down, bd, perm)


def _combine_kernel(slot_ref, wgt_ref, xs_ref, mod_ref, fin_ref, y_ref, out_ref, buf_ref, sem,
                    *, tm, final):
    def issue(g, carry):
        t0 = pl.multiple_of(g * SUBLANES, SUBLANES)
        for u in range(SUBLANES):
            for k in range(TOP_K):
                pltpu.make_async_copy(
                    y_ref.at[pl.ds(slot_ref[0, (t0 + u) * TOP_K + k], 1)],
                    buf_ref.at[k, pl.ds(t0 + u, 1)], sem).start(priority=k % 2)
        return carry
    lax.fori_loop(0, tm // SUBLANES, issue, 0)
    for k in range(TOP_K):
        pltpu.make_async_copy(y_ref.at[pl.ds(0, tm)], buf_ref.at[k], sem).wait()

    wgt = wgt_ref[...]
    acc = wgt[:, 0:1] * buf_ref[0]
    for k in range(1, TOP_K):
        acc = acc + wgt[:, k:k + 1] * buf_ref[k]
    x = xs_ref[...] + mod_ref[5:6, :] * acc
    out_ref[...] = _rms(x, fin_ref[...]) if final else x


def _combine_call(slots3, wgt, xs, mod, fin, y_sorted, nl, nb, final):
    n, d = xs.shape
    tm = slots3.shape[2] // TOP_K
    return pl.pallas_call(
        functools.partial(_combine_kernel, tm=tm, final=final),
        grid=(n // tm,),
        in_specs=[pl.BlockSpec((None, 1, tm * TOP_K), lambda i: (i, 0, 0), memory_space=pltpu.SMEM),
                  pl.BlockSpec((tm, LANES), lambda i: (i, 0)),
                  pl.BlockSpec((tm, d), lambda i: (i, 0)),
                  pl.BlockSpec((None, N_MOD, d), lambda i: (jnp.minimum(i * tm // nl, nb), 0, 0)),
                  pl.BlockSpec((1, d), lambda i: (0, 0)),
                  pl.BlockSpec(memory_space=pl.ANY)],
        out_specs=pl.BlockSpec((tm, d), lambda i: (i, 0)),
        out_shape=jax.ShapeDtypeStruct((n, d), F32),
        scratch_shapes=[pltpu.VMEM((TOP_K, tm, d), F32), pltpu.SemaphoreType.DMA(())],
        compiler_params=_cparams(("arbitrary",), 48),
        name="combine",
    )(slots3, wgt, xs, mod, fin, y_sorted)


MOE_BM = 512


def _sorted_rows(n_tokens):
    return (-(-(n_tokens * TOP_K) // MOE_BM) + N_EXPERTS) * MOE_BM


def _moe(li, xn, idx, wgt, rank, counts, xs_mid, mod, fin, w_gu, bgu_r, w_down, bd, nl, nb, final,
         sorted_buf):
    n = xn.shape[0]
    bm = MOE_BM
    tm = _pick(math.gcd(n, nl), (1024, 512, 256, 128))
    n_blocks = sorted_buf.shape[0] // bm
    cnt = counts[0, :N_EXPERTS].astype(jnp.int32)
    padded = (cnt + bm - 1) // bm * bm
    pad_end = jnp.cumsum(padded)
    pad_start = pad_end - padded
    n_used = (pad_end[-1] // bm).astype(jnp.int32)
    blk = jnp.arange(n_blocks, dtype=jnp.int32)
    first_row = jnp.minimum(blk, n_used - 1) * bm
    block_e = jnp.minimum(jnp.sum(pad_end[None, :] <= first_row[:, None], axis=1),
                          N_EXPERTS - 1).astype(jnp.int32)
    start_row = jnp.zeros((1, LANES), F32).at[0, :N_EXPERTS].set(pad_start.astype(F32))
    slots = _slot_call(idx, rank, start_row)
    slots3 = slots[:, :TOP_K].reshape(n // tm, 1, tm * TOP_K)
    x_sorted = _dispatch_call(slots3, xn, sorted_buf)
    y_sorted = _experts_call(li, block_e, n_used.reshape(1), x_sorted, w_gu, bgu_r, w_down, bd, bm)
    return _combine_call(slots3, wgt, xs_mid, mod, fin, y_sorted, nl, nb, final), x_sorted


def _rope_tables(nl, nc):
    rows = nl // GRID_W
    r = np.broadcast_to(np.arange(rows)[:, None], (rows, GRID_W)).reshape(-1).astype(np.float32)
    c = np.broadcast_to(np.arange(GRID_W)[None, :], (rows, GRID_W)).reshape(-1).astype(np.float32)
    n_freq = DK // 4
    inv = (ROPE_BASE ** (-jnp.arange(n_freq, dtype=F32) / n_freq))
    ang = jnp.concatenate([jnp.asarray(r)[:, None] * inv, jnp.asarray(c)[:, None] * inv], axis=-1)
    cos, sin = jnp.cos(ang), jnp.sin(ang)
    cos2 = jnp.concatenate([cos, cos], axis=-1)
    sin2 = jnp.concatenate([-sin, sin], axis=-1)
    cos_t = jnp.concatenate([cos2, jnp.ones((nc, DK), F32)], axis=0)
    sin_t = jnp.concatenate([sin2, jnp.zeros((nc, DK), F32)], axis=0)
    return cos_t, sin_t


def kernel(x, c, ctx, c_ctx, w_ada, b_ada, norm1, norm2, w_in, gla_wa2, gla_ba, gla_norm, lru_conv_w, lru_conv_b, lru_wa, lru_ba, lru_wi, lru_bi, lru_lam, ret_norm, w_branch, w_out, w_router, b_router, w_gu, b_gu, w_down, b_down, final_norm):
    nb, nl, d = x.shape
    nc = ctx.shape[1]
    depth = w_ada.shape[0]
    n_lat = nb * nl
    f = w_down.shape[2]

    xs = jnp.concatenate([x.reshape(n_lat, d), ctx.reshape(nb * nc, d)], axis=0)
    mod_rows = -(-(nb + 1) // SUBLANES) * SUBLANES
    cc = jnp.zeros((mod_rows, d), F32).at[:nb].set(c).at[nb].set(c_ctx)
    mod_all = _ada_call(cc, w_ada, b_ada).reshape(depth, mod_rows, N_MOD, d)
    cos_t, sin_t = _rope_tables(nl, nc)
    bgu_r = b_gu.reshape(depth, N_EXPERTS, 2 * f // (2 * LANES), LANES, 2)
    bgu_r = jnp.swapaxes(bgu_r, -1, -2).reshape(depth, N_EXPERTS, 1, 2 * f)
    bd_r = b_down.reshape(depth, N_EXPERTS, 1, d)
    sorted_buf = jnp.zeros((_sorted_rows(xs.shape[0]), d), F32)
    lru_buf = jnp.zeros((xs.shape[0], lru_conv_w.shape[2]), _MXU)

    out = None
    for li in range(depth):
        last = li == depth - 1
        mod = mod_all[li]
        w = w_in[li]
        w_in_r = jnp.concatenate(
            [w[:, :3072], w[:, 3104:], w[:, 3072:3104],
             jnp.zeros((d, N_PROJ - w.shape[1]), w.dtype)], axis=1).astype(_MXU)
        p = _proj_call(xs, mod, norm1[li].reshape(1, d), w_in_r, nl, nb)

        wa2p = jnp.zeros((2, LANES, HEADS * DK), F32)
        wa2p = wa2p.at[0, :GLA_RANK].set(gla_wa2[li, 0]).at[1, GLA_RANK:2 * GLA_RANK].set(gla_wa2[li, 1])
        gla = _la_call("gla", p, (wa2p.astype(_MXU), gla_ba[li].reshape(2, 1, HEADS * DK)),
                       gla_norm[li].reshape(1, HEADS * DV), nb, nl, nc)
        ret = _la_call("ret", p, (cos_t, sin_t), ret_norm[li].reshape(1, HEADS * DV), nb, nl, nc)

        lw = (lru_conv_w[li], lru_conv_b[li].reshape(1, -1), (0.5 * lru_wa[li]).astype(_MXU),
              0.5 * lru_ba[li], (0.5 * lru_wi[li]).astype(_MXU), 0.5 * lru_bi[li], lru_lam[li])
        width = lru_conv_w.shape[2]
        lru_c, h_ctx = _lru_call(p, lru_buf, jnp.zeros((nb, 2, width), F32), lw, nb, nc, n_lat // nc)
        lru, _ = _lru_call(p, lru_c, h_ctx, lw, nb, nl, 0)
        lru_buf = lru

        n_rows = n_lat if last else xs.shape[0]
        wr = jnp.zeros((d, LANES), F32).at[:, :N_EXPERTS].set(w_router[li]).astype(_MXU)
        br = jnp.full((1, LANES), NEG_BIG, F32).at[0, :N_EXPERTS].set(b_router[li])
        xs_mid, xn2, idx, wgt, rank, counts = _finish_call(
            gla, lru, ret, p, w_branch[li].astype(_MXU), w_out[li].astype(_MXU), xs, mod,
            norm2[li].reshape(1, d), wr, br, n_rows, nl, nb)

        out, sorted_buf = _moe(li, xn2, idx, wgt, rank, counts, xs_mid, mod, final_norm.reshape(1, d),
                               w_gu, bgu_r, w_down, bd_r, nl, nb, last, sorted_buf)
        xs = out
    return out.reshape(nb, nl, d)
```

```python
import functools
import math

import numpy as np
import jax
import jax.numpy as jnp
from jax import lax
from jax.experimental import pallas as pl
from jax.experimental.pallas import tpu as pltpu

F32 = jnp.float32
_MXU = jnp.bfloat16

EPS = 1e-6
N_MOD = 6
GRID_W = 64
CHUNK = 64
HEADS = 4
DK = 128
DV = 256
GLA_RANK = 16
GLA_GATE_NORM = 16.0
LRU_BLOCKS = 4
LRU_C = 8.0
ROPE_BASE = 10000.0
N_EXPERTS = 32
TOP_K = 4
SWIGLU_LIMIT = 7.0
SWIGLU_ALPHA = 1.702
LANES = 128
SUBLANES = 8
NEG_BIG = -1e30

C_GLA_Q, C_GLA_K, C_GLA_V, C_GLA_G = 0, 512, 1024, 2048
C_LRU_X, C_LRU_G = 3072, 4096
C_RET_Q, C_RET_K, C_RET_V, C_RET_G = 5120, 5632, 6144, 7168
C_MERGE = 8192
C_LR = 11264
N_PROJ = 11520
PROJ_TN = 3840


def _pick(n, prefs):
    for p in prefs:
        if n % p == 0:
            return p
    raise ValueError(f"no tile for {n} in {prefs}")


def _cparams(sem, vmem_mb):
    return pltpu.CompilerParams(dimension_semantics=sem, vmem_limit_bytes=vmem_mb * 1024 * 1024)


def _dot(a, b):
    return jnp.dot(a, b, preferred_element_type=F32)


def _dot_nt(a, b):
    return lax.dot_general(a, b, (((1,), (1,)), ((), ())), preferred_element_type=F32)


def _dot_tn(a, b):
    return lax.dot_general(a, b, (((0,), (0,)), ((), ())), preferred_element_type=F32)


def _sigmoid(x):
    return 0.5 * jnp.tanh(0.5 * x) + 0.5


def _log_sigmoid(x):
    return jnp.minimum(x, 0.0) - jnp.log(1.0 + jnp.exp(-jnp.abs(x)))


def _silu(x):
    return x * _sigmoid(x)


def _gelu_tanh(x):
    return 0.5 * x * (1.0 + jnp.tanh(math.sqrt(2.0 / math.pi) * (x + 0.044715 * (x * x * x))))


def _rms(x, gain):
    return x * lax.rsqrt(jnp.mean(x * x, axis=-1, keepdims=True) + EPS) * gain


def _dot01_exact(tri, g):
    hi = g.astype(_MXU)
    r1 = g - hi.astype(F32)
    mid = r1.astype(_MXU)
    lo = (r1 - mid.astype(F32)).astype(_MXU)
    return _dot(tri, hi) + _dot(tri, mid) + _dot(tri, lo)


def _ada_kernel(c_ref, w_ref, b_ref, o_ref):
    s = _silu(c_ref[...])
    o_ref[...] = _dot(s.astype(_MXU), w_ref[...].astype(_MXU)) + b_ref[...]


def _ada_call(cc, w_ada, b_ada):
    depth, d, n = w_ada.shape
    rows = cc.shape[0]
    tn = _pick(n, (1536, 1024, 512, 128))
    return pl.pallas_call(
        _ada_kernel,
        grid=(depth, n // tn),
        in_specs=[pl.BlockSpec((rows, d), lambda l, j: (0, 0)),
                  pl.BlockSpec((None, d, tn), lambda l, j: (l, 0, j)),
                  pl.BlockSpec((None, 1, tn), lambda l, j: (l, 0, j))],
        out_specs=pl.BlockSpec((None, rows, tn), lambda l, j: (l, 0, j)),
        out_shape=jax.ShapeDtypeStruct((depth, rows, n), F32),
        compiler_params=_cparams(("arbitrary", "arbitrary"), 24),
        name="ada",
    )(cc, w_ada, b_ada.reshape(depth, 1, n))


def _proj_kernel(x_ref, mod_ref, g_ref, w_ref, o_ref, xn_ref):
    @pl.when(pl.program_id(1) == 0)
    def _():
        y = _rms(x_ref[...], g_ref[...])
        xn_ref[...] = (y * (1.0 + mod_ref[1:2, :]) + mod_ref[0:1, :]).astype(xn_ref.dtype)

    o_ref[...] = _dot(xn_ref[...], w_ref[...]).astype(o_ref.dtype)


def _proj_call(xs, mod, gain, w_in_r, nl, nb):
    t, d = xs.shape
    tm = _pick(math.gcd(t, nl), (1024, 512, 256))
    return pl.pallas_call(
        _proj_kernel,
        grid=(t // tm, N_PROJ // PROJ_TN),
        in_specs=[pl.BlockSpec((tm, d), lambda i, j: (i, 0)),
                  pl.BlockSpec((None, N_MOD, d), lambda i, j: (jnp.minimum(i * tm // nl, nb), 0, 0)),
                  pl.BlockSpec((1, d), lambda i, j: (0, 0)),
                  pl.BlockSpec((d, PROJ_TN), lambda i, j: (0, j))],
        out_specs=pl.BlockSpec((tm, PROJ_TN), lambda i, j: (i, j)),
        out_shape=jax.ShapeDtypeStruct((t, N_PROJ), _MXU),
        scratch_shapes=[pltpu.VMEM((tm, d), _MXU)],
        compiler_params=_cparams(("arbitrary", "arbitrary"), 52),
        name="proj",
    )(xs, mod, gain, w_in_r)


def _la_kernel(*refs, kind, tt, n_ct, n_lt):
    if kind == "gla":
        (q_ref, k_ref, v_ref, gate_ref, lr_ref, wa2_ref, ba_ref, gain_ref,
         out_ref, st_ref, of_ref) = refs
    else:
        (q_ref, k_ref, v_ref, gate_ref, cos_ref, sin_ref, gain_ref,
         out_ref, st_ref, of_ref, dm_ref) = refs
    ph = pl.program_id(1)
    s = pl.program_id(2)
    ck = CHUNK if kind == "gla" else tt
    n_chunks = tt // ck

    @pl.when(s == 0)
    def _():
        st_ref[...] = jnp.zeros_like(st_ref)

    row = lax.broadcasted_iota(jnp.int32, (tt, tt), 0)
    col = lax.broadcasted_iota(jnp.int32, (tt, tt), 1)
    shift = ck.bit_length() - 1
    same_chunk = (row >> shift) == (col >> shift)
    tpos = lax.broadcasted_iota(jnp.int32, (tt, 1), 0).astype(F32)

    if kind == "ret":
        first = jnp.logical_and(pl.program_id(0) == 0, jnp.logical_and(ph == 0, s == 0))

        @pl.when(first)
        def _():
            for h in range(HEADS):
                lg = math.log(1.0 - 2.0 ** (-5.0 - h))
                dm_ref[h] = jnp.where(col <= row, jnp.exp((row - col).astype(F32) * lg), 0.0)
                dm_ref[HEADS + h] = jnp.where(col > row, jnp.exp((col - row).astype(F32) * lg), 0.0)

    def tile_outputs(backward):
        if backward:
            mask = jnp.logical_and(same_chunk, col > row)
            tri = jnp.logical_and(same_chunk, col >= row).astype(_MXU)
        else:
            mask = jnp.logical_and(same_chunk, col <= row)
            tri = jnp.logical_and(same_chunk, col <= row).astype(_MXU)
        order = list(reversed(range(n_chunks))) if backward else list(range(n_chunks))
        if kind == "gla":
            d = 1 if backward else 0
            z = _dot(lr_ref[...], wa2_ref[d]) + ba_ref[d]
            g = _log_sigmoid(z) * (1.0 / GLA_GATE_NORM)
            big_g = _dot01_exact(tri, g)
            lasts = [big_g[c * ck:c * ck + 1, :] if backward else big_g[(c + 1) * ck - 1:(c + 1) * ck, :]
                     for c in range(n_chunks)]
            g_last = jnp.concatenate([jnp.broadcast_to(r, (ck, r.shape[1])) for r in lasts], axis=0)
            e_pos = jnp.exp(big_g)
            e_neg = jnp.exp(-big_g)
            e_end = jnp.exp(g_last - big_g)
            decs = [jnp.exp(r) for r in lasts]
        outs = []
        for h in range(HEADS):
            ks = slice(h * DK, (h + 1) * DK)
            vs = slice(h * DV, (h + 1) * DV)
            qh = q_ref[:, ks].astype(F32)
            kh = k_ref[:, ks].astype(F32)
            vh = v_ref[:, vs]
            if kind == "gla":
                qh = qh * DK ** -0.5
                q_dec = (qh * e_pos[:, ks]).astype(_MXU)
                k_inv = (kh * e_neg[:, ks]).astype(_MXU)
                k_end = (kh * e_end[:, ks]).astype(_MXU)
                sc = jnp.where(mask, _dot_nt(q_dec, k_inv), 0.0)
                dec_h = [dcy[:, ks] for dcy in decs]
            else:
                kh = kh * DK ** -0.5
                cos = cos_ref[...]
                sin = sin_ref[...]
                qh = qh * cos + pltpu.roll(qh, DK // 2, 1) * sin
                kh = kh * cos + pltpu.roll(kh, DK // 2, 1) * sin
                lg = math.log(1.0 - 2.0 ** (-5.0 - h))
                steps = (ck - tpos) if backward else (tpos + 1.0)
                q_dec = (qh * jnp.exp(steps * lg)).astype(_MXU)
                k_end = (kh * jnp.exp((ck - steps) * lg)).astype(_MXU)
                sc = _dot_nt(qh.astype(_MXU), kh.astype(_MXU)) * dm_ref[(HEADS if backward else 0) + h]
                dec_h = [math.exp(ck * lg)] * n_chunks
            o_intra = _dot(sc.astype(_MXU), vh)
            st = st_ref[h]
            parts = [None] * n_chunks
            for c in order:
                rs = slice(c * ck, (c + 1) * ck)
                parts[c] = o_intra[rs] + _dot_nt(q_dec[rs], st.astype(_MXU))
                st = st * dec_h[c] + _dot_tn(vh[rs], k_end[rs])
            st_ref[h] = st
            outs.append(parts[0] if n_chunks == 1 else jnp.concatenate(parts, axis=0))
        return jnp.concatenate(outs, axis=1)

    @pl.when(ph == 0)
    def _():
        base = pl.multiple_of(s * tt, tt)
        of_ref[pl.ds(base, tt), :] = tile_outputs(False)

    @pl.when(ph == 1)
    def _():
        loc = jnp.where(s < n_ct, n_ct - 1 - s, n_ct + n_lt - 1 - (s - n_ct))
        base = pl.multiple_of(loc * tt, tt)
        o = of_ref[pl.ds(base, tt), :] + tile_outputs(True)
        gate = gate_ref[...].astype(F32)
        parts = []
        for h in range(HEADS):
            oh = o[:, h * DV:(h + 1) * DV]
            if kind == "ret":
                oh = oh - jnp.mean(oh, axis=-1, keepdims=True)
            parts.append(oh * lax.rsqrt(jnp.mean(oh * oh, axis=-1, keepdims=True) + EPS))
        normed = jnp.concatenate(parts, axis=1) * gain_ref[...]
        out_ref[...] = (normed * _silu(gate)).astype(out_ref.dtype)


def _la_call(kind, p, extra, gain, nb, nl, nc):
    t = p.shape[0]
    tt = _pick(math.gcd(nl, nc), (256, 128, 64))
    n_ct, n_lt = nc // tt, nl // tt
    lat_tiles = nb * n_lt
    cq, ck, cv, cg = ((C_GLA_Q, C_GLA_K, C_GLA_V, C_GLA_G) if kind == "gla"
                      else (C_RET_Q, C_RET_K, C_RET_V, C_RET_G))
    hd = HEADS * DK
    hv = HEADS * DV

    def loc_of(ph, s):
        back = jnp.where(s < n_ct, n_ct - 1 - s, n_ct + n_lt - 1 - (s - n_ct))
        return jnp.where(ph == 0, s, back)

    def row_blk(b, loc):
        return jnp.where(loc < n_ct, lat_tiles + b * n_ct + loc, b * n_lt + (loc - n_ct))

    def in_map(cblk):
        return lambda b, ph, s: (row_blk(b, loc_of(ph, s)), cblk)

    def second_pass_map(cblk):
        return lambda b, ph, s: (row_blk(b, loc_of(1, jnp.where(ph == 0, 0, s))), cblk)

    in_specs = [pl.BlockSpec((tt, hd), in_map(cq // hd)),
                pl.BlockSpec((tt, hd), in_map(ck // hd)),
                pl.BlockSpec((tt, hv), in_map(cv // hv)),
                pl.BlockSpec((tt, hv), second_pass_map(cg // hv))]
    args = [p, p, p, p]
    if kind == "gla":
        wa2p, ba = extra
        in_specs += [pl.BlockSpec((tt, LANES), in_map(C_LR // LANES)),
                     pl.BlockSpec((2, LANES, hd), lambda b, ph, s: (0, 0, 0)),
                     pl.BlockSpec((2, 1, hd), lambda b, ph, s: (0, 0, 0))]
        args += [p, wa2p, ba]
    else:
        cos_t, sin_t = extra

        def rope_map(b, ph, s):
            loc = loc_of(ph, s)
            return (jnp.where(loc < n_ct, n_lt + loc, loc - n_ct), 0)
        in_specs += [pl.BlockSpec((tt, DK), rope_map), pl.BlockSpec((tt, DK), rope_map)]
        args += [cos_t, sin_t]
    in_specs.append(pl.BlockSpec((1, hv), lambda b, ph, s: (0, 0)))
    args.append(gain)
    return pl.pallas_call(
        functools.partial(_la_kernel, kind=kind, tt=tt, n_ct=n_ct, n_lt=n_lt),
        grid=(nb, 2, n_ct + n_lt),
        in_specs=in_specs,
        out_specs=pl.BlockSpec((tt, hv), second_pass_map(0)),
        out_shape=jax.ShapeDtypeStruct((t, hv), _MXU),
        scratch_shapes=[pltpu.VMEM((HEADS, DV, DK), F32),
                        pltpu.VMEM((nc + nl, hv), F32)]
        + ([pltpu.VMEM((2 * HEADS, tt, tt), F32)] if kind == "ret" else []),
        compiler_params=_cparams(("arbitrary", "arbitrary", "arbitrary"), 48),
        name=kind,
    )(*args)


def _scan_group(a, b, h, reverse):
    row = lax.broadcasted_iota(jnp.int32, a.shape, 0)
    for sft in (1, 2, 4):
        if reverse:
            a_sh = pltpu.roll(a, SUBLANES - sft, 0)
            b_sh = pltpu.roll(b, SUBLANES - sft, 0)
            m = row < SUBLANES - sft
        else:
            a_sh = pltpu.roll(a, sft, 0)
            b_sh = pltpu.roll(b, sft, 0)
            m = row >= sft
        b = jnp.where(m, a * b_sh + b, b)
        a = jnp.where(m, a * a_sh, a)
    hh = a * h + b
    return hh, (hh[0:1, :] if reverse else hh[SUBLANES - 1:SUBLANES, :])


def _lru_kernel(x_ref, gel_ref, cw_ref, cb_ref, wa_ref, ba_ref, wi_ref, bi_ref, lam_ref, h0_ref,
                out_ref, hn_ref, xpad_ref, xc_ref, hf_ref, *, seg, tile):
    n_tiles = seg // tile
    groups = tile // SUBLANES
    cw = cw_ref[...]
    cb = cb_ref[...]
    zeros8 = jnp.zeros((SUBLANES, xpad_ref.shape[1]), F32)
    xpad_ref[0:SUBLANES, :] = zeros8
    xpad_ref[SUBLANES + seg:2 * SUBLANES + seg, :] = zeros8

    def copy_in(i, carry):
        t0 = pl.multiple_of(i * tile, tile)
        xpad_ref[pl.ds(t0 + SUBLANES, tile), :] = x_ref[pl.ds(t0, tile), :].astype(F32)
        return carry
    lax.fori_loop(0, n_tiles, copy_in, 0)

    neg_c = [(-0.5 * LRU_C) * _log_sigmoid(lam_ref[d:d + 1, :]) for d in range(2)]

    def gates(xc, d):
        xb = xc.astype(_MXU)
        tr = jnp.tanh(_dot(xb, wa_ref[d]) + ba_ref[d:d + 1, :])
        ig = 0.5 * jnp.tanh(_dot(xb, wi_ref[d]) + bi_ref[d:d + 1, :]) + 0.5
        nla = neg_c[d] * tr + neg_c[d]
        a = jnp.exp(-nla)
        b = jnp.sqrt(jnp.tanh(nla) * (a * a + 1.0)) * (ig * xc)
        return a, b

    def fwd_tile(i, h):
        t0 = pl.multiple_of(i * tile, tile)
        win = xpad_ref[pl.ds(t0, tile + 2 * SUBLANES), :]
        xc = (cw[0:1, :] * win[6:6 + tile] + cw[1:2, :] * win[7:7 + tile]
              + cw[2:3, :] * win[8:8 + tile] + cw[3:4, :] * win[9:9 + tile] + cb)
        xc_ref[pl.ds(t0, tile), :] = xc
        a, b = gates(xc, 0)
        for g in range(groups):
            rs = slice(g * SUBLANES, (g + 1) * SUBLANES)
            hh, h = _scan_group(a[rs], b[rs], h, False)
            hf_ref[pl.ds(t0 + g * SUBLANES, SUBLANES), :] = hh
        return h
    h_f = lax.fori_loop(0, n_tiles, fwd_tile, h0_ref[0:1, :])

    def bwd_tile(j, h):
        i = n_tiles - 1 - j
        t0 = pl.multiple_of(i * tile, tile)
        xc = xc_ref[pl.ds(t0, tile), :]
        a, b = gates(xc, 1)
        for g in reversed(range(groups)):
            rs = slice(g * SUBLANES, (g + 1) * SUBLANES)
            hh, h = _scan_group(a[rs], b[rs], h, True)
            rows = pl.ds(t0 + g * SUBLANES, SUBLANES)
            hsum = hf_ref[rows, :] + hh
            hf_ref[rows, :] = hsum
        gel = gel_ref[pl.ds(t0, tile), :].astype(F32)
        out_ref[pl.ds(t0, tile), :] = (hf_ref[pl.ds(t0, tile), :] * _gelu_tanh(gel)).astype(out_ref.dtype)
        return h
    h_b = lax.fori_loop(0, n_tiles, bwd_tile, h0_ref[1:2, :])
    hn_ref[0:1, :] = h_f
    hn_ref[1:2, :] = h_b


def _lru_call(p, prev_out, h0, lw, nb, seg, row_blk0):
    cw, cb, wa, ba, wi, bi, lam = lw
    t = p.shape[0]
    width = LRU_BLOCKS * DV
    cbw = width // LRU_BLOCKS
    tile = _pick(seg, (512, 256, 128, 64))
    xblk = C_LRU_X // cbw
    gblk = C_LRU_G // cbw
    in_specs = [pl.BlockSpec((seg, cbw), lambda b, c: (row_blk0 + b, xblk + c)),
                pl.BlockSpec((seg, cbw), lambda b, c: (row_blk0 + b, gblk + c)),
                pl.BlockSpec((4, cbw), lambda b, c: (0, c)),
                pl.BlockSpec((1, cbw), lambda b, c: (0, c)),
                pl.BlockSpec((2, None, cbw, cbw), lambda b, c: (0, c, 0, 0)),
                pl.BlockSpec((2, cbw), lambda b, c: (0, c)),
                pl.BlockSpec((2, None, cbw, cbw), lambda b, c: (0, c, 0, 0)),
                pl.BlockSpec((2, cbw), lambda b, c: (0, c)),
                pl.BlockSpec((2, cbw), lambda b, c: (0, c)),
                pl.BlockSpec((None, 2, cbw), lambda b, c: (b, 0, c))]
    in_specs.append(pl.BlockSpec(memory_space=pl.ANY))
    args = [p, p, cw, cb, wa, ba, wi, bi, lam, h0, prev_out]

    def body(*refs):
        _lru_kernel(*(refs[:10] + refs[11:]), seg=seg, tile=tile)

    return pl.pallas_call(
        body,
        grid=(nb, LRU_BLOCKS),
        in_specs=in_specs,
        out_specs=[pl.BlockSpec((seg, cbw), lambda b, c: (row_blk0 + b, c)),
                   pl.BlockSpec((None, 2, cbw), lambda b, c: (b, 0, c))],
        out_shape=[jax.ShapeDtypeStruct((t, width), _MXU),
                   jax.ShapeDtypeStruct((nb, 2, width), F32)],
        scratch_shapes=[pltpu.VMEM((seg + 2 * SUBLANES, cbw), F32),
                        pltpu.VMEM((seg, cbw), F32),
                        pltpu.VMEM((seg, cbw), F32)],
        input_output_aliases={10: 0},
        compiler_params=_cparams(("arbitrary", "arbitrary"), 48),
        name="lru",
    )(*args)


def _finish_kernel(gla_ref, lru_ref, ret_ref, m0_ref, m1_ref, m2_ref, wb_ref, wo_ref, xs_ref,
                   mod_ref, g2_ref, wr_ref, br_ref, tril_ref,
                   xo_ref, xn_ref, idx_ref, wgt_ref, rank_ref, cnt_ref, carry_ref):
    @pl.when(pl.program_id(0) == 0)
    def _():
        carry_ref[...] = jnp.zeros_like(carry_ref)

    hm = tril_ref.shape[0]
    total = carry_ref[0:1, :]
    for sub in range(xs_ref.shape[0] // hm):
        rs = slice(sub * hm, (sub + 1) * hm)
        merged = (_sigmoid(m0_ref[rs, :].astype(F32)) * _dot(gla_ref[rs, :], wb_ref[0])
                  + _sigmoid(m1_ref[rs, :].astype(F32)) * _dot(lru_ref[rs, :], wb_ref[1])
                  + _sigmoid(m2_ref[rs, :].astype(F32)) * _dot(ret_ref[rs, :], wb_ref[2]))
        x = xs_ref[rs, :] + mod_ref[2:3, :] * _dot(merged.astype(_MXU), wo_ref[...])
        xo_ref[rs, :] = x
        xn = _rms(x, g2_ref[...]) * (1.0 + mod_ref[4:5, :]) + mod_ref[3:4, :]
        xn_ref[rs, :] = xn

        logits = _dot(xn.astype(_MXU), wr_ref[...]) + br_ref[...]
        lane = lax.broadcasted_iota(jnp.int32, logits.shape, 1)
        ids, vals = [], []
        for _ in range(TOP_K):
            m = jnp.max(logits, axis=1, keepdims=True)
            sel = jnp.min(jnp.where(logits == m, lane.astype(F32), float(LANES)), axis=1,
                          keepdims=True).astype(jnp.int32)
            ids.append(sel)
            vals.append(m)
            logits = jnp.where(lane == sel, -jnp.inf, logits)
        ex = [jnp.exp(v - vals[0]) for v in vals]
        denom = ex[0] + ex[1] + ex[2] + ex[3]
        onehot = jnp.zeros(logits.shape, F32)
        for sel in ids:
            onehot = onehot + (lane == sel).astype(F32)
        before = _dot(tril_ref[...], onehot.astype(_MXU)) + total
        idx_o = jnp.zeros(logits.shape, jnp.int32)
        rank_o = jnp.zeros(logits.shape, jnp.int32)
        wgt_o = jnp.zeros(logits.shape, F32)
        for k in range(TOP_K):
            rk = jnp.sum(jnp.where(lane == ids[k], before, 0.0), axis=1, keepdims=True)
            idx_o = jnp.where(lane == k, ids[k], idx_o)
            rank_o = jnp.where(lane == k, rk.astype(jnp.int32), rank_o)
            wgt_o = jnp.where(lane == k, ex[k] / denom, wgt_o)
        idx_ref[rs, :] = idx_o
        rank_ref[rs, :] = rank_o
        wgt_ref[rs, :] = wgt_o
        total = total + jnp.sum(onehot, axis=0, keepdims=True)
    carry_ref[...] = jnp.broadcast_to(total, carry_ref.shape)
    cnt_ref[...] = jnp.broadcast_to(total, cnt_ref.shape)


def _finish_call(gla, lru, ret, p, wb, wo, xs, mod, g2, wr, br, n_rows, nl, nb):
    d = xs.shape[1]
    tm = _pick(math.gcd(n_rows, nl), (512, 256))
    hm = tm
    tril = jnp.asarray(np.tril(np.ones((hm, hm), np.float32), -1), _MXU)
    row = lambda i: (i, 0)
    const2 = lambda i: (0, 0)
    mblk = C_MERGE // d
    in_specs = [pl.BlockSpec((tm, d), row), pl.BlockSpec((tm, d), row), pl.BlockSpec((tm, d), row),
                pl.BlockSpec((tm, d), lambda i: (i, mblk)),
                pl.BlockSpec((tm, d), lambda i: (i, mblk + 1)),
                pl.BlockSpec((tm, d), lambda i: (i, mblk + 2)),
                pl.BlockSpec((3, d, d), lambda i: (0, 0, 0)),
                pl.BlockSpec((d, d), const2),
                pl.BlockSpec((tm, d), row),
                pl.BlockSpec((None, N_MOD, d), lambda i: (jnp.minimum(i * tm // nl, nb), 0, 0)),
                pl.BlockSpec((1, d), const2),
                pl.BlockSpec((d, LANES), const2),
                pl.BlockSpec((1, LANES), const2),
                pl.BlockSpec((hm, hm), const2)]
    out_specs = [pl.BlockSpec((tm, d), row), pl.BlockSpec((tm, d), row),
                 pl.BlockSpec((tm, LANES), row), pl.BlockSpec((tm, LANES), row),
                 pl.BlockSpec((tm, LANES), row), pl.BlockSpec((SUBLANES, LANES), const2)]
    out_shape = [jax.ShapeDtypeStruct((n_rows, d), F32), jax.ShapeDtypeStruct((n_rows, d), F32),
                 jax.ShapeDtypeStruct((n_rows, LANES), jnp.int32),
                 jax.ShapeDtypeStruct((n_rows, LANES), F32),
                 jax.ShapeDtypeStruct((n_rows, LANES), jnp.int32),
                 jax.ShapeDtypeStruct((SUBLANES, LANES), F32)]
    return pl.pallas_call(
        _finish_kernel,
        grid=(n_rows // tm,),
        in_specs=in_specs,
        out_specs=out_specs,
        out_shape=out_shape,
        scratch_shapes=[pltpu.VMEM((SUBLANES, LANES), F32)],
        compiler_params=_cparams(("arbitrary",), 56),
        name="finish",
    )(gla, lru, ret, p, p, p, wb, wo, xs, mod, g2, wr, br, tril)


def _slot_kernel(idx_ref, rank_ref, start_ref, slot_ref):
    idx = idx_ref[...].astype(F32)
    lane = lax.broadcasted_iota(jnp.int32, idx.shape, 1)
    start = start_ref[...]
    out = jnp.zeros(idx.shape, F32)
    for k in range(TOP_K):
        sel = jnp.sum(jnp.where(lane == k, idx, 0.0), axis=1, keepdims=True).astype(jnp.int32)
        st = jnp.sum(jnp.where(lane == sel, start, 0.0), axis=1, keepdims=True)
        out = jnp.where(lane == k, st, out)
    slot_ref[...] = out.astype(jnp.int32) + rank_ref[...]


def _slot_call(idx, rank, pad_start):
    n = idx.shape[0]
    tm = _pick(n, (2048, 1024, 512, 256))
    row = lambda i: (i, 0)
    return pl.pallas_call(
        _slot_kernel,
        grid=(n // tm,),
        in_specs=[pl.BlockSpec((tm, LANES), row), pl.BlockSpec((tm, LANES), row),
                  pl.BlockSpec((1, LANES), lambda i: (0, 0))],
        out_specs=pl.BlockSpec((tm, LANES), row),
        out_shape=jax.ShapeDtypeStruct((n, LANES), jnp.int32),
        compiler_params=_cparams(("arbitrary",), 24),
        name="slots",
    )(idx, rank, pad_start)


def _dispatch_kernel(fill_ref, slot_ref, x_ref, dst_ref, zero_ref, sem, fill_sem, *, tm, bm, n_blocks):
    @pl.when(pl.program_id(0) == 0)
    def _():
        zero_ref[...] = jnp.zeros_like(zero_ref)

        def clear(blk):
            return pltpu.make_async_copy(zero_ref, dst_ref.at[pl.ds(pl.multiple_of(blk * bm, bm), bm)],
                                         fill_sem)

        def tail_start(j, carry):
            clear(j).start()
            return carry

        def tail_wait(j, carry):
            clear(j).wait()
            return carry

        for e in range(N_EXPERTS):
            @pl.when(fill_ref[0, e] >= 0)
            def _():
                clear(fill_ref[0, e]).start()
        lax.fori_loop(fill_ref[0, N_EXPERTS], n_blocks, tail_start, 0)
        for e in range(N_EXPERTS):
            @pl.when(fill_ref[0, e] >= 0)
            def _():
                clear(fill_ref[0, e]).wait()
        lax.fori_loop(fill_ref[0, N_EXPERTS], n_blocks, tail_wait, 0)

    def issue(g, carry):
        t0 = pl.multiple_of(g * SUBLANES, SUBLANES)
        for u in range(SUBLANES):
            for k in range(TOP_K):
                pltpu.make_async_copy(
                    x_ref.at[pl.ds(t0 + u, 1)],
                    dst_ref.at[pl.ds(slot_ref[0, (t0 + u) * TOP_K + k], 1)],
                    sem).start(priority=k % 2)
        return carry
    lax.fori_loop(0, tm // SUBLANES, issue, 0)
    for k in range(TOP_K):
        pltpu.make_async_copy(x_ref, dst_ref.at[pl.ds(0, tm)], sem).wait()


def _dispatch_call(fill, slots3, xn, n_blocks, bm):
    n, d = xn.shape
    tm = slots3.shape[2] // TOP_K
    return pl.pallas_call(
        functools.partial(_dispatch_kernel, tm=tm, bm=bm, n_blocks=n_blocks),
        grid=(n // tm,),
        in_specs=[pl.BlockSpec((1, LANES), lambda i: (0, 0), memory_space=pltpu.SMEM),
                  pl.BlockSpec((None, 1, tm * TOP_K), lambda i: (i, 0, 0), memory_space=pltpu.SMEM),
                  pl.BlockSpec((tm, d), lambda i: (i, 0))],
        out_specs=pl.BlockSpec(memory_space=pl.ANY),
        out_shape=jax.ShapeDtypeStruct((n_blocks * bm, d), F32),
        scratch_shapes=[pltpu.VMEM((bm, d), F32), pltpu.SemaphoreType.DMA(()),
                        pltpu.SemaphoreType.DMA(())],
        compiler_params=_cparams(("arbitrary",), 24),
        name="dispatch",
    )(fill, slots3, xn)


def _experts_kernel(be_ref, nu_ref, x_ref, wgu_ref, bgu_ref, wd_ref, bd_ref, perm_ref, y_ref,
                    wgu_s, wd_s):
    j = pl.program_id(0)
    f = wd_ref.shape[0]
    grp = perm_ref.shape[0]
    half = grp // 2
    active = j < nu_ref[0]
    changed = jnp.logical_or(j == 0, be_ref[j] != be_ref[jnp.maximum(j - 1, 0)])

    @pl.when(jnp.logical_and(active, changed))
    def _():
        for g in range(2 * f // grp):
            cols = slice(g * grp, (g + 1) * grp)
            wgu_s[:, cols] = _dot(wgu_ref[:, cols].astype(_MXU), perm_ref[...]).astype(_MXU)
        wd_s[...] = wd_ref[...].astype(_MXU)

    @pl.when(active)
    def _():
        gu = _dot(x_ref[...].astype(_MXU), wgu_s[...]) + bgu_ref[...]
        acts = []
        for g in range(2 * f // grp):
            gate = jnp.minimum(gu[:, g * grp:g * grp + half], SWIGLU_LIMIT)
            up = jnp.clip(gu[:, g * grp + half:(g + 1) * grp], -SWIGLU_LIMIT, SWIGLU_LIMIT)
            acts.append((gate * _sigmoid(SWIGLU_ALPHA * gate) * (up + 1.0)).astype(_MXU))
        y_ref[...] = _dot(jnp.concatenate(acts, axis=1), wd_s[...]) + bd_ref[...]

    @pl.when(jnp.logical_not(active))
    def _():
        y_ref[...] = jnp.zeros_like(y_ref)


def _experts_call(li, block_e, n_used, x_sorted, w_gu, bgu_r, w_down, bd, bm):
    n_slots, d = x_sorted.shape
    f = w_down.shape[2]
    grp = 2 * LANES
    src = np.concatenate([np.arange(0, grp, 2), np.arange(1, grp, 2)])
    perm_np = np.zeros((grp, grp), np.float32)
    perm_np[src, np.arange(grp)] = 1.0
    perm = jnp.asarray(perm_np, _MXU)

    def xmap(j, be, nu):
        return (jnp.minimum(j, nu[0] - 1), 0)

    def wmap(j, be, nu):
        return (li, be[j], 0, 0)

    grid_spec = pltpu.PrefetchScalarGridSpec(
        num_scalar_prefetch=2,
        grid=(n_slots // bm,),
        in_specs=[pl.BlockSpec((bm, d), xmap),
                  pl.BlockSpec((None, None, d, 2 * f), wmap),
                  pl.BlockSpec((None, None, 1, 2 * f), wmap),
                  pl.BlockSpec((None, None, f, d), wmap),
                  pl.BlockSpec((None, None, 1, d), wmap),
                  pl.BlockSpec((grp, grp), lambda j, be, nu: (0, 0))],
        out_specs=pl.BlockSpec((bm, d), lambda j, be, nu: (j, 0)),
        scratch_shapes=[pltpu.VMEM((d, 2 * f), _MXU), pltpu.VMEM((f, d), _MXU)])
    return pl.pallas_call(
        _experts_kernel,
        grid_spec=grid_spec,
        out_shape=jax.ShapeDtypeStruct((n_slots, d), F32),
        compiler_params=_cparams(("arbitrary",), 56),
        name="experts",
    )(block_e, n_used, x_sorted, w_gu, bgu_r, w_down, bd, perm)


def _combine_kernel(slot_ref, wgt_ref, xs_ref, mod_ref, fin_ref, y_ref, out_ref, buf_ref, sem,
                    *, tm, final):
    def issue(g, carry):
        t0 = pl.multiple_of(g * SUBLANES, SUBLANES)
        for u in range(SUBLANES):
            for k in range(TOP_K):
                pltpu.make_async_copy(
                    y_ref.at[pl.ds(slot_ref[0, (t0 + u) * TOP_K + k], 1)],
                    buf_ref.at[k, pl.ds(t0 + u, 1)], sem).start(priority=k % 2)
        return carry
    lax.fori_loop(0, tm // SUBLANES, issue, 0)
    for k in range(TOP_K):
        pltpu.make_async_copy(y_ref.at[pl.ds(0, tm)], buf_ref.at[k], sem).wait()

    wgt = wgt_ref[...]
    acc = wgt[:, 0:1] * buf_ref[0]
    for k in range(1, TOP_K):
        acc = acc + wgt[:, k:k + 1] * buf_ref[k]
    x = xs_ref[...] + mod_ref[5:6, :] * acc
    out_ref[...] = _rms(x, fin_ref[...]) if final else x


def _combine_call(slots3, wgt, xs, mod, fin, y_sorted, nl, nb, final):
    n, d = xs.shape
    tm = slots3.shape[2] // TOP_K
    return pl.pallas_call(
        functools.partial(_combine_kernel, tm=tm, final=final),
        grid=(n // tm,),
        in_specs=[pl.BlockSpec((None, 1, tm * TOP_K), lambda i: (i, 0, 0), memory_space=pltpu.SMEM),
                  pl.BlockSpec((tm, LANES), lambda i: (i, 0)),
                  pl.BlockSpec((tm, d), lambda i: (i, 0)),
                  pl.BlockSpec((None, N_MOD, d), lambda i: (jnp.minimum(i * tm // nl, nb), 0, 0)),
                  pl.BlockSpec((1, d), lambda i: (0, 0)),
                  pl.BlockSpec(memory_space=pl.ANY)],
        out_specs=pl.BlockSpec((tm, d), lambda i: (i, 0)),
        out_shape=jax.ShapeDtypeStruct((n, d), F32),
        scratch_shapes=[pltpu.VMEM((TOP_K, tm, d), F32), pltpu.SemaphoreType.DMA(())],
        compiler_params=_cparams(("arbitrary",), 48),
        name="combine",
    )(slots3, wgt, xs, mod, fin, y_sorted)


MOE_BM = 512


def _moe(li, xn, idx, wgt, rank, counts, xs_mid, mod, fin, w_gu, bgu_r, w_down, bd, nl, nb, final):
    n = xn.shape[0]
    bm = MOE_BM
    tm = _pick(math.gcd(n, nl), (1024, 512, 256, 128))
    n_blocks = -(-(n * TOP_K) // bm) + N_EXPERTS
    cnt = counts[0, :N_EXPERTS].astype(jnp.int32)
    padded = (cnt + bm - 1) // bm * bm
    pad_end = jnp.cumsum(padded)
    pad_start = pad_end - padded
    n_used = (pad_end[-1] // bm).astype(jnp.int32)
    blk = jnp.arange(n_blocks, dtype=jnp.int32)
    first_row = jnp.minimum(blk, n_used - 1) * bm
    block_e = jnp.minimum(jnp.sum(pad_end[None, :] <= first_row[:, None], axis=1),
                          N_EXPERTS - 1).astype(jnp.int32)
    start_row = jnp.zeros((1, LANES), F32).at[0, :N_EXPERTS].set(pad_start.astype(F32))
    slots = _slot_call(idx, rank, start_row)
    slots3 = slots[:, :TOP_K].reshape(n // tm, 1, tm * TOP_K)
    last_blk = jnp.where(cnt > 0, pad_end // bm - 1, -1).astype(jnp.int32)
    fill = jnp.zeros((1, LANES), jnp.int32).at[0, :N_EXPERTS].set(last_blk).at[0, N_EXPERTS].set(n_used)
    x_sorted = _dispatch_call(fill, slots3, xn, n_blocks, bm)
    y_sorted = _experts_call(li, block_e, n_used.reshape(1), x_sorted, w_gu, bgu_r, w_down, bd, bm)
    return _combine_call(slots3, wgt, xs_mid, mod, fin, y_sorted, nl, nb, final)


def _rope_tables(nl, nc):
    rows = nl // GRID_W
    r = np.broadcast_to(np.arange(rows)[:, None], (rows, GRID_W)).reshape(-1).astype(np.float32)
    c = np.broadcast_to(np.arange(GRID_W)[None, :], (rows, GRID_W)).reshape(-1).astype(np.float32)
    n_freq = DK // 4
    inv = (ROPE_BASE ** (-jnp.arange(n_freq, dtype=F32) / n_freq))
    ang = jnp.concatenate([jnp.asarray(r)[:, None] * inv, jnp.asarray(c)[:, None] * inv], axis=-1)
    cos, sin = jnp.cos(ang), jnp.sin(ang)
    cos2 = jnp.concatenate([cos, cos], axis=-1)
    sin2 = jnp.concatenate([-sin, sin], axis=-1)
    cos_t = jnp.concatenate([cos2, jnp.ones((nc, DK), F32)], axis=0)
    sin_t = jnp.concatenate([sin2, jnp.zeros((nc, DK), F32)], axis=0)
    return cos_t, sin_t


def kernel(x, c, ctx, c_ctx, w_ada, b_ada, norm1, norm2, w_in, gla_wa2, gla_ba, gla_norm, lru_conv_w, lru_conv_b, lru_wa, lru_ba, lru_wi, lru_bi, lru_lam, ret_norm, w_branch, w_out, w_router, b_router, w_gu, b_gu, w_down, b_down, final_norm):
    nb, nl, d = x.shape
    nc = ctx.shape[1]
    depth = w_ada.shape[0]
    n_lat = nb * nl
    f = w_down.shape[2]

    xs = jnp.concatenate([x.reshape(n_lat, d), ctx.reshape(nb * nc, d)], axis=0)
    mod_rows = -(-(nb + 1) // SUBLANES) * SUBLANES
    cc = jnp.zeros((mod_rows, d), F32).at[:nb].set(c).at[nb].set(c_ctx)
    mod_all = _ada_call(cc, w_ada, b_ada).reshape(depth, mod_rows, N_MOD, d)
    cos_t, sin_t = _rope_tables(nl, nc)
    bgu_r = b_gu.reshape(depth, N_EXPERTS, 2 * f // (2 * LANES), LANES, 2)
    bgu_r = jnp.swapaxes(bgu_r, -1, -2).reshape(depth, N_EXPERTS, 1, 2 * f)
    bd_r = b_down.reshape(depth, N_EXPERTS, 1, d)
    lru_buf = jnp.zeros((xs.shape[0], lru_conv_w.shape[2]), _MXU)

    out = None
    for li in range(depth):
        last = li == depth - 1
        mod = mod_all[li]
        w = w_in[li]
        w_in_r = jnp.concatenate(
            [w[:, :3072], w[:, 3104:], w[:, 3072:3104],
             jnp.zeros((d, N_PROJ - w.shape[1]), w.dtype)], axis=1).astype(_MXU)
        p = _proj_call(xs, mod, norm1[li].reshape(1, d), w_in_r, nl, nb)

        wa2p = jnp.zeros((2, LANES, HEADS * DK), F32)
        wa2p = wa2p.at[0, :GLA_RANK].set(gla_wa2[li, 0]).at[1, GLA_RANK:2 * GLA_RANK].set(gla_wa2[li, 1])
        gla = _la_call("gla", p, (wa2p.astype(_MXU), gla_ba[li].reshape(2, 1, HEADS * DK)),
                       gla_norm[li].reshape(1, HEADS * DV), nb, nl, nc)
        ret = _la_call("ret", p, (cos_t, sin_t), ret_norm[li].reshape(1, HEADS * DV), nb, nl, nc)

        lw = (lru_conv_w[li], lru_conv_b[li].reshape(1, -1), (0.5 * lru_wa[li]).astype(_MXU),
              0.5 * lru_ba[li], (0.5 * lru_wi[li]).astype(_MXU), 0.5 * lru_bi[li], lru_lam[li])
        width = lru_conv_w.shape[2]
        lru_c, h_ctx = _lru_call(p, lru_buf, jnp.zeros((nb, 2, width), F32), lw, nb, nc, n_lat // nc)
        lru, _ = _lru_call(p, lru_c, h_ctx, lw, nb, nl, 0)
        lru_buf = lru

        n_rows = n_lat if last else xs.shape[0]
        wr = jnp.zeros((d, LANES), F32).at[:, :N_EXPERTS].set(w_router[li]).astype(_MXU)
        br = jnp.full((1, LANES), NEG_BIG, F32).at[0, :N_EXPERTS].set(b_router[li])
        xs_mid, xn2, idx, wgt, rank, counts = _finish_call(
            gla, lru, ret, p, w_branch[li].astype(_MXU), w_out[li].astype(_MXU), xs, mod,
            norm2[li].reshape(1, d), wr, br, n_rows, nl, nb)

        out = _moe(li, xn2, idx, wgt, rank, counts, xs_mid, mod, final_norm.reshape(1, d),
                   w_gu, bgu_r, w_down, bd_r, nl, nb, last)
        xs = out
    return out.reshape(nb, nl, d)
```

```python
import functools
import math

import numpy as np
import jax
import jax.numpy as jnp
from jax import lax
from jax.experimental import pallas as pl
from jax.experimental.pallas import tpu as pltpu

F32 = jnp.float32
_MXU = jnp.bfloat16

EPS = 1e-6
N_MOD = 6
GRID_W = 64
CHUNK = 64
HEADS = 4
DK = 128
DV = 256
GLA_RANK = 16
GLA_GATE_NORM = 16.0
LRU_BLOCKS = 4
LRU_C = 8.0
ROPE_BASE = 10000.0
N_EXPERTS = 32
TOP_K = 4
SWIGLU_LIMIT = 7.0
SWIGLU_ALPHA = 1.702
LANES = 128
SUBLANES = 8
NEG_BIG = -1e30

C_GLA_Q, C_GLA_K, C_GLA_V, C_GLA_G = 0, 512, 1024, 2048
C_LRU_X, C_LRU_G = 3072, 4096
C_RET_Q, C_RET_K, C_RET_V, C_RET_G = 5120, 5632, 6144, 7168
C_MERGE = 8192
C_LR = 11264
N_PROJ = 11520
PROJ_TN = 3840


def _pick(n, prefs):
    for p in prefs:
        if n % p == 0:
            return p
    raise ValueError(f"no tile for {n} in {prefs}")


def _cparams(sem, vmem_mb):
    return pltpu.CompilerParams(dimension_semantics=sem, vmem_limit_bytes=vmem_mb * 1024 * 1024)


def _dot(a, b):
    return jnp.dot(a, b, preferred_element_type=F32)


def _dot_nt(a, b):
    return lax.dot_general(a, b, (((1,), (1,)), ((), ())), preferred_element_type=F32)


def _dot_tn(a, b):
    return lax.dot_general(a, b, (((0,), (0,)), ((), ())), preferred_element_type=F32)


def _sigmoid(x):
    return 0.5 * jnp.tanh(0.5 * x) + 0.5


def _log_sigmoid(x):
    return jnp.minimum(x, 0.0) - jnp.log(1.0 + jnp.exp(-jnp.abs(x)))


def _silu(x):
    return x * _sigmoid(x)


def _gelu_tanh(x):
    return 0.5 * x * (1.0 + jnp.tanh(math.sqrt(2.0 / math.pi) * (x + 0.044715 * (x * x * x))))


def _rms(x, gain):
    return x * lax.rsqrt(jnp.mean(x * x, axis=-1, keepdims=True) + EPS) * gain


def _dot01_exact(tri, g):
    hi = g.astype(_MXU)
    r1 = g - hi.astype(F32)
    mid = r1.astype(_MXU)
    lo = (r1 - mid.astype(F32)).astype(_MXU)
    return _dot(tri, hi) + _dot(tri, mid) + _dot(tri, lo)


def _ada_kernel(c_ref, w_ref, b_ref, o_ref):
    s = _silu(c_ref[...])
    o_ref[...] = _dot(s.astype(_MXU), w_ref[...].astype(_MXU)) + b_ref[...]


def _ada_call(cc, w_ada, b_ada):
    depth, d, n = w_ada.shape
    rows = cc.shape[0]
    tn = _pick(n, (1536, 1024, 512, 128))
    return pl.pallas_call(
        _ada_kernel,
        grid=(depth, n // tn),
        in_specs=[pl.BlockSpec((rows, d), lambda l, j: (0, 0)),
                  pl.BlockSpec((None, d, tn), lambda l, j: (l, 0, j)),
                  pl.BlockSpec((None, 1, tn), lambda l, j: (l, 0, j))],
        out_specs=pl.BlockSpec((None, rows, tn), lambda l, j: (l, 0, j)),
        out_shape=jax.ShapeDtypeStruct((depth, rows, n), F32),
        compiler_params=_cparams(("arbitrary", "arbitrary"), 24),
        name="ada",
    )(cc, w_ada, b_ada.reshape(depth, 1, n))


def _row_specs(parts, tm, d):
    offs = np.cumsum([0] + [p.shape[0] // tm for p in parts])
    specs = []
    for k, p in enumerate(parts):
        lo, n_tiles = int(offs[k]), p.shape[0] // tm

        def imap(i, *rest, lo=lo, n_tiles=n_tiles):
            return (jnp.clip(i - lo, 0, n_tiles - 1), 0)
        specs.append(pl.BlockSpec((tm, d), imap))
    return specs, [int(o) for o in offs[1:-1]]


def _pick_rows(refs, bounds, rows=slice(None)):
    val = refs[0][rows, :]
    for ref, lo in zip(refs[1:], bounds):
        val = jnp.where(pl.program_id(0) >= lo, ref[rows, :], val)
    return val


def _proj_kernel(*refs, bounds):
    n_parts = len(bounds) + 1
    x_refs = refs[:n_parts]
    mod_ref, g_ref, w_ref, o_ref, xn_ref = refs[n_parts:]

    @pl.when(pl.program_id(1) == 0)
    def _():
        y = _rms(_pick_rows(x_refs, bounds), g_ref[...])
        xn_ref[...] = (y * (1.0 + mod_ref[1:2, :]) + mod_ref[0:1, :]).astype(xn_ref.dtype)

    o_ref[...] = _dot(xn_ref[...], w_ref[...]).astype(o_ref.dtype)


def _proj_call(x_parts, mod, gain, w_in_r, nl, nb):
    t = sum(p.shape[0] for p in x_parts)
    d = x_parts[0].shape[1]
    tm = _pick(math.gcd(math.gcd(*[p.shape[0] for p in x_parts], t), nl), (1024, 512, 256))
    x_specs, bounds = _row_specs(x_parts, tm, d)
    return pl.pallas_call(
        functools.partial(_proj_kernel, bounds=bounds),
        grid=(t // tm, N_PROJ // PROJ_TN),
        in_specs=x_specs + [
            pl.BlockSpec((None, N_MOD, d), lambda i, j: (jnp.minimum(i * tm // nl, nb), 0, 0)),
            pl.BlockSpec((1, d), lambda i, j: (0, 0)),
            pl.BlockSpec((d, PROJ_TN), lambda i, j: (0, j))],
        out_specs=pl.BlockSpec((tm, PROJ_TN), lambda i, j: (i, j)),
        out_shape=jax.ShapeDtypeStruct((t, N_PROJ), _MXU),
        scratch_shapes=[pltpu.VMEM((tm, d), _MXU)],
        compiler_params=_cparams(("arbitrary", "arbitrary"), 56),
        name="proj",
    )(*x_parts, mod, gain, w_in_r)


def _la_kernel(*refs, kind, tt, n_ct, n_lt):
    if kind == "gla":
        (q_ref, k_ref, v_ref, gate_ref, lr_ref, wa2_ref, ba_ref, gain_ref,
         out_ref, st_ref, of_ref) = refs
    else:
        (q_ref, k_ref, v_ref, gate_ref, cos_ref, sin_ref, gain_ref,
         out_ref, st_ref, of_ref, dm_ref) = refs
    ph = pl.program_id(1)
    s = pl.program_id(2)
    ck = CHUNK if kind == "gla" else tt
    n_chunks = tt // ck

    @pl.when(s == 0)
    def _():
        st_ref[...] = jnp.zeros_like(st_ref)

    row = lax.broadcasted_iota(jnp.int32, (tt, tt), 0)
    col = lax.broadcasted_iota(jnp.int32, (tt, tt), 1)
    shift = ck.bit_length() - 1
    same_chunk = (row >> shift) == (col >> shift)
    tpos = lax.broadcasted_iota(jnp.int32, (tt, 1), 0).astype(F32)

    if kind == "ret":
        first = jnp.logical_and(pl.program_id(0) == 0, jnp.logical_and(ph == 0, s == 0))

        @pl.when(first)
        def _():
            for h in range(HEADS):
                lg = math.log(1.0 - 2.0 ** (-5.0 - h))
                dm_ref[h] = jnp.where(col <= row, jnp.exp((row - col).astype(F32) * lg), 0.0)
                dm_ref[HEADS + h] = jnp.where(col > row, jnp.exp((col - row).astype(F32) * lg), 0.0)

    def tile_outputs(backward):
        if backward:
            mask = jnp.logical_and(same_chunk, col > row)
            tri = jnp.logical_and(same_chunk, col >= row).astype(_MXU)
        else:
            mask = jnp.logical_and(same_chunk, col <= row)
            tri = jnp.logical_and(same_chunk, col <= row).astype(_MXU)
        order = list(reversed(range(n_chunks))) if backward else list(range(n_chunks))
        if kind == "gla":
            d = 1 if backward else 0
            z = _dot(lr_ref[...], wa2_ref[d]) + ba_ref[d]
            g = _log_sigmoid(z) * (1.0 / GLA_GATE_NORM)
            big_g = _dot01_exact(tri, g)
            lasts = [big_g[c * ck:c * ck + 1, :] if backward else big_g[(c + 1) * ck - 1:(c + 1) * ck, :]
                     for c in range(n_chunks)]
            g_last = jnp.concatenate([jnp.broadcast_to(r, (ck, r.shape[1])) for r in lasts], axis=0)
            e_pos = jnp.exp(big_g)
            e_neg = jnp.exp(-big_g)
            e_end = jnp.exp(g_last - big_g)
            decs = [jnp.exp(r) for r in lasts]
        outs = []
        for h in range(HEADS):
            ks = slice(h * DK, (h + 1) * DK)
            vs = slice(h * DV, (h + 1) * DV)
            qh = q_ref[:, ks].astype(F32)
            kh = k_ref[:, ks].astype(F32)
            vh = v_ref[:, vs]
            if kind == "gla":
                qh = qh * DK ** -0.5
                q_dec = (qh * e_pos[:, ks]).astype(_MXU)
                k_inv = (kh * e_neg[:, ks]).astype(_MXU)
                k_end = (kh * e_end[:, ks]).astype(_MXU)
                sc = jnp.where(mask, _dot_nt(q_dec, k_inv), 0.0)
                dec_h = [dcy[:, ks] for dcy in decs]
            else:
                kh = kh * DK ** -0.5
                cos = cos_ref[...]
                sin = sin_ref[...]
                qh = qh * cos + pltpu.roll(qh, DK // 2, 1) * sin
                kh = kh * cos + pltpu.roll(kh, DK // 2, 1) * sin
                lg = math.log(1.0 - 2.0 ** (-5.0 - h))
                steps = (ck - tpos) if backward else (tpos + 1.0)
                q_dec = (qh * jnp.exp(steps * lg)).astype(_MXU)
                k_end = (kh * jnp.exp((ck - steps) * lg)).astype(_MXU)
                sc = _dot_nt(qh.astype(_MXU), kh.astype(_MXU)) * dm_ref[(HEADS if backward else 0) + h]
                dec_h = [math.exp(ck * lg)] * n_chunks
            o_intra = _dot(sc.astype(_MXU), vh)
            st = st_ref[h]
            parts = [None] * n_chunks
            for c in order:
                rs = slice(c * ck, (c + 1) * ck)
                parts[c] = o_intra[rs] + _dot_nt(q_dec[rs], st.astype(_MXU))
                st = st * dec_h[c] + _dot_tn(vh[rs], k_end[rs])
            st_ref[h] = st
            outs.append(parts[0] if n_chunks == 1 else jnp.concatenate(parts, axis=0))
        return jnp.concatenate(outs, axis=1)

    @pl.when(ph == 0)
    def _():
        base = pl.multiple_of(s * tt, tt)
        of_ref[pl.ds(base, tt), :] = tile_outputs(False)

    @pl.when(ph == 1)
    def _():
        loc = jnp.where(s < n_ct, n_ct - 1 - s, n_ct + n_lt - 1 - (s - n_ct))
        base = pl.multiple_of(loc * tt, tt)
        o = of_ref[pl.ds(base, tt), :] + tile_outputs(True)
        gate = gate_ref[...].astype(F32)
        parts = []
        for h in range(HEADS):
            oh = o[:, h * DV:(h + 1) * DV]
            if kind == "ret":
                oh = oh - jnp.mean(oh, axis=-1, keepdims=True)
            parts.append(oh * lax.rsqrt(jnp.mean(oh * oh, axis=-1, keepdims=True) + EPS))
        normed = jnp.concatenate(parts, axis=1) * gain_ref[...]
        out_ref[...] = (normed * _silu(gate)).astype(out_ref.dtype)


def _la_call(kind, p, extra, gain, nb, nl, nc):
    t = p.shape[0]
    tt = _pick(math.gcd(nl, nc), (256, 128, 64))
    n_ct, n_lt = nc // tt, nl // tt
    lat_tiles = nb * n_lt
    cq, ck, cv, cg = ((C_GLA_Q, C_GLA_K, C_GLA_V, C_GLA_G) if kind == "gla"
                      else (C_RET_Q, C_RET_K, C_RET_V, C_RET_G))
    hd = HEADS * DK
    hv = HEADS * DV

    def loc_of(ph, s):
        back = jnp.where(s < n_ct, n_ct - 1 - s, n_ct + n_lt - 1 - (s - n_ct))
        return jnp.where(ph == 0, s, back)

    def row_blk(b, loc):
        return jnp.where(loc < n_ct, lat_tiles + b * n_ct + loc, b * n_lt + (loc - n_ct))

    def in_map(cblk):
        return lambda b, ph, s: (row_blk(b, loc_of(ph, s)), cblk)

    def second_pass_map(cblk):
        return lambda b, ph, s: (row_blk(b, loc_of(1, jnp.where(ph == 0, 0, s))), cblk)

    in_specs = [pl.BlockSpec((tt, hd), in_map(cq // hd)),
                pl.BlockSpec((tt, hd), in_map(ck // hd)),
                pl.BlockSpec((tt, hv), in_map(cv // hv)),
                pl.BlockSpec((tt, hv), second_pass_map(cg // hv))]
    args = [p, p, p, p]
    if kind == "gla":
        wa2p, ba = extra
        in_specs += [pl.BlockSpec((tt, LANES), in_map(C_LR // LANES)),
                     pl.BlockSpec((2, LANES, hd), lambda b, ph, s: (0, 0, 0)),
                     pl.BlockSpec((2, 1, hd), lambda b, ph, s: (0, 0, 0))]
        args += [p, wa2p, ba]
    else:
        cos_t, sin_t = extra

        def rope_map(b, ph, s):
            loc = loc_of(ph, s)
            return (jnp.where(loc < n_ct, n_lt + loc, loc - n_ct), 0)
        in_specs += [pl.BlockSpec((tt, DK), rope_map), pl.BlockSpec((tt, DK), rope_map)]
        args += [cos_t, sin_t]
    in_specs.append(pl.BlockSpec((1, hv), lambda b, ph, s: (0, 0)))
    args.append(gain)
    return pl.pallas_call(
        functools.partial(_la_kernel, kind=kind, tt=tt, n_ct=n_ct, n_lt=n_lt),
        grid=(nb, 2, n_ct + n_lt),
        in_specs=in_specs,
        out_specs=pl.BlockSpec((tt, hv), second_pass_map(0)),
        out_shape=jax.ShapeDtypeStruct((t, hv), _MXU),
        scratch_shapes=[pltpu.VMEM((HEADS, DV, DK), F32),
                        pltpu.VMEM((nc + nl, hv), F32)]
        + ([pltpu.VMEM((2 * HEADS, tt, tt), F32)] if kind == "ret" else []),
        compiler_params=_cparams(("arbitrary", "arbitrary", "arbitrary"), 48),
        name=kind,
    )(*args)


def _scan_group(a, b, h, reverse):
    row = lax.broadcasted_iota(jnp.int32, a.shape, 0)
    for sft in (1, 2, 4):
        if reverse:
            a_sh = pltpu.roll(a, SUBLANES - sft, 0)
            b_sh = pltpu.roll(b, SUBLANES - sft, 0)
            m = row < SUBLANES - sft
        else:
            a_sh = pltpu.roll(a, sft, 0)
            b_sh = pltpu.roll(b, sft, 0)
            m = row >= sft
        b = jnp.where(m, a * b_sh + b, b)
        a = jnp.where(m, a * a_sh, a)
    hh = a * h + b
    return hh, (hh[0:1, :] if reverse else hh[SUBLANES - 1:SUBLANES, :])


def _lru_kernel(x_ref, gel_ref, cw_ref, cb_ref, wa_ref, ba_ref, wi_ref, bi_ref, lam_ref, h0_ref,
                out_ref, hn_ref, xpad_ref, xc_ref, hf_ref, *, seg, tile):
    n_tiles = seg // tile
    groups = tile // SUBLANES
    cw = cw_ref[...]
    cb = cb_ref[...]
    zeros8 = jnp.zeros((SUBLANES, xpad_ref.shape[1]), F32)
    xpad_ref[0:SUBLANES, :] = zeros8
    xpad_ref[SUBLANES + seg:2 * SUBLANES + seg, :] = zeros8

    def copy_in(i, carry):
        t0 = pl.multiple_of(i * tile, tile)
        xpad_ref[pl.ds(t0 + SUBLANES, tile), :] = x_ref[pl.ds(t0, tile), :].astype(F32)
        return carry
    lax.fori_loop(0, n_tiles, copy_in, 0)

    neg_c = [(-0.5 * LRU_C) * _log_sigmoid(lam_ref[d:d + 1, :]) for d in range(2)]

    def gates(xc, d):
        xb = xc.astype(_MXU)
        tr = jnp.tanh(_dot(xb, wa_ref[d]) + ba_ref[d:d + 1, :])
        ig = 0.5 * jnp.tanh(_dot(xb, wi_ref[d]) + bi_ref[d:d + 1, :]) + 0.5
        nla = neg_c[d] * tr + neg_c[d]
        a = jnp.exp(-nla)
        b = jnp.sqrt(jnp.tanh(nla) * (a * a + 1.0)) * (ig * xc)
        return a, b

    def fwd_tile(i, h):
        t0 = pl.multiple_of(i * tile, tile)
        win = xpad_ref[pl.ds(t0, tile + 2 * SUBLANES), :]
        xc = (cw[0:1, :] * win[6:6 + tile] + cw[1:2, :] * win[7:7 + tile]
              + cw[2:3, :] * win[8:8 + tile] + cw[3:4, :] * win[9:9 + tile] + cb)
        xc_ref[pl.ds(t0, tile), :] = xc
        a, b = gates(xc, 0)
        for g in range(groups):
            rs = slice(g * SUBLANES, (g + 1) * SUBLANES)
            hh, h = _scan_group(a[rs], b[rs], h, False)
            hf_ref[pl.ds(t0 + g * SUBLANES, SUBLANES), :] = hh
        return h
    h_f = lax.fori_loop(0, n_tiles, fwd_tile, h0_ref[0:1, :])

    def bwd_tile(j, h):
        i = n_tiles - 1 - j
        t0 = pl.multiple_of(i * tile, tile)
        xc = xc_ref[pl.ds(t0, tile), :]
        a, b = gates(xc, 1)
        for g in reversed(range(groups)):
            rs = slice(g * SUBLANES, (g + 1) * SUBLANES)
            hh, h = _scan_group(a[rs], b[rs], h, True)
            rows = pl.ds(t0 + g * SUBLANES, SUBLANES)
            hsum = hf_ref[rows, :] + hh
            hf_ref[rows, :] = hsum
        gel = gel_ref[pl.ds(t0, tile), :].astype(F32)
        out_ref[pl.ds(t0, tile), :] = (hf_ref[pl.ds(t0, tile), :] * _gelu_tanh(gel)).astype(out_ref.dtype)
        return h
    h_b = lax.fori_loop(0, n_tiles, bwd_tile, h0_ref[1:2, :])
    hn_ref[0:1, :] = h_f
    hn_ref[1:2, :] = h_b


def _lru_call(p, prev_out, h0, lw, nb, seg, row_blk0):
    cw, cb, wa, ba, wi, bi, lam = lw
    t = p.shape[0]
    width = LRU_BLOCKS * DV
    cbw = width // LRU_BLOCKS
    tile = _pick(seg, (512, 256, 128, 64))
    xblk = C_LRU_X // cbw
    gblk = C_LRU_G // cbw
    in_specs = [pl.BlockSpec((seg, cbw), lambda b, c: (row_blk0 + b, xblk + c)),
                pl.BlockSpec((seg, cbw), lambda b, c: (row_blk0 + b, gblk + c)),
                pl.BlockSpec((4, cbw), lambda b, c: (0, c)),
                pl.BlockSpec((1, cbw), lambda b, c: (0, c)),
                pl.BlockSpec((2, None, cbw, cbw), lambda b, c: (0, c, 0, 0)),
                pl.BlockSpec((2, cbw), lambda b, c: (0, c)),
                pl.BlockSpec((2, None, cbw, cbw), lambda b, c: (0, c, 0, 0)),
                pl.BlockSpec((2, cbw), lambda b, c: (0, c)),
                pl.BlockSpec((2, cbw), lambda b, c: (0, c)),
                pl.BlockSpec((None, 2, cbw), lambda b, c: (b, 0, c))]
    in_specs.append(pl.BlockSpec(memory_space=pl.ANY))
    args = [p, p, cw, cb, wa, ba, wi, bi, lam, h0, prev_out]

    def body(*refs):
        _lru_kernel(*(refs[:10] + refs[11:]), seg=seg, tile=tile)

    return pl.pallas_call(
        body,
        grid=(nb, LRU_BLOCKS),
        in_specs=in_specs,
        out_specs=[pl.BlockSpec((seg, cbw), lambda b, c: (row_blk0 + b, c)),
                   pl.BlockSpec((None, 2, cbw), lambda b, c: (b, 0, c))],
        out_shape=[jax.ShapeDtypeStruct((t, width), _MXU),
                   jax.ShapeDtypeStruct((nb, 2, width), F32)],
        scratch_shapes=[pltpu.VMEM((seg + 2 * SUBLANES, cbw), F32),
                        pltpu.VMEM((seg, cbw), F32),
                        pltpu.VMEM((seg, cbw), F32)],
        input_output_aliases={10: 0},
        compiler_params=_cparams(("arbitrary", "arbitrary"), 48),
        name="lru",
    )(*args)


def _finish_kernel(*refs, bounds):
    n_parts = len(bounds) + 1
    xs_refs = refs[:n_parts]
    (gla_ref, lru_ref, ret_ref, m0_ref, m1_ref, m2_ref, wb_ref, wo_ref,
     mod_ref, g2_ref, wr_ref, br_ref, tril_ref,
     xo_ref, xn_ref, idx_ref, wgt_ref, rank_ref, cnt_ref, carry_ref) = refs[n_parts:]

    @pl.when(pl.program_id(0) == 0)
    def _():
        carry_ref[...] = jnp.zeros_like(carry_ref)

    hm = tril_ref.shape[0]
    total = carry_ref[0:1, :]
    for sub in range(xo_ref.shape[0] // hm):
        rs = slice(sub * hm, (sub + 1) * hm)
        merged = (_sigmoid(m0_ref[rs, :].astype(F32)) * _dot(gla_ref[rs, :], wb_ref[0])
                  + _sigmoid(m1_ref[rs, :].astype(F32)) * _dot(lru_ref[rs, :], wb_ref[1])
                  + _sigmoid(m2_ref[rs, :].astype(F32)) * _dot(ret_ref[rs, :], wb_ref[2]))
        x = _pick_rows(xs_refs, bounds, rs) + mod_ref[2:3, :] * _dot(merged.astype(_MXU), wo_ref[...])
        xo_ref[rs, :] = x
        xn = _rms(x, g2_ref[...]) * (1.0 + mod_ref[4:5, :]) + mod_ref[3:4, :]
        xn_ref[rs, :] = xn

        logits = _dot(xn.astype(_MXU), wr_ref[...]) + br_ref[...]
        lane = lax.broadcasted_iota(jnp.int32, logits.shape, 1)
        ids, vals = [], []
        for _ in range(TOP_K):
            m = jnp.max(logits, axis=1, keepdims=True)
            sel = jnp.min(jnp.where(logits == m, lane.astype(F32), float(LANES)), axis=1,
                          keepdims=True).astype(jnp.int32)
            ids.append(sel)
            vals.append(m)
            logits = jnp.where(lane == sel, -jnp.inf, logits)
        ex = [jnp.exp(v - vals[0]) for v in vals]
        denom = ex[0] + ex[1] + ex[2] + ex[3]
        onehot = jnp.zeros(logits.shape, F32)
        for sel in ids:
            onehot = onehot + (lane == sel).astype(F32)
        before = _dot(tril_ref[...], onehot.astype(_MXU)) + total
        idx_o = jnp.zeros(logits.shape, jnp.int32)
        rank_o = jnp.zeros(logits.shape, jnp.int32)
        wgt_o = jnp.zeros(logits.shape, F32)
        for k in range(TOP_K):
            rk = jnp.sum(jnp.where(lane == ids[k], before, 0.0), axis=1, keepdims=True)
            idx_o = jnp.where(lane == k, ids[k], idx_o)
            rank_o = jnp.where(lane == k, rk.astype(jnp.int32), rank_o)
            wgt_o = jnp.where(lane == k, ex[k] / denom, wgt_o)
        idx_ref[rs, :] = idx_o
        rank_ref[rs, :] = rank_o
        wgt_ref[rs, :] = wgt_o
        total = total + jnp.sum(onehot, axis=0, keepdims=True)
    carry_ref[...] = jnp.broadcast_to(total, carry_ref.shape)
    cnt_ref[...] = jnp.broadcast_to(total, cnt_ref.shape)


def _finish_call(gla, lru, ret, p, wb, wo, xs_parts, mod, g2, wr, br, n_rows, nl, nb):
    d = xs_parts[0].shape[1]
    tm = _pick(math.gcd(math.gcd(*[q.shape[0] for q in xs_parts], n_rows), nl), (512, 256))
    xs_specs, bounds = _row_specs(xs_parts, tm, d)
    hm = tm
    tril = jnp.asarray(np.tril(np.ones((hm, hm), np.float32), -1), _MXU)
    row = lambda i: (i, 0)
    const2 = lambda i: (0, 0)
    mblk = C_MERGE // d
    in_specs = xs_specs + [
                pl.BlockSpec((tm, d), row), pl.BlockSpec((tm, d), row), pl.BlockSpec((tm, d), row),
                pl.BlockSpec((tm, d), lambda i: (i, mblk)),
                pl.BlockSpec((tm, d), lambda i: (i, mblk + 1)),
                pl.BlockSpec((tm, d), lambda i: (i, mblk + 2)),
                pl.BlockSpec((3, d, d), lambda i: (0, 0, 0)),
                pl.BlockSpec((d, d), const2),
                pl.BlockSpec((None, N_MOD, d), lambda i: (jnp.minimum(i * tm // nl, nb), 0, 0)),
                pl.BlockSpec((1, d), const2),
                pl.BlockSpec((d, LANES), const2),
                pl.BlockSpec((1, LANES), const2),
                pl.BlockSpec((hm, hm), const2)]
    out_specs = [pl.BlockSpec((tm, d), row), pl.BlockSpec((tm, d), row),
                 pl.BlockSpec((tm, LANES), row), pl.BlockSpec((tm, LANES), row),
                 pl.BlockSpec((tm, LANES), row), pl.BlockSpec((SUBLANES, LANES), const2)]
    out_shape = [jax.ShapeDtypeStruct((n_rows, d), F32), jax.ShapeDtypeStruct((n_rows, d), F32),
                 jax.ShapeDtypeStruct((n_rows, LANES), jnp.int32),
                 jax.ShapeDtypeStruct((n_rows, LANES), F32),
                 jax.ShapeDtypeStruct((n_rows, LANES), jnp.int32),
                 jax.ShapeDtypeStruct((SUBLANES, LANES), F32)]
    return pl.pallas_call(
        functools.partial(_finish_kernel, bounds=bounds),
        grid=(n_rows // tm,),
        in_specs=in_specs,
        out_specs=out_specs,
        out_shape=out_shape,
        scratch_shapes=[pltpu.VMEM((SUBLANES, LANES), F32)],
        compiler_params=_cparams(("arbitrary",), 56),
        name="finish",
    )(*xs_parts, gla, lru, ret, p, p, p, wb, wo, mod, g2, wr, br, tril)


def _slot_kernel(idx_ref, rank_ref, start_ref, slot_ref):
    idx = idx_ref[...].astype(F32)
    lane = lax.broadcasted_iota(jnp.int32, idx.shape, 1)
    start = start_ref[...]
    out = jnp.zeros(idx.shape, F32)
    for k in range(TOP_K):
        sel = jnp.sum(jnp.where(lane == k, idx, 0.0), axis=1, keepdims=True).astype(jnp.int32)
        st = jnp.sum(jnp.where(lane == sel, start, 0.0), axis=1, keepdims=True)
        out = jnp.where(lane == k, st, out)
    slot_ref[...] = out.astype(jnp.int32) + rank_ref[...]


def _slot_call(idx, rank, pad_start):
    n = idx.shape[0]
    tm = _pick(n, (2048, 1024, 512, 256))
    row = lambda i: (i, 0)
    return pl.pallas_call(
        _slot_kernel,
        grid=(n // tm,),
        in_specs=[pl.BlockSpec((tm, LANES), row), pl.BlockSpec((tm, LANES), row),
                  pl.BlockSpec((1, LANES), lambda i: (0, 0))],
        out_specs=pl.BlockSpec((tm, LANES), row),
        out_shape=jax.ShapeDtypeStruct((n, LANES), jnp.int32),
        compiler_params=_cparams(("arbitrary",), 24),
        name="slots",
    )(idx, rank, pad_start)


def _dispatch_kernel(fill_ref, slot_ref, x_ref, dst_ref, zero_ref, sem, fill_sem, *, tm, bm, n_blocks):
    @pl.when(pl.program_id(0) == 0)
    def _():
        zero_ref[...] = jnp.zeros_like(zero_ref)

        def clear(blk):
            return pltpu.make_async_copy(zero_ref, dst_ref.at[pl.ds(pl.multiple_of(blk * bm, bm), bm)],
                                         fill_sem)

        def tail_start(j, carry):
            clear(j).start()
            return carry

        def tail_wait(j, carry):
            clear(j).wait()
            return carry

        for e in range(N_EXPERTS):
            @pl.when(fill_ref[0, e] >= 0)
            def _():
                clear(fill_ref[0, e]).start()
        lax.fori_loop(fill_ref[0, N_EXPERTS], n_blocks, tail_start, 0)
        for e in range(N_EXPERTS):
            @pl.when(fill_ref[0, e] >= 0)
            def _():
                clear(fill_ref[0, e]).wait()
        lax.fori_loop(fill_ref[0, N_EXPERTS], n_blocks, tail_wait, 0)

    def issue(g, carry):
        t0 = pl.multiple_of(g * SUBLANES, SUBLANES)
        for u in range(SUBLANES):
            for k in range(TOP_K):
                pltpu.make_async_copy(
                    x_ref.at[pl.ds(t0 + u, 1)],
                    dst_ref.at[pl.ds(slot_ref[0, (t0 + u) * TOP_K + k], 1)],
                    sem).start(priority=k % 2)
        return carry
    lax.fori_loop(0, tm // SUBLANES, issue, 0)
    for k in range(TOP_K):
        pltpu.make_async_copy(x_ref, dst_ref.at[pl.ds(0, tm)], sem).wait()


def _dispatch_call(fill, slots3, xn, n_blocks, bm):
    n, d = xn.shape
    tm = slots3.shape[2] // TOP_K
    return pl.pallas_call(
        functools.partial(_dispatch_kernel, tm=tm, bm=bm, n_blocks=n_blocks),
        grid=(n // tm,),
        in_specs=[pl.BlockSpec((1, LANES), lambda i: (0, 0), memory_space=pltpu.SMEM),
                  pl.BlockSpec((None, 1, tm * TOP_K), lambda i: (i, 0, 0), memory_space=pltpu.SMEM),
                  pl.BlockSpec((tm, d), lambda i: (i, 0))],
        out_specs=pl.BlockSpec(memory_space=pl.ANY),
        out_shape=jax.ShapeDtypeStruct((n_blocks * bm, d), F32),
        scratch_shapes=[pltpu.VMEM((bm, d), F32), pltpu.SemaphoreType.DMA(()),
                        pltpu.SemaphoreType.DMA(())],
        compiler_params=_cparams(("arbitrary",), 24),
        name="dispatch",
    )(fill, slots3, xn)


def _experts_kernel(be_ref, nu_ref, x_ref, wgu_ref, bgu_ref, wd_ref, bd_ref, perm_ref, y_ref,
                    wgu_s, wd_s):
    j = pl.program_id(0)
    f = wd_ref.shape[0]
    grp = perm_ref.shape[0]
    half = grp // 2
    active = j < nu_ref[0]
    changed = jnp.logical_or(j == 0, be_ref[j] != be_ref[jnp.maximum(j - 1, 0)])

    @pl.when(jnp.logical_and(active, changed))
    def _():
        for g in range(2 * f // grp):
            cols = slice(g * grp, (g + 1) * grp)
            wgu_s[:, cols] = _dot(wgu_ref[:, cols].astype(_MXU), perm_ref[...]).astype(_MXU)
        wd_s[...] = wd_ref[...].astype(_MXU)

    @pl.when(active)
    def _():
        gu = _dot(x_ref[...].astype(_MXU), wgu_s[...]) + bgu_ref[...]
        acts = []
        for g in range(2 * f // grp):
            gate = jnp.minimum(gu[:, g * grp:g * grp + half], SWIGLU_LIMIT)
            up = jnp.clip(gu[:, g * grp + half:(g + 1) * grp], -SWIGLU_LIMIT, SWIGLU_LIMIT)
            acts.append((gate * _sigmoid(SWIGLU_ALPHA * gate) * (up + 1.0)).astype(_MXU))
        y_ref[...] = _dot(jnp.concatenate(acts, axis=1), wd_s[...]) + bd_ref[...]

    @pl.when(jnp.logical_not(active))
    def _():
        y_ref[...] = jnp.zeros_like(y_ref)


def _experts_call(li, block_e, n_used, x_sorted, w_gu, bgu_r, w_down, bd, bm):
    n_slots, d = x_sorted.shape
    f = w_down.shape[2]
    grp = 2 * LANES
    src = np.concatenate([np.arange(0, grp, 2), np.arange(1, grp, 2)])
    perm_np = np.zeros((grp, grp), np.float32)
    perm_np[src, np.arange(grp)] = 1.0
    perm = jnp.asarray(perm_np, _MXU)

    def xmap(j, be, nu):
        return (jnp.minimum(j, nu[0] - 1), 0)

    def wmap(j, be, nu):
        return (li, be[j], 0, 0)

    grid_spec = pltpu.PrefetchScalarGridSpec(
        num_scalar_prefetch=2,
        grid=(n_slots // bm,),
        in_specs=[pl.BlockSpec((bm, d), xmap),
                  pl.BlockSpec((None, None, d, 2 * f), wmap),
                  pl.BlockSpec((None, None, 1, 2 * f), wmap),
                  pl.BlockSpec((None, None, f, d), wmap),
                  pl.BlockSpec((None, None, 1, d), wmap),
                  pl.BlockSpec((grp, grp), lambda j, be, nu: (0, 0))],
        out_specs=pl.BlockSpec((bm, d), lambda j, be, nu: (j, 0)),
        scratch_shapes=[pltpu.VMEM((d, 2 * f), _MXU), pltpu.VMEM((f, d), _MXU)])
    return pl.pallas_call(
        _experts_kernel,
        grid_spec=grid_spec,
        out_shape=jax.ShapeDtypeStruct((n_slots, d), F32),
        compiler_params=_cparams(("arbitrary",), 56),
        name="experts",
    )(block_e, n_used, x_sorted, w_gu, bgu_r, w_down, bd, perm)


def _combine_kernel(slot_ref, wgt_ref, xs_ref, mod_ref, fin_ref, y_ref, out_ref, buf_ref, sem,
                    *, tm, final):
    def issue(g, carry):
        t0 = pl.multiple_of(g * SUBLANES, SUBLANES)
        for u in range(SUBLANES):
            for k in range(TOP_K):
                pltpu.make_async_copy(
                    y_ref.at[pl.ds(slot_ref[0, (t0 + u) * TOP_K + k], 1)],
                    buf_ref.at[k, pl.ds(t0 + u, 1)], sem).start(priority=k % 2)
        return carry
    lax.fori_loop(0, tm // SUBLANES, issue, 0)
    for k in range(TOP_K):
        pltpu.make_async_copy(y_ref.at[pl.ds(0, tm)], buf_ref.at[k], sem).wait()

    wgt = wgt_ref[...]
    acc = wgt[:, 0:1] * buf_ref[0]
    for k in range(1, TOP_K):
        acc = acc + wgt[:, k:k + 1] * buf_ref[k]
    x = xs_ref[...] + mod_ref[5:6, :] * acc
    out_ref[...] = _rms(x, fin_ref[...]) if final else x


def _combine_call(slots3, wgt, xs, mod, fin, y_sorted, nl, nb, final):
    n, d = xs.shape
    tm = slots3.shape[2] // TOP_K
    return pl.pallas_call(
        functools.partial(_combine_kernel, tm=tm, final=final),
        grid=(n // tm,),
        in_specs=[pl.BlockSpec((None, 1, tm * TOP_K), lambda i: (i, 0, 0), memory_space=pltpu.SMEM),
                  pl.BlockSpec((tm, LANES), lambda i: (i, 0)),
                  pl.BlockSpec((tm, d), lambda i: (i, 0)),
                  pl.BlockSpec((None, N_MOD, d), lambda i: (jnp.minimum(i * tm // nl, nb), 0, 0)),
                  pl.BlockSpec((1, d), lambda i: (0, 0)),
                  pl.BlockSpec(memory_space=pl.ANY)],
        out_specs=pl.BlockSpec((tm, d), lambda i: (i, 0)),
        out_shape=jax.ShapeDtypeStruct((n, d), F32),
        scratch_shapes=[pltpu.VMEM((TOP_K, tm, d), F32), pltpu.SemaphoreType.DMA(())],
        compiler_params=_cparams(("arbitrary",), 48),
        name="combine",
    )(slots3, wgt, xs, mod, fin, y_sorted)


MOE_BM = 512


def _moe(li, xn, idx, wgt, rank, counts, xs_mid, mod, fin, w_gu, bgu_r, w_down, bd, nl, nb, final):
    n = xn.shape[0]
    bm = MOE_BM
    tm = _pick(math.gcd(n, nl), (1024, 512, 256, 128))
    n_blocks = -(-(n * TOP_K) // bm) + N_EXPERTS
    cnt = counts[0, :N_EXPERTS].astype(jnp.int32)
    padded = (cnt + bm - 1) // bm * bm
    pad_end = jnp.cumsum(padded)
    pad_start = pad_end - padded
    n_used = (pad_end[-1] // bm).astype(jnp.int32)
    blk = jnp.arange(n_blocks, dtype=jnp.int32)
    first_row = jnp.minimum(blk, n_used - 1) * bm
    block_e = jnp.minimum(jnp.sum(pad_end[None, :] <= first_row[:, None], axis=1),
                          N_EXPERTS - 1).astype(jnp.int32)
    start_row = jnp.zeros((1, LANES), F32).at[0, :N_EXPERTS].set(pad_start.astype(F32))
    slots = _slot_call(idx, rank, start_row)
    slots3 = slots[:, :TOP_K].reshape(n // tm, 1, tm * TOP_K)
    last_blk = jnp.where(cnt > 0, pad_end // bm - 1, -1).astype(jnp.int32)
    fill = jnp.zeros((1, LANES), jnp.int32).at[0, :N_EXPERTS].set(last_blk).at[0, N_EXPERTS].set(n_used)
    x_sorted = _dispatch_call(fill, slots3, xn, n_blocks, bm)
    y_sorted = _experts_call(li, block_e, n_used.reshape(1), x_sorted, w_gu, bgu_r, w_down, bd, bm)
    return _combine_call(slots3, wgt, xs_mid, mod, fin, y_sorted, nl, nb, final)


def _rope_tables(nl, nc):
    rows = nl // GRID_W
    r = np.broadcast_to(np.arange(rows)[:, None], (rows, GRID_W)).reshape(-1).astype(np.float32)
    c = np.broadcast_to(np.arange(GRID_W)[None, :], (rows, GRID_W)).reshape(-1).astype(np.float32)
    n_freq = DK // 4
    inv = (ROPE_BASE ** (-jnp.arange(n_freq, dtype=F32) / n_freq))
    ang = jnp.concatenate([jnp.asarray(r)[:, None] * inv, jnp.asarray(c)[:, None] * inv], axis=-1)
    cos, sin = jnp.cos(ang), jnp.sin(ang)
    cos2 = jnp.concatenate([cos, cos], axis=-1)
    sin2 = jnp.concatenate([-sin, sin], axis=-1)
    cos_t = jnp.concatenate([cos2, jnp.ones((nc, DK), F32)], axis=0)
    sin_t = jnp.concatenate([sin2, jnp.zeros((nc, DK), F32)], axis=0)
    return cos_t, sin_t


def kernel(x, c, ctx, c_ctx, w_ada, b_ada, norm1, norm2, w_in, gla_wa2, gla_ba, gla_norm, lru_conv_w, lru_conv_b, lru_wa, lru_ba, lru_wi, lru_bi, lru_lam, ret_norm, w_branch, w_out, w_router, b_router, w_gu, b_gu, w_down, b_down, final_norm):
    nb, nl, d = x.shape
    nc = ctx.shape[1]
    depth = w_ada.shape[0]
    n_lat = nb * nl
    f = w_down.shape[2]

    n_tok = n_lat + nb * nc
    xs = (x.reshape(n_lat, d), ctx.reshape(nb * nc, d))
    mod_rows = -(-(nb + 1) // SUBLANES) * SUBLANES
    cc = jnp.zeros((mod_rows, d), F32).at[:nb].set(c).at[nb].set(c_ctx)
    mod_all = _ada_call(cc, w_ada, b_ada).reshape(depth, mod_rows, N_MOD, d)
    cos_t, sin_t = _rope_tables(nl, nc)
    bgu_r = b_gu.reshape(depth, N_EXPERTS, 2 * f // (2 * LANES), LANES, 2)
    bgu_r = jnp.swapaxes(bgu_r, -1, -2).reshape(depth, N_EXPERTS, 1, 2 * f)
    bd_r = b_down.reshape(depth, N_EXPERTS, 1, d)
    lru_buf = jnp.zeros((n_tok, lru_conv_w.shape[2]), _MXU)

    out = None
    for li in range(depth):
        last = li == depth - 1
        mod = mod_all[li]
        w = w_in[li]
        w_in_r = jnp.concatenate(
            [w[:, :3072], w[:, 3104:], w[:, 3072:3104],
             jnp.zeros((d, N_PROJ - w.shape[1]), w.dtype)], axis=1).astype(_MXU)
        p = _proj_call(xs, mod, norm1[li].reshape(1, d), w_in_r, nl, nb)

        wa2p = jnp.zeros((2, LANES, HEADS * DK), F32)
        wa2p = wa2p.at[0, :GLA_RANK].set(gla_wa2[li, 0]).at[1, GLA_RANK:2 * GLA_RANK].set(gla_wa2[li, 1])
        gla = _la_call("gla", p, (wa2p.astype(_MXU), gla_ba[li].reshape(2, 1, HEADS * DK)),
                       gla_norm[li].reshape(1, HEADS * DV), nb, nl, nc)
        ret = _la_call("ret", p, (cos_t, sin_t), ret_norm[li].reshape(1, HEADS * DV), nb, nl, nc)

        lw = (lru_conv_w[li], lru_conv_b[li].reshape(1, -1), (0.5 * lru_wa[li]).astype(_MXU),
              0.5 * lru_ba[li], (0.5 * lru_wi[li]).astype(_MXU), 0.5 * lru_bi[li], lru_lam[li])
        width = lru_conv_w.shape[2]
        lru_c, h_ctx = _lru_call(p, lru_buf, jnp.zeros((nb, 2, width), F32), lw, nb, nc, n_lat // nc)
        lru, _ = _lru_call(p, lru_c, h_ctx, lw, nb, nl, 0)
        lru_buf = lru

        n_rows = n_lat if last else n_tok
        wr = jnp.zeros((d, LANES), F32).at[:, :N_EXPERTS].set(w_router[li]).astype(_MXU)
        br = jnp.full((1, LANES), NEG_BIG, F32).at[0, :N_EXPERTS].set(b_router[li])
        xs_mid, xn2, idx, wgt, rank, counts = _finish_call(
            gla, lru, ret, p, w_branch[li].astype(_MXU), w_out[li].astype(_MXU), xs, mod,
            norm2[li].reshape(1, d), wr, br, n_rows, nl, nb)

        out = _moe(li, xn2, idx, wgt, rank, counts, xs_mid, mod, final_norm.reshape(1, d),
                   w_gu, bgu_r, w_down, bd_r, nl, nb, last)
        xs = (out,)
    return out.reshape(nb, nl, d)
```

```python
import functools
import math

import numpy as np
import jax
import jax.numpy as jnp
from jax import lax
from jax.experimental import pallas as pl
from jax.experimental.pallas import tpu as pltpu

F32 = jnp.float32
_MXU = jnp.bfloat16

EPS = 1e-6
N_MOD = 6
GRID_W = 64
CHUNK = 64
HEADS = 4
DK = 128
DV = 256
GLA_RANK = 16
GLA_GATE_NORM = 16.0
LRU_BLOCKS = 4
LRU_C = 8.0
ROPE_BASE = 10000.0
N_EXPERTS = 32
TOP_K = 4
SWIGLU_LIMIT = 7.0
SWIGLU_ALPHA = 1.702
LANES = 128
SUBLANES = 8
NEG_BIG = -1e30

C_GLA_Q, C_GLA_K, C_GLA_V, C_GLA_G = 0, 512, 1024, 2048
C_LRU_X, C_LRU_G = 3072, 4096
C_RET_Q, C_RET_K, C_RET_V, C_RET_G = 5120, 5632, 6144, 7168
C_MERGE = 8192
C_LR = 11264
N_PROJ = 11520
PROJ_TN = 3840


def _pick(n, prefs):
    for p in prefs:
        if n % p == 0:
            return p
    raise ValueError(f"no tile for {n} in {prefs}")


def _cparams(sem, vmem_mb):
    return pltpu.CompilerParams(dimension_semantics=sem, vmem_limit_bytes=vmem_mb * 1024 * 1024)


def _dot(a, b):
    return jnp.dot(a, b, preferred_element_type=F32)


def _dot_nt(a, b):
    return lax.dot_general(a, b, (((1,), (1,)), ((), ())), preferred_element_type=F32)


def _dot_tn(a, b):
    return lax.dot_general(a, b, (((0,), (0,)), ((), ())), preferred_element_type=F32)


def _sigmoid(x):
    return 0.5 * jnp.tanh(0.5 * x) + 0.5


def _log_sigmoid(x):
    return jnp.minimum(x, 0.0) - jnp.log(1.0 + jnp.exp(-jnp.abs(x)))


def _silu(x):
    return x * _sigmoid(x)


def _gelu_tanh(x):
    return 0.5 * x * (1.0 + jnp.tanh(math.sqrt(2.0 / math.pi) * (x + 0.044715 * (x * x * x))))


def _rms(x, gain):
    return x * lax.rsqrt(jnp.mean(x * x, axis=-1, keepdims=True) + EPS) * gain


def _dot01_exact(tri, g):
    hi = g.astype(_MXU)
    r1 = g - hi.astype(F32)
    mid = r1.astype(_MXU)
    lo = (r1 - mid.astype(F32)).astype(_MXU)
    return _dot(tri, hi) + _dot(tri, mid) + _dot(tri, lo)


def _ada_kernel(c_ref, w_ref, b_ref, o_ref):
    s = _silu(c_ref[...])
    o_ref[...] = _dot(s.astype(_MXU), w_ref[...].astype(_MXU)) + b_ref[...]


def _ada_call(cc, w_ada, b_ada):
    depth, d, n = w_ada.shape
    rows = cc.shape[0]
    tn = _pick(n, (1536, 1024, 512, 128))
    return pl.pallas_call(
        _ada_kernel,
        grid=(depth, n // tn),
        in_specs=[pl.BlockSpec((rows, d), lambda l, j: (0, 0)),
                  pl.BlockSpec((None, d, tn), lambda l, j: (l, 0, j)),
                  pl.BlockSpec((None, 1, tn), lambda l, j: (l, 0, j))],
        out_specs=pl.BlockSpec((None, rows, tn), lambda l, j: (l, 0, j)),
        out_shape=jax.ShapeDtypeStruct((depth, rows, n), F32),
        compiler_params=_cparams(("arbitrary", "arbitrary"), 24),
        name="ada",
    )(cc, w_ada, b_ada.reshape(depth, 1, n))


def _row_specs(parts, tm, d):
    offs = np.cumsum([0] + [p.shape[0] // tm for p in parts])
    specs = []
    for k, p in enumerate(parts):
        lo, n_tiles = int(offs[k]), p.shape[0] // tm

        def imap(i, *rest, lo=lo, n_tiles=n_tiles):
            return (jnp.clip(i - lo, 0, n_tiles - 1), 0)
        specs.append(pl.BlockSpec((tm, d), imap))
    return specs, [int(o) for o in offs[1:-1]]


def _pick_rows(refs, bounds, rows=slice(None)):
    val = refs[0][rows, :]
    for ref, lo in zip(refs[1:], bounds):
        val = jnp.where(pl.program_id(0) >= lo, ref[rows, :], val)
    return val


def _proj_kernel(*refs, bounds):
    n_parts = len(bounds) + 1
    x_refs = refs[:n_parts]
    mod_ref, g_ref, w_ref, o_ref, xn_ref = refs[n_parts:]

    @pl.when(pl.program_id(1) == 0)
    def _():
        y = _rms(_pick_rows(x_refs, bounds), g_ref[...])
        xn_ref[...] = (y * (1.0 + mod_ref[1:2, :]) + mod_ref[0:1, :]).astype(xn_ref.dtype)

    o_ref[...] = _dot(xn_ref[...], w_ref[...]).astype(o_ref.dtype)


def _proj_call(x_parts, mod, gain, w_in_r, nl, nb):
    t = sum(p.shape[0] for p in x_parts)
    d = x_parts[0].shape[1]
    tm = _pick(math.gcd(math.gcd(*[p.shape[0] for p in x_parts], t), nl), (1024, 512, 256))
    x_specs, bounds = _row_specs(x_parts, tm, d)
    return pl.pallas_call(
        functools.partial(_proj_kernel, bounds=bounds),
        grid=(t // tm, N_PROJ // PROJ_TN),
        in_specs=x_specs + [
            pl.BlockSpec((None, N_MOD, d), lambda i, j: (jnp.minimum(i * tm // nl, nb), 0, 0)),
            pl.BlockSpec((1, d), lambda i, j: (0, 0)),
            pl.BlockSpec((d, PROJ_TN), lambda i, j: (0, j))],
        out_specs=pl.BlockSpec((tm, PROJ_TN), lambda i, j: (i, j)),
        out_shape=jax.ShapeDtypeStruct((t, N_PROJ), _MXU),
        scratch_shapes=[pltpu.VMEM((tm, d), _MXU)],
        compiler_params=_cparams(("arbitrary", "arbitrary"), 56),
        name="proj",
    )(*x_parts, mod, gain, w_in_r)


def _la_kernel(*refs, kind, tt, n_ct, n_lt):
    if kind == "gla":
        (q_ref, k_ref, v_ref, gate_ref, lr_ref, wa2_ref, ba_ref, gain_ref,
         out_ref, st_ref, of_ref) = refs
    else:
        (q_ref, k_ref, v_ref, gate_ref, cos_ref, sin_ref, gain_ref,
         out_ref, st_ref, of_ref, dm_ref) = refs
    ph = pl.program_id(1)
    s = pl.program_id(2)
    ck = CHUNK if kind == "gla" else tt
    n_chunks = tt // ck

    @pl.when(s == 0)
    def _():
        st_ref[...] = jnp.zeros_like(st_ref)

    row = lax.broadcasted_iota(jnp.int32, (tt, tt), 0)
    col = lax.broadcasted_iota(jnp.int32, (tt, tt), 1)
    shift = ck.bit_length() - 1
    same_chunk = (row >> shift) == (col >> shift)
    tpos = lax.broadcasted_iota(jnp.int32, (tt, 1), 0).astype(F32)

    if kind == "ret":
        first = jnp.logical_and(pl.program_id(0) == 0, jnp.logical_and(ph == 0, s == 0))

        @pl.when(first)
        def _():
            for h in range(HEADS):
                lg = math.log(1.0 - 2.0 ** (-5.0 - h))
                dm_ref[h] = jnp.where(col <= row, jnp.exp((row - col).astype(F32) * lg), 0.0)
                dm_ref[HEADS + h] = jnp.where(col > row, jnp.exp((col - row).astype(F32) * lg), 0.0)

    def tile_outputs(backward):
        if backward:
            mask = jnp.logical_and(same_chunk, col > row)
            tri = jnp.logical_and(same_chunk, col >= row).astype(_MXU)
        else:
            mask = jnp.logical_and(same_chunk, col <= row)
            tri = jnp.logical_and(same_chunk, col <= row).astype(_MXU)
        order = list(reversed(range(n_chunks))) if backward else list(range(n_chunks))
        if kind == "gla":
            d = 1 if backward else 0
            z = _dot(lr_ref[...], wa2_ref[d]) + ba_ref[d]
            g = _log_sigmoid(z) * (1.0 / GLA_GATE_NORM)
            big_g = _dot01_exact(tri, g)
            lasts = [big_g[c * ck:c * ck + 1, :] if backward else big_g[(c + 1) * ck - 1:(c + 1) * ck, :]
                     for c in range(n_chunks)]
            g_last = jnp.concatenate([jnp.broadcast_to(r, (ck, r.shape[1])) for r in lasts], axis=0)
            e_pos = jnp.exp(big_g)
            e_neg = jnp.exp(-big_g)
            e_end = jnp.exp(g_last - big_g)
            decs = [jnp.exp(r) for r in lasts]
        outs = []
        for h in range(HEADS):
            ks = slice(h * DK, (h + 1) * DK)
            vs = slice(h * DV, (h + 1) * DV)
            qh = q_ref[:, ks].astype(F32)
            kh = k_ref[:, ks].astype(F32)
            vh = v_ref[:, vs]
            if kind == "gla":
                qh = qh * DK ** -0.5
                q_dec = (qh * e_pos[:, ks]).astype(_MXU)
                k_inv = (kh * e_neg[:, ks]).astype(_MXU)
                k_end = (kh * e_end[:, ks]).astype(_MXU)
                sc = jnp.where(mask, _dot_nt(q_dec, k_inv), 0.0)
                dec_h = [dcy[:, ks] for dcy in decs]
            else:
                kh = kh * DK ** -0.5
                cos = cos_ref[...]
                sin = sin_ref[...]
                qh = qh * cos + pltpu.roll(qh, DK // 2, 1) * sin
                kh = kh * cos + pltpu.roll(kh, DK // 2, 1) * sin
                lg = math.log(1.0 - 2.0 ** (-5.0 - h))
                steps = (ck - tpos) if backward else (tpos + 1.0)
                q_dec = (qh * jnp.exp(steps * lg)).astype(_MXU)
                k_end = (kh * jnp.exp((ck - steps) * lg)).astype(_MXU)
                sc = _dot_nt(qh.astype(_MXU), kh.astype(_MXU)) * dm_ref[(HEADS if backward else 0) + h]
                dec_h = [math.exp(ck * lg)] * n_chunks
            o_intra = _dot(sc.astype(_MXU), vh)
            st = st_ref[h]
            parts = [None] * n_chunks
            for c in order:
                rs = slice(c * ck, (c + 1) * ck)
                parts[c] = o_intra[rs] + _dot_nt(q_dec[rs], st.astype(_MXU))
                st = st * dec_h[c] + _dot_tn(vh[rs], k_end[rs])
            st_ref[h] = st
            outs.append(parts[0] if n_chunks == 1 else jnp.concatenate(parts, axis=0))
        return jnp.concatenate(outs, axis=1)

    @pl.when(ph == 0)
    def _():
        base = pl.multiple_of(s * tt, tt)
        of_ref[pl.ds(base, tt), :] = tile_outputs(False)

    @pl.when(ph == 1)
    def _():
        loc = jnp.where(s < n_ct, n_ct - 1 - s, n_ct + n_lt - 1 - (s - n_ct))
        base = pl.multiple_of(loc * tt, tt)
        o = of_ref[pl.ds(base, tt), :] + tile_outputs(True)
        gate = gate_ref[...].astype(F32)
        parts = []
        for h in range(HEADS):
            oh = o[:, h * DV:(h + 1) * DV]
            if kind == "ret":
                oh = oh - jnp.mean(oh, axis=-1, keepdims=True)
            parts.append(oh * lax.rsqrt(jnp.mean(oh * oh, axis=-1, keepdims=True) + EPS))
        normed = jnp.concatenate(parts, axis=1) * gain_ref[...]
        out_ref[...] = (normed * _silu(gate)).astype(out_ref.dtype)


def _la_call(kind, p, extra, gain, nb, nl, nc):
    t = p.shape[0]
    tt = _pick(math.gcd(nl, nc), (256, 128, 64))
    n_ct, n_lt = nc // tt, nl // tt
    lat_tiles = nb * n_lt
    cq, ck, cv, cg = ((C_GLA_Q, C_GLA_K, C_GLA_V, C_GLA_G) if kind == "gla"
                      else (C_RET_Q, C_RET_K, C_RET_V, C_RET_G))
    hd = HEADS * DK
    hv = HEADS * DV

    def loc_of(ph, s):
        back = jnp.where(s < n_ct, n_ct - 1 - s, n_ct + n_lt - 1 - (s - n_ct))
        return jnp.where(ph == 0, s, back)

    def row_blk(b, loc):
        return jnp.where(loc < n_ct, lat_tiles + b * n_ct + loc, b * n_lt + (loc - n_ct))

    def in_map(cblk):
        return lambda b, ph, s: (row_blk(b, loc_of(ph, s)), cblk)

    def second_pass_map(cblk):
        return lambda b, ph, s: (row_blk(b, loc_of(1, jnp.where(ph == 0, 0, s))), cblk)

    in_specs = [pl.BlockSpec((tt, hd), in_map(cq // hd)),
                pl.BlockSpec((tt, hd), in_map(ck // hd)),
                pl.BlockSpec((tt, hv), in_map(cv // hv)),
                pl.BlockSpec((tt, hv), second_pass_map(cg // hv))]
    args = [p, p, p, p]
    if kind == "gla":
        wa2p, ba = extra
        in_specs += [pl.BlockSpec((tt, LANES), in_map(C_LR // LANES)),
                     pl.BlockSpec((2, LANES, hd), lambda b, ph, s: (0, 0, 0)),
                     pl.BlockSpec((2, 1, hd), lambda b, ph, s: (0, 0, 0))]
        args += [p, wa2p, ba]
    else:
        cos_t, sin_t = extra

        def rope_map(b, ph, s):
            loc = loc_of(ph, s)
            return (jnp.where(loc < n_ct, n_lt + loc, loc - n_ct), 0)
        in_specs += [pl.BlockSpec((tt, DK), rope_map), pl.BlockSpec((tt, DK), rope_map)]
        args += [cos_t, sin_t]
    in_specs.append(pl.BlockSpec((1, hv), lambda b, ph, s: (0, 0)))
    args.append(gain)
    return pl.pallas_call(
        functools.partial(_la_kernel, kind=kind, tt=tt, n_ct=n_ct, n_lt=n_lt),
        grid=(nb, 2, n_ct + n_lt),
        in_specs=in_specs,
        out_specs=pl.BlockSpec((tt, hv), second_pass_map(0)),
        out_shape=jax.ShapeDtypeStruct((t, hv), _MXU),
        scratch_shapes=[pltpu.VMEM((HEADS, DV, DK), F32),
                        pltpu.VMEM((nc + nl, hv), F32)]
        + ([pltpu.VMEM((2 * HEADS, tt, tt), F32)] if kind == "ret" else []),
        compiler_params=_cparams(("arbitrary", "arbitrary", "arbitrary"), 48),
        name=kind,
    )(*args)


def _scan_group(a, b, h, reverse):
    row = lax.broadcasted_iota(jnp.int32, a.shape, 0)
    for sft in (1, 2, 4):
        if reverse:
            a_sh = pltpu.roll(a, SUBLANES - sft, 0)
            b_sh = pltpu.roll(b, SUBLANES - sft, 0)
            m = row < SUBLANES - sft
        else:
            a_sh = pltpu.roll(a, sft, 0)
            b_sh = pltpu.roll(b, sft, 0)
            m = row >= sft
        b = jnp.where(m, a * b_sh + b, b)
        a = jnp.where(m, a * a_sh, a)
    hh = a * h + b
    return hh, (hh[0:1, :] if reverse else hh[SUBLANES - 1:SUBLANES, :])


def _lru_kernel(x_ref, gel_ref, cw_ref, cb_ref, wa_ref, ba_ref, wi_ref, bi_ref, lam_ref, h0_ref,
                out_ref, hn_ref, xpad_ref, xc_ref, hf_ref, *, seg, tile):
    n_tiles = seg // tile
    groups = tile // SUBLANES
    cw = cw_ref[...]
    cb = cb_ref[...]
    zeros8 = jnp.zeros((SUBLANES, xpad_ref.shape[1]), F32)
    xpad_ref[0:SUBLANES, :] = zeros8
    xpad_ref[SUBLANES + seg:2 * SUBLANES + seg, :] = zeros8

    def copy_in(i, carry):
        t0 = pl.multiple_of(i * tile, tile)
        xpad_ref[pl.ds(t0 + SUBLANES, tile), :] = x_ref[pl.ds(t0, tile), :].astype(F32)
        return carry
    lax.fori_loop(0, n_tiles, copy_in, 0)

    neg_c = [(-0.5 * LRU_C) * _log_sigmoid(lam_ref[d:d + 1, :]) for d in range(2)]

    def gates(xc, d):
        xb = xc.astype(_MXU)
        tr = jnp.tanh(_dot(xb, wa_ref[d]) + ba_ref[d:d + 1, :])
        ig = 0.5 * jnp.tanh(_dot(xb, wi_ref[d]) + bi_ref[d:d + 1, :]) + 0.5
        nla = neg_c[d] * tr + neg_c[d]
        a = jnp.exp(-nla)
        b = jnp.sqrt(jnp.tanh(nla) * (a * a + 1.0)) * (ig * xc)
        return a, b

    def fwd_tile(i, h):
        t0 = pl.multiple_of(i * tile, tile)
        win = xpad_ref[pl.ds(t0, tile + 2 * SUBLANES), :]
        xc = (cw[0:1, :] * win[6:6 + tile] + cw[1:2, :] * win[7:7 + tile]
              + cw[2:3, :] * win[8:8 + tile] + cw[3:4, :] * win[9:9 + tile] + cb)
        xc_ref[pl.ds(t0, tile), :] = xc
        a, b = gates(xc, 0)
        for g in range(groups):
            rs = slice(g * SUBLANES, (g + 1) * SUBLANES)
            hh, h = _scan_group(a[rs], b[rs], h, False)
            hf_ref[pl.ds(t0 + g * SUBLANES, SUBLANES), :] = hh
        return h
    h_f = lax.fori_loop(0, n_tiles, fwd_tile, h0_ref[0:1, :])

    def bwd_tile(j, h):
        i = n_tiles - 1 - j
        t0 = pl.multiple_of(i * tile, tile)
        xc = xc_ref[pl.ds(t0, tile), :]
        a, b = gates(xc, 1)
        pair = 2 * SUBLANES
        for q in reversed(range(tile // pair)):
            hi = slice(q * pair + SUBLANES, (q + 1) * pair)
            lo = slice(q * pair, q * pair + SUBLANES)
            hh_hi, h = _scan_group(a[hi], b[hi], h, True)
            hh_lo, h = _scan_group(a[lo], b[lo], h, True)
            rows = pl.ds(pl.multiple_of(t0 + q * pair, pair), pair)
            hsum = hf_ref[rows, :] + jnp.concatenate([hh_lo, hh_hi], axis=0)
            gel = gel_ref[rows, :].astype(F32)
            out_ref[rows, :] = (hsum * _gelu_tanh(gel)).astype(out_ref.dtype)
        return h
    h_b = lax.fori_loop(0, n_tiles, bwd_tile, h0_ref[1:2, :])
    hn_ref[0:1, :] = h_f
    hn_ref[1:2, :] = h_b


def _lru_call(p, prev_out, h0, lw, nb, seg, row_blk0):
    cw, cb, wa, ba, wi, bi, lam = lw
    t = p.shape[0]
    width = LRU_BLOCKS * DV
    cbw = width // LRU_BLOCKS
    tile = _pick(seg, (512, 256, 128, 64))
    xblk = C_LRU_X // cbw
    gblk = C_LRU_G // cbw
    in_specs = [pl.BlockSpec((seg, cbw), lambda b, c: (row_blk0 + b, xblk + c)),
                pl.BlockSpec((seg, cbw), lambda b, c: (row_blk0 + b, gblk + c)),
                pl.BlockSpec((4, cbw), lambda b, c: (0, c)),
                pl.BlockSpec((1, cbw), lambda b, c: (0, c)),
                pl.BlockSpec((2, None, cbw, cbw), lambda b, c: (0, c, 0, 0)),
                pl.BlockSpec((2, cbw), lambda b, c: (0, c)),
                pl.BlockSpec((2, None, cbw, cbw), lambda b, c: (0, c, 0, 0)),
                pl.BlockSpec((2, cbw), lambda b, c: (0, c)),
                pl.BlockSpec((2, cbw), lambda b, c: (0, c)),
                pl.BlockSpec((None, 2, cbw), lambda b, c: (b, 0, c))]
    in_specs.append(pl.BlockSpec(memory_space=pl.ANY))
    args = [p, p, cw, cb, wa, ba, wi, bi, lam, h0, prev_out]

    def body(*refs):
        _lru_kernel(*(refs[:10] + refs[11:]), seg=seg, tile=tile)

    return pl.pallas_call(
        body,
        grid=(nb, LRU_BLOCKS),
        in_specs=in_specs,
        out_specs=[pl.BlockSpec((seg, cbw), lambda b, c: (row_blk0 + b, c)),
                   pl.BlockSpec((None, 2, cbw), lambda b, c: (b, 0, c))],
        out_shape=[jax.ShapeDtypeStruct((t, width), _MXU),
                   jax.ShapeDtypeStruct((nb, 2, width), F32)],
        scratch_shapes=[pltpu.VMEM((seg + 2 * SUBLANES, cbw), F32),
                        pltpu.VMEM((seg, cbw), F32),
                        pltpu.VMEM((seg, cbw), F32)],
        input_output_aliases={10: 0},
        compiler_params=_cparams(("arbitrary", "arbitrary"), 48),
        name="lru",
    )(*args)


def _finish_kernel(*refs, bounds):
    n_parts = len(bounds) + 1
    xs_refs = refs[:n_parts]
    (gla_ref, lru_ref, ret_ref, m0_ref, m1_ref, m2_ref, wb_ref, wo_ref,
     mod_ref, g2_ref, wr_ref, br_ref, tril_ref,
     xo_ref, xn_ref, idx_ref, wgt_ref, rank_ref, cnt_ref, carry_ref) = refs[n_parts:]

    @pl.when(pl.program_id(0) == 0)
    def _():
        carry_ref[...] = jnp.zeros_like(carry_ref)

    hm = tril_ref.shape[0]
    total = carry_ref[0:1, :]
    for sub in range(xo_ref.shape[0] // hm):
        rs = slice(sub * hm, (sub + 1) * hm)
        merged = (_sigmoid(m0_ref[rs, :].astype(F32)) * _dot(gla_ref[rs, :], wb_ref[0])
                  + _sigmoid(m1_ref[rs, :].astype(F32)) * _dot(lru_ref[rs, :], wb_ref[1])
                  + _sigmoid(m2_ref[rs, :].astype(F32)) * _dot(ret_ref[rs, :], wb_ref[2]))
        x = _pick_rows(xs_refs, bounds, rs) + mod_ref[2:3, :] * _dot(merged.astype(_MXU), wo_ref[...])
        xo_ref[rs, :] = x
        xn = _rms(x, g2_ref[...]) * (1.0 + mod_ref[4:5, :]) + mod_ref[3:4, :]
        xn_ref[rs, :] = xn

        logits = _dot(xn.astype(_MXU), wr_ref[...]) + br_ref[...]
        lane = lax.broadcasted_iota(jnp.int32, logits.shape, 1)
        ids, vals = [], []
        for _ in range(TOP_K):
            m = jnp.max(logits, axis=1, keepdims=True)
            sel = jnp.min(jnp.where(logits == m, lane.astype(F32), float(LANES)), axis=1,
                          keepdims=True).astype(jnp.int32)
            ids.append(sel)
            vals.append(m)
            logits = jnp.where(lane == sel, -jnp.inf, logits)
        ex = [jnp.exp(v - vals[0]) for v in vals]
        denom = ex[0] + ex[1] + ex[2] + ex[3]
        onehot = jnp.zeros(logits.shape, F32)
        for sel in ids:
            onehot = onehot + (lane == sel).astype(F32)
        before = _dot(tril_ref[...], onehot.astype(_MXU)) + total
        idx_o = jnp.zeros(logits.shape, jnp.int32)
        rank_o = jnp.zeros(logits.shape, jnp.int32)
        wgt_o = jnp.zeros(logits.shape, F32)
        for k in range(TOP_K):
            rk = jnp.sum(jnp.where(lane == ids[k], before, 0.0), axis=1, keepdims=True)
            idx_o = jnp.where(lane == k, ids[k], idx_o)
            rank_o = jnp.where(lane == k, rk.astype(jnp.int32), rank_o)
            wgt_o = jnp.where(lane == k, ex[k] / denom, wgt_o)
        idx_ref[rs, :] = idx_o
        rank_ref[rs, :] = rank_o
        wgt_ref[rs, :] = wgt_o
        total = total + jnp.sum(onehot, axis=0, keepdims=True)
    carry_ref[...] = jnp.broadcast_to(total, carry_ref.shape)
    cnt_ref[...] = jnp.broadcast_to(total, cnt_ref.shape)


def _finish_call(gla, lru, ret, p, wb, wo, xs_parts, mod, g2, wr, br, n_rows, nl, nb):
    d = xs_parts[0].shape[1]
    tm = _pick(math.gcd(math.gcd(*[q.shape[0] for q in xs_parts], n_rows), nl), (512, 256))
    xs_specs, bounds = _row_specs(xs_parts, tm, d)
    hm = tm
    tril = jnp.asarray(np.tril(np.ones((hm, hm), np.float32), -1), _MXU)
    row = lambda i: (i, 0)
    const2 = lambda i: (0, 0)
    mblk = C_MERGE // d
    in_specs = xs_specs + [
                pl.BlockSpec((tm, d), row), pl.BlockSpec((tm, d), row), pl.BlockSpec((tm, d), row),
                pl.BlockSpec((tm, d), lambda i: (i, mblk)),
                pl.BlockSpec((tm, d), lambda i: (i, mblk + 1)),
                pl.BlockSpec((tm, d), lambda i: (i, mblk + 2)),
                pl.BlockSpec((3, d, d), lambda i: (0, 0, 0)),
                pl.BlockSpec((d, d), const2),
                pl.BlockSpec((None, N_MOD, d), lambda i: (jnp.minimum(i * tm // nl, nb), 0, 0)),
                pl.BlockSpec((1, d), const2),
                pl.BlockSpec((d, LANES), const2),
                pl.BlockSpec((1, LANES), const2),
                pl.BlockSpec((hm, hm), const2)]
    out_specs = [pl.BlockSpec((tm, d), row), pl.BlockSpec((tm, d), row),
                 pl.BlockSpec((tm, LANES), row), pl.BlockSpec((tm, LANES), row),
                 pl.BlockSpec((tm, LANES), row), pl.BlockSpec((SUBLANES, LANES), const2)]
    out_shape = [jax.ShapeDtypeStruct((n_rows, d), F32), jax.ShapeDtypeStruct((n_rows, d), F32),
                 jax.ShapeDtypeStruct((n_rows, LANES), jnp.int32),
                 jax.ShapeDtypeStruct((n_rows, LANES), F32),
                 jax.ShapeDtypeStruct((n_rows, LANES), jnp.int32),
                 jax.ShapeDtypeStruct((SUBLANES, LANES), F32)]
    return pl.pallas_call(
        functools.partial(_finish_kernel, bounds=bounds),
        grid=(n_rows // tm,),
        in_specs=in_specs,
        out_specs=out_specs,
        out_shape=out_shape,
        scratch_shapes=[pltpu.VMEM((SUBLANES, LANES), F32)],
        compiler_params=_cparams(("arbitrary",), 56),
        name="finish",
    )(*xs_parts, gla, lru, ret, p, p, p, wb, wo, mod, g2, wr, br, tril)


def _slot_kernel(idx_ref, rank_ref, start_ref, slot_ref):
    idx = idx_ref[...].astype(F32)
    lane = lax.broadcasted_iota(jnp.int32, idx.shape, 1)
    start = start_ref[...]
    out = jnp.zeros(idx.shape, F32)
    for k in range(TOP_K):
        sel = jnp.sum(jnp.where(lane == k, idx, 0.0), axis=1, keepdims=True).astype(jnp.int32)
        st = jnp.sum(jnp.where(lane == sel, start, 0.0), axis=1, keepdims=True)
        out = jnp.where(lane == k, st, out)
    slot_ref[...] = out.astype(jnp.int32) + rank_ref[...]


def _slot_call(idx, rank, pad_start):
    n = idx.shape[0]
    tm = _pick(n, (2048, 1024, 512, 256))
    row = lambda i: (i, 0)
    return pl.pallas_call(
        _slot_kernel,
        grid=(n // tm,),
        in_specs=[pl.BlockSpec((tm, LANES), row), pl.BlockSpec((tm, LANES), row),
                  pl.BlockSpec((1, LANES), lambda i: (0, 0))],
        out_specs=pl.BlockSpec((tm, LANES), row),
        out_shape=jax.ShapeDtypeStruct((n, LANES), jnp.int32),
        compiler_params=_cparams(("arbitrary",), 24),
        name="slots",
    )(idx, rank, pad_start)


def _dispatch_kernel(fill_ref, slot_ref, x_ref, dst_ref, zero_ref, sem, fill_sem, *, tm, bm, n_blocks):
    @pl.when(pl.program_id(0) == 0)
    def _():
        zero_ref[...] = jnp.zeros_like(zero_ref)

        def clear(blk):
            return pltpu.make_async_copy(zero_ref, dst_ref.at[pl.ds(pl.multiple_of(blk * bm, bm), bm)],
                                         fill_sem)

        def tail_start(j, carry):
            clear(j).start()
            return carry

        def tail_wait(j, carry):
            clear(j).wait()
            return carry

        for e in range(N_EXPERTS):
            @pl.when(fill_ref[0, e] >= 0)
            def _():
                clear(fill_ref[0, e]).start()
        lax.fori_loop(fill_ref[0, N_EXPERTS], n_blocks, tail_start, 0)
        for e in range(N_EXPERTS):
            @pl.when(fill_ref[0, e] >= 0)
            def _():
                clear(fill_ref[0, e]).wait()
        lax.fori_loop(fill_ref[0, N_EXPERTS], n_blocks, tail_wait, 0)

    def issue(g, carry):
        t0 = pl.multiple_of(g * SUBLANES, SUBLANES)
        for u in range(SUBLANES):
            for k in range(TOP_K):
                pltpu.make_async_copy(
                    x_ref.at[pl.ds(t0 + u, 1)],
                    dst_ref.at[pl.ds(slot_ref[0, (t0 + u) * TOP_K + k], 1)],
                    sem).start(priority=k % 2)
        return carry
    lax.fori_loop(0, tm // SUBLANES, issue, 0)
    for k in range(TOP_K):
        pltpu.make_async_copy(x_ref, dst_ref.at[pl.ds(0, tm)], sem).wait()


def _dispatch_call(fill, slots3, xn, n_blocks, bm):
    n, d = xn.shape
    tm = slots3.shape[2] // TOP_K
    return pl.pallas_call(
        functools.partial(_dispatch_kernel, tm=tm, bm=bm, n_blocks=n_blocks),
        grid=(n // tm,),
        in_specs=[pl.BlockSpec((1, LANES), lambda i: (0, 0), memory_space=pltpu.SMEM),
                  pl.BlockSpec((None, 1, tm * TOP_K), lambda i: (i, 0, 0), memory_space=pltpu.SMEM),
                  pl.BlockSpec((tm, d), lambda i: (i, 0))],
        out_specs=pl.BlockSpec(memory_space=pl.ANY),
        out_shape=jax.ShapeDtypeStruct((n_blocks * bm, d), F32),
        scratch_shapes=[pltpu.VMEM((bm, d), F32), pltpu.SemaphoreType.DMA(()),
                        pltpu.SemaphoreType.DMA(())],
        compiler_params=_cparams(("arbitrary",), 24),
        name="dispatch",
    )(fill, slots3, xn)


def _experts_kernel(be_ref, nu_ref, x_ref, wgu_ref, bgu_ref, wd_ref, bd_ref, perm_ref, y_ref,
                    wgu_s, wd_s):
    j = pl.program_id(0)
    f = wd_ref.shape[0]
    grp = perm_ref.shape[0]
    half = grp // 2
    active = j < nu_ref[0]
    changed = jnp.logical_or(j == 0, be_ref[j] != be_ref[jnp.maximum(j - 1, 0)])

    @pl.when(jnp.logical_and(active, changed))
    def _():
        for g in range(2 * f // grp):
            cols = slice(g * grp, (g + 1) * grp)
            wgu_s[:, cols] = _dot(wgu_ref[:, cols].astype(_MXU), perm_ref[...]).astype(_MXU)
        wd_s[...] = wd_ref[...].astype(_MXU)

    @pl.when(active)
    def _():
        gu = _dot(x_ref[...].astype(_MXU), wgu_s[...]) + bgu_ref[...]
        acts = []
        for g in range(2 * f // grp):
            gate = jnp.minimum(gu[:, g * grp:g * grp + half], SWIGLU_LIMIT)
            up = jnp.clip(gu[:, g * grp + half:(g + 1) * grp], -SWIGLU_LIMIT, SWIGLU_LIMIT)
            acts.append((gate * _sigmoid(SWIGLU_ALPHA * gate) * (up + 1.0)).astype(_MXU))
        y_ref[...] = _dot(jnp.concatenate(acts, axis=1), wd_s[...]) + bd_ref[...]

    @pl.when(jnp.logical_not(active))
    def _():
        y_ref[...] = jnp.zeros_like(y_ref)


def _experts_call(li, block_e, n_used, x_sorted, w_gu, bgu_r, w_down, bd, bm):
    n_slots, d = x_sorted.shape
    f = w_down.shape[2]
    grp = 2 * LANES
    src = np.concatenate([np.arange(0, grp, 2), np.arange(1, grp, 2)])
    perm_np = np.zeros((grp, grp), np.float32)
    perm_np[src, np.arange(grp)] = 1.0
    perm = jnp.asarray(perm_np, _MXU)

    def xmap(j, be, nu):
        return (jnp.minimum(j, nu[0] - 1), 0)

    def wmap(j, be, nu):
        return (li, be[j], 0, 0)

    grid_spec = pltpu.PrefetchScalarGridSpec(
        num_scalar_prefetch=2,
        grid=(n_slots // bm,),
        in_specs=[pl.BlockSpec((bm, d), xmap),
                  pl.BlockSpec((None, None, d, 2 * f), wmap),
                  pl.BlockSpec((None, None, 1, 2 * f), wmap),
                  pl.BlockSpec((None, None, f, d), wmap),
                  pl.BlockSpec((None, None, 1, d), wmap),
                  pl.BlockSpec((grp, grp), lambda j, be, nu: (0, 0))],
        out_specs=pl.BlockSpec((bm, d), lambda j, be, nu: (j, 0)),
        scratch_shapes=[pltpu.VMEM((d, 2 * f), _MXU), pltpu.VMEM((f, d), _MXU)])
    return pl.pallas_call(
        _experts_kernel,
        grid_spec=grid_spec,
        out_shape=jax.ShapeDtypeStruct((n_slots, d), F32),
        compiler_params=_cparams(("arbitrary",), 56),
        name="experts",
    )(block_e, n_used, x_sorted, w_gu, bgu_r, w_down, bd, perm)


def _combine_kernel(slot_ref, wgt_ref, xs_ref, mod_ref, fin_ref, y_ref, out_ref, buf_ref, sem,
                    *, tm, final):
    def issue(g, carry):
        t0 = pl.multiple_of(g * SUBLANES, SUBLANES)
        for u in range(SUBLANES):
            for k in range(TOP_K):
                pltpu.make_async_copy(
                    y_ref.at[pl.ds(slot_ref[0, (t0 + u) * TOP_K + k], 1)],
                    buf_ref.at[k, pl.ds(t0 + u, 1)], sem).start(priority=k % 2)
        return carry
    lax.fori_loop(0, tm // SUBLANES, issue, 0)
    for k in range(TOP_K):
        pltpu.make_async_copy(y_ref.at[pl.ds(0, tm)], buf_ref.at[k], sem).wait()

    wgt = wgt_ref[...]
    acc = wgt[:, 0:1] * buf_ref[0]
    for k in range(1, TOP_K):
        acc = acc + wgt[:, k:k + 1] * buf_ref[k]
    x = xs_ref[...] + mod_ref[5:6, :] * acc
    out_ref[...] = _rms(x, fin_ref[...]) if final else x


def _combine_call(slots3, wgt, xs, mod, fin, y_sorted, nl, nb, final):
    n, d = xs.shape
    tm = slots3.shape[2] // TOP_K
    return pl.pallas_call(
        functools.partial(_combine_kernel, tm=tm, final=final),
        grid=(n // tm,),
        in_specs=[pl.BlockSpec((None, 1, tm * TOP_K), lambda i: (i, 0, 0), memory_space=pltpu.SMEM),
                  pl.BlockSpec((tm, LANES), lambda i: (i, 0)),
                  pl.BlockSpec((tm, d), lambda i: (i, 0)),
                  pl.BlockSpec((None, N_MOD, d), lambda i: (jnp.minimum(i * tm // nl, nb), 0, 0)),
                  pl.BlockSpec((1, d), lambda i: (0, 0)),
                  pl.BlockSpec(memory_space=pl.ANY)],
        out_specs=pl.BlockSpec((tm, d), lambda i: (i, 0)),
        out_shape=jax.ShapeDtypeStruct((n, d), F32),
        scratch_shapes=[pltpu.VMEM((TOP_K, tm, d), F32), pltpu.SemaphoreType.DMA(())],
        compiler_params=_cparams(("arbitrary",), 48),
        name="combine",
    )(slots3, wgt, xs, mod, fin, y_sorted)


MOE_BM = 512


def _moe(li, xn, idx, wgt, rank, counts, xs_mid, mod, fin, w_gu, bgu_r, w_down, bd, nl, nb, final):
    n = xn.shape[0]
    bm = MOE_BM
    tm = _pick(math.gcd(n, nl), (1024, 512, 256, 128))
    n_blocks = -(-(n * TOP_K) // bm) + N_EXPERTS
    cnt = counts[0, :N_EXPERTS].astype(jnp.int32)
    padded = (cnt + bm - 1) // bm * bm
    pad_end = jnp.cumsum(padded)
    pad_start = pad_end - padded
    n_used = (pad_end[-1] // bm).astype(jnp.int32)
    blk = jnp.arange(n_blocks, dtype=jnp.int32)
    first_row = jnp.minimum(blk, n_used - 1) * bm
    block_e = jnp.minimum(jnp.sum(pad_end[None, :] <= first_row[:, None], axis=1),
                          N_EXPERTS - 1).astype(jnp.int32)
    start_row = jnp.zeros((1, LANES), F32).at[0, :N_EXPERTS].set(pad_start.astype(F32))
    slots = _slot_call(idx, rank, start_row)
    slots3 = slots[:, :TOP_K].reshape(n // tm, 1, tm * TOP_K)
    last_blk = jnp.where(cnt > 0, pad_end // bm - 1, -1).astype(jnp.int32)
    fill = jnp.zeros((1, LANES), jnp.int32).at[0, :N_EXPERTS].set(last_blk).at[0, N_EXPERTS].set(n_used)
    x_sorted = _dispatch_call(fill, slots3, xn, n_blocks, bm)
    y_sorted = _experts_call(li, block_e, n_used.reshape(1), x_sorted, w_gu, bgu_r, w_down, bd, bm)
    return _combine_call(slots3, wgt, xs_mid, mod, fin, y_sorted, nl, nb, final)


def _rope_tables(nl, nc):
    rows = nl // GRID_W
    r = np.broadcast_to(np.arange(rows)[:, None], (rows, GRID_W)).reshape(-1).astype(np.float32)
    c = np.broadcast_to(np.arange(GRID_W)[None, :], (rows, GRID_W)).reshape(-1).astype(np.float32)
    n_freq = DK // 4
    inv = (ROPE_BASE ** (-jnp.arange(n_freq, dtype=F32) / n_freq))
    ang = jnp.concatenate([jnp.asarray(r)[:, None] * inv, jnp.asarray(c)[:, None] * inv], axis=-1)
    cos, sin = jnp.cos(ang), jnp.sin(ang)
    cos2 = jnp.concatenate([cos, cos], axis=-1)
    sin2 = jnp.concatenate([-sin, sin], axis=-1)
    cos_t = jnp.concatenate([cos2, jnp.ones((nc, DK), F32)], axis=0)
    sin_t = jnp.concatenate([sin2, jnp.zeros((nc, DK), F32)], axis=0)
    return cos_t, sin_t


def kernel(x, c, ctx, c_ctx, w_ada, b_ada, norm1, norm2, w_in, gla_wa2, gla_ba, gla_norm, lru_conv_w, lru_conv_b, lru_wa, lru_ba, lru_wi, lru_bi, lru_lam, ret_norm, w_branch, w_out, w_router, b_router, w_gu, b_gu, w_down, b_down, final_norm):
    nb, nl, d = x.shape
    nc = ctx.shape[1]
    depth = w_ada.shape[0]
    n_lat = nb * nl
    f = w_down.shape[2]

    n_tok = n_lat + nb * nc
    xs = (x.reshape(n_lat, d), ctx.reshape(nb * nc, d))
    mod_rows = -(-(nb + 1) // SUBLANES) * SUBLANES
    cc = jnp.zeros((mod_rows, d), F32).at[:nb].set(c).at[nb].set(c_ctx)
    mod_all = _ada_call(cc, w_ada, b_ada).reshape(depth, mod_rows, N_MOD, d)
    cos_t, sin_t = _rope_tables(nl, nc)
    bgu_r = b_gu.reshape(depth, N_EXPERTS, 2 * f // (2 * LANES), LANES, 2)
    bgu_r = jnp.swapaxes(bgu_r, -1, -2).reshape(depth, N_EXPERTS, 1, 2 * f)
    bd_r = b_down.reshape(depth, N_EXPERTS, 1, d)
    lru_buf = jnp.zeros((n_tok, lru_conv_w.shape[2]), _MXU)

    out = None
    for li in range(depth):
        last = li == depth - 1
        mod = mod_all[li]
        w = w_in[li]
        w_in_r = jnp.concatenate(
            [w[:, :3072], w[:, 3104:], w[:, 3072:3104],
             jnp.zeros((d, N_PROJ - w.shape[1]), w.dtype)], axis=1).astype(_MXU)
        p = _proj_call(xs, mod, norm1[li].reshape(1, d), w_in_r, nl, nb)

        wa2p = jnp.zeros((2, LANES, HEADS * DK), F32)
        wa2p = wa2p.at[0, :GLA_RANK].set(gla_wa2[li, 0]).at[1, GLA_RANK:2 * GLA_RANK].set(gla_wa2[li, 1])
        gla = _la_call("gla", p, (wa2p.astype(_MXU), gla_ba[li].reshape(2, 1, HEADS * DK)),
                       gla_norm[li].reshape(1, HEADS * DV), nb, nl, nc)
        ret = _la_call("ret", p, (cos_t, sin_t), ret_norm[li].reshape(1, HEADS * DV), nb, nl, nc)

        lw = (lru_conv_w[li], lru_conv_b[li].reshape(1, -1), (0.5 * lru_wa[li]).astype(_MXU),
              0.5 * lru_ba[li], (0.5 * lru_wi[li]).astype(_MXU), 0.5 * lru_bi[li], lru_lam[li])
        width = lru_conv_w.shape[2]
        lru_c, h_ctx = _lru_call(p, lru_buf, jnp.zeros((nb, 2, width), F32), lw, nb, nc, n_lat // nc)
        lru, _ = _lru_call(p, lru_c, h_ctx, lw, nb, nl, 0)
        lru_buf = lru

        n_rows = n_lat if last else n_tok
        wr = jnp.zeros((d, LANES), F32).at[:, :N_EXPERTS].set(w_router[li]).astype(_MXU)
        br = jnp.full((1, LANES), NEG_BIG, F32).at[0, :N_EXPERTS].set(b_router[li])
        xs_mid, xn2, idx, wgt, rank, counts = _finish_call(
            gla, lru, ret, p, w_branch[li].astype(_MXU), w_out[li].astype(_MXU), xs, mod,
            norm2[li].reshape(1, d), wr, br, n_rows, nl, nb)

        out = _moe(li, xn2, idx, wgt, rank, counts, xs_mid, mod, final_norm.reshape(1, d),
                   w_gu, bgu_r, w_down, bd_r, nl, nb, last)
        xs = (out,)
    return out.reshape(nb, nl, d)
```

```python
import functools
import math

import numpy as np
import jax
import jax.numpy as jnp
from jax import lax
from jax.experimental import pallas as pl
from jax.experimental.pallas import tpu as pltpu

F32 = jnp.float32
_MXU = jnp.bfloat16

EPS = 1e-6
N_MOD = 6
GRID_W = 64
CHUNK = 64
HEADS = 4
DK = 128
DV = 256
GLA_RANK = 16
GLA_GATE_NORM = 16.0
LRU_BLOCKS = 4
LRU_C = 8.0
ROPE_BASE = 10000.0
N_EXPERTS = 32
TOP_K = 4
SWIGLU_LIMIT = 7.0
SWIGLU_ALPHA = 1.702
LANES = 128
SUBLANES = 8
NEG_BIG = -1e30

C_GLA_Q, C_GLA_K, C_GLA_V, C_GLA_G = 0, 512, 1024, 2048
C_LRU_X, C_LRU_G = 3072, 4096
C_RET_Q, C_RET_K, C_RET_V, C_RET_G = 5120, 5632, 6144, 7168
C_MERGE = 8192
C_LR = 11264
N_PROJ = 11520
PROJ_TN = 3840


def _pick(n, prefs):
    for p in prefs:
        if n % p == 0:
            return p
    raise ValueError(f"no tile for {n} in {prefs}")


def _cparams(sem, vmem_mb):
    return pltpu.CompilerParams(dimension_semantics=sem, vmem_limit_bytes=vmem_mb * 1024 * 1024)


def _dot(a, b):
    return jnp.dot(a, b, preferred_element_type=F32)


def _dot_nt(a, b):
    return lax.dot_general(a, b, (((1,), (1,)), ((), ())), preferred_element_type=F32)


def _dot_tn(a, b):
    return lax.dot_general(a, b, (((0,), (0,)), ((), ())), preferred_element_type=F32)


def _sigmoid(x):
    return 0.5 * jnp.tanh(0.5 * x) + 0.5


def _log_sigmoid(x):
    return jnp.minimum(x, 0.0) - jnp.log(1.0 + jnp.exp(-jnp.abs(x)))


def _silu(x):
    return x * _sigmoid(x)


def _gelu_tanh(x):
    return 0.5 * x * (1.0 + jnp.tanh(math.sqrt(2.0 / math.pi) * (x + 0.044715 * (x * x * x))))


def _rms(x, gain):
    return x * lax.rsqrt(jnp.mean(x * x, axis=-1, keepdims=True) + EPS) * gain


def _dot01_exact(tri, g):
    hi = g.astype(_MXU)
    r1 = g - hi.astype(F32)
    mid = r1.astype(_MXU)
    lo = (r1 - mid.astype(F32)).astype(_MXU)
    return _dot(tri, hi) + _dot(tri, mid) + _dot(tri, lo)


def _ada_kernel(c_ref, w_ref, b_ref, o_ref):
    s = _silu(c_ref[...])
    o_ref[...] = _dot(s.astype(_MXU), w_ref[...].astype(_MXU)) + b_ref[...]


def _ada_call(cc, w_ada, b_ada):
    depth, d, n = w_ada.shape
    rows = cc.shape[0]
    tn = _pick(n, (1536, 1024, 512, 128))
    return pl.pallas_call(
        _ada_kernel,
        grid=(depth, n // tn),
        in_specs=[pl.BlockSpec((rows, d), lambda l, j: (0, 0)),
                  pl.BlockSpec((None, d, tn), lambda l, j: (l, 0, j)),
                  pl.BlockSpec((None, 1, tn), lambda l, j: (l, 0, j))],
        out_specs=pl.BlockSpec((None, rows, tn), lambda l, j: (l, 0, j)),
        out_shape=jax.ShapeDtypeStruct((depth, rows, n), F32),
        compiler_params=_cparams(("arbitrary", "arbitrary"), 24),
        name="ada",
    )(cc, w_ada, b_ada.reshape(depth, 1, n))


def _row_specs(parts, tm, d):
    offs = np.cumsum([0] + [p.shape[0] // tm for p in parts])
    specs = []
    for k, p in enumerate(parts):
        lo, n_tiles = int(offs[k]), p.shape[0] // tm

        def imap(i, *rest, lo=lo, n_tiles=n_tiles):
            return (jnp.clip(i - lo, 0, n_tiles - 1), 0)
        specs.append(pl.BlockSpec((tm, d), imap))
    return specs, [int(o) for o in offs[1:-1]]


def _pick_rows(refs, bounds, rows=slice(None)):
    val = refs[0][rows, :]
    for ref, lo in zip(refs[1:], bounds):
        val = jnp.where(pl.program_id(0) >= lo, ref[rows, :], val)
    return val


def _proj_kernel(*refs, bounds):
    n_parts = len(bounds) + 1
    x_refs = refs[:n_parts]
    mod_ref, g_ref, w_ref, o_ref, xn_ref = refs[n_parts:]

    @pl.when(pl.program_id(1) == 0)
    def _():
        y = _rms(_pick_rows(x_refs, bounds), g_ref[...])
        xn_ref[...] = (y * (1.0 + mod_ref[1:2, :]) + mod_ref[0:1, :]).astype(xn_ref.dtype)

    o_ref[...] = _dot(xn_ref[...], w_ref[...]).astype(o_ref.dtype)


def _proj_call(x_parts, mod, gain, w_in_r, nl, nb):
    t = sum(p.shape[0] for p in x_parts)
    d = x_parts[0].shape[1]
    tm = _pick(math.gcd(math.gcd(*[p.shape[0] for p in x_parts], t), nl), (1024, 512, 256))
    x_specs, bounds = _row_specs(x_parts, tm, d)
    return pl.pallas_call(
        functools.partial(_proj_kernel, bounds=bounds),
        grid=(t // tm, N_PROJ // PROJ_TN),
        in_specs=x_specs + [
            pl.BlockSpec((None, N_MOD, d), lambda i, j: (jnp.minimum(i * tm // nl, nb), 0, 0)),
            pl.BlockSpec((1, d), lambda i, j: (0, 0)),
            pl.BlockSpec((d, PROJ_TN), lambda i, j: (0, j))],
        out_specs=pl.BlockSpec((tm, PROJ_TN), lambda i, j: (i, j)),
        out_shape=jax.ShapeDtypeStruct((t, N_PROJ), _MXU),
        scratch_shapes=[pltpu.VMEM((tm, d), _MXU)],
        compiler_params=_cparams(("arbitrary", "arbitrary"), 56),
        name="proj",
    )(*x_parts, mod, gain, w_in_r)


def _la_kernel(*refs, kind, tt, n_ct, n_lt):
    if kind == "gla":
        (q_ref, k_ref, v_ref, gate_ref, lr_ref, wa2_ref, ba_ref, gain_ref,
         out_ref, st_ref, of_ref) = refs
    else:
        (q_ref, k_ref, v_ref, gate_ref, cos_ref, sin_ref, gain_ref,
         out_ref, st_ref, of_ref, dm_ref) = refs
    ph = pl.program_id(1)
    s = pl.program_id(2)
    ck = CHUNK if kind == "gla" else tt
    n_chunks = tt // ck

    @pl.when(s == 0)
    def _():
        st_ref[...] = jnp.zeros_like(st_ref)

    row = lax.broadcasted_iota(jnp.int32, (tt, tt), 0)
    col = lax.broadcasted_iota(jnp.int32, (tt, tt), 1)
    shift = ck.bit_length() - 1
    same_chunk = (row >> shift) == (col >> shift)
    tpos = lax.broadcasted_iota(jnp.int32, (tt, 1), 0).astype(F32)

    if kind == "ret":
        first = jnp.logical_and(pl.program_id(0) == 0, jnp.logical_and(ph == 0, s == 0))

        @pl.when(first)
        def _():
            for h in range(HEADS):
                lg = math.log(1.0 - 2.0 ** (-5.0 - h))
                dm_ref[h] = jnp.where(col <= row, jnp.exp((row - col).astype(F32) * lg), 0.0)
                dm_ref[HEADS + h] = jnp.where(col > row, jnp.exp((col - row).astype(F32) * lg), 0.0)

    def tile_outputs(backward):
        if backward:
            mask = jnp.logical_and(same_chunk, col > row)
            tri = jnp.logical_and(same_chunk, col >= row).astype(_MXU)
        else:
            mask = jnp.logical_and(same_chunk, col <= row)
            tri = jnp.logical_and(same_chunk, col <= row).astype(_MXU)
        order = list(reversed(range(n_chunks))) if backward else list(range(n_chunks))
        if kind == "gla":
            d = 1 if backward else 0
            z = _dot(lr_ref[...], wa2_ref[d]) + ba_ref[d]
            g = _log_sigmoid(z) * (1.0 / GLA_GATE_NORM)
            big_g = _dot01_exact(tri, g)
            lasts = [big_g[c * ck:c * ck + 1, :] if backward else big_g[(c + 1) * ck - 1:(c + 1) * ck, :]
                     for c in range(n_chunks)]
            g_last = jnp.concatenate([jnp.broadcast_to(r, (ck, r.shape[1])) for r in lasts], axis=0)
            e_pos = jnp.exp(big_g)
            e_neg = jnp.exp(-big_g)
            e_end = jnp.exp(g_last - big_g)
            decs = [jnp.exp(r) for r in lasts]
        outs = []
        for h in range(HEADS):
            ks = slice(h * DK, (h + 1) * DK)
            vs = slice(h * DV, (h + 1) * DV)
            qh = q_ref[:, ks].astype(F32)
            kh = k_ref[:, ks].astype(F32)
            vh = v_ref[:, vs]
            if kind == "gla":
                qh = qh * DK ** -0.5
                q_dec = (qh * e_pos[:, ks]).astype(_MXU)
                k_inv = (kh * e_neg[:, ks]).astype(_MXU)
                k_end = (kh * e_end[:, ks]).astype(_MXU)
                sc = jnp.where(mask, _dot_nt(q_dec, k_inv), 0.0)
                dec_h = [dcy[:, ks] for dcy in decs]
            else:
                kh = kh * DK ** -0.5
                cos = cos_ref[...]
                sin = sin_ref[...]
                qh = qh * cos + pltpu.roll(qh, DK // 2, 1) * sin
                kh = kh * cos + pltpu.roll(kh, DK // 2, 1) * sin
                lg = math.log(1.0 - 2.0 ** (-5.0 - h))
                steps = (ck - tpos) if backward else (tpos + 1.0)
                q_dec = (qh * jnp.exp(steps * lg)).astype(_MXU)
                k_end = (kh * jnp.exp((ck - steps) * lg)).astype(_MXU)
                sc = _dot_nt(qh.astype(_MXU), kh.astype(_MXU)) * dm_ref[(HEADS if backward else 0) + h]
                dec_h = [math.exp(ck * lg)] * n_chunks
            o_intra = _dot(sc.astype(_MXU), vh)
            st = st_ref[h]
            parts = [None] * n_chunks
            for c in order:
                rs = slice(c * ck, (c + 1) * ck)
                parts[c] = o_intra[rs] + _dot_nt(q_dec[rs], st.astype(_MXU))
                st = st * dec_h[c] + _dot_tn(vh[rs], k_end[rs])
            st_ref[h] = st
            outs.append(parts[0] if n_chunks == 1 else jnp.concatenate(parts, axis=0))
        return jnp.concatenate(outs, axis=1)

    @pl.when(ph == 0)
    def _():
        base = pl.multiple_of(s * tt, tt)
        of_ref[pl.ds(base, tt), :] = tile_outputs(False)

    @pl.when(ph == 1)
    def _():
        loc = jnp.where(s < n_ct, n_ct - 1 - s, n_ct + n_lt - 1 - (s - n_ct))
        base = pl.multiple_of(loc * tt, tt)
        o = of_ref[pl.ds(base, tt), :] + tile_outputs(True)
        gate = gate_ref[...].astype(F32)
        parts = []
        for h in range(HEADS):
            oh = o[:, h * DV:(h + 1) * DV]
            if kind == "ret":
                oh = oh - jnp.mean(oh, axis=-1, keepdims=True)
            parts.append(oh * lax.rsqrt(jnp.mean(oh * oh, axis=-1, keepdims=True) + EPS))
        normed = jnp.concatenate(parts, axis=1) * gain_ref[...]
        out_ref[...] = (normed * _silu(gate)).astype(out_ref.dtype)


def _la_call(kind, p, extra, gain, nb, nl, nc):
    t = p.shape[0]
    tt = _pick(math.gcd(nl, nc), (256, 128, 64))
    n_ct, n_lt = nc // tt, nl // tt
    lat_tiles = nb * n_lt
    cq, ck, cv, cg = ((C_GLA_Q, C_GLA_K, C_GLA_V, C_GLA_G) if kind == "gla"
                      else (C_RET_Q, C_RET_K, C_RET_V, C_RET_G))
    hd = HEADS * DK
    hv = HEADS * DV

    def loc_of(ph, s):
        back = jnp.where(s < n_ct, n_ct - 1 - s, n_ct + n_lt - 1 - (s - n_ct))
        return jnp.where(ph == 0, s, back)

    def row_blk(b, loc):
        return jnp.where(loc < n_ct, lat_tiles + b * n_ct + loc, b * n_lt + (loc - n_ct))

    def in_map(cblk):
        return lambda b, ph, s: (row_blk(b, loc_of(ph, s)), cblk)

    def second_pass_map(cblk):
        return lambda b, ph, s: (row_blk(b, loc_of(1, jnp.where(ph == 0, 0, s))), cblk)

    in_specs = [pl.BlockSpec((tt, hd), in_map(cq // hd)),
                pl.BlockSpec((tt, hd), in_map(ck // hd)),
                pl.BlockSpec((tt, hv), in_map(cv // hv)),
                pl.BlockSpec((tt, hv), second_pass_map(cg // hv))]
    args = [p, p, p, p]
    if kind == "gla":
        wa2p, ba = extra
        in_specs += [pl.BlockSpec((tt, LANES), in_map(C_LR // LANES)),
                     pl.BlockSpec((2, LANES, hd), lambda b, ph, s: (0, 0, 0)),
                     pl.BlockSpec((2, 1, hd), lambda b, ph, s: (0, 0, 0))]
        args += [p, wa2p, ba]
    else:
        cos_t, sin_t = extra

        def rope_map(b, ph, s):
            loc = loc_of(ph, s)
            return (jnp.where(loc < n_ct, n_lt + loc, loc - n_ct), 0)
        in_specs += [pl.BlockSpec((tt, DK), rope_map), pl.BlockSpec((tt, DK), rope_map)]
        args += [cos_t, sin_t]
    in_specs.append(pl.BlockSpec((1, hv), lambda b, ph, s: (0, 0)))
    args.append(gain)
    return pl.pallas_call(
        functools.partial(_la_kernel, kind=kind, tt=tt, n_ct=n_ct, n_lt=n_lt),
        grid=(nb, 2, n_ct + n_lt),
        in_specs=in_specs,
        out_specs=pl.BlockSpec((tt, hv), second_pass_map(0)),
        out_shape=jax.ShapeDtypeStruct((t, hv), _MXU),
        scratch_shapes=[pltpu.VMEM((HEADS, DV, DK), F32),
                        pltpu.VMEM((nc + nl, hv), F32)]
        + ([pltpu.VMEM((2 * HEADS, tt, tt), F32)] if kind == "ret" else []),
        compiler_params=_cparams(("arbitrary", "arbitrary", "arbitrary"), 48),
        name=kind,
    )(*args)


def _scan_group(a, b, h, reverse):
    row = lax.broadcasted_iota(jnp.int32, a.shape, 0)
    for sft in (1, 2, 4):
        if reverse:
            a_sh = pltpu.roll(a, SUBLANES - sft, 0)
            b_sh = pltpu.roll(b, SUBLANES - sft, 0)
            m = row < SUBLANES - sft
        else:
            a_sh = pltpu.roll(a, sft, 0)
            b_sh = pltpu.roll(b, sft, 0)
            m = row >= sft
        b = jnp.where(m, a * b_sh + b, b)
        a = jnp.where(m, a * a_sh, a)
    hh = a * h + b
    return hh, (hh[0:1, :] if reverse else hh[SUBLANES - 1:SUBLANES, :])


def _lru_kernel(x_ref, gel_ref, cw_ref, cb_ref, wa_ref, ba_ref, wi_ref, bi_ref, lam_ref, h0_ref,
                out_ref, hn_ref, xpad_ref, xc_ref, hf_ref, *, seg, tile):
    n_tiles = seg // tile
    groups = tile // SUBLANES
    cw = cw_ref[...]
    cb = cb_ref[...]
    zeros8 = jnp.zeros((SUBLANES, xpad_ref.shape[1]), F32)
    xpad_ref[0:SUBLANES, :] = zeros8
    xpad_ref[SUBLANES + seg:2 * SUBLANES + seg, :] = zeros8

    def copy_in(i, carry):
        t0 = pl.multiple_of(i * tile, tile)
        xpad_ref[pl.ds(t0 + SUBLANES, tile), :] = x_ref[pl.ds(t0, tile), :].astype(F32)
        return carry
    lax.fori_loop(0, n_tiles, copy_in, 0)

    neg_c = [(-0.5 * LRU_C) * _log_sigmoid(lam_ref[d:d + 1, :]) for d in range(2)]

    def gates(xc, d):
        xb = xc.astype(_MXU)
        tr = jnp.tanh(_dot(xb, wa_ref[d]) + ba_ref[d:d + 1, :])
        ig = 0.5 * jnp.tanh(_dot(xb, wi_ref[d]) + bi_ref[d:d + 1, :]) + 0.5
        nla = neg_c[d] * tr + neg_c[d]
        a = jnp.exp(-nla)
        b = jnp.sqrt(jnp.tanh(nla) * (a * a + 1.0)) * (ig * xc)
        return a, b

    def fwd_tile(i, h):
        t0 = pl.multiple_of(i * tile, tile)
        win = xpad_ref[pl.ds(t0, tile + 2 * SUBLANES), :]
        xc = (cw[0:1, :] * win[6:6 + tile] + cw[1:2, :] * win[7:7 + tile]
              + cw[2:3, :] * win[8:8 + tile] + cw[3:4, :] * win[9:9 + tile] + cb)
        xc_ref[pl.ds(t0, tile), :] = xc
        a, b = gates(xc, 0)
        for g in range(groups):
            rs = slice(g * SUBLANES, (g + 1) * SUBLANES)
            hh, h = _scan_group(a[rs], b[rs], h, False)
            hf_ref[pl.ds(t0 + g * SUBLANES, SUBLANES), :] = hh
        return h
    h_f = lax.fori_loop(0, n_tiles, fwd_tile, h0_ref[0:1, :])

    def bwd_tile(j, h):
        i = n_tiles - 1 - j
        t0 = pl.multiple_of(i * tile, tile)
        xc = xc_ref[pl.ds(t0, tile), :]
        a, b = gates(xc, 1)
        pair = 2 * SUBLANES
        for q in reversed(range(tile // pair)):
            hi = slice(q * pair + SUBLANES, (q + 1) * pair)
            lo = slice(q * pair, q * pair + SUBLANES)
            hh_hi, h = _scan_group(a[hi], b[hi], h, True)
            hh_lo, h = _scan_group(a[lo], b[lo], h, True)
            rows = pl.ds(pl.multiple_of(t0 + q * pair, pair), pair)
            hsum = hf_ref[rows, :] + jnp.concatenate([hh_lo, hh_hi], axis=0)
            gel = gel_ref[rows, :].astype(F32)
            out_ref[rows, :] = (hsum * _gelu_tanh(gel)).astype(out_ref.dtype)
        return h
    h_b = lax.fori_loop(0, n_tiles, bwd_tile, h0_ref[1:2, :])
    hn_ref[0:1, :] = h_f
    hn_ref[1:2, :] = h_b


def _lru_call(p, prev_out, h0, lw, nb, seg, row_blk0):
    cw, cb, wa, ba, wi, bi, lam = lw
    t = p.shape[0]
    width = LRU_BLOCKS * DV
    cbw = width // LRU_BLOCKS
    tile = _pick(seg, (512, 256, 128, 64))
    xblk = C_LRU_X // cbw
    gblk = C_LRU_G // cbw
    in_specs = [pl.BlockSpec((seg, cbw), lambda b, c: (row_blk0 + b, xblk + c)),
                pl.BlockSpec((seg, cbw), lambda b, c: (row_blk0 + b, gblk + c)),
                pl.BlockSpec((4, cbw), lambda b, c: (0, c)),
                pl.BlockSpec((1, cbw), lambda b, c: (0, c)),
                pl.BlockSpec((2, None, cbw, cbw), lambda b, c: (0, c, 0, 0)),
                pl.BlockSpec((2, cbw), lambda b, c: (0, c)),
                pl.BlockSpec((2, None, cbw, cbw), lambda b, c: (0, c, 0, 0)),
                pl.BlockSpec((2, cbw), lambda b, c: (0, c)),
                pl.BlockSpec((2, cbw), lambda b, c: (0, c)),
                pl.BlockSpec((None, 2, cbw), lambda b, c: (b, 0, c))]
    in_specs.append(pl.BlockSpec(memory_space=pl.ANY))
    args = [p, p, cw, cb, wa, ba, wi, bi, lam, h0, prev_out]

    def body(*refs):
        _lru_kernel(*(refs[:10] + refs[11:]), seg=seg, tile=tile)

    return pl.pallas_call(
        body,
        grid=(nb, LRU_BLOCKS),
        in_specs=in_specs,
        out_specs=[pl.BlockSpec((seg, cbw), lambda b, c: (row_blk0 + b, c)),
                   pl.BlockSpec((None, 2, cbw), lambda b, c: (b, 0, c))],
        out_shape=[jax.ShapeDtypeStruct((t, width), _MXU),
                   jax.ShapeDtypeStruct((nb, 2, width), F32)],
        scratch_shapes=[pltpu.VMEM((seg + 2 * SUBLANES, cbw), F32),
                        pltpu.VMEM((seg, cbw), F32),
                        pltpu.VMEM((seg, cbw), F32)],
        input_output_aliases={10: 0},
        compiler_params=_cparams(("arbitrary", "arbitrary"), 48),
        name="lru",
    )(*args)


def _finish_kernel(*refs, bounds):
    n_parts = len(bounds) + 1
    xs_refs = refs[:n_parts]
    (gla_ref, lru_ref, ret_ref, m0_ref, m1_ref, m2_ref, wb_ref, wo_ref,
     mod_ref, g2_ref, wr_ref, br_ref, tril_ref,
     xo_ref, xn_ref, idx_ref, wgt_ref, rank_ref, cnt_ref, carry_ref) = refs[n_parts:]

    @pl.when(pl.program_id(0) == 0)
    def _():
        carry_ref[...] = jnp.zeros_like(carry_ref)

    hm = tril_ref.shape[0]
    total = carry_ref[0:1, :]
    for sub in range(xo_ref.shape[0] // hm):
        rs = slice(sub * hm, (sub + 1) * hm)
        merged = (_sigmoid(m0_ref[rs, :].astype(F32)) * _dot(gla_ref[rs, :], wb_ref[0])
                  + _sigmoid(m1_ref[rs, :].astype(F32)) * _dot(lru_ref[rs, :], wb_ref[1])
                  + _sigmoid(m2_ref[rs, :].astype(F32)) * _dot(ret_ref[rs, :], wb_ref[2]))
        x = _pick_rows(xs_refs, bounds, rs) + mod_ref[2:3, :] * _dot(merged.astype(_MXU), wo_ref[...])
        xo_ref[rs, :] = x
        xn = _rms(x, g2_ref[...]) * (1.0 + mod_ref[4:5, :]) + mod_ref[3:4, :]
        xn_ref[rs, :] = xn

        logits = _dot(xn.astype(_MXU), wr_ref[...]) + br_ref[...]
        lane = lax.broadcasted_iota(jnp.int32, logits.shape, 1)
        ids, vals = [], []
        for _ in range(TOP_K):
            m = jnp.max(logits, axis=1, keepdims=True)
            sel = jnp.min(jnp.where(logits == m, lane.astype(F32), float(LANES)), axis=1,
                          keepdims=True).astype(jnp.int32)
            ids.append(sel)
            vals.append(m)
            logits = jnp.where(lane == sel, -jnp.inf, logits)
        ex = [jnp.exp(v - vals[0]) for v in vals]
        denom = ex[0] + ex[1] + ex[2] + ex[3]
        onehot = jnp.zeros(logits.shape, F32)
        for sel in ids:
            onehot = onehot + (lane == sel).astype(F32)
        before = _dot(tril_ref[...], onehot.astype(_MXU)) + total
        idx_o = jnp.zeros(logits.shape, jnp.int32)
        rank_o = jnp.zeros(logits.shape, jnp.int32)
        wgt_o = jnp.zeros(logits.shape, F32)
        for k in range(TOP_K):
            rk = jnp.sum(jnp.where(lane == ids[k], before, 0.0), axis=1, keepdims=True)
            idx_o = jnp.where(lane == k, ids[k], idx_o)
            rank_o = jnp.where(lane == k, rk.astype(jnp.int32), rank_o)
            wgt_o = jnp.where(lane == k, ex[k] / denom, wgt_o)
        idx_ref[rs, :] = idx_o
        rank_ref[rs, :] = rank_o
        wgt_ref[rs, :] = wgt_o
        total = total + jnp.sum(onehot, axis=0, keepdims=True)
    carry_ref[...] = jnp.broadcast_to(total, carry_ref.shape)
    cnt_ref[...] = jnp.broadcast_to(total, cnt_ref.shape)


def _finish_call(gla, lru, ret, p, wb, wo, xs_parts, mod, g2, wr, br, n_rows, nl, nb):
    d = xs_parts[0].shape[1]
    tm = _pick(math.gcd(math.gcd(*[q.shape[0] for q in xs_parts], n_rows), nl), (512, 256))
    xs_specs, bounds = _row_specs(xs_parts, tm, d)
    hm = tm
    tril = jnp.asarray(np.tril(np.ones((hm, hm), np.float32), -1), _MXU)
    row = lambda i: (i, 0)
    const2 = lambda i: (0, 0)
    mblk = C_MERGE // d
    in_specs = xs_specs + [
                pl.BlockSpec((tm, d), row), pl.BlockSpec((tm, d), row), pl.BlockSpec((tm, d), row),
                pl.BlockSpec((tm, d), lambda i: (i, mblk)),
                pl.BlockSpec((tm, d), lambda i: (i, mblk + 1)),
                pl.BlockSpec((tm, d), lambda i: (i, mblk + 2)),
                pl.BlockSpec((3, d, d), lambda i: (0, 0, 0)),
                pl.BlockSpec((d, d), const2),
                pl.BlockSpec((None, N_MOD, d), lambda i: (jnp.minimum(i * tm // nl, nb), 0, 0)),
                pl.BlockSpec((1, d), const2),
                pl.BlockSpec((d, LANES), const2),
                pl.BlockSpec((1, LANES), const2),
                pl.BlockSpec((hm, hm), const2)]
    out_specs = [pl.BlockSpec((tm, d), row), pl.BlockSpec((tm, d), row),
                 pl.BlockSpec((tm, LANES), row), pl.BlockSpec((tm, LANES), row),
                 pl.BlockSpec((tm, LANES), row), pl.BlockSpec((SUBLANES, LANES), const2)]
    out_shape = [jax.ShapeDtypeStruct((n_rows, d), F32), jax.ShapeDtypeStruct((n_rows, d), F32),
                 jax.ShapeDtypeStruct((n_rows, LANES), jnp.int32),
                 jax.ShapeDtypeStruct((n_rows, LANES), F32),
                 jax.ShapeDtypeStruct((n_rows, LANES), jnp.int32),
                 jax.ShapeDtypeStruct((SUBLANES, LANES), F32)]
    return pl.pallas_call(
        functools.partial(_finish_kernel, bounds=bounds),
        grid=(n_rows // tm,),
        in_specs=in_specs,
        out_specs=out_specs,
        out_shape=out_shape,
        scratch_shapes=[pltpu.VMEM((SUBLANES, LANES), F32)],
        compiler_params=_cparams(("arbitrary",), 56),
        name="finish",
    )(*xs_parts, gla, lru, ret, p, p, p, wb, wo, mod, g2, wr, br, tril)


def _slot_kernel(idx_ref, rank_ref, start_ref, slot_ref):
    idx = idx_ref[...].astype(F32)
    lane = lax.broadcasted_iota(jnp.int32, idx.shape, 1)
    start = start_ref[...]
    out = jnp.zeros(idx.shape, F32)
    for k in range(TOP_K):
        sel = jnp.sum(jnp.where(lane == k, idx, 0.0), axis=1, keepdims=True).astype(jnp.int32)
        st = jnp.sum(jnp.where(lane == sel, start, 0.0), axis=1, keepdims=True)
        out = jnp.where(lane == k, st, out)
    slot_ref[...] = out.astype(jnp.int32) + rank_ref[...]


def _slot_call(idx, rank, pad_start):
    n = idx.shape[0]
    tm = _pick(n, (2048, 1024, 512, 256))
    row = lambda i: (i, 0)
    return pl.pallas_call(
        _slot_kernel,
        grid=(n // tm,),
        in_specs=[pl.BlockSpec((tm, LANES), row), pl.BlockSpec((tm, LANES), row),
                  pl.BlockSpec((1, LANES), lambda i: (0, 0))],
        out_specs=pl.BlockSpec((tm, LANES), row),
        out_shape=jax.ShapeDtypeStruct((n, LANES), jnp.int32),
        compiler_params=_cparams(("arbitrary",), 24),
        name="slots",
    )(idx, rank, pad_start)


def _dispatch_kernel(fill_ref, slot_ref, x_ref, dst_ref, zero_ref, sem, fill_sem, *, tm, bm, n_blocks):
    @pl.when(pl.program_id(0) == 0)
    def _():
        zero_ref[...] = jnp.zeros_like(zero_ref)

        def clear(blk):
            return pltpu.make_async_copy(zero_ref, dst_ref.at[pl.ds(pl.multiple_of(blk * bm, bm), bm)],
                                         fill_sem)

        def tail_start(j, carry):
            clear(j).start()
            return carry

        def tail_wait(j, carry):
            clear(j).wait()
            return carry

        for e in range(N_EXPERTS):
            @pl.when(fill_ref[0, e] >= 0)
            def _():
                clear(fill_ref[0, e]).start()
        lax.fori_loop(fill_ref[0, N_EXPERTS], n_blocks, tail_start, 0)
        for e in range(N_EXPERTS):
            @pl.when(fill_ref[0, e] >= 0)
            def _():
                clear(fill_ref[0, e]).wait()
        lax.fori_loop(fill_ref[0, N_EXPERTS], n_blocks, tail_wait, 0)

    def issue(g, carry):
        t0 = pl.multiple_of(g * SUBLANES, SUBLANES)
        for u in range(SUBLANES):
            for k in range(TOP_K):
                pltpu.make_async_copy(
                    x_ref.at[pl.ds(t0 + u, 1)],
                    dst_ref.at[pl.ds(slot_ref[0, (t0 + u) * TOP_K + k], 1)],
                    sem).start(priority=k % 2)
        return carry
    lax.fori_loop(0, tm // SUBLANES, issue, 0)
    for k in range(TOP_K):
        pltpu.make_async_copy(x_ref, dst_ref.at[pl.ds(0, tm)], sem).wait()


def _dispatch_call(fill, slots3, xn, n_blocks, bm):
    n, d = xn.shape
    tm = slots3.shape[2] // TOP_K
    return pl.pallas_call(
        functools.partial(_dispatch_kernel, tm=tm, bm=bm, n_blocks=n_blocks),
        grid=(n // tm,),
        in_specs=[pl.BlockSpec((1, LANES), lambda i: (0, 0), memory_space=pltpu.SMEM),
                  pl.BlockSpec((None, 1, tm * TOP_K), lambda i: (i, 0, 0), memory_space=pltpu.SMEM),
                  pl.BlockSpec((tm, d), lambda i: (i, 0))],
        out_specs=pl.BlockSpec(memory_space=pl.ANY),
        out_shape=jax.ShapeDtypeStruct((n_blocks * bm, d), F32),
        scratch_shapes=[pltpu.VMEM((bm, d), F32), pltpu.SemaphoreType.DMA(()),
                        pltpu.SemaphoreType.DMA(())],
        compiler_params=_cparams(("arbitrary",), 24),
        name="dispatch",
    )(fill, slots3, xn)


def _experts_kernel(be_ref, nu_ref, x_ref, wgu_ref, bgu_ref, wd_ref, bd_ref, perm_ref, y_ref,
                    wgu_s, wd_s):
    j = pl.program_id(0)
    f = wd_ref.shape[0]
    grp = perm_ref.shape[0]
    half = grp // 2
    active = j < nu_ref[0]
    changed = jnp.logical_or(j == 0, be_ref[j] != be_ref[jnp.maximum(j - 1, 0)])

    @pl.when(jnp.logical_and(active, changed))
    def _():
        for g in range(2 * f // grp):
            cols = slice(g * grp, (g + 1) * grp)
            wgu_s[:, cols] = _dot(wgu_ref[:, cols].astype(_MXU), perm_ref[...]).astype(_MXU)
        wd_s[...] = wd_ref[...].astype(_MXU)

    @pl.when(active)
    def _():
        xb = x_ref[...].astype(_MXU)
        acts = []
        for g in range(2 * f // grp):
            cols = slice(g * grp, (g + 1) * grp)
            gu = _dot(xb, wgu_s[:, cols]) + bgu_ref[:, cols]
            gate = jnp.minimum(gu[:, :half], SWIGLU_LIMIT)
            up = jnp.clip(gu[:, half:], -SWIGLU_LIMIT, SWIGLU_LIMIT)
            acts.append((gate * _sigmoid(SWIGLU_ALPHA * gate) * (up + 1.0)).astype(_MXU))
        y_ref[...] = _dot(jnp.concatenate(acts, axis=1), wd_s[...]) + bd_ref[...]

    @pl.when(jnp.logical_not(active))
    def _():
        y_ref[...] = jnp.zeros_like(y_ref)


def _experts_call(li, block_e, n_used, x_sorted, w_gu, bgu_r, w_down, bd, bm):
    n_slots, d = x_sorted.shape
    f = w_down.shape[2]
    grp = 2 * LANES
    src = np.concatenate([np.arange(0, grp, 2), np.arange(1, grp, 2)])
    perm_np = np.zeros((grp, grp), np.float32)
    perm_np[src, np.arange(grp)] = 1.0
    perm = jnp.asarray(perm_np, _MXU)

    def xmap(j, be, nu):
        return (jnp.minimum(j, nu[0] - 1), 0)

    def wmap(j, be, nu):
        return (li, be[j], 0, 0)

    grid_spec = pltpu.PrefetchScalarGridSpec(
        num_scalar_prefetch=2,
        grid=(n_slots // bm,),
        in_specs=[pl.BlockSpec((bm, d), xmap),
                  pl.BlockSpec((None, None, d, 2 * f), wmap),
                  pl.BlockSpec((None, None, 1, 2 * f), wmap),
                  pl.BlockSpec((None, None, f, d), wmap),
                  pl.BlockSpec((None, None, 1, d), wmap),
                  pl.BlockSpec((grp, grp), lambda j, be, nu: (0, 0))],
        out_specs=pl.BlockSpec((bm, d), lambda j, be, nu: (j, 0)),
        scratch_shapes=[pltpu.VMEM((d, 2 * f), _MXU), pltpu.VMEM((f, d), _MXU)])
    return pl.pallas_call(
        _experts_kernel,
        grid_spec=grid_spec,
        out_shape=jax.ShapeDtypeStruct((n_slots, d), F32),
        compiler_params=_cparams(("arbitrary",), 56),
        name="experts",
    )(block_e, n_used, x_sorted, w_gu, bgu_r, w_down, bd, perm)


def _combine_kernel(slot_ref, wgt_ref, xs_ref, mod_ref, fin_ref, y_ref, out_ref, buf_ref, sem,
                    *, tm, final):
    def issue(g, carry):
        t0 = pl.multiple_of(g * SUBLANES, SUBLANES)
        for u in range(SUBLANES):
            for k in range(TOP_K):
                pltpu.make_async_copy(
                    y_ref.at[pl.ds(slot_ref[0, (t0 + u) * TOP_K + k], 1)],
                    buf_ref.at[k, pl.ds(t0 + u, 1)], sem).start(priority=k % 2)
        return carry
    lax.fori_loop(0, tm // SUBLANES, issue, 0)
    for k in range(TOP_K):
        pltpu.make_async_copy(y_ref.at[pl.ds(0, tm)], buf_ref.at[k], sem).wait()

    wgt = wgt_ref[...]
    acc = wgt[:, 0:1] * buf_ref[0]
    for k in range(1, TOP_K):
        acc = acc + wgt[:, k:k + 1] * buf_ref[k]
    x = xs_ref[...] + mod_ref[5:6, :] * acc
    out_ref[...] = _rms(x, fin_ref[...]) if final else x


def _combine_call(slots3, wgt, xs, mod, fin, y_sorted, nl, nb, final):
    n, d = xs.shape
    tm = slots3.shape[2] // TOP_K
    return pl.pallas_call(
        functools.partial(_combine_kernel, tm=tm, final=final),
        grid=(n // tm,),
        in_specs=[pl.BlockSpec((None, 1, tm * TOP_K), lambda i: (i, 0, 0), memory_space=pltpu.SMEM),
                  pl.BlockSpec((tm, LANES), lambda i: (i, 0)),
                  pl.BlockSpec((tm, d), lambda i: (i, 0)),
                  pl.BlockSpec((None, N_MOD, d), lambda i: (jnp.minimum(i * tm // nl, nb), 0, 0)),
                  pl.BlockSpec((1, d), lambda i: (0, 0)),
                  pl.BlockSpec(memory_space=pl.ANY)],
        out_specs=pl.BlockSpec((tm, d), lambda i: (i, 0)),
        out_shape=jax.ShapeDtypeStruct((n, d), F32),
        scratch_shapes=[pltpu.VMEM((TOP_K, tm, d), F32), pltpu.SemaphoreType.DMA(())],
        compiler_params=_cparams(("arbitrary",), 48),
        name="combine",
    )(slots3, wgt, xs, mod, fin, y_sorted)


MOE_BM = 512


def _moe(li, xn, idx, wgt, rank, counts, xs_mid, mod, fin, w_gu, bgu_r, w_down, bd, nl, nb, final):
    n = xn.shape[0]
    bm = MOE_BM
    tm = _pick(math.gcd(n, nl), (1024, 512, 256, 128))
    n_blocks = -(-(n * TOP_K) // bm) + N_EXPERTS
    cnt = counts[0, :N_EXPERTS].astype(jnp.int32)
    padded = (cnt + bm - 1) // bm * bm
    pad_end = jnp.cumsum(padded)
    pad_start = pad_end - padded
    n_used = (pad_end[-1] // bm).astype(jnp.int32)
    blk = jnp.arange(n_blocks, dtype=jnp.int32)
    first_row = jnp.minimum(blk, n_used - 1) * bm
    block_e = jnp.minimum(jnp.sum(pad_end[None, :] <= first_row[:, None], axis=1),
                          N_EXPERTS - 1).astype(jnp.int32)
    start_row = jnp.zeros((1, LANES), F32).at[0, :N_EXPERTS].set(pad_start.astype(F32))
    slots = _slot_call(idx, rank, start_row)
    slots3 = slots[:, :TOP_K].reshape(n // tm, 1, tm * TOP_K)
    last_blk = jnp.where(cnt > 0, pad_end // bm - 1, -1).astype(jnp.int32)
    fill = jnp.zeros((1, LANES), jnp.int32).at[0, :N_EXPERTS].set(last_blk).at[0, N_EXPERTS].set(n_used)
    x_sorted = _dispatch_call(fill, slots3, xn, n_blocks, bm)
    y_sorted = _experts_call(li, block_e, n_used.reshape(1), x_sorted, w_gu, bgu_r, w_down, bd, bm)
    return _combine_call(slots3, wgt, xs_mid, mod, fin, y_sorted, nl, nb, final)


def _rope_tables(nl, nc):
    rows = nl // GRID_W
    r = np.broadcast_to(np.arange(rows)[:, None], (rows, GRID_W)).reshape(-1).astype(np.float32)
    c = np.broadcast_to(np.arange(GRID_W)[None, :], (rows, GRID_W)).reshape(-1).astype(np.float32)
    n_freq = DK // 4
    inv = (ROPE_BASE ** (-jnp.arange(n_freq, dtype=F32) / n_freq))
    ang = jnp.concatenate([jnp.asarray(r)[:, None] * inv, jnp.asarray(c)[:, None] * inv], axis=-1)
    cos, sin = jnp.cos(ang), jnp.sin(ang)
    cos2 = jnp.concatenate([cos, cos], axis=-1)
    sin2 = jnp.concatenate([-sin, sin], axis=-1)
    cos_t = jnp.concatenate([cos2, jnp.ones((nc, DK), F32)], axis=0)
    sin_t = jnp.concatenate([sin2, jnp.zeros((nc, DK), F32)], axis=0)
    return cos_t, sin_t


def kernel(x, c, ctx, c_ctx, w_ada, b_ada, norm1, norm2, w_in, gla_wa2, gla_ba, gla_norm, lru_conv_w, lru_conv_b, lru_wa, lru_ba, lru_wi, lru_bi, lru_lam, ret_norm, w_branch, w_out, w_router, b_router, w_gu, b_gu, w_down, b_down, final_norm):
    nb, nl, d = x.shape
    nc = ctx.shape[1]
    depth = w_ada.shape[0]
    n_lat = nb * nl
    f = w_down.shape[2]

    n_tok = n_lat + nb * nc
    xs = (x.reshape(n_lat, d), ctx.reshape(nb * nc, d))
    mod_rows = -(-(nb + 1) // SUBLANES) * SUBLANES
    cc = jnp.zeros((mod_rows, d), F32).at[:nb].set(c).at[nb].set(c_ctx)
    mod_all = _ada_call(cc, w_ada, b_ada).reshape(depth, mod_rows, N_MOD, d)
    cos_t, sin_t = _rope_tables(nl, nc)
    bgu_r = b_gu.reshape(depth, N_EXPERTS, 2 * f // (2 * LANES), LANES, 2)
    bgu_r = jnp.swapaxes(bgu_r, -1, -2).reshape(depth, N_EXPERTS, 1, 2 * f)
    bd_r = b_down.reshape(depth, N_EXPERTS, 1, d)
    lru_buf = jnp.zeros((n_tok, lru_conv_w.shape[2]), _MXU)

    out = None
    for li in range(depth):
        last = li == depth - 1
        mod = mod_all[li]
        w = w_in[li]
        w_in_r = jnp.concatenate(
            [w[:, :3072], w[:, 3104:], w[:, 3072:3104],
             jnp.zeros((d, N_PROJ - w.shape[1]), w.dtype)], axis=1).astype(_MXU)
        p = _proj_call(xs, mod, norm1[li].reshape(1, d), w_in_r, nl, nb)

        wa2p = jnp.zeros((2, LANES, HEADS * DK), F32)
        wa2p = wa2p.at[0, :GLA_RANK].set(gla_wa2[li, 0]).at[1, GLA_RANK:2 * GLA_RANK].set(gla_wa2[li, 1])
        gla = _la_call("gla", p, (wa2p.astype(_MXU), gla_ba[li].reshape(2, 1, HEADS * DK)),
                       gla_norm[li].reshape(1, HEADS * DV), nb, nl, nc)
        ret = _la_call("ret", p, (cos_t, sin_t), ret_norm[li].reshape(1, HEADS * DV), nb, nl, nc)

        lw = (lru_conv_w[li], lru_conv_b[li].reshape(1, -1), (0.5 * lru_wa[li]).astype(_MXU),
              0.5 * lru_ba[li], (0.5 * lru_wi[li]).astype(_MXU), 0.5 * lru_bi[li], lru_lam[li])
        width = lru_conv_w.shape[2]
        lru_c, h_ctx = _lru_call(p, lru_buf, jnp.zeros((nb, 2, width), F32), lw, nb, nc, n_lat // nc)
        lru, _ = _lru_call(p, lru_c, h_ctx, lw, nb, nl, 0)
        lru_buf = lru

        n_rows = n_lat if last else n_tok
        wr = jnp.zeros((d, LANES), F32).at[:, :N_EXPERTS].set(w_router[li]).astype(_MXU)
        br = jnp.full((1, LANES), NEG_BIG, F32).at[0, :N_EXPERTS].set(b_router[li])
        xs_mid, xn2, idx, wgt, rank, counts = _finish_call(
            gla, lru, ret, p, w_branch[li].astype(_MXU), w_out[li].astype(_MXU), xs, mod,
            norm2[li].reshape(1, d), wr, br, n_rows, nl, nb)

        out = _moe(li, xn2, idx, wgt, rank, counts, xs_mid, mod, final_norm.reshape(1, d),
                   w_gu, bgu_r, w_down, bd_r, nl, nb, last)
        xs = (out,)
    return out.reshape(nb, nl, d)
```

```python
import functools
import math

import numpy as np
import jax
import jax.numpy as jnp
from jax import lax
from jax.experimental import pallas as pl
from jax.experimental.pallas import tpu as pltpu

F32 = jnp.float32
_MXU = jnp.bfloat16

EPS = 1e-6
N_MOD = 6
GRID_W = 64
CHUNK = 64
HEADS = 4
DK = 128
DV = 256
GLA_RANK = 16
GLA_GATE_NORM = 16.0
LRU_BLOCKS = 4
LRU_C = 8.0
ROPE_BASE = 10000.0
N_EXPERTS = 32
TOP_K = 4
SWIGLU_LIMIT = 7.0
SWIGLU_ALPHA = 1.702
LANES = 128
SUBLANES = 8
NEG_BIG = -1e30

C_GLA_Q, C_GLA_K, C_GLA_V, C_GLA_G = 0, 512, 1024, 2048
C_LRU_X, C_LRU_G = 3072, 4096
C_RET_Q, C_RET_K, C_RET_V, C_RET_G = 5120, 5632, 6144, 7168
C_MERGE = 8192
C_LR = 11264
N_PROJ = 11520
PROJ_TN = 3840


def _pick(n, prefs):
    for p in prefs:
        if n % p == 0:
            return p
    raise ValueError(f"no tile for {n} in {prefs}")


def _cparams(sem, vmem_mb, fuse_inputs=None):
    return pltpu.CompilerParams(dimension_semantics=sem, vmem_limit_bytes=vmem_mb * 1024 * 1024,
                                allow_input_fusion=fuse_inputs)


def _dot(a, b):
    return jnp.dot(a, b, preferred_element_type=F32)


def _dot_nt(a, b):
    return lax.dot_general(a, b, (((1,), (1,)), ((), ())), preferred_element_type=F32)


def _dot_tn(a, b):
    return lax.dot_general(a, b, (((0,), (0,)), ((), ())), preferred_element_type=F32)


def _sigmoid(x):
    return 0.5 * jnp.tanh(0.5 * x) + 0.5


def _log_sigmoid(x):
    return jnp.minimum(x, 0.0) - jnp.log(1.0 + jnp.exp(-jnp.abs(x)))


def _silu(x):
    return x * _sigmoid(x)


def _gelu_tanh(x):
    return 0.5 * x * (1.0 + jnp.tanh(math.sqrt(2.0 / math.pi) * (x + 0.044715 * (x * x * x))))


def _rms(x, gain):
    return x * lax.rsqrt(jnp.mean(x * x, axis=-1, keepdims=True) + EPS) * gain


def _dot01_exact(tri, g):
    hi = g.astype(_MXU)
    r1 = g - hi.astype(F32)
    mid = r1.astype(_MXU)
    lo = (r1 - mid.astype(F32)).astype(_MXU)
    return _dot(tri, hi) + _dot(tri, mid) + _dot(tri, lo)


def _ada_kernel(c_ref, w_ref, b_ref, o_ref):
    s = _silu(c_ref[...])
    o_ref[...] = _dot(s.astype(_MXU), w_ref[...].astype(_MXU)) + b_ref[...]


def _ada_call(cc, w_ada, b_ada):
    depth, d, n = w_ada.shape
    rows = cc.shape[0]
    tn = _pick(n, (1536, 1024, 512, 128))
    return pl.pallas_call(
        _ada_kernel,
        grid=(depth, n // tn),
        in_specs=[pl.BlockSpec((rows, d), lambda l, j: (0, 0)),
                  pl.BlockSpec((None, d, tn), lambda l, j: (l, 0, j)),
                  pl.BlockSpec((None, 1, tn), lambda l, j: (l, 0, j))],
        out_specs=pl.BlockSpec((None, rows, tn), lambda l, j: (l, 0, j)),
        out_shape=jax.ShapeDtypeStruct((depth, rows, n), F32),
        compiler_params=_cparams(("arbitrary", "arbitrary"), 24),
        name="ada",
    )(cc, w_ada, b_ada.reshape(depth, 1, n))


def _row_specs(parts, tm, d):
    offs = np.cumsum([0] + [p.shape[0] // tm for p in parts])
    specs = []
    for k, p in enumerate(parts):
        lo, n_tiles = int(offs[k]), p.shape[0] // tm

        def imap(i, *rest, lo=lo, n_tiles=n_tiles):
            return (jnp.clip(i - lo, 0, n_tiles - 1), 0)
        specs.append(pl.BlockSpec((tm, d), imap))
    return specs, [int(o) for o in offs[1:-1]]


def _pick_rows(refs, bounds, rows=slice(None)):
    val = refs[0][rows, :]
    for ref, lo in zip(refs[1:], bounds):
        val = jnp.where(pl.program_id(0) >= lo, ref[rows, :], val)
    return val


def _proj_kernel(*refs, bounds):
    n_parts = len(bounds) + 1
    x_refs = refs[:n_parts]
    mod_ref, g_ref, w_ref, o_ref, xn_ref = refs[n_parts:]

    @pl.when(pl.program_id(1) == 0)
    def _():
        y = _rms(_pick_rows(x_refs, bounds), g_ref[...])
        xn_ref[...] = (y * (1.0 + mod_ref[1:2, :]) + mod_ref[0:1, :]).astype(xn_ref.dtype)

    o_ref[...] = _dot(xn_ref[...], w_ref[...]).astype(o_ref.dtype)


def _proj_call(x_parts, mod, gain, w_in_r, nl, nb):
    t = sum(p.shape[0] for p in x_parts)
    d = x_parts[0].shape[1]
    tm = _pick(math.gcd(math.gcd(*[p.shape[0] for p in x_parts], t), nl), (1024, 512, 256))
    x_specs, bounds = _row_specs(x_parts, tm, d)
    return pl.pallas_call(
        functools.partial(_proj_kernel, bounds=bounds),
        grid=(t // tm, N_PROJ // PROJ_TN),
        in_specs=x_specs + [
            pl.BlockSpec((None, N_MOD, d), lambda i, j: (jnp.minimum(i * tm // nl, nb), 0, 0)),
            pl.BlockSpec((1, d), lambda i, j: (0, 0)),
            pl.BlockSpec((d, PROJ_TN), lambda i, j: (0, j))],
        out_specs=pl.BlockSpec((tm, PROJ_TN), lambda i, j: (i, j)),
        out_shape=jax.ShapeDtypeStruct((t, N_PROJ), _MXU),
        scratch_shapes=[pltpu.VMEM((tm, d), _MXU)],
        compiler_params=_cparams(("arbitrary", "arbitrary"), 56),
        name="proj",
    )(*x_parts, mod, gain, w_in_r)


def _la_kernel(*refs, kind, tt, n_ct, n_lt):
    if kind == "gla":
        (q_ref, k_ref, v_ref, gate_ref, lr_ref, wa2_ref, ba_ref, gain_ref,
         out_ref, st_ref, of_ref) = refs
    else:
        (q_ref, k_ref, v_ref, gate_ref, cos_ref, sin_ref, gain_ref,
         out_ref, st_ref, of_ref, dm_ref) = refs
    ph = pl.program_id(1)
    s = pl.program_id(2)
    ck = CHUNK if kind == "gla" else tt
    n_chunks = tt // ck

    @pl.when(s == 0)
    def _():
        st_ref[...] = jnp.zeros_like(st_ref)

    row = lax.broadcasted_iota(jnp.int32, (tt, tt), 0)
    col = lax.broadcasted_iota(jnp.int32, (tt, tt), 1)
    shift = ck.bit_length() - 1
    same_chunk = (row >> shift) == (col >> shift)
    tpos = lax.broadcasted_iota(jnp.int32, (tt, 1), 0).astype(F32)

    if kind == "ret":
        first = jnp.logical_and(pl.program_id(0) == 0, jnp.logical_and(ph == 0, s == 0))

        @pl.when(first)
        def _():
            for h in range(HEADS):
                lg = math.log(1.0 - 2.0 ** (-5.0 - h))
                dm_ref[h] = jnp.where(col <= row, jnp.exp((row - col).astype(F32) * lg), 0.0)
                dm_ref[HEADS + h] = jnp.where(col > row, jnp.exp((col - row).astype(F32) * lg), 0.0)

    def tile_outputs(backward):
        if backward:
            mask = jnp.logical_and(same_chunk, col > row)
            tri = jnp.logical_and(same_chunk, col >= row).astype(_MXU)
        else:
            mask = jnp.logical_and(same_chunk, col <= row)
            tri = jnp.logical_and(same_chunk, col <= row).astype(_MXU)
        order = list(reversed(range(n_chunks))) if backward else list(range(n_chunks))
        if kind == "gla":
            d = 1 if backward else 0
            z = _dot(lr_ref[...], wa2_ref[d]) + ba_ref[d]
            g = _log_sigmoid(z) * (1.0 / GLA_GATE_NORM)
            big_g = _dot01_exact(tri, g)
            lasts = [big_g[c * ck:c * ck + 1, :] if backward else big_g[(c + 1) * ck - 1:(c + 1) * ck, :]
                     for c in range(n_chunks)]
            g_last = jnp.concatenate([jnp.broadcast_to(r, (ck, r.shape[1])) for r in lasts], axis=0)
            e_pos = jnp.exp(big_g)
            e_neg = jnp.exp(-big_g)
            e_end = jnp.exp(g_last - big_g)
            decs = [jnp.exp(r) for r in lasts]
        outs = []
        for h in range(HEADS):
            ks = slice(h * DK, (h + 1) * DK)
            vs = slice(h * DV, (h + 1) * DV)
            qh = q_ref[:, ks].astype(F32)
            kh = k_ref[:, ks].astype(F32)
            vh = v_ref[:, vs]
            if kind == "gla":
                qh = qh * DK ** -0.5
                q_dec = (qh * e_pos[:, ks]).astype(_MXU)
                k_inv = (kh * e_neg[:, ks]).astype(_MXU)
                k_end = (kh * e_end[:, ks]).astype(_MXU)
                sc = jnp.where(mask, _dot_nt(q_dec, k_inv), 0.0)
                dec_h = [dcy[:, ks] for dcy in decs]
            else:
                kh = kh * DK ** -0.5
                cos = cos_ref[...]
                sin = sin_ref[...]
                qh = qh * cos + pltpu.roll(qh, DK // 2, 1) * sin
                kh = kh * cos + pltpu.roll(kh, DK // 2, 1) * sin
                lg = math.log(1.0 - 2.0 ** (-5.0 - h))
                steps = (ck - tpos) if backward else (tpos + 1.0)
                q_dec = (qh * jnp.exp(steps * lg)).astype(_MXU)
                k_end = (kh * jnp.exp((ck - steps) * lg)).astype(_MXU)
                sc = _dot_nt(qh.astype(_MXU), kh.astype(_MXU)) * dm_ref[(HEADS if backward else 0) + h]
                dec_h = [math.exp(ck * lg)] * n_chunks
            o_intra = _dot(sc.astype(_MXU), vh)
            st = st_ref[h]
            parts = [None] * n_chunks
            for c in order:
                rs = slice(c * ck, (c + 1) * ck)
                parts[c] = o_intra[rs] + _dot_nt(q_dec[rs], st.astype(_MXU))
                st = st * dec_h[c] + _dot_tn(vh[rs], k_end[rs])
            st_ref[h] = st
            outs.append(parts[0] if n_chunks == 1 else jnp.concatenate(parts, axis=0))
        return jnp.concatenate(outs, axis=1)

    @pl.when(ph == 0)
    def _():
        base = pl.multiple_of(s * tt, tt)
        of_ref[pl.ds(base, tt), :] = tile_outputs(False)

    @pl.when(ph == 1)
    def _():
        loc = jnp.where(s < n_ct, n_ct - 1 - s, n_ct + n_lt - 1 - (s - n_ct))
        base = pl.multiple_of(loc * tt, tt)
        o = of_ref[pl.ds(base, tt), :] + tile_outputs(True)
        gate = gate_ref[...].astype(F32)
        parts = []
        for h in range(HEADS):
            oh = o[:, h * DV:(h + 1) * DV]
            if kind == "ret":
                oh = oh - jnp.mean(oh, axis=-1, keepdims=True)
            parts.append(oh * lax.rsqrt(jnp.mean(oh * oh, axis=-1, keepdims=True) + EPS))
        normed = jnp.concatenate(parts, axis=1) * gain_ref[...]
        out_ref[...] = (normed * _silu(gate)).astype(out_ref.dtype)


def _la_call(kind, p, extra, gain, nb, nl, nc):
    t = p.shape[0]
    tt = _pick(math.gcd(nl, nc), (256, 128, 64))
    n_ct, n_lt = nc // tt, nl // tt
    lat_tiles = nb * n_lt
    cq, ck, cv, cg = ((C_GLA_Q, C_GLA_K, C_GLA_V, C_GLA_G) if kind == "gla"
                      else (C_RET_Q, C_RET_K, C_RET_V, C_RET_G))
    hd = HEADS * DK
    hv = HEADS * DV

    def loc_of(ph, s):
        back = jnp.where(s < n_ct, n_ct - 1 - s, n_ct + n_lt - 1 - (s - n_ct))
        return jnp.where(ph == 0, s, back)

    def row_blk(b, loc):
        return jnp.where(loc < n_ct, lat_tiles + b * n_ct + loc, b * n_lt + (loc - n_ct))

    def in_map(cblk):
        return lambda b, ph, s: (row_blk(b, loc_of(ph, s)), cblk)

    def second_pass_map(cblk):
        return lambda b, ph, s: (row_blk(b, loc_of(1, jnp.where(ph == 0, 0, s))), cblk)

    in_specs = [pl.BlockSpec((tt, hd), in_map(cq // hd)),
                pl.BlockSpec((tt, hd), in_map(ck // hd)),
                pl.BlockSpec((tt, hv), in_map(cv // hv)),
                pl.BlockSpec((tt, hv), second_pass_map(cg // hv))]
    args = [p, p, p, p]
    if kind == "gla":
        wa2p, ba = extra
        in_specs += [pl.BlockSpec((tt, LANES), in_map(C_LR // LANES)),
                     pl.BlockSpec((2, LANES, hd), lambda b, ph, s: (0, 0, 0)),
                     pl.BlockSpec((2, 1, hd), lambda b, ph, s: (0, 0, 0))]
        args += [p, wa2p, ba]
    else:
        cos_t, sin_t = extra

        def rope_map(b, ph, s):
            loc = loc_of(ph, s)
            return (jnp.where(loc < n_ct, n_lt + loc, loc - n_ct), 0)
        in_specs += [pl.BlockSpec((tt, DK), rope_map), pl.BlockSpec((tt, DK), rope_map)]
        args += [cos_t, sin_t]
    in_specs.append(pl.BlockSpec((1, hv), lambda b, ph, s: (0, 0)))
    args.append(gain)
    return pl.pallas_call(
        functools.partial(_la_kernel, kind=kind, tt=tt, n_ct=n_ct, n_lt=n_lt),
        grid=(nb, 2, n_ct + n_lt),
        in_specs=in_specs,
        out_specs=pl.BlockSpec((tt, hv), second_pass_map(0)),
        out_shape=jax.ShapeDtypeStruct((t, hv), _MXU),
        scratch_shapes=[pltpu.VMEM((HEADS, DV, DK), F32),
                        pltpu.VMEM((nc + nl, hv), F32)]
        + ([pltpu.VMEM((2 * HEADS, tt, tt), F32)] if kind == "ret" else []),
        compiler_params=_cparams(("arbitrary", "arbitrary", "arbitrary"), 48),
        name=kind,
    )(*args)


def _scan_group(a, b, h, reverse):
    row = lax.broadcasted_iota(jnp.int32, a.shape, 0)
    for sft in (1, 2, 4):
        if reverse:
            a_sh = pltpu.roll(a, SUBLANES - sft, 0)
            b_sh = pltpu.roll(b, SUBLANES - sft, 0)
            m = row < SUBLANES - sft
        else:
            a_sh = pltpu.roll(a, sft, 0)
            b_sh = pltpu.roll(b, sft, 0)
            m = row >= sft
        b = jnp.where(m, a * b_sh + b, b)
        a = jnp.where(m, a * a_sh, a)
    hh = a * h + b
    return hh, (hh[0:1, :] if reverse else hh[SUBLANES - 1:SUBLANES, :])


def _lru_kernel(x_ref, gel_ref, cw_ref, cb_ref, wa_ref, ba_ref, wi_ref, bi_ref, lam_ref, h0_ref,
                out_ref, hn_ref, xpad_ref, xc_ref, hf_ref, *, seg, tile):
    n_tiles = seg // tile
    groups = tile // SUBLANES
    cw = cw_ref[...]
    cb = cb_ref[...]
    zeros8 = jnp.zeros((SUBLANES, xpad_ref.shape[1]), F32)
    xpad_ref[0:SUBLANES, :] = zeros8
    xpad_ref[SUBLANES + seg:2 * SUBLANES + seg, :] = zeros8

    def copy_in(i, carry):
        t0 = pl.multiple_of(i * tile, tile)
        xpad_ref[pl.ds(t0 + SUBLANES, tile), :] = x_ref[pl.ds(t0, tile), :].astype(F32)
        return carry
    lax.fori_loop(0, n_tiles, copy_in, 0)

    neg_c = [(-0.5 * LRU_C) * _log_sigmoid(lam_ref[d:d + 1, :]) for d in range(2)]

    def gates(xc, d):
        xb = xc.astype(_MXU)
        tr = jnp.tanh(_dot(xb, wa_ref[d]) + ba_ref[d:d + 1, :])
        ig = 0.5 * jnp.tanh(_dot(xb, wi_ref[d]) + bi_ref[d:d + 1, :]) + 0.5
        nla = neg_c[d] * tr + neg_c[d]
        a = jnp.exp(-nla)
        b = jnp.sqrt(jnp.tanh(nla) * (a * a + 1.0)) * (ig * xc)
        return a, b

    def fwd_tile(i, h):
        t0 = pl.multiple_of(i * tile, tile)
        win = xpad_ref[pl.ds(t0, tile + 2 * SUBLANES), :]
        xc = (cw[0:1, :] * win[6:6 + tile] + cw[1:2, :] * win[7:7 + tile]
              + cw[2:3, :] * win[8:8 + tile] + cw[3:4, :] * win[9:9 + tile] + cb)
        xc_ref[pl.ds(t0, tile), :] = xc
        a, b = gates(xc, 0)
        for g in range(groups):
            rs = slice(g * SUBLANES, (g + 1) * SUBLANES)
            hh, h = _scan_group(a[rs], b[rs], h, False)
            hf_ref[pl.ds(t0 + g * SUBLANES, SUBLANES), :] = hh
        return h
    h_f = lax.fori_loop(0, n_tiles, fwd_tile, h0_ref[0:1, :])

    def bwd_tile(j, h):
        i = n_tiles - 1 - j
        t0 = pl.multiple_of(i * tile, tile)
        xc = xc_ref[pl.ds(t0, tile), :]
        a, b = gates(xc, 1)
        pair = 2 * SUBLANES
        for q in reversed(range(tile // pair)):
            hi = slice(q * pair + SUBLANES, (q + 1) * pair)
            lo = slice(q * pair, q * pair + SUBLANES)
            hh_hi, h = _scan_group(a[hi], b[hi], h, True)
            hh_lo, h = _scan_group(a[lo], b[lo], h, True)
            rows = pl.ds(pl.multiple_of(t0 + q * pair, pair), pair)
            hsum = hf_ref[rows, :] + jnp.concatenate([hh_lo, hh_hi], axis=0)
            gel = gel_ref[rows, :].astype(F32)
            out_ref[rows, :] = (hsum * _gelu_tanh(gel)).astype(out_ref.dtype)
        return h
    h_b = lax.fori_loop(0, n_tiles, bwd_tile, h0_ref[1:2, :])
    hn_ref[0:1, :] = h_f
    hn_ref[1:2, :] = h_b


def _lru_call(p, prev_out, h0, lw, nb, seg, row_blk0):
    cw, cb, wa, ba, wi, bi, lam = lw
    t = p.shape[0]
    width = LRU_BLOCKS * DV
    cbw = width // LRU_BLOCKS
    tile = _pick(seg, (512, 256, 128, 64))
    xblk = C_LRU_X // cbw
    gblk = C_LRU_G // cbw
    in_specs = [pl.BlockSpec((seg, cbw), lambda b, c: (row_blk0 + b, xblk + c)),
                pl.BlockSpec((seg, cbw), lambda b, c: (row_blk0 + b, gblk + c)),
                pl.BlockSpec((4, cbw), lambda b, c: (0, c)),
                pl.BlockSpec((1, cbw), lambda b, c: (0, c)),
                pl.BlockSpec((2, None, cbw, cbw), lambda b, c: (0, c, 0, 0)),
                pl.BlockSpec((2, cbw), lambda b, c: (0, c)),
                pl.BlockSpec((2, None, cbw, cbw), lambda b, c: (0, c, 0, 0)),
                pl.BlockSpec((2, cbw), lambda b, c: (0, c)),
                pl.BlockSpec((2, cbw), lambda b, c: (0, c)),
                pl.BlockSpec((None, 2, cbw), lambda b, c: (b, 0, c))]
    in_specs.append(pl.BlockSpec(memory_space=pl.ANY))
    args = [p, p, cw, cb, wa, ba, wi, bi, lam, h0, prev_out]

    def body(*refs):
        _lru_kernel(*(refs[:10] + refs[11:]), seg=seg, tile=tile)

    return pl.pallas_call(
        body,
        grid=(nb, LRU_BLOCKS),
        in_specs=in_specs,
        out_specs=[pl.BlockSpec((seg, cbw), lambda b, c: (row_blk0 + b, c)),
                   pl.BlockSpec((None, 2, cbw), lambda b, c: (b, 0, c))],
        out_shape=[jax.ShapeDtypeStruct((t, width), _MXU),
                   jax.ShapeDtypeStruct((nb, 2, width), F32)],
        scratch_shapes=[pltpu.VMEM((seg + 2 * SUBLANES, cbw), F32),
                        pltpu.VMEM((seg, cbw), F32),
                        pltpu.VMEM((seg, cbw), F32)],
        input_output_aliases={10: 0},
        compiler_params=_cparams(("arbitrary", "arbitrary"), 48),
        name="lru",
    )(*args)


def _finish_kernel(*refs, bounds):
    n_parts = len(bounds) + 1
    xs_refs = refs[:n_parts]
    (gla_ref, lru_ref, ret_ref, m0_ref, m1_ref, m2_ref, wb_ref, wo_ref,
     mod_ref, g2_ref, wr_ref, br_ref, tril_ref,
     xo_ref, xn_ref, idx_ref, wgt_ref, rank_ref, cnt_ref, carry_ref) = refs[n_parts:]

    @pl.when(pl.program_id(0) == 0)
    def _():
        carry_ref[...] = jnp.zeros_like(carry_ref)

    hm = tril_ref.shape[0]
    total = carry_ref[0:1, :]
    for sub in range(xo_ref.shape[0] // hm):
        rs = slice(sub * hm, (sub + 1) * hm)
        merged = (_sigmoid(m0_ref[rs, :].astype(F32)) * _dot(gla_ref[rs, :], wb_ref[0])
                  + _sigmoid(m1_ref[rs, :].astype(F32)) * _dot(lru_ref[rs, :], wb_ref[1])
                  + _sigmoid(m2_ref[rs, :].astype(F32)) * _dot(ret_ref[rs, :], wb_ref[2]))
        x = _pick_rows(xs_refs, bounds, rs) + mod_ref[2:3, :] * _dot(merged.astype(_MXU), wo_ref[...])
        xo_ref[rs, :] = x
        xn = _rms(x, g2_ref[...]) * (1.0 + mod_ref[4:5, :]) + mod_ref[3:4, :]
        xn_ref[rs, :] = xn

        logits = _dot(xn.astype(_MXU), wr_ref[...]) + br_ref[...]
        lane = lax.broadcasted_iota(jnp.int32, logits.shape, 1)
        ids, vals = [], []
        for _ in range(TOP_K):
            m = jnp.max(logits, axis=1, keepdims=True)
            sel = jnp.min(jnp.where(logits == m, lane.astype(F32), float(LANES)), axis=1,
                          keepdims=True).astype(jnp.int32)
            ids.append(sel)
            vals.append(m)
            logits = jnp.where(lane == sel, -jnp.inf, logits)
        ex = [jnp.exp(v - vals[0]) for v in vals]
        denom = ex[0] + ex[1] + ex[2] + ex[3]
        onehot = jnp.zeros(logits.shape, F32)
        for sel in ids:
            onehot = onehot + (lane == sel).astype(F32)
        before = _dot(tril_ref[...], onehot.astype(_MXU)) + total
        idx_o = jnp.zeros(logits.shape, jnp.int32)
        rank_o = jnp.zeros(logits.shape, jnp.int32)
        wgt_o = jnp.zeros(logits.shape, F32)
        for k in range(TOP_K):
            rk = jnp.sum(jnp.where(lane == ids[k], before, 0.0), axis=1, keepdims=True)
            idx_o = jnp.where(lane == k, ids[k], idx_o)
            rank_o = jnp.where(lane == k, rk.astype(jnp.int32), rank_o)
            wgt_o = jnp.where(lane == k, ex[k] / denom, wgt_o)
        idx_ref[rs, :] = idx_o
        rank_ref[rs, :] = rank_o
        wgt_ref[rs, :] = wgt_o
        total = total + jnp.sum(onehot, axis=0, keepdims=True)
    carry_ref[...] = jnp.broadcast_to(total, carry_ref.shape)
    cnt_ref[...] = jnp.broadcast_to(total, cnt_ref.shape)


def _finish_call(gla, lru, ret, p, wb, wo, xs_parts, mod, g2, wr, br, n_rows, nl, nb):
    d = xs_parts[0].shape[1]
    tm = _pick(math.gcd(math.gcd(*[q.shape[0] for q in xs_parts], n_rows), nl), (512, 256))
    xs_specs, bounds = _row_specs(xs_parts, tm, d)
    hm = tm
    tril = jnp.asarray(np.tril(np.ones((hm, hm), np.float32), -1), _MXU)
    row = lambda i: (i, 0)
    const2 = lambda i: (0, 0)
    mblk = C_MERGE // d
    in_specs = xs_specs + [
                pl.BlockSpec((tm, d), row), pl.BlockSpec((tm, d), row), pl.BlockSpec((tm, d), row),
                pl.BlockSpec((tm, d), lambda i: (i, mblk)),
                pl.BlockSpec((tm, d), lambda i: (i, mblk + 1)),
                pl.BlockSpec((tm, d), lambda i: (i, mblk + 2)),
                pl.BlockSpec((3, d, d), lambda i: (0, 0, 0)),
                pl.BlockSpec((d, d), const2),
                pl.BlockSpec((None, N_MOD, d), lambda i: (jnp.minimum(i * tm // nl, nb), 0, 0)),
                pl.BlockSpec((1, d), const2),
                pl.BlockSpec((d, LANES), const2),
                pl.BlockSpec((1, LANES), const2),
                pl.BlockSpec((hm, hm), const2)]
    out_specs = [pl.BlockSpec((tm, d), row), pl.BlockSpec((tm, d), row),
                 pl.BlockSpec((tm, LANES), row), pl.BlockSpec((tm, LANES), row),
                 pl.BlockSpec((tm, LANES), row), pl.BlockSpec((SUBLANES, LANES), const2)]
    out_shape = [jax.ShapeDtypeStruct((n_rows, d), F32), jax.ShapeDtypeStruct((n_rows, d), F32),
                 jax.ShapeDtypeStruct((n_rows, LANES), jnp.int32),
                 jax.ShapeDtypeStruct((n_rows, LANES), F32),
                 jax.ShapeDtypeStruct((n_rows, LANES), jnp.int32),
                 jax.ShapeDtypeStruct((SUBLANES, LANES), F32)]
    return pl.pallas_call(
        functools.partial(_finish_kernel, bounds=bounds),
        grid=(n_rows // tm,),
        in_specs=in_specs,
        out_specs=out_specs,
        out_shape=out_shape,
        scratch_shapes=[pltpu.VMEM((SUBLANES, LANES), F32)],
        compiler_params=_cparams(("arbitrary",), 56,
                                 [False] * (len(xs_parts) + 6) + [True, True, False, False, True, True, False]),
        name="finish",
    )(*xs_parts, gla, lru, ret, p, p, p, wb, wo, mod, g2, wr, br, tril)


def _slot_kernel(idx_ref, rank_ref, start_ref, slot_ref):
    idx = idx_ref[...].astype(F32)
    lane = lax.broadcasted_iota(jnp.int32, idx.shape, 1)
    start = start_ref[...]
    out = jnp.zeros(idx.shape, F32)
    for k in range(TOP_K):
        sel = jnp.sum(jnp.where(lane == k, idx, 0.0), axis=1, keepdims=True).astype(jnp.int32)
        st = jnp.sum(jnp.where(lane == sel, start, 0.0), axis=1, keepdims=True)
        out = jnp.where(lane == k, st, out)
    slot_ref[...] = out.astype(jnp.int32) + rank_ref[...]


def _slot_call(idx, rank, pad_start):
    n = idx.shape[0]
    tm = _pick(n, (2048, 1024, 512, 256))
    row = lambda i: (i, 0)
    return pl.pallas_call(
        _slot_kernel,
        grid=(n // tm,),
        in_specs=[pl.BlockSpec((tm, LANES), row), pl.BlockSpec((tm, LANES), row),
                  pl.BlockSpec((1, LANES), lambda i: (0, 0))],
        out_specs=pl.BlockSpec((tm, LANES), row),
        out_shape=jax.ShapeDtypeStruct((n, LANES), jnp.int32),
        compiler_params=_cparams(("arbitrary",), 24),
        name="slots",
    )(idx, rank, pad_start)


def _dispatch_kernel(fill_ref, slot_ref, x_ref, dst_ref, zero_ref, sem, fill_sem, *, tm, bm, n_blocks):
    @pl.when(pl.program_id(0) == 0)
    def _():
        zero_ref[...] = jnp.zeros_like(zero_ref)

        def clear(blk):
            return pltpu.make_async_copy(zero_ref, dst_ref.at[pl.ds(pl.multiple_of(blk * bm, bm), bm)],
                                         fill_sem)

        def tail_start(j, carry):
            clear(j).start()
            return carry

        def tail_wait(j, carry):
            clear(j).wait()
            return carry

        for e in range(N_EXPERTS):
            @pl.when(fill_ref[0, e] >= 0)
            def _():
                clear(fill_ref[0, e]).start()
        lax.fori_loop(fill_ref[0, N_EXPERTS], n_blocks, tail_start, 0)
        for e in range(N_EXPERTS):
            @pl.when(fill_ref[0, e] >= 0)
            def _():
                clear(fill_ref[0, e]).wait()
        lax.fori_loop(fill_ref[0, N_EXPERTS], n_blocks, tail_wait, 0)

    def issue(g, carry):
        t0 = pl.multiple_of(g * SUBLANES, SUBLANES)
        for u in range(SUBLANES):
            for k in range(TOP_K):
                pltpu.make_async_copy(
                    x_ref.at[pl.ds(t0 + u, 1)],
                    dst_ref.at[pl.ds(slot_ref[0, (t0 + u) * TOP_K + k], 1)],
                    sem).start(priority=k % 2)
        return carry
    lax.fori_loop(0, tm // SUBLANES, issue, 0)
    for k in range(TOP_K):
        pltpu.make_async_copy(x_ref, dst_ref.at[pl.ds(0, tm)], sem).wait()


def _dispatch_call(fill, slots3, xn, n_blocks, bm):
    n, d = xn.shape
    tm = slots3.shape[2] // TOP_K
    return pl.pallas_call(
        functools.partial(_dispatch_kernel, tm=tm, bm=bm, n_blocks=n_blocks),
        grid=(n // tm,),
        in_specs=[pl.BlockSpec((1, LANES), lambda i: (0, 0), memory_space=pltpu.SMEM),
                  pl.BlockSpec((None, 1, tm * TOP_K), lambda i: (i, 0, 0), memory_space=pltpu.SMEM),
                  pl.BlockSpec((tm, d), lambda i: (i, 0))],
        out_specs=pl.BlockSpec(memory_space=pl.ANY),
        out_shape=jax.ShapeDtypeStruct((n_blocks * bm, d), F32),
        scratch_shapes=[pltpu.VMEM((bm, d), F32), pltpu.SemaphoreType.DMA(()),
                        pltpu.SemaphoreType.DMA(())],
        compiler_params=_cparams(("arbitrary",), 24),
        name="dispatch",
    )(fill, slots3, xn)


def _experts_kernel(be_ref, nu_ref, x_ref, wgu_ref, bgu_ref, wd_ref, bd_ref, perm_ref, y_ref,
                    wgu_s, wd_s):
    j = pl.program_id(0)
    f = wd_ref.shape[0]
    grp = perm_ref.shape[0]
    half = grp // 2
    active = j < nu_ref[0]
    changed = jnp.logical_or(j == 0, be_ref[j] != be_ref[jnp.maximum(j - 1, 0)])

    @pl.when(jnp.logical_and(active, changed))
    def _():
        for g in range(2 * f // grp):
            cols = slice(g * grp, (g + 1) * grp)
            wgu_s[:, cols] = _dot(wgu_ref[:, cols].astype(_MXU), perm_ref[...]).astype(_MXU)
        wd_s[...] = wd_ref[...].astype(_MXU)

    @pl.when(active)
    def _():
        gu = _dot(x_ref[...].astype(_MXU), wgu_s[...]) + bgu_ref[...]
        acts = []
        for g in range(2 * f // grp):
            gate = jnp.minimum(gu[:, g * grp:g * grp + half], SWIGLU_LIMIT)
            up = jnp.clip(gu[:, g * grp + half:(g + 1) * grp], -SWIGLU_LIMIT, SWIGLU_LIMIT)
            acts.append((gate * _sigmoid(SWIGLU_ALPHA * gate) * (up + 1.0)).astype(_MXU))
        y_ref[...] = _dot(jnp.concatenate(acts, axis=1), wd_s[...]) + bd_ref[...]

    @pl.when(jnp.logical_not(active))
    def _():
        y_ref[...] = jnp.zeros_like(y_ref)


def _experts_call(li, block_e, n_used, x_sorted, w_gu, bgu_r, w_down, bd, bm):
    n_slots, d = x_sorted.shape
    f = w_down.shape[2]
    grp = 2 * LANES
    src = np.concatenate([np.arange(0, grp, 2), np.arange(1, grp, 2)])
    perm_np = np.zeros((grp, grp), np.float32)
    perm_np[src, np.arange(grp)] = 1.0
    perm = jnp.asarray(perm_np, _MXU)

    def xmap(j, be, nu):
        return (jnp.minimum(j, nu[0] - 1), 0)

    def wmap(j, be, nu):
        return (li, be[j], 0, 0)

    grid_spec = pltpu.PrefetchScalarGridSpec(
        num_scalar_prefetch=2,
        grid=(n_slots // bm,),
        in_specs=[pl.BlockSpec((bm, d), xmap),
                  pl.BlockSpec((None, None, d, 2 * f), wmap),
                  pl.BlockSpec((None, None, 1, 2 * f), wmap),
                  pl.BlockSpec((None, None, f, d), wmap),
                  pl.BlockSpec((None, None, 1, d), wmap),
                  pl.BlockSpec((grp, grp), lambda j, be, nu: (0, 0))],
        out_specs=pl.BlockSpec((bm, d), lambda j, be, nu: (j, 0)),
        scratch_shapes=[pltpu.VMEM((d, 2 * f), _MXU), pltpu.VMEM((f, d), _MXU)])
    return pl.pallas_call(
        _experts_kernel,
        grid_spec=grid_spec,
        out_shape=jax.ShapeDtypeStruct((n_slots, d), F32),
        compiler_params=_cparams(("arbitrary",), 56),
        name="experts",
    )(block_e, n_used, x_sorted, w_gu, bgu_r, w_down, bd, perm)


def _combine_kernel(slot_ref, wgt_ref, xs_ref, mod_ref, fin_ref, y_ref, out_ref, buf_ref, sem,
                    *, tm, final):
    def issue(g, carry):
        t0 = pl.multiple_of(g * SUBLANES, SUBLANES)
        for u in range(SUBLANES):
            for k in range(TOP_K):
                pltpu.make_async_copy(
                    y_ref.at[pl.ds(slot_ref[0, (t0 + u) * TOP_K + k], 1)],
                    buf_ref.at[k, pl.ds(t0 + u, 1)], sem).start(priority=k % 2)
        return carry
    lax.fori_loop(0, tm // SUBLANES, issue, 0)
    for k in range(TOP_K):
        pltpu.make_async_copy(y_ref.at[pl.ds(0, tm)], buf_ref.at[k], sem).wait()

    wgt = wgt_ref[...]
    acc = wgt[:, 0:1] * buf_ref[0]
    for k in range(1, TOP_K):
        acc = acc + wgt[:, k:k + 1] * buf_ref[k]
    x = xs_ref[...] + mod_ref[5:6, :] * acc
    out_ref[...] = _rms(x, fin_ref[...]) if final else x


def _combine_call(slots3, wgt, xs, mod, fin, y_sorted, nl, nb, final):
    n, d = xs.shape
    tm = slots3.shape[2] // TOP_K
    return pl.pallas_call(
        functools.partial(_combine_kernel, tm=tm, final=final),
        grid=(n // tm,),
        in_specs=[pl.BlockSpec((None, 1, tm * TOP_K), lambda i: (i, 0, 0), memory_space=pltpu.SMEM),
                  pl.BlockSpec((tm, LANES), lambda i: (i, 0)),
                  pl.BlockSpec((tm, d), lambda i: (i, 0)),
                  pl.BlockSpec((None, N_MOD, d), lambda i: (jnp.minimum(i * tm // nl, nb), 0, 0)),
                  pl.BlockSpec((1, d), lambda i: (0, 0)),
                  pl.BlockSpec(memory_space=pl.ANY)],
        out_specs=pl.BlockSpec((tm, d), lambda i: (i, 0)),
        out_shape=jax.ShapeDtypeStruct((n, d), F32),
        scratch_shapes=[pltpu.VMEM((TOP_K, tm, d), F32), pltpu.SemaphoreType.DMA(())],
        compiler_params=_cparams(("arbitrary",), 48),
        name="combine",
    )(slots3, wgt, xs, mod, fin, y_sorted)


MOE_BM = 512


def _moe(li, xn, idx, wgt, rank, counts, xs_mid, mod, fin, w_gu, bgu_r, w_down, bd, nl, nb, final):
    n = xn.shape[0]
    bm = MOE_BM
    tm = _pick(math.gcd(n, nl), (1024, 512, 256, 128))
    n_blocks = -(-(n * TOP_K) // bm) + N_EXPERTS
    cnt = counts[0, :N_EXPERTS].astype(jnp.int32)
    padded = (cnt + bm - 1) // bm * bm
    pad_end = jnp.cumsum(padded)
    pad_start = pad_end - padded
    n_used = (pad_end[-1] // bm).astype(jnp.int32)
    blk = jnp.arange(n_blocks, dtype=jnp.int32)
    first_row = jnp.minimum(blk, n_used - 1) * bm
    block_e = jnp.minimum(jnp.sum(pad_end[None, :] <= first_row[:, None], axis=1),
                          N_EXPERTS - 1).astype(jnp.int32)
    start_row = jnp.zeros((1, LANES), F32).at[0, :N_EXPERTS].set(pad_start.astype(F32))
    slots = _slot_call(idx, rank, start_row)
    slots3 = slots[:, :TOP_K].reshape(n // tm, 1, tm * TOP_K)
    last_blk = jnp.where(cnt > 0, pad_end // bm - 1, -1).astype(jnp.int32)
    fill = jnp.zeros((1, LANES), jnp.int32).at[0, :N_EXPERTS].set(last_blk).at[0, N_EXPERTS].set(n_used)
    x_sorted = _dispatch_call(fill, slots3, xn, n_blocks, bm)
    y_sorted = _experts_call(li, block_e, n_used.reshape(1), x_sorted, w_gu, bgu_r, w_down, bd, bm)
    return _combine_call(slots3, wgt, xs_mid, mod, fin, y_sorted, nl, nb, final)


def _rope_tables(nl, nc):
    rows = nl // GRID_W
    r = np.broadcast_to(np.arange(rows)[:, None], (rows, GRID_W)).reshape(-1).astype(np.float32)
    c = np.broadcast_to(np.arange(GRID_W)[None, :], (rows, GRID_W)).reshape(-1).astype(np.float32)
    n_freq = DK // 4
    inv = (ROPE_BASE ** (-jnp.arange(n_freq, dtype=F32) / n_freq))
    ang = jnp.concatenate([jnp.asarray(r)[:, None] * inv, jnp.asarray(c)[:, None] * inv], axis=-1)
    cos, sin = jnp.cos(ang), jnp.sin(ang)
    cos2 = jnp.concatenate([cos, cos], axis=-1)
    sin2 = jnp.concatenate([-sin, sin], axis=-1)
    cos_t = jnp.concatenate([cos2, jnp.ones((nc, DK), F32)], axis=0)
    sin_t = jnp.concatenate([sin2, jnp.zeros((nc, DK), F32)], axis=0)
    return cos_t, sin_t


def kernel(x, c, ctx, c_ctx, w_ada, b_ada, norm1, norm2, w_in, gla_wa2, gla_ba, gla_norm, lru_conv_w, lru_conv_b, lru_wa, lru_ba, lru_wi, lru_bi, lru_lam, ret_norm, w_branch, w_out, w_router, b_router, w_gu, b_gu, w_down, b_down, final_norm):
    nb, nl, d = x.shape
    nc = ctx.shape[1]
    depth = w_ada.shape[0]
    n_lat = nb * nl
    f = w_down.shape[2]

    n_tok = n_lat + nb * nc
    xs = (x.reshape(n_lat, d), ctx.reshape(nb * nc, d))
    mod_rows = -(-(nb + 1) // SUBLANES) * SUBLANES
    cc = jnp.zeros((mod_rows, d), F32).at[:nb].set(c).at[nb].set(c_ctx)
    mod_all = _ada_call(cc, w_ada, b_ada).reshape(depth, mod_rows, N_MOD, d)
    cos_t, sin_t = _rope_tables(nl, nc)
    bgu_r = b_gu.reshape(depth, N_EXPERTS, 2 * f // (2 * LANES), LANES, 2)
    bgu_r = jnp.swapaxes(bgu_r, -1, -2).reshape(depth, N_EXPERTS, 1, 2 * f)
    bd_r = b_down.reshape(depth, N_EXPERTS, 1, d)
    lru_buf = jnp.zeros((n_tok, lru_conv_w.shape[2]), _MXU)

    out = None
    for li in range(depth):
        last = li == depth - 1
        mod = mod_all[li]
        w = w_in[li]
        w_in_r = jnp.concatenate(
            [w[:, :3072], w[:, 3104:], w[:, 3072:3104],
             jnp.zeros((d, N_PROJ - w.shape[1]), w.dtype)], axis=1).astype(_MXU)
        p = _proj_call(xs, mod, norm1[li].reshape(1, d), w_in_r, nl, nb)

        wa2p = jnp.zeros((2, LANES, HEADS * DK), F32)
        wa2p = wa2p.at[0, :GLA_RANK].set(gla_wa2[li, 0]).at[1, GLA_RANK:2 * GLA_RANK].set(gla_wa2[li, 1])
        gla = _la_call("gla", p, (wa2p.astype(_MXU), gla_ba[li].reshape(2, 1, HEADS * DK)),
                       gla_norm[li].reshape(1, HEADS * DV), nb, nl, nc)
        ret = _la_call("ret", p, (cos_t, sin_t), ret_norm[li].reshape(1, HEADS * DV), nb, nl, nc)

        lw = (lru_conv_w[li], lru_conv_b[li].reshape(1, -1), (0.5 * lru_wa[li]).astype(_MXU),
              0.5 * lru_ba[li], (0.5 * lru_wi[li]).astype(_MXU), 0.5 * lru_bi[li], lru_lam[li])
        width = lru_conv_w.shape[2]
        lru_c, h_ctx = _lru_call(p, lru_buf, jnp.zeros((nb, 2, width), F32), lw, nb, nc, n_lat // nc)
        lru, _ = _lru_call(p, lru_c, h_ctx, lw, nb, nl, 0)
        lru_buf = lru

        n_rows = n_lat if last else n_tok
        wr = jnp.zeros((d, LANES), F32).at[:, :N_EXPERTS].set(w_router[li]).astype(_MXU)
        br = jnp.full((1, LANES), NEG_BIG, F32).at[0, :N_EXPERTS].set(b_router[li])
        xs_mid, xn2, idx, wgt, rank, counts = _finish_call(
            gla, lru, ret, p, w_branch[li].astype(_MXU), w_out[li].astype(_MXU), xs, mod,
            norm2[li].reshape(1, d), wr, br, n_rows, nl, nb)

        out = _moe(li, xn2, idx, wgt, rank, counts, xs_mid, mod, final_norm.reshape(1, d),
                   w_gu, bgu_r, w_down, bd_r, nl, nb, last)
        xs = (out,)
    return out.reshape(nb, nl, d)
```

```python
import functools
import math

import numpy as np
import jax
import jax.numpy as jnp
from jax import lax
from jax.experimental import pallas as pl
from jax.experimental.pallas import tpu as pltpu

F32 = jnp.float32
_MXU = jnp.bfloat16

EPS = 1e-6
N_MOD = 6
GRID_W = 64
CHUNK = 64
HEADS = 4
DK = 128
DV = 256
GLA_RANK = 16
GLA_GATE_NORM = 16.0
LRU_BLOCKS = 4
LRU_C = 8.0
ROPE_BASE = 10000.0
N_EXPERTS = 32
TOP_K = 4
SWIGLU_LIMIT = 7.0
SWIGLU_ALPHA = 1.702
LANES = 128
SUBLANES = 8
NEG_BIG = -1e30

C_GLA_Q, C_GLA_K, C_GLA_V, C_GLA_G = 0, 512, 1024, 2048
C_LRU_X, C_LRU_G = 3072, 4096
C_RET_Q, C_RET_K, C_RET_V, C_RET_G = 5120, 5632, 6144, 7168
C_MERGE = 8192
C_LR = 11264
N_PROJ = 11520
PROJ_TN = 3840


def _pick(n, prefs):
    for p in prefs:
        if n % p == 0:
            return p
    raise ValueError(f"no tile for {n} in {prefs}")


def _cparams(sem, vmem_mb):
    return pltpu.CompilerParams(dimension_semantics=sem, vmem_limit_bytes=vmem_mb * 1024 * 1024)


def _dot(a, b):
    return jnp.dot(a, b, preferred_element_type=F32)


def _dot_nt(a, b):
    return lax.dot_general(a, b, (((1,), (1,)), ((), ())), preferred_element_type=F32)


def _dot_tn(a, b):
    return lax.dot_general(a, b, (((0,), (0,)), ((), ())), preferred_element_type=F32)


def _sigmoid(x):
    return 0.5 * jnp.tanh(0.5 * x) + 0.5


def _log_sigmoid(x):
    return jnp.minimum(x, 0.0) - jnp.log(1.0 + jnp.exp(-jnp.abs(x)))


def _silu(x):
    return x * _sigmoid(x)


def _gelu_tanh(x):
    return 0.5 * x * (1.0 + jnp.tanh(math.sqrt(2.0 / math.pi) * (x + 0.044715 * (x * x * x))))


def _rms(x, gain):
    return x * lax.rsqrt(jnp.mean(x * x, axis=-1, keepdims=True) + EPS) * gain


def _dot01_exact(tri, g):
    hi = g.astype(_MXU)
    r1 = g - hi.astype(F32)
    mid = r1.astype(_MXU)
    lo = (r1 - mid.astype(F32)).astype(_MXU)
    return _dot(tri, hi) + _dot(tri, mid) + _dot(tri, lo)


def _ada_kernel(c_ref, w_ref, b_ref, o_ref):
    s = _silu(c_ref[...])
    o_ref[...] = _dot(s.astype(_MXU), w_ref[...].astype(_MXU)) + b_ref[...]


def _ada_call(cc, w_ada, b_ada):
    depth, d, n = w_ada.shape
    rows = cc.shape[0]
    tn = _pick(n, (1536, 1024, 512, 128))
    return pl.pallas_call(
        _ada_kernel,
        grid=(depth, n // tn),
        in_specs=[pl.BlockSpec((rows, d), lambda l, j: (0, 0)),
                  pl.BlockSpec((None, d, tn), lambda l, j: (l, 0, j)),
                  pl.BlockSpec((None, 1, tn), lambda l, j: (l, 0, j))],
        out_specs=pl.BlockSpec((None, rows, tn), lambda l, j: (l, 0, j)),
        out_shape=jax.ShapeDtypeStruct((depth, rows, n), F32),
        compiler_params=_cparams(("arbitrary", "arbitrary"), 24),
        name="ada",
    )(cc, w_ada, b_ada.reshape(depth, 1, n))


def _row_specs(parts, tm, d):
    offs = np.cumsum([0] + [p.shape[0] // tm for p in parts])
    specs = []
    for k, p in enumerate(parts):
        lo, n_tiles = int(offs[k]), p.shape[0] // tm

        def imap(i, *rest, lo=lo, n_tiles=n_tiles):
            return (jnp.clip(i - lo, 0, n_tiles - 1), 0)
        specs.append(pl.BlockSpec((tm, d), imap))
    return specs, [int(o) for o in offs[1:-1]]


def _pick_rows(refs, bounds, rows=slice(None)):
    val = refs[0][rows, :]
    for ref, lo in zip(refs[1:], bounds):
        val = jnp.where(pl.program_id(0) >= lo, ref[rows, :], val)
    return val


def _proj_kernel(*refs, bounds):
    n_parts = len(bounds) + 1
    x_refs = refs[:n_parts]
    mod_ref, g_ref, w_ref, o_ref, xn_ref = refs[n_parts:]

    @pl.when(pl.program_id(1) == 0)
    def _():
        y = _rms(_pick_rows(x_refs, bounds), g_ref[...])
        xn_ref[...] = (y * (1.0 + mod_ref[1:2, :]) + mod_ref[0:1, :]).astype(xn_ref.dtype)

    o_ref[...] = _dot(xn_ref[...], w_ref[...]).astype(o_ref.dtype)


def _proj_call(x_parts, mod, gain, w_in_r, nl, nb):
    t = sum(p.shape[0] for p in x_parts)
    d = x_parts[0].shape[1]
    tm = _pick(math.gcd(math.gcd(*[p.shape[0] for p in x_parts], t), nl), (1024, 512, 256))
    x_specs, bounds = _row_specs(x_parts, tm, d)
    return pl.pallas_call(
        functools.partial(_proj_kernel, bounds=bounds),
        grid=(t // tm, N_PROJ // PROJ_TN),
        in_specs=x_specs + [
            pl.BlockSpec((None, N_MOD, d), lambda i, j: (jnp.minimum(i * tm // nl, nb), 0, 0)),
            pl.BlockSpec((1, d), lambda i, j: (0, 0)),
            pl.BlockSpec((d, PROJ_TN), lambda i, j: (0, j))],
        out_specs=pl.BlockSpec((tm, PROJ_TN), lambda i, j: (i, j)),
        out_shape=jax.ShapeDtypeStruct((t, N_PROJ), _MXU),
        scratch_shapes=[pltpu.VMEM((tm, d), _MXU)],
        compiler_params=_cparams(("arbitrary", "arbitrary"), 56),
        name="proj",
    )(*x_parts, mod, gain, w_in_r)


def _la_kernel(*refs, kind, tt, n_ct, n_lt):
    if kind == "gla":
        (q_ref, k_ref, v_ref, gate_ref, lr_ref, wa2_ref, ba_ref, gain_ref,
         out_ref, st_ref, of_ref) = refs
    else:
        (q_ref, k_ref, v_ref, gate_ref, cos_ref, sin_ref, gain_ref,
         out_ref, st_ref, of_ref, dm_ref) = refs
    ph = pl.program_id(1)
    s = pl.program_id(2)
    ck = CHUNK if kind == "gla" else tt
    n_chunks = tt // ck

    @pl.when(s == 0)
    def _():
        st_ref[...] = jnp.zeros_like(st_ref)

    row = lax.broadcasted_iota(jnp.int32, (tt, tt), 0)
    col = lax.broadcasted_iota(jnp.int32, (tt, tt), 1)
    shift = ck.bit_length() - 1
    same_chunk = (row >> shift) == (col >> shift)
    tpos = lax.broadcasted_iota(jnp.int32, (tt, 1), 0).astype(F32)

    if kind == "ret":
        first = jnp.logical_and(pl.program_id(0) == 0, jnp.logical_and(ph == 0, s == 0))

        @pl.when(first)
        def _():
            for h in range(HEADS):
                lg = math.log(1.0 - 2.0 ** (-5.0 - h))
                dm_ref[h] = jnp.where(col <= row, jnp.exp((row - col).astype(F32) * lg), 0.0)
                dm_ref[HEADS + h] = jnp.where(col > row, jnp.exp((col - row).astype(F32) * lg), 0.0)

    def tile_outputs(backward):
        if backward:
            mask = jnp.logical_and(same_chunk, col > row)
            tri = jnp.logical_and(same_chunk, col >= row).astype(_MXU)
        else:
            mask = jnp.logical_and(same_chunk, col <= row)
            tri = jnp.logical_and(same_chunk, col <= row).astype(_MXU)
        order = list(reversed(range(n_chunks))) if backward else list(range(n_chunks))
        if kind == "gla":
            d = 1 if backward else 0
            z = _dot(lr_ref[...], wa2_ref[d]) + ba_ref[d]
            g = _log_sigmoid(z) * (1.0 / GLA_GATE_NORM)
            big_g = _dot01_exact(tri, g)
            lasts = [big_g[c * ck:c * ck + 1, :] if backward else big_g[(c + 1) * ck - 1:(c + 1) * ck, :]
                     for c in range(n_chunks)]
            g_last = jnp.concatenate([jnp.broadcast_to(r, (ck, r.shape[1])) for r in lasts], axis=0)
            e_pos = jnp.exp(big_g)
            e_neg = jnp.exp(-big_g)
            e_end = jnp.exp(g_last - big_g)
            decs = [jnp.exp(r) for r in lasts]
        outs = []
        for h in range(HEADS):
            ks = slice(h * DK, (h + 1) * DK)
            vs = slice(h * DV, (h + 1) * DV)
            qh = q_ref[:, ks].astype(F32)
            kh = k_ref[:, ks].astype(F32)
            vh = v_ref[:, vs]
            if kind == "gla":
                qh = qh * DK ** -0.5
                q_dec = (qh * e_pos[:, ks]).astype(_MXU)
                k_inv = (kh * e_neg[:, ks]).astype(_MXU)
                k_end = (kh * e_end[:, ks]).astype(_MXU)
                sc = jnp.where(mask, _dot_nt(q_dec, k_inv), 0.0)
                dec_h = [dcy[:, ks] for dcy in decs]
            else:
                kh = kh * DK ** -0.5
                cos = cos_ref[...]
                sin = sin_ref[...]
                qh = qh * cos + pltpu.roll(qh, DK // 2, 1) * sin
                kh = kh * cos + pltpu.roll(kh, DK // 2, 1) * sin
                lg = math.log(1.0 - 2.0 ** (-5.0 - h))
                steps = (ck - tpos) if backward else (tpos + 1.0)
                q_dec = (qh * jnp.exp(steps * lg)).astype(_MXU)
                k_end = (kh * jnp.exp((ck - steps) * lg)).astype(_MXU)
                sc = _dot_nt(qh.astype(_MXU), kh.astype(_MXU)) * dm_ref[(HEADS if backward else 0) + h]
                dec_h = [math.exp(ck * lg)] * n_chunks
            o_intra = _dot(sc.astype(_MXU), vh)
            st = st_ref[h]
            parts = [None] * n_chunks
            for c in order:
                rs = slice(c * ck, (c + 1) * ck)
                parts[c] = o_intra[rs] + _dot_nt(q_dec[rs], st.astype(_MXU))
                st = st * dec_h[c] + _dot_tn(vh[rs], k_end[rs])
            st_ref[h] = st
            outs.append(parts[0] if n_chunks == 1 else jnp.concatenate(parts, axis=0))
        return jnp.concatenate(outs, axis=1)

    @pl.when(ph == 0)
    def _():
        base = pl.multiple_of(s * tt, tt)
        of_ref[pl.ds(base, tt), :] = tile_outputs(False)

    @pl.when(ph == 1)
    def _():
        loc = jnp.where(s < n_ct, n_ct - 1 - s, n_ct + n_lt - 1 - (s - n_ct))
        base = pl.multiple_of(loc * tt, tt)
        o = of_ref[pl.ds(base, tt), :] + tile_outputs(True)
        gate = gate_ref[...].astype(F32)
        parts = []
        for h in range(HEADS):
            oh = o[:, h * DV:(h + 1) * DV]
            if kind == "ret":
                oh = oh - jnp.mean(oh, axis=-1, keepdims=True)
            parts.append(oh * lax.rsqrt(jnp.mean(oh * oh, axis=-1, keepdims=True) + EPS))
        normed = jnp.concatenate(parts, axis=1) * gain_ref[...]
        out_ref[...] = (normed * _silu(gate)).astype(out_ref.dtype)


def _la_call(kind, p, extra, gain, nb, nl, nc):
    t = p.shape[0]
    tt = _pick(math.gcd(nl, nc), (256, 128, 64))
    n_ct, n_lt = nc // tt, nl // tt
    lat_tiles = nb * n_lt
    cq, ck, cv, cg = ((C_GLA_Q, C_GLA_K, C_GLA_V, C_GLA_G) if kind == "gla"
                      else (C_RET_Q, C_RET_K, C_RET_V, C_RET_G))
    hd = HEADS * DK
    hv = HEADS * DV

    def loc_of(ph, s):
        back = jnp.where(s < n_ct, n_ct - 1 - s, n_ct + n_lt - 1 - (s - n_ct))
        return jnp.where(ph == 0, s, back)

    def row_blk(b, loc):
        return jnp.where(loc < n_ct, lat_tiles + b * n_ct + loc, b * n_lt + (loc - n_ct))

    def in_map(cblk):
        return lambda b, ph, s: (row_blk(b, loc_of(ph, s)), cblk)

    def second_pass_map(cblk):
        return lambda b, ph, s: (row_blk(b, loc_of(1, jnp.where(ph == 0, 0, s))), cblk)

    in_specs = [pl.BlockSpec((tt, hd), in_map(cq // hd)),
                pl.BlockSpec((tt, hd), in_map(ck // hd)),
                pl.BlockSpec((tt, hv), in_map(cv // hv)),
                pl.BlockSpec((tt, hv), second_pass_map(cg // hv))]
    args = [p, p, p, p]
    if kind == "gla":
        wa2p, ba = extra
        in_specs += [pl.BlockSpec((tt, LANES), in_map(C_LR // LANES)),
                     pl.BlockSpec((2, LANES, hd), lambda b, ph, s: (0, 0, 0)),
                     pl.BlockSpec((2, 1, hd), lambda b, ph, s: (0, 0, 0))]
        args += [p, wa2p, ba]
    else:
        cos_t, sin_t = extra

        def rope_map(b, ph, s):
            loc = loc_of(ph, s)
            return (jnp.where(loc < n_ct, n_lt + loc, loc - n_ct), 0)
        in_specs += [pl.BlockSpec((tt, DK), rope_map), pl.BlockSpec((tt, DK), rope_map)]
        args += [cos_t, sin_t]
    in_specs.append(pl.BlockSpec((1, hv), lambda b, ph, s: (0, 0)))
    args.append(gain)
    return pl.pallas_call(
        functools.partial(_la_kernel, kind=kind, tt=tt, n_ct=n_ct, n_lt=n_lt),
        grid=(nb, 2, n_ct + n_lt),
        in_specs=in_specs,
        out_specs=pl.BlockSpec((tt, hv), second_pass_map(0)),
        out_shape=jax.ShapeDtypeStruct((t, hv), _MXU),
        scratch_shapes=[pltpu.VMEM((HEADS, DV, DK), F32),
                        pltpu.VMEM((nc + nl, hv), F32)]
        + ([pltpu.VMEM((2 * HEADS, tt, tt), F32)] if kind == "ret" else []),
        compiler_params=_cparams(("arbitrary", "arbitrary", "arbitrary"), 48),
        name=kind,
    )(*args)


def _scan_group(a, b, h, reverse):
    row = lax.broadcasted_iota(jnp.int32, a.shape, 0)
    for sft in (1, 2, 4):
        if reverse:
            a_sh = pltpu.roll(a, SUBLANES - sft, 0)
            b_sh = pltpu.roll(b, SUBLANES - sft, 0)
            m = row < SUBLANES - sft
        else:
            a_sh = pltpu.roll(a, sft, 0)
            b_sh = pltpu.roll(b, sft, 0)
            m = row >= sft
        b = jnp.where(m, a * b_sh + b, b)
        a = jnp.where(m, a * a_sh, a)
    hh = a * h + b
    return hh, (hh[0:1, :] if reverse else hh[SUBLANES - 1:SUBLANES, :])


def _lru_kernel(x_ref, gel_ref, cw_ref, cb_ref, wa_ref, ba_ref, wi_ref, bi_ref, lam_ref, h0_ref,
                out_ref, hn_ref, xpad_ref, xc_ref, hf_ref, *, seg, tile):
    n_tiles = seg // tile
    groups = tile // SUBLANES
    cw = cw_ref[...]
    cb = cb_ref[...]
    zeros8 = jnp.zeros((SUBLANES, xpad_ref.shape[1]), F32)
    xpad_ref[0:SUBLANES, :] = zeros8
    xpad_ref[SUBLANES + seg:2 * SUBLANES + seg, :] = zeros8

    def copy_in(i, carry):
        t0 = pl.multiple_of(i * tile, tile)
        xpad_ref[pl.ds(t0 + SUBLANES, tile), :] = x_ref[pl.ds(t0, tile), :].astype(F32)
        return carry
    lax.fori_loop(0, n_tiles, copy_in, 0)

    neg_c = [(-0.5 * LRU_C) * _log_sigmoid(lam_ref[d:d + 1, :]) for d in range(2)]

    def gates(xc, d):
        xb = xc.astype(_MXU)
        tr = jnp.tanh(_dot(xb, wa_ref[d]) + ba_ref[d:d + 1, :])
        ig = 0.5 * jnp.tanh(_dot(xb, wi_ref[d]) + bi_ref[d:d + 1, :]) + 0.5
        nla = neg_c[d] * tr + neg_c[d]
        a = jnp.exp(-nla)
        b = jnp.sqrt(jnp.tanh(nla) * (a * a + 1.0)) * (ig * xc)
        return a, b

    def fwd_tile(i, h):
        t0 = pl.multiple_of(i * tile, tile)
        win = xpad_ref[pl.ds(t0, tile + 2 * SUBLANES), :]
        xc = (cw[0:1, :] * win[6:6 + tile] + cw[1:2, :] * win[7:7 + tile]
              + cw[2:3, :] * win[8:8 + tile] + cw[3:4, :] * win[9:9 + tile] + cb)
        xc_ref[pl.ds(t0, tile), :] = xc
        a, b = gates(xc, 0)
        for g in range(groups):
            rs = slice(g * SUBLANES, (g + 1) * SUBLANES)
            hh, h = _scan_group(a[rs], b[rs], h, False)
            hf_ref[pl.ds(t0 + g * SUBLANES, SUBLANES), :] = hh
        return h
    h_f = lax.fori_loop(0, n_tiles, fwd_tile, h0_ref[0:1, :])

    def bwd_tile(j, h):
        i = n_tiles - 1 - j
        t0 = pl.multiple_of(i * tile, tile)
        xc = xc_ref[pl.ds(t0, tile), :]
        a, b = gates(xc, 1)
        pair = 2 * SUBLANES
        for q in reversed(range(tile // pair)):
            hi = slice(q * pair + SUBLANES, (q + 1) * pair)
            lo = slice(q * pair, q * pair + SUBLANES)
            hh_hi, h = _scan_group(a[hi], b[hi], h, True)
            hh_lo, h = _scan_group(a[lo], b[lo], h, True)
            rows = pl.ds(pl.multiple_of(t0 + q * pair, pair), pair)
            hsum = hf_ref[rows, :] + jnp.concatenate([hh_lo, hh_hi], axis=0)
            gel = gel_ref[rows, :].astype(F32)
            out_ref[rows, :] = (hsum * _gelu_tanh(gel)).astype(out_ref.dtype)
        return h
    h_b = lax.fori_loop(0, n_tiles, bwd_tile, h0_ref[1:2, :])
    hn_ref[0:1, :] = h_f
    hn_ref[1:2, :] = h_b


def _lru_call(p, prev_out, h0, lw, nb, seg, row_blk0):
    cw, cb, wa, ba, wi, bi, lam = lw
    t = p.shape[0]
    width = LRU_BLOCKS * DV
    cbw = width // LRU_BLOCKS
    tile = _pick(seg, (1024, 512, 256, 128, 64))
    xblk = C_LRU_X // cbw
    gblk = C_LRU_G // cbw
    in_specs = [pl.BlockSpec((seg, cbw), lambda b, c: (row_blk0 + b, xblk + c)),
                pl.BlockSpec((seg, cbw), lambda b, c: (row_blk0 + b, gblk + c)),
                pl.BlockSpec((4, cbw), lambda b, c: (0, c)),
                pl.BlockSpec((1, cbw), lambda b, c: (0, c)),
                pl.BlockSpec((2, None, cbw, cbw), lambda b, c: (0, c, 0, 0)),
                pl.BlockSpec((2, cbw), lambda b, c: (0, c)),
                pl.BlockSpec((2, None, cbw, cbw), lambda b, c: (0, c, 0, 0)),
                pl.BlockSpec((2, cbw), lambda b, c: (0, c)),
                pl.BlockSpec((2, cbw), lambda b, c: (0, c)),
                pl.BlockSpec((None, 2, cbw), lambda b, c: (b, 0, c))]
    in_specs.append(pl.BlockSpec(memory_space=pl.ANY))
    args = [p, p, cw, cb, wa, ba, wi, bi, lam, h0, prev_out]

    def body(*refs):
        _lru_kernel(*(refs[:10] + refs[11:]), seg=seg, tile=tile)

    return pl.pallas_call(
        body,
        grid=(nb, LRU_BLOCKS),
        in_specs=in_specs,
        out_specs=[pl.BlockSpec((seg, cbw), lambda b, c: (row_blk0 + b, c)),
                   pl.BlockSpec((None, 2, cbw), lambda b, c: (b, 0, c))],
        out_shape=[jax.ShapeDtypeStruct((t, width), _MXU),
                   jax.ShapeDtypeStruct((nb, 2, width), F32)],
        scratch_shapes=[pltpu.VMEM((seg + 2 * SUBLANES, cbw), F32),
                        pltpu.VMEM((seg, cbw), F32),
                        pltpu.VMEM((seg, cbw), F32)],
        input_output_aliases={10: 0},
        compiler_params=_cparams(("arbitrary", "arbitrary"), 48),
        name="lru",
    )(*args)


def _finish_kernel(*refs, bounds):
    n_parts = len(bounds) + 1
    xs_refs = refs[:n_parts]
    (gla_ref, lru_ref, ret_ref, m0_ref, m1_ref, m2_ref, wb_ref, wo_ref,
     mod_ref, g2_ref, wr_ref, br_ref, tril_ref,
     xo_ref, xn_ref, idx_ref, wgt_ref, rank_ref, cnt_ref, carry_ref) = refs[n_parts:]

    @pl.when(pl.program_id(0) == 0)
    def _():
        carry_ref[...] = jnp.zeros_like(carry_ref)

    hm = tril_ref.shape[0]
    total = carry_ref[0:1, :]
    for sub in range(xo_ref.shape[0] // hm):
        rs = slice(sub * hm, (sub + 1) * hm)
        merged = (_sigmoid(m0_ref[rs, :].astype(F32)) * _dot(gla_ref[rs, :], wb_ref[0])
                  + _sigmoid(m1_ref[rs, :].astype(F32)) * _dot(lru_ref[rs, :], wb_ref[1])
                  + _sigmoid(m2_ref[rs, :].astype(F32)) * _dot(ret_ref[rs, :], wb_ref[2]))
        x = _pick_rows(xs_refs, bounds, rs) + mod_ref[2:3, :] * _dot(merged.astype(_MXU), wo_ref[...])
        xo_ref[rs, :] = x
        xn = _rms(x, g2_ref[...]) * (1.0 + mod_ref[4:5, :]) + mod_ref[3:4, :]
        xn_ref[rs, :] = xn

        logits = _dot(xn.astype(_MXU), wr_ref[...]) + br_ref[...]
        lane = lax.broadcasted_iota(jnp.int32, logits.shape, 1)
        ids, vals = [], []
        for _ in range(TOP_K):
            m = jnp.max(logits, axis=1, keepdims=True)
            sel = jnp.min(jnp.where(logits == m, lane.astype(F32), float(LANES)), axis=1,
                          keepdims=True).astype(jnp.int32)
            ids.append(sel)
            vals.append(m)
            logits = jnp.where(lane == sel, -jnp.inf, logits)
        ex = [jnp.exp(v - vals[0]) for v in vals]
        denom = ex[0] + ex[1] + ex[2] + ex[3]
        onehot = jnp.zeros(logits.shape, F32)
        for sel in ids:
            onehot = onehot + (lane == sel).astype(F32)
        before = _dot(tril_ref[...], onehot.astype(_MXU)) + total
        idx_o = jnp.zeros(logits.shape, jnp.int32)
        rank_o = jnp.zeros(logits.shape, jnp.int32)
        wgt_o = jnp.zeros(logits.shape, F32)
        for k in range(TOP_K):
            rk = jnp.sum(jnp.where(lane == ids[k], before, 0.0), axis=1, keepdims=True)
            idx_o = jnp.where(lane == k, ids[k], idx_o)
            rank_o = jnp.where(lane == k, rk.astype(jnp.int32), rank_o)
            wgt_o = jnp.where(lane == k, ex[k] / denom, wgt_o)
        idx_ref[rs, :] = idx_o
        rank_ref[rs, :] = rank_o
        wgt_ref[rs, :] = wgt_o
        total = total + jnp.sum(onehot, axis=0, keepdims=True)
    carry_ref[...] = jnp.broadcast_to(total, carry_ref.shape)
    cnt_ref[...] = jnp.broadcast_to(total, cnt_ref.shape)


def _finish_call(gla, lru, ret, p, wb, wo, xs_parts, mod, g2, wr, br, n_rows, nl, nb):
    d = xs_parts[0].shape[1]
    tm = _pick(math.gcd(math.gcd(*[q.shape[0] for q in xs_parts], n_rows), nl), (512, 256))
    xs_specs, bounds = _row_specs(xs_parts, tm, d)
    hm = tm
    tril = jnp.asarray(np.tril(np.ones((hm, hm), np.float32), -1), _MXU)
    row = lambda i: (i, 0)
    const2 = lambda i: (0, 0)
    mblk = C_MERGE // d
    in_specs = xs_specs + [
                pl.BlockSpec((tm, d), row), pl.BlockSpec((tm, d), row), pl.BlockSpec((tm, d), row),
                pl.BlockSpec((tm, d), lambda i: (i, mblk)),
                pl.BlockSpec((tm, d), lambda i: (i, mblk + 1)),
                pl.BlockSpec((tm, d), lambda i: (i, mblk + 2)),
                pl.BlockSpec((3, d, d), lambda i: (0, 0, 0)),
                pl.BlockSpec((d, d), const2),
                pl.BlockSpec((None, N_MOD, d), lambda i: (jnp.minimum(i * tm // nl, nb), 0, 0)),
                pl.BlockSpec((1, d), const2),
                pl.BlockSpec((d, LANES), const2),
                pl.BlockSpec((1, LANES), const2),
                pl.BlockSpec((hm, hm), const2)]
    out_specs = [pl.BlockSpec((tm, d), row), pl.BlockSpec((tm, d), row),
                 pl.BlockSpec((tm, LANES), row), pl.BlockSpec((tm, LANES), row),
                 pl.BlockSpec((tm, LANES), row), pl.BlockSpec((SUBLANES, LANES), const2)]
    out_shape = [jax.ShapeDtypeStruct((n_rows, d), F32), jax.ShapeDtypeStruct((n_rows, d), F32),
                 jax.ShapeDtypeStruct((n_rows, LANES), jnp.int32),
                 jax.ShapeDtypeStruct((n_rows, LANES), F32),
                 jax.ShapeDtypeStruct((n_rows, LANES), jnp.int32),
                 jax.ShapeDtypeStruct((SUBLANES, LANES), F32)]
    return pl.pallas_call(
        functools.partial(_finish_kernel, bounds=bounds),
        grid=(n_rows // tm,),
        in_specs=in_specs,
        out_specs=out_specs,
        out_shape=out_shape,
        scratch_shapes=[pltpu.VMEM((SUBLANES, LANES), F32)],
        compiler_params=_cparams(("arbitrary",), 56),
        name="finish",
    )(*xs_parts, gla, lru, ret, p, p, p, wb, wo, mod, g2, wr, br, tril)


def _slot_kernel(idx_ref, rank_ref, start_ref, slot_ref):
    idx = idx_ref[...].astype(F32)
    lane = lax.broadcasted_iota(jnp.int32, idx.shape, 1)
    start = start_ref[...]
    out = jnp.zeros(idx.shape, F32)
    for k in range(TOP_K):
        sel = jnp.sum(jnp.where(lane == k, idx, 0.0), axis=1, keepdims=True).astype(jnp.int32)
        st = jnp.sum(jnp.where(lane == sel, start, 0.0), axis=1, keepdims=True)
        out = jnp.where(lane == k, st, out)
    slot_ref[...] = out.astype(jnp.int32) + rank_ref[...]


def _slot_call(idx, rank, pad_start):
    n = idx.shape[0]
    tm = _pick(n, (2048, 1024, 512, 256))
    row = lambda i: (i, 0)
    return pl.pallas_call(
        _slot_kernel,
        grid=(n // tm,),
        in_specs=[pl.BlockSpec((tm, LANES), row), pl.BlockSpec((tm, LANES), row),
                  pl.BlockSpec((1, LANES), lambda i: (0, 0))],
        out_specs=pl.BlockSpec((tm, LANES), row),
        out_shape=jax.ShapeDtypeStruct((n, LANES), jnp.int32),
        compiler_params=_cparams(("arbitrary",), 24),
        name="slots",
    )(idx, rank, pad_start)


def _dispatch_kernel(fill_ref, slot_ref, x_ref, dst_ref, zero_ref, sem, fill_sem, *, tm, bm, n_blocks):
    @pl.when(pl.program_id(0) == 0)
    def _():
        zero_ref[...] = jnp.zeros_like(zero_ref)

        def clear(blk):
            return pltpu.make_async_copy(zero_ref, dst_ref.at[pl.ds(pl.multiple_of(blk * bm, bm), bm)],
                                         fill_sem)

        def tail_start(j, carry):
            clear(j).start()
            return carry

        def tail_wait(j, carry):
            clear(j).wait()
            return carry

        for e in range(N_EXPERTS):
            @pl.when(fill_ref[0, e] >= 0)
            def _():
                clear(fill_ref[0, e]).start()
        lax.fori_loop(fill_ref[0, N_EXPERTS], n_blocks, tail_start, 0)
        for e in range(N_EXPERTS):
            @pl.when(fill_ref[0, e] >= 0)
            def _():
                clear(fill_ref[0, e]).wait()
        lax.fori_loop(fill_ref[0, N_EXPERTS], n_blocks, tail_wait, 0)

    def issue(g, carry):
        t0 = pl.multiple_of(g * SUBLANES, SUBLANES)
        for u in range(SUBLANES):
            for k in range(TOP_K):
                pltpu.make_async_copy(
                    x_ref.at[pl.ds(t0 + u, 1)],
                    dst_ref.at[pl.ds(slot_ref[0, (t0 + u) * TOP_K + k], 1)],
                    sem).start(priority=k % 2)
        return carry
    lax.fori_loop(0, tm // SUBLANES, issue, 0)
    for k in range(TOP_K):
        pltpu.make_async_copy(x_ref, dst_ref.at[pl.ds(0, tm)], sem).wait()


def _dispatch_call(fill, slots3, xn, n_blocks, bm):
    n, d = xn.shape
    tm = slots3.shape[2] // TOP_K
    return pl.pallas_call(
        functools.partial(_dispatch_kernel, tm=tm, bm=bm, n_blocks=n_blocks),
        grid=(n // tm,),
        in_specs=[pl.BlockSpec((1, LANES), lambda i: (0, 0), memory_space=pltpu.SMEM),
                  pl.BlockSpec((None, 1, tm * TOP_K), lambda i: (i, 0, 0), memory_space=pltpu.SMEM),
                  pl.BlockSpec((tm, d), lambda i: (i, 0))],
        out_specs=pl.BlockSpec(memory_space=pl.ANY),
        out_shape=jax.ShapeDtypeStruct((n_blocks * bm, d), F32),
        scratch_shapes=[pltpu.VMEM((bm, d), F32), pltpu.SemaphoreType.DMA(()),
                        pltpu.SemaphoreType.DMA(())],
        compiler_params=_cparams(("arbitrary",), 24),
        name="dispatch",
    )(fill, slots3, xn)


def _experts_kernel(be_ref, nu_ref, x_ref, wgu_ref, bgu_ref, wd_ref, bd_ref, perm_ref, y_ref,
                    wgu_s, wd_s):
    j = pl.program_id(0)
    f = wd_ref.shape[0]
    grp = perm_ref.shape[0]
    half = grp // 2
    active = j < nu_ref[0]
    changed = jnp.logical_or(j == 0, be_ref[j] != be_ref[jnp.maximum(j - 1, 0)])

    @pl.when(jnp.logical_and(active, changed))
    def _():
        for g in range(2 * f // grp):
            cols = slice(g * grp, (g + 1) * grp)
            wgu_s[:, cols] = _dot(wgu_ref[:, cols].astype(_MXU), perm_ref[...]).astype(_MXU)
        wd_s[...] = wd_ref[...].astype(_MXU)

    @pl.when(active)
    def _():
        gu = _dot(x_ref[...].astype(_MXU), wgu_s[...]) + bgu_ref[...]
        acts = []
        for g in range(2 * f // grp):
            gate = jnp.minimum(gu[:, g * grp:g * grp + half], SWIGLU_LIMIT)
            up = jnp.clip(gu[:, g * grp + half:(g + 1) * grp], -SWIGLU_LIMIT, SWIGLU_LIMIT)
            acts.append((gate * _sigmoid(SWIGLU_ALPHA * gate) * (up + 1.0)).astype(_MXU))
        y_ref[...] = _dot(jnp.concatenate(acts, axis=1), wd_s[...]) + bd_ref[...]

    @pl.when(jnp.logical_not(active))
    def _():
        y_ref[...] = jnp.zeros_like(y_ref)


def _experts_call(li, block_e, n_used, x_sorted, w_gu, bgu_r, w_down, bd, bm):
    n_slots, d = x_sorted.shape
    f = w_down.shape[2]
    grp = 2 * LANES
    src = np.concatenate([np.arange(0, grp, 2), np.arange(1, grp, 2)])
    perm_np = np.zeros((grp, grp), np.float32)
    perm_np[src, np.arange(grp)] = 1.0
    perm = jnp.asarray(perm_np, _MXU)

    def xmap(j, be, nu):
        return (jnp.minimum(j, nu[0] - 1), 0)

    def wmap(j, be, nu):
        return (li, be[j], 0, 0)

    grid_spec = pltpu.PrefetchScalarGridSpec(
        num_scalar_prefetch=2,
        grid=(n_slots // bm,),
        in_specs=[pl.BlockSpec((bm, d), xmap),
                  pl.BlockSpec((None, None, d, 2 * f), wmap),
                  pl.BlockSpec((None, None, 1, 2 * f), wmap),
                  pl.BlockSpec((None, None, f, d), wmap),
                  pl.BlockSpec((None, None, 1, d), wmap),
                  pl.BlockSpec((grp, grp), lambda j, be, nu: (0, 0))],
        out_specs=pl.BlockSpec((bm, d), lambda j, be, nu: (j, 0)),
        scratch_shapes=[pltpu.VMEM((d, 2 * f), _MXU), pltpu.VMEM((f, d), _MXU)])
    return pl.pallas_call(
        _experts_kernel,
        grid_spec=grid_spec,
        out_shape=jax.ShapeDtypeStruct((n_slots, d), F32),
        compiler_params=_cparams(("arbitrary",), 56),
        name="experts",
    )(block_e, n_used, x_sorted, w_gu, bgu_r, w_down, bd, perm)


def _combine_kernel(slot_ref, wgt_ref, xs_ref, mod_ref, fin_ref, y_ref, out_ref, buf_ref, sem,
                    *, tm, final):
    def issue(g, carry):
        t0 = pl.multiple_of(g * SUBLANES, SUBLANES)
        for u in range(SUBLANES):
            for k in range(TOP_K):
                pltpu.make_async_copy(
                    y_ref.at[pl.ds(slot_ref[0, (t0 + u) * TOP_K + k], 1)],
                    buf_ref.at[k, pl.ds(t0 + u, 1)], sem).start(priority=k % 2)
        return carry
    lax.fori_loop(0, tm // SUBLANES, issue, 0)
    for k in range(TOP_K):
        pltpu.make_async_copy(y_ref.at[pl.ds(0, tm)], buf_ref.at[k], sem).wait()

    wgt = wgt_ref[...]
    acc = wgt[:, 0:1] * buf_ref[0]
    for k in range(1, TOP_K):
        acc = acc + wgt[:, k:k + 1] * buf_ref[k]
    x = xs_ref[...] + mod_ref[5:6, :] * acc
    out_ref[...] = _rms(x, fin_ref[...]) if final else x


def _combine_call(slots3, wgt, xs, mod, fin, y_sorted, nl, nb, final):
    n, d = xs.shape
    tm = slots3.shape[2] // TOP_K
    return pl.pallas_call(
        functools.partial(_combine_kernel, tm=tm, final=final),
        grid=(n // tm,),
        in_specs=[pl.BlockSpec((None, 1, tm * TOP_K), lambda i: (i, 0, 0), memory_space=pltpu.SMEM),
                  pl.BlockSpec((tm, LANES), lambda i: (i, 0)),
                  pl.BlockSpec((tm, d), lambda i: (i, 0)),
                  pl.BlockSpec((None, N_MOD, d), lambda i: (jnp.minimum(i * tm // nl, nb), 0, 0)),
                  pl.BlockSpec((1, d), lambda i: (0, 0)),
                  pl.BlockSpec(memory_space=pl.ANY)],
        out_specs=pl.BlockSpec((tm, d), lambda i: (i, 0)),
        out_shape=jax.ShapeDtypeStruct((n, d), F32),
        scratch_shapes=[pltpu.VMEM((TOP_K, tm, d), F32), pltpu.SemaphoreType.DMA(())],
        compiler_params=_cparams(("arbitrary",), 48),
        name="combine",
    )(slots3, wgt, xs, mod, fin, y_sorted)


MOE_BM = 512


def _moe(li, xn, idx, wgt, rank, counts, xs_mid, mod, fin, w_gu, bgu_r, w_down, bd, nl, nb, final):
    n = xn.shape[0]
    bm = MOE_BM
    tm = _pick(math.gcd(n, nl), (1024, 512, 256, 128))
    n_blocks = -(-(n * TOP_K) // bm) + N_EXPERTS
    cnt = counts[0, :N_EXPERTS].astype(jnp.int32)
    padded = (cnt + bm - 1) // bm * bm
    pad_end = jnp.cumsum(padded)
    pad_start = pad_end - padded
    n_used = (pad_end[-1] // bm).astype(jnp.int32)
    blk = jnp.arange(n_blocks, dtype=jnp.int32)
    first_row = jnp.minimum(blk, n_used - 1) * bm
    block_e = jnp.minimum(jnp.sum(pad_end[None, :] <= first_row[:, None], axis=1),
                          N_EXPERTS - 1).astype(jnp.int32)
    start_row = jnp.zeros((1, LANES), F32).at[0, :N_EXPERTS].set(pad_start.astype(F32))
    slots = _slot_call(idx, rank, start_row)
    slots3 = slots[:, :TOP_K].reshape(n // tm, 1, tm * TOP_K)
    last_blk = jnp.where(cnt > 0, pad_end // bm - 1, -1).astype(jnp.int32)
    fill = jnp.zeros((1, LANES), jnp.int32).at[0, :N_EXPERTS].set(last_blk).at[0, N_EXPERTS].set(n_used)
    x_sorted = _dispatch_call(fill, slots3, xn, n_blocks, bm)
    y_sorted = _experts_call(li, block_e, n_used.reshape(1), x_sorted, w_gu, bgu_r, w_down, bd, bm)
    return _combine_call(slots3, wgt, xs_mid, mod, fin, y_sorted, nl, nb, final)


def _rope_tables(nl, nc):
    rows = nl // GRID_W
    r = np.broadcast_to(np.arange(rows)[:, None], (rows, GRID_W)).reshape(-1).astype(np.float32)
    c = np.broadcast_to(np.arange(GRID_W)[None, :], (rows, GRID_W)).reshape(-1).astype(np.float32)
    n_freq = DK // 4
    inv = (ROPE_BASE ** (-jnp.arange(n_freq, dtype=F32) / n_freq))
    ang = jnp.concatenate([jnp.asarray(r)[:, None] * inv, jnp.asarray(c)[:, None] * inv], axis=-1)
    cos, sin = jnp.cos(ang), jnp.sin(ang)
    cos2 = jnp.concatenate([cos, cos], axis=-1)
    sin2 = jnp.concatenate([-sin, sin], axis=-1)
    cos_t = jnp.concatenate([cos2, jnp.ones((nc, DK), F32)], axis=0)
    sin_t = jnp.concatenate([sin2, jnp.zeros((nc, DK), F32)], axis=0)
    return cos_t, sin_t


def kernel(x, c, ctx, c_ctx, w_ada, b_ada, norm1, norm2, w_in, gla_wa2, gla_ba, gla_norm, lru_conv_w, lru_conv_b, lru_wa, lru_ba, lru_wi, lru_bi, lru_lam, ret_norm, w_branch, w_out, w_router, b_router, w_gu, b_gu, w_down, b_down, final_norm):
    nb, nl, d = x.shape
    nc = ctx.shape[1]
    depth = w_ada.shape[0]
    n_lat = nb * nl
    f = w_down.shape[2]

    n_tok = n_lat + nb * nc
    xs = (x.reshape(n_lat, d), ctx.reshape(nb * nc, d))
    mod_rows = -(-(nb + 1) // SUBLANES) * SUBLANES
    cc = jnp.zeros((mod_rows, d), F32).at[:nb].set(c).at[nb].set(c_ctx)
    mod_all = _ada_call(cc, w_ada, b_ada).reshape(depth, mod_rows, N_MOD, d)
    cos_t, sin_t = _rope_tables(nl, nc)
    bgu_r = b_gu.reshape(depth, N_EXPERTS, 2 * f // (2 * LANES), LANES, 2)
    bgu_r = jnp.swapaxes(bgu_r, -1, -2).reshape(depth, N_EXPERTS, 1, 2 * f)
    bd_r = b_down.reshape(depth, N_EXPERTS, 1, d)
    lru_buf = jnp.zeros((n_tok, lru_conv_w.shape[2]), _MXU)

    out = None
    for li in range(depth):
        last = li == depth - 1
        mod = mod_all[li]
        w = w_in[li]
        w_in_r = jnp.concatenate(
            [w[:, :3072], w[:, 3104:], w[:, 3072:3104],
             jnp.zeros((d, N_PROJ - w.shape[1]), w.dtype)], axis=1).astype(_MXU)
        p = _proj_call(xs, mod, norm1[li].reshape(1, d), w_in_r, nl, nb)

        wa2p = jnp.zeros((2, LANES, HEADS * DK), F32)
        wa2p = wa2p.at[0, :GLA_RANK].set(gla_wa2[li, 0]).at[1, GLA_RANK:2 * GLA_RANK].set(gla_wa2[li, 1])
        gla = _la_call("gla", p, (wa2p.astype(_MXU), gla_ba[li].reshape(2, 1, HEADS * DK)),
                       gla_norm[li].reshape(1, HEADS * DV), nb, nl, nc)
        ret = _la_call("ret", p, (cos_t, sin_t), ret_norm[li].reshape(1, HEADS * DV), nb, nl, nc)

        lw = (lru_conv_w[li], lru_conv_b[li].reshape(1, -1), (0.5 * lru_wa[li]).astype(_MXU),
              0.5 * lru_ba[li], (0.5 * lru_wi[li]).astype(_MXU), 0.5 * lru_bi[li], lru_lam[li])
        width = lru_conv_w.shape[2]
        lru_c, h_ctx = _lru_call(p, lru_buf, jnp.zeros((nb, 2, width), F32), lw, nb, nc, n_lat // nc)
        lru, _ = _lru_call(p, lru_c, h_ctx, lw, nb, nl, 0)
        lru_buf = lru

        n_rows = n_lat if last else n_tok
        wr = jnp.zeros((d, LANES), F32).at[:, :N_EXPERTS].set(w_router[li]).astype(_MXU)
        br = jnp.full((1, LANES), NEG_BIG, F32).at[0, :N_EXPERTS].set(b_router[li])
        xs_mid, xn2, idx, wgt, rank, counts = _finish_call(
            gla, lru, ret, p, w_branch[li].astype(_MXU), w_out[li].astype(_MXU), xs, mod,
            norm2[li].reshape(1, d), wr, br, n_rows, nl, nb)

        out = _moe(li, xn2, idx, wgt, rank, counts, xs_mid, mod, final_norm.reshape(1, d),
                   w_gu, bgu_r, w_down, bd_r, nl, nb, last)
        xs = (out,)
    return out.reshape(nb, nl, d)
```

```python
import functools
import math

import numpy as np
import jax
import jax.numpy as jnp
from jax import lax
from jax.experimental import pallas as pl
from jax.experimental.pallas import tpu as pltpu

F32 = jnp.float32
_MXU = jnp.bfloat16

EPS = 1e-6
N_MOD = 6
GRID_W = 64
CHUNK = 64
HEADS = 4
DK = 128
DV = 256
GLA_RANK = 16
GLA_GATE_NORM = 16.0
LRU_BLOCKS = 4
LRU_C = 8.0
ROPE_BASE = 10000.0
N_EXPERTS = 32
TOP_K = 4
SWIGLU_LIMIT = 7.0
SWIGLU_ALPHA = 1.702
LANES = 128
SUBLANES = 8
NEG_BIG = -1e30

C_GLA_Q, C_GLA_K, C_GLA_V, C_GLA_G = 0, 512, 1024, 2048
C_LRU_X, C_LRU_G = 3072, 4096
C_RET_Q, C_RET_K, C_RET_V, C_RET_G = 5120, 5632, 6144, 7168
C_MERGE = 8192
C_LR = 11264
N_PROJ = 11520
PROJ_TN = 3840


def _pick(n, prefs):
    for p in prefs:
        if n % p == 0:
            return p
    raise ValueError(f"no tile for {n} in {prefs}")


def _cparams(sem, vmem_mb):
    return pltpu.CompilerParams(dimension_semantics=sem, vmem_limit_bytes=vmem_mb * 1024 * 1024)


def _dot(a, b):
    return jnp.dot(a, b, preferred_element_type=F32)


def _dot_nt(a, b):
    return lax.dot_general(a, b, (((1,), (1,)), ((), ())), preferred_element_type=F32)


def _dot_tn(a, b):
    return lax.dot_general(a, b, (((0,), (0,)), ((), ())), preferred_element_type=F32)


def _sigmoid(x):
    return 0.5 * jnp.tanh(0.5 * x) + 0.5


def _log_sigmoid(x):
    return jnp.minimum(x, 0.0) - jnp.log(1.0 + jnp.exp(-jnp.abs(x)))


def _silu(x):
    return x * _sigmoid(x)


def _gelu_tanh(x):
    return 0.5 * x * (1.0 + jnp.tanh(math.sqrt(2.0 / math.pi) * (x + 0.044715 * (x * x * x))))


def _rms(x, gain):
    return x * lax.rsqrt(jnp.mean(x * x, axis=-1, keepdims=True) + EPS) * gain


def _dot01_exact(tri, g):
    hi = g.astype(_MXU)
    r1 = g - hi.astype(F32)
    mid = r1.astype(_MXU)
    lo = (r1 - mid.astype(F32)).astype(_MXU)
    return _dot(tri, hi) + _dot(tri, mid) + _dot(tri, lo)


def _ada_kernel(c_ref, w_ref, b_ref, o_ref):
    s = _silu(c_ref[...])
    o_ref[...] = _dot(s.astype(_MXU), w_ref[...].astype(_MXU)) + b_ref[...]


def _ada_call(cc, w_ada, b_ada):
    depth, d, n = w_ada.shape
    rows = cc.shape[0]
    tn = _pick(n, (1536, 1024, 512, 128))
    return pl.pallas_call(
        _ada_kernel,
        grid=(depth, n // tn),
        in_specs=[pl.BlockSpec((rows, d), lambda l, j: (0, 0)),
                  pl.BlockSpec((None, d, tn), lambda l, j: (l, 0, j)),
                  pl.BlockSpec((None, 1, tn), lambda l, j: (l, 0, j))],
        out_specs=pl.BlockSpec((None, rows, tn), lambda l, j: (l, 0, j)),
        out_shape=jax.ShapeDtypeStruct((depth, rows, n), F32),
        compiler_params=_cparams(("arbitrary", "arbitrary"), 24),
        name="ada",
    )(cc, w_ada, b_ada.reshape(depth, 1, n))


def _row_specs(parts, tm, d):
    offs = np.cumsum([0] + [p.shape[0] // tm for p in parts])
    specs = []
    for k, p in enumerate(parts):
        lo, n_tiles = int(offs[k]), p.shape[0] // tm

        def imap(i, *rest, lo=lo, n_tiles=n_tiles):
            return (jnp.clip(i - lo, 0, n_tiles - 1), 0)
        specs.append(pl.BlockSpec((tm, d), imap))
    return specs, [int(o) for o in offs[1:-1]]


def _pick_rows(refs, bounds, rows=slice(None)):
    val = refs[0][rows, :]
    for ref, lo in zip(refs[1:], bounds):
        val = jnp.where(pl.program_id(0) >= lo, ref[rows, :], val)
    return val


def _proj_kernel(*refs, bounds):
    n_parts = len(bounds) + 1
    x_refs = refs[:n_parts]
    mod_ref, g_ref, w_ref, o_ref, xn_ref = refs[n_parts:]

    @pl.when(pl.program_id(1) == 0)
    def _():
        y = _rms(_pick_rows(x_refs, bounds), g_ref[...])
        xn_ref[...] = (y * (1.0 + mod_ref[1:2, :]) + mod_ref[0:1, :]).astype(xn_ref.dtype)

    o_ref[...] = _dot(xn_ref[...], w_ref[...]).astype(o_ref.dtype)


def _proj_call(x_parts, mod, gain, w_in_r, nl, nb):
    t = sum(p.shape[0] for p in x_parts)
    d = x_parts[0].shape[1]
    tm = _pick(math.gcd(math.gcd(*[p.shape[0] for p in x_parts], t), nl), (1024, 512, 256))
    x_specs, bounds = _row_specs(x_parts, tm, d)
    return pl.pallas_call(
        functools.partial(_proj_kernel, bounds=bounds),
        grid=(t // tm, N_PROJ // PROJ_TN),
        in_specs=x_specs + [
            pl.BlockSpec((None, N_MOD, d), lambda i, j: (jnp.minimum(i * tm // nl, nb), 0, 0)),
            pl.BlockSpec((1, d), lambda i, j: (0, 0)),
            pl.BlockSpec((d, PROJ_TN), lambda i, j: (0, j))],
        out_specs=pl.BlockSpec((tm, PROJ_TN), lambda i, j: (i, j)),
        out_shape=jax.ShapeDtypeStruct((t, N_PROJ), _MXU),
        scratch_shapes=[pltpu.VMEM((tm, d), _MXU)],
        compiler_params=_cparams(("arbitrary", "arbitrary"), 56),
        name="proj",
    )(*x_parts, mod, gain, w_in_r)


def _la_kernel(*refs, kind, tt, n_ct, n_lt):
    if kind == "gla":
        (q_ref, k_ref, v_ref, gate_ref, lr_ref, wa2_ref, ba_ref, gain_ref,
         out_ref, st_ref, of_ref) = refs
    else:
        (q_ref, k_ref, v_ref, gate_ref, cos_ref, sin_ref, gain_ref,
         out_ref, st_ref, of_ref, dm_ref) = refs
    ph = pl.program_id(1)
    s = pl.program_id(2)
    ck = CHUNK if kind == "gla" else tt
    n_chunks = tt // ck

    @pl.when(s == 0)
    def _():
        st_ref[...] = jnp.zeros_like(st_ref)

    row = lax.broadcasted_iota(jnp.int32, (tt, tt), 0)
    col = lax.broadcasted_iota(jnp.int32, (tt, tt), 1)
    shift = ck.bit_length() - 1
    same_chunk = (row >> shift) == (col >> shift)
    tpos = lax.broadcasted_iota(jnp.int32, (tt, 1), 0).astype(F32)

    if kind == "ret":
        first = jnp.logical_and(pl.program_id(0) == 0, jnp.logical_and(ph == 0, s == 0))

        @pl.when(first)
        def _():
            for h in range(HEADS):
                lg = math.log(1.0 - 2.0 ** (-5.0 - h))
                dm_ref[h] = jnp.where(col <= row, jnp.exp((row - col).astype(F32) * lg), 0.0)
                dm_ref[HEADS + h] = jnp.where(col > row, jnp.exp((col - row).astype(F32) * lg), 0.0)

    def tile_outputs(backward):
        if backward:
            mask = jnp.logical_and(same_chunk, col > row)
            tri = jnp.logical_and(same_chunk, col >= row).astype(_MXU)
        else:
            mask = jnp.logical_and(same_chunk, col <= row)
            tri = jnp.logical_and(same_chunk, col <= row).astype(_MXU)
        order = list(reversed(range(n_chunks))) if backward else list(range(n_chunks))
        if kind == "gla":
            d = 1 if backward else 0
            z = _dot(lr_ref[...], wa2_ref[d]) + ba_ref[d]
            g = _log_sigmoid(z) * (1.0 / GLA_GATE_NORM)
            big_g = _dot01_exact(tri, g)
            lasts = [big_g[c * ck:c * ck + 1, :] if backward else big_g[(c + 1) * ck - 1:(c + 1) * ck, :]
                     for c in range(n_chunks)]
            g_last = jnp.concatenate([jnp.broadcast_to(r, (ck, r.shape[1])) for r in lasts], axis=0)
            e_pos = jnp.exp(big_g)
            e_neg = jnp.exp(-big_g)
            e_end = jnp.exp(g_last - big_g)
            decs = [jnp.exp(r) for r in lasts]
        outs = []
        for h in range(HEADS):
            ks = slice(h * DK, (h + 1) * DK)
            vs = slice(h * DV, (h + 1) * DV)
            qh = q_ref[:, ks].astype(F32)
            kh = k_ref[:, ks].astype(F32)
            vh = v_ref[:, vs]
            if kind == "gla":
                qh = qh * DK ** -0.5
                q_dec = (qh * e_pos[:, ks]).astype(_MXU)
                k_inv = (kh * e_neg[:, ks]).astype(_MXU)
                k_end = (kh * e_end[:, ks]).astype(_MXU)
                sc = jnp.where(mask, _dot_nt(q_dec, k_inv), 0.0)
                dec_h = [dcy[:, ks] for dcy in decs]
            else:
                kh = kh * DK ** -0.5
                cos = cos_ref[...]
                sin = sin_ref[...]
                qh = qh * cos + pltpu.roll(qh, DK // 2, 1) * sin
                kh = kh * cos + pltpu.roll(kh, DK // 2, 1) * sin
                lg = math.log(1.0 - 2.0 ** (-5.0 - h))
                steps = (ck - tpos) if backward else (tpos + 1.0)
                q_dec = (qh * jnp.exp(steps * lg)).astype(_MXU)
                k_end = (kh * jnp.exp((ck - steps) * lg)).astype(_MXU)
                sc = _dot_nt(qh.astype(_MXU), kh.astype(_MXU)) * dm_ref[(HEADS if backward else 0) + h]
                dec_h = [math.exp(ck * lg)] * n_chunks
            o_intra = _dot(sc.astype(_MXU), vh)
            st = st_ref[h]
            parts = [None] * n_chunks
            for c in order:
                rs = slice(c * ck, (c + 1) * ck)
                parts[c] = o_intra[rs] + _dot_nt(q_dec[rs], st.astype(_MXU))
                st = st * dec_h[c] + _dot_tn(vh[rs], k_end[rs])
            st_ref[h] = st
            outs.append(parts[0] if n_chunks == 1 else jnp.concatenate(parts, axis=0))
        return jnp.concatenate(outs, axis=1)

    @pl.when(ph == 0)
    def _():
        base = pl.multiple_of(s * tt, tt)
        of_ref[pl.ds(base, tt), :] = tile_outputs(False)

    @pl.when(ph == 1)
    def _():
        loc = jnp.where(s < n_ct, n_ct - 1 - s, n_ct + n_lt - 1 - (s - n_ct))
        base = pl.multiple_of(loc * tt, tt)
        o = of_ref[pl.ds(base, tt), :] + tile_outputs(True)
        gate = gate_ref[...].astype(F32)
        parts = []
        for h in range(HEADS):
            oh = o[:, h * DV:(h + 1) * DV]
            if kind == "ret":
                oh = oh - jnp.mean(oh, axis=-1, keepdims=True)
            parts.append(oh * lax.rsqrt(jnp.mean(oh * oh, axis=-1, keepdims=True) + EPS))
        normed = jnp.concatenate(parts, axis=1) * gain_ref[...]
        out_ref[...] = (normed * _silu(gate)).astype(out_ref.dtype)


def _la_call(kind, p, extra, gain, nb, nl, nc):
    t = p.shape[0]
    tt = _pick(math.gcd(nl, nc), (256, 128, 64))
    n_ct, n_lt = nc // tt, nl // tt
    lat_tiles = nb * n_lt
    cq, ck, cv, cg = ((C_GLA_Q, C_GLA_K, C_GLA_V, C_GLA_G) if kind == "gla"
                      else (C_RET_Q, C_RET_K, C_RET_V, C_RET_G))
    hd = HEADS * DK
    hv = HEADS * DV

    def loc_of(ph, s):
        back = jnp.where(s < n_ct, n_ct - 1 - s, n_ct + n_lt - 1 - (s - n_ct))
        return jnp.where(ph == 0, s, back)

    def row_blk(b, loc):
        return jnp.where(loc < n_ct, lat_tiles + b * n_ct + loc, b * n_lt + (loc - n_ct))

    def in_map(cblk):
        return lambda b, ph, s: (row_blk(b, loc_of(ph, s)), cblk)

    def second_pass_map(cblk):
        return lambda b, ph, s: (row_blk(b, loc_of(1, jnp.where(ph == 0, 0, s))), cblk)

    in_specs = [pl.BlockSpec((tt, hd), in_map(cq // hd)),
                pl.BlockSpec((tt, hd), in_map(ck // hd)),
                pl.BlockSpec((tt, hv), in_map(cv // hv)),
                pl.BlockSpec((tt, hv), second_pass_map(cg // hv))]
    args = [p, p, p, p]
    if kind == "gla":
        wa2p, ba = extra
        in_specs += [pl.BlockSpec((tt, LANES), in_map(C_LR // LANES)),
                     pl.BlockSpec((2, LANES, hd), lambda b, ph, s: (0, 0, 0)),
                     pl.BlockSpec((2, 1, hd), lambda b, ph, s: (0, 0, 0))]
        args += [p, wa2p, ba]
    else:
        cos_t, sin_t = extra

        def rope_map(b, ph, s):
            loc = loc_of(ph, s)
            return (jnp.where(loc < n_ct, n_lt + loc, loc - n_ct), 0)
        in_specs += [pl.BlockSpec((tt, DK), rope_map), pl.BlockSpec((tt, DK), rope_map)]
        args += [cos_t, sin_t]
    in_specs.append(pl.BlockSpec((1, hv), lambda b, ph, s: (0, 0)))
    args.append(gain)
    return pl.pallas_call(
        functools.partial(_la_kernel, kind=kind, tt=tt, n_ct=n_ct, n_lt=n_lt),
        grid=(nb, 2, n_ct + n_lt),
        in_specs=in_specs,
        out_specs=pl.BlockSpec((tt, hv), second_pass_map(0)),
        out_shape=jax.ShapeDtypeStruct((t, hv), _MXU),
        scratch_shapes=[pltpu.VMEM((HEADS, DV, DK), F32),
                        pltpu.VMEM((nc + nl, hv), F32)]
        + ([pltpu.VMEM((2 * HEADS, tt, tt), F32)] if kind == "ret" else []),
        compiler_params=_cparams(("arbitrary", "arbitrary", "arbitrary"), 48),
        name=kind,
    )(*args)


def _scan_group(a, b, h, reverse):
    row = lax.broadcasted_iota(jnp.int32, a.shape, 0)
    for sft in (1, 2, 4):
        if reverse:
            a_sh = pltpu.roll(a, SUBLANES - sft, 0)
            b_sh = pltpu.roll(b, SUBLANES - sft, 0)
            m = row < SUBLANES - sft
        else:
            a_sh = pltpu.roll(a, sft, 0)
            b_sh = pltpu.roll(b, sft, 0)
            m = row >= sft
        b = jnp.where(m, a * b_sh + b, b)
        a = jnp.where(m, a * a_sh, a)
    hh = a * h + b
    return hh, (hh[0:1, :] if reverse else hh[SUBLANES - 1:SUBLANES, :])


def _lru_kernel(x_ref, gel_ref, cw_ref, cb_ref, wa_ref, ba_ref, wi_ref, bi_ref, lam_ref, h0_ref,
                out_ref, hn_ref, xpad_ref, xc_ref, hf_ref, *, seg, tile):
    n_tiles = seg // tile
    groups = tile // SUBLANES
    cw = cw_ref[...]
    cb = cb_ref[...]
    zeros8 = jnp.zeros((SUBLANES, xpad_ref.shape[1]), F32)
    xpad_ref[0:SUBLANES, :] = zeros8
    xpad_ref[SUBLANES + seg:2 * SUBLANES + seg, :] = zeros8

    def copy_in(i, carry):
        t0 = pl.multiple_of(i * tile, tile)
        xpad_ref[pl.ds(t0 + SUBLANES, tile), :] = x_ref[pl.ds(t0, tile), :].astype(F32)
        return carry
    lax.fori_loop(0, n_tiles, copy_in, 0)

    neg_c = [(-0.5 * LRU_C) * _log_sigmoid(lam_ref[d:d + 1, :]) for d in range(2)]

    def gates(xc, d):
        xb = xc.astype(_MXU)
        tr = jnp.tanh(_dot(xb, wa_ref[d]) + ba_ref[d:d + 1, :])
        ig = 0.5 * jnp.tanh(_dot(xb, wi_ref[d]) + bi_ref[d:d + 1, :]) + 0.5
        nla = neg_c[d] * tr + neg_c[d]
        a = jnp.exp(-nla)
        b = jnp.sqrt(jnp.tanh(nla) * (a * a + 1.0)) * (ig * xc)
        return a, b

    def fwd_tile(i, h):
        t0 = pl.multiple_of(i * tile, tile)
        win = xpad_ref[pl.ds(t0, tile + 2 * SUBLANES), :]
        xc = (cw[0:1, :] * win[6:6 + tile] + cw[1:2, :] * win[7:7 + tile]
              + cw[2:3, :] * win[8:8 + tile] + cw[3:4, :] * win[9:9 + tile] + cb)
        xc_ref[pl.ds(t0, tile), :] = xc
        a, b = gates(xc, 0)
        for g in range(groups):
            rs = slice(g * SUBLANES, (g + 1) * SUBLANES)
            hh, h = _scan_group(a[rs], b[rs], h, False)
            hf_ref[pl.ds(t0 + g * SUBLANES, SUBLANES), :] = hh
        return h
    h_f = lax.fori_loop(0, n_tiles, fwd_tile, h0_ref[0:1, :])

    def bwd_tile(j, h):
        i = n_tiles - 1 - j
        t0 = pl.multiple_of(i * tile, tile)
        xc = xc_ref[pl.ds(t0, tile), :]
        a, b = gates(xc, 1)
        pair = 2 * SUBLANES
        for q in reversed(range(tile // pair)):
            hi = slice(q * pair + SUBLANES, (q + 1) * pair)
            lo = slice(q * pair, q * pair + SUBLANES)
            hh_hi, h = _scan_group(a[hi], b[hi], h, True)
            hh_lo, h = _scan_group(a[lo], b[lo], h, True)
            rows = pl.ds(pl.multiple_of(t0 + q * pair, pair), pair)
            hsum = hf_ref[rows, :] + jnp.concatenate([hh_lo, hh_hi], axis=0)
            gel = gel_ref[rows, :].astype(F32)
            out_ref[rows, :] = (hsum * _gelu_tanh(gel)).astype(out_ref.dtype)
        return h
    h_b = lax.fori_loop(0, n_tiles, bwd_tile, h0_ref[1:2, :])
    hn_ref[0:1, :] = h_f
    hn_ref[1:2, :] = h_b


def _lru_call(p, prev_out, h0, lw, nb, seg, row_blk0):
    cw, cb, wa, ba, wi, bi, lam = lw
    t = p.shape[0]
    width = LRU_BLOCKS * DV
    cbw = width // LRU_BLOCKS
    tile = _pick(seg, (2048, 1024, 512, 256, 128, 64))
    xblk = C_LRU_X // cbw
    gblk = C_LRU_G // cbw
    in_specs = [pl.BlockSpec((seg, cbw), lambda b, c: (row_blk0 + b, xblk + c)),
                pl.BlockSpec((seg, cbw), lambda b, c: (row_blk0 + b, gblk + c)),
                pl.BlockSpec((4, cbw), lambda b, c: (0, c)),
                pl.BlockSpec((1, cbw), lambda b, c: (0, c)),
                pl.BlockSpec((2, None, cbw, cbw), lambda b, c: (0, c, 0, 0)),
                pl.BlockSpec((2, cbw), lambda b, c: (0, c)),
                pl.BlockSpec((2, None, cbw, cbw), lambda b, c: (0, c, 0, 0)),
                pl.BlockSpec((2, cbw), lambda b, c: (0, c)),
                pl.BlockSpec((2, cbw), lambda b, c: (0, c)),
                pl.BlockSpec((None, 2, cbw), lambda b, c: (b, 0, c))]
    in_specs.append(pl.BlockSpec(memory_space=pl.ANY))
    args = [p, p, cw, cb, wa, ba, wi, bi, lam, h0, prev_out]

    def body(*refs):
        _lru_kernel(*(refs[:10] + refs[11:]), seg=seg, tile=tile)

    return pl.pallas_call(
        body,
        grid=(nb, LRU_BLOCKS),
        in_specs=in_specs,
        out_specs=[pl.BlockSpec((seg, cbw), lambda b, c: (row_blk0 + b, c)),
                   pl.BlockSpec((None, 2, cbw), lambda b, c: (b, 0, c))],
        out_shape=[jax.ShapeDtypeStruct((t, width), _MXU),
                   jax.ShapeDtypeStruct((nb, 2, width), F32)],
        scratch_shapes=[pltpu.VMEM((seg + 2 * SUBLANES, cbw), F32),
                        pltpu.VMEM((seg, cbw), F32),
                        pltpu.VMEM((seg, cbw), F32)],
        input_output_aliases={10: 0},
        compiler_params=_cparams(("arbitrary", "arbitrary"), 48),
        name="lru",
    )(*args)


def _finish_kernel(*refs, bounds):
    n_parts = len(bounds) + 1
    xs_refs = refs[:n_parts]
    (gla_ref, lru_ref, ret_ref, m0_ref, m1_ref, m2_ref, wb_ref, wo_ref,
     mod_ref, g2_ref, wr_ref, br_ref, tril_ref,
     xo_ref, xn_ref, idx_ref, wgt_ref, rank_ref, cnt_ref, carry_ref) = refs[n_parts:]

    @pl.when(pl.program_id(0) == 0)
    def _():
        carry_ref[...] = jnp.zeros_like(carry_ref)

    hm = tril_ref.shape[0]
    total = carry_ref[0:1, :]
    for sub in range(xo_ref.shape[0] // hm):
        rs = slice(sub * hm, (sub + 1) * hm)
        merged = (_sigmoid(m0_ref[rs, :].astype(F32)) * _dot(gla_ref[rs, :], wb_ref[0])
                  + _sigmoid(m1_ref[rs, :].astype(F32)) * _dot(lru_ref[rs, :], wb_ref[1])
                  + _sigmoid(m2_ref[rs, :].astype(F32)) * _dot(ret_ref[rs, :], wb_ref[2]))
        x = _pick_rows(xs_refs, bounds, rs) + mod_ref[2:3, :] * _dot(merged.astype(_MXU), wo_ref[...])
        xo_ref[rs, :] = x
        xn = _rms(x, g2_ref[...]) * (1.0 + mod_ref[4:5, :]) + mod_ref[3:4, :]
        xn_ref[rs, :] = xn

        logits = _dot(xn.astype(_MXU), wr_ref[...]) + br_ref[...]
        lane = lax.broadcasted_iota(jnp.int32, logits.shape, 1)
        ids, vals = [], []
        for _ in range(TOP_K):
            m = jnp.max(logits, axis=1, keepdims=True)
            sel = jnp.min(jnp.where(logits == m, lane.astype(F32), float(LANES)), axis=1,
                          keepdims=True).astype(jnp.int32)
            ids.append(sel)
            vals.append(m)
            logits = jnp.where(lane == sel, -jnp.inf, logits)
        ex = [jnp.exp(v - vals[0]) for v in vals]
        denom = ex[0] + ex[1] + ex[2] + ex[3]
        onehot = jnp.zeros(logits.shape, F32)
        for sel in ids:
            onehot = onehot + (lane == sel).astype(F32)
        before = _dot(tril_ref[...], onehot.astype(_MXU)) + total
        idx_o = jnp.zeros(logits.shape, jnp.int32)
        rank_o = jnp.zeros(logits.shape, jnp.int32)
        wgt_o = jnp.zeros(logits.shape, F32)
        for k in range(TOP_K):
            rk = jnp.sum(jnp.where(lane == ids[k], before, 0.0), axis=1, keepdims=True)
            idx_o = jnp.where(lane == k, ids[k], idx_o)
            rank_o = jnp.where(lane == k, rk.astype(jnp.int32), rank_o)
            wgt_o = jnp.where(lane == k, ex[k] / denom, wgt_o)
        idx_ref[rs, :] = idx_o
        rank_ref[rs, :] = rank_o
        wgt_ref[rs, :] = wgt_o
        total = total + jnp.sum(onehot, axis=0, keepdims=True)
    carry_ref[...] = jnp.broadcast_to(total, carry_ref.shape)
    cnt_ref[...] = jnp.broadcast_to(total, cnt_ref.shape)


def _finish_call(gla, lru, ret, p, wb, wo, xs_parts, mod, g2, wr, br, n_rows, nl, nb):
    d = xs_parts[0].shape[1]
    tm = _pick(math.gcd(math.gcd(*[q.shape[0] for q in xs_parts], n_rows), nl), (512, 256))
    xs_specs, bounds = _row_specs(xs_parts, tm, d)
    hm = tm
    tril = jnp.asarray(np.tril(np.ones((hm, hm), np.float32), -1), _MXU)
    row = lambda i: (i, 0)
    const2 = lambda i: (0, 0)
    mblk = C_MERGE // d
    in_specs = xs_specs + [
                pl.BlockSpec((tm, d), row), pl.BlockSpec((tm, d), row), pl.BlockSpec((tm, d), row),
                pl.BlockSpec((tm, d), lambda i: (i, mblk)),
                pl.BlockSpec((tm, d), lambda i: (i, mblk + 1)),
                pl.BlockSpec((tm, d), lambda i: (i, mblk + 2)),
                pl.BlockSpec((3, d, d), lambda i: (0, 0, 0)),
                pl.BlockSpec((d, d), const2),
                pl.BlockSpec((None, N_MOD, d), lambda i: (jnp.minimum(i * tm // nl, nb), 0, 0)),
                pl.BlockSpec((1, d), const2),
                pl.BlockSpec((d, LANES), const2),
                pl.BlockSpec((1, LANES), const2),
                pl.BlockSpec((hm, hm), const2)]
    out_specs = [pl.BlockSpec((tm, d), row), pl.BlockSpec((tm, d), row),
                 pl.BlockSpec((tm, LANES), row), pl.BlockSpec((tm, LANES), row),
                 pl.BlockSpec((tm, LANES), row), pl.BlockSpec((SUBLANES, LANES), const2)]
    out_shape = [jax.ShapeDtypeStruct((n_rows, d), F32), jax.ShapeDtypeStruct((n_rows, d), F32),
                 jax.ShapeDtypeStruct((n_rows, LANES), jnp.int32),
                 jax.ShapeDtypeStruct((n_rows, LANES), F32),
                 jax.ShapeDtypeStruct((n_rows, LANES), jnp.int32),
                 jax.ShapeDtypeStruct((SUBLANES, LANES), F32)]
    return pl.pallas_call(
        functools.partial(_finish_kernel, bounds=bounds),
        grid=(n_rows // tm,),
        in_specs=in_specs,
        out_specs=out_specs,
        out_shape=out_shape,
        scratch_shapes=[pltpu.VMEM((SUBLANES, LANES), F32)],
        compiler_params=_cparams(("arbitrary",), 56),
        name="finish",
    )(*xs_parts, gla, lru, ret, p, p, p, wb, wo, mod, g2, wr, br, tril)


def _slot_kernel(idx_ref, rank_ref, start_ref, slot_ref):
    idx = idx_ref[...].astype(F32)
    lane = lax.broadcasted_iota(jnp.int32, idx.shape, 1)
    start = start_ref[...]
    out = jnp.zeros(idx.shape, F32)
    for k in range(TOP_K):
        sel = jnp.sum(jnp.where(lane == k, idx, 0.0), axis=1, keepdims=True).astype(jnp.int32)
        st = jnp.sum(jnp.where(lane == sel, start, 0.0), axis=1, keepdims=True)
        out = jnp.where(lane == k, st, out)
    slot_ref[...] = out.astype(jnp.int32) + rank_ref[...]


def _slot_call(idx, rank, pad_start):
    n = idx.shape[0]
    tm = _pick(n, (2048, 1024, 512, 256))
    row = lambda i: (i, 0)
    return pl.pallas_call(
        _slot_kernel,
        grid=(n // tm,),
        in_specs=[pl.BlockSpec((tm, LANES), row), pl.BlockSpec((tm, LANES), row),
                  pl.BlockSpec((1, LANES), lambda i: (0, 0))],
        out_specs=pl.BlockSpec((tm, LANES), row),
        out_shape=jax.ShapeDtypeStruct((n, LANES), jnp.int32),
        compiler_params=_cparams(("arbitrary",), 24),
        name="slots",
    )(idx, rank, pad_start)


def _dispatch_kernel(fill_ref, slot_ref, x_ref, dst_ref, zero_ref, sem, fill_sem, *, tm, bm, n_blocks):
    @pl.when(pl.program_id(0) == 0)
    def _():
        zero_ref[...] = jnp.zeros_like(zero_ref)

        def clear(blk):
            return pltpu.make_async_copy(zero_ref, dst_ref.at[pl.ds(pl.multiple_of(blk * bm, bm), bm)],
                                         fill_sem)

        def tail_start(j, carry):
            clear(j).start()
            return carry

        def tail_wait(j, carry):
            clear(j).wait()
            return carry

        for e in range(N_EXPERTS):
            @pl.when(fill_ref[0, e] >= 0)
            def _():
                clear(fill_ref[0, e]).start()
        lax.fori_loop(fill_ref[0, N_EXPERTS], n_blocks, tail_start, 0)
        for e in range(N_EXPERTS):
            @pl.when(fill_ref[0, e] >= 0)
            def _():
                clear(fill_ref[0, e]).wait()
        lax.fori_loop(fill_ref[0, N_EXPERTS], n_blocks, tail_wait, 0)

    def issue(g, carry):
        t0 = pl.multiple_of(g * SUBLANES, SUBLANES)
        for u in range(SUBLANES):
            for k in range(TOP_K):
                pltpu.make_async_copy(
                    x_ref.at[pl.ds(t0 + u, 1)],
                    dst_ref.at[pl.ds(slot_ref[0, (t0 + u) * TOP_K + k], 1)],
                    sem).start(priority=k % 2)
        return carry
    lax.fori_loop(0, tm // SUBLANES, issue, 0)
    for k in range(TOP_K):
        pltpu.make_async_copy(x_ref, dst_ref.at[pl.ds(0, tm)], sem).wait()


def _dispatch_call(fill, slots3, xn, n_blocks, bm):
    n, d = xn.shape
    tm = slots3.shape[2] // TOP_K
    return pl.pallas_call(
        functools.partial(_dispatch_kernel, tm=tm, bm=bm, n_blocks=n_blocks),
        grid=(n // tm,),
        in_specs=[pl.BlockSpec((1, LANES), lambda i: (0, 0), memory_space=pltpu.SMEM),
                  pl.BlockSpec((None, 1, tm * TOP_K), lambda i: (i, 0, 0), memory_space=pltpu.SMEM),
                  pl.BlockSpec((tm, d), lambda i: (i, 0))],
        out_specs=pl.BlockSpec(memory_space=pl.ANY),
        out_shape=jax.ShapeDtypeStruct((n_blocks * bm, d), F32),
        scratch_shapes=[pltpu.VMEM((bm, d), F32), pltpu.SemaphoreType.DMA(()),
                        pltpu.SemaphoreType.DMA(())],
        compiler_params=_cparams(("arbitrary",), 24),
        name="dispatch",
    )(fill, slots3, xn)


def _experts_kernel(be_ref, nu_ref, x_ref, wgu_ref, bgu_ref, wd_ref, bd_ref, perm_ref, y_ref,
                    wgu_s, wd_s):
    j = pl.program_id(0)
    f = wd_ref.shape[0]
    grp = perm_ref.shape[0]
    half = grp // 2
    active = j < nu_ref[0]
    changed = jnp.logical_or(j == 0, be_ref[j] != be_ref[jnp.maximum(j - 1, 0)])

    @pl.when(jnp.logical_and(active, changed))
    def _():
        for g in range(2 * f // grp):
            cols = slice(g * grp, (g + 1) * grp)
            wgu_s[:, cols] = _dot(wgu_ref[:, cols].astype(_MXU), perm_ref[...]).astype(_MXU)
        wd_s[...] = wd_ref[...].astype(_MXU)

    @pl.when(active)
    def _():
        gu = _dot(x_ref[...].astype(_MXU), wgu_s[...]) + bgu_ref[...]
        acts = []
        for g in range(2 * f // grp):
            gate = jnp.minimum(gu[:, g * grp:g * grp + half], SWIGLU_LIMIT)
            up = jnp.clip(gu[:, g * grp + half:(g + 1) * grp], -SWIGLU_LIMIT, SWIGLU_LIMIT)
            acts.append((gate * _sigmoid(SWIGLU_ALPHA * gate) * (up + 1.0)).astype(_MXU))
        y_ref[...] = _dot(jnp.concatenate(acts, axis=1), wd_s[...]) + bd_ref[...]

    @pl.when(jnp.logical_not(active))
    def _():
        y_ref[...] = jnp.zeros_like(y_ref)


def _experts_call(li, block_e, n_used, x_sorted, w_gu, bgu_r, w_down, bd, bm):
    n_slots, d = x_sorted.shape
    f = w_down.shape[2]
    grp = 2 * LANES
    src = np.concatenate([np.arange(0, grp, 2), np.arange(1, grp, 2)])
    perm_np = np.zeros((grp, grp), np.float32)
    perm_np[src, np.arange(grp)] = 1.0
    perm = jnp.asarray(perm_np, _MXU)

    def xmap(j, be, nu):
        return (jnp.minimum(j, nu[0] - 1), 0)

    def wmap(j, be, nu):
        return (li, be[j], 0, 0)

    grid_spec = pltpu.PrefetchScalarGridSpec(
        num_scalar_prefetch=2,
        grid=(n_slots // bm,),
        in_specs=[pl.BlockSpec((bm, d), xmap),
                  pl.BlockSpec((None, None, d, 2 * f), wmap),
                  pl.BlockSpec((None, None, 1, 2 * f), wmap),
                  pl.BlockSpec((None, None, f, d), wmap),
                  pl.BlockSpec((None, None, 1, d), wmap),
                  pl.BlockSpec((grp, grp), lambda j, be, nu: (0, 0))],
        out_specs=pl.BlockSpec((bm, d), lambda j, be, nu: (j, 0)),
        scratch_shapes=[pltpu.VMEM((d, 2 * f), _MXU), pltpu.VMEM((f, d), _MXU)])
    return pl.pallas_call(
        _experts_kernel,
        grid_spec=grid_spec,
        out_shape=jax.ShapeDtypeStruct((n_slots, d), F32),
        compiler_params=_cparams(("arbitrary",), 56),
        name="experts",
    )(block_e, n_used, x_sorted, w_gu, bgu_r, w_down, bd, perm)


def _combine_kernel(slot_ref, wgt_ref, xs_ref, mod_ref, fin_ref, y_ref, out_ref, buf_ref, sem,
                    *, tm, final):
    def issue(g, carry):
        t0 = pl.multiple_of(g * SUBLANES, SUBLANES)
        for u in range(SUBLANES):
            for k in range(TOP_K):
                pltpu.make_async_copy(
                    y_ref.at[pl.ds(slot_ref[0, (t0 + u) * TOP_K + k], 1)],
                    buf_ref.at[k, pl.ds(t0 + u, 1)], sem).start(priority=k % 2)
        return carry
    lax.fori_loop(0, tm // SUBLANES, issue, 0)
    for k in range(TOP_K):
        pltpu.make_async_copy(y_ref.at[pl.ds(0, tm)], buf_ref.at[k], sem).wait()

    wgt = wgt_ref[...]
    acc = wgt[:, 0:1] * buf_ref[0]
    for k in range(1, TOP_K):
        acc = acc + wgt[:, k:k + 1] * buf_ref[k]
    x = xs_ref[...] + mod_ref[5:6, :] * acc
    out_ref[...] = _rms(x, fin_ref[...]) if final else x


def _combine_call(slots3, wgt, xs, mod, fin, y_sorted, nl, nb, final):
    n, d = xs.shape
    tm = slots3.shape[2] // TOP_K
    return pl.pallas_call(
        functools.partial(_combine_kernel, tm=tm, final=final),
        grid=(n // tm,),
        in_specs=[pl.BlockSpec((None, 1, tm * TOP_K), lambda i: (i, 0, 0), memory_space=pltpu.SMEM),
                  pl.BlockSpec((tm, LANES), lambda i: (i, 0)),
                  pl.BlockSpec((tm, d), lambda i: (i, 0)),
                  pl.BlockSpec((None, N_MOD, d), lambda i: (jnp.minimum(i * tm // nl, nb), 0, 0)),
                  pl.BlockSpec((1, d), lambda i: (0, 0)),
                  pl.BlockSpec(memory_space=pl.ANY)],
        out_specs=pl.BlockSpec((tm, d), lambda i: (i, 0)),
        out_shape=jax.ShapeDtypeStruct((n, d), F32),
        scratch_shapes=[pltpu.VMEM((TOP_K, tm, d), F32), pltpu.SemaphoreType.DMA(())],
        compiler_params=_cparams(("arbitrary",), 48),
        name="combine",
    )(slots3, wgt, xs, mod, fin, y_sorted)


MOE_BM = 512


def _moe(li, xn, idx, wgt, rank, counts, xs_mid, mod, fin, w_gu, bgu_r, w_down, bd, nl, nb, final):
    n = xn.shape[0]
    bm = MOE_BM
    tm = _pick(math.gcd(n, nl), (1024, 512, 256, 128))
    n_blocks = -(-(n * TOP_K) // bm) + N_EXPERTS
    cnt = counts[0, :N_EXPERTS].astype(jnp.int32)
    padded = (cnt + bm - 1) // bm * bm
    pad_end = jnp.cumsum(padded)
    pad_start = pad_end - padded
    n_used = (pad_end[-1] // bm).astype(jnp.int32)
    blk = jnp.arange(n_blocks, dtype=jnp.int32)
    first_row = jnp.minimum(blk, n_used - 1) * bm
    block_e = jnp.minimum(jnp.sum(pad_end[None, :] <= first_row[:, None], axis=1),
                          N_EXPERTS - 1).astype(jnp.int32)
    start_row = jnp.zeros((1, LANES), F32).at[0, :N_EXPERTS].set(pad_start.astype(F32))
    slots = _slot_call(idx, rank, start_row)
    slots3 = slots[:, :TOP_K].reshape(n // tm, 1, tm * TOP_K)
    last_blk = jnp.where(cnt > 0, pad_end // bm - 1, -1).astype(jnp.int32)
    fill = jnp.zeros((1, LANES), jnp.int32).at[0, :N_EXPERTS].set(last_blk).at[0, N_EXPERTS].set(n_used)
    x_sorted = _dispatch_call(fill, slots3, xn, n_blocks, bm)
    y_sorted = _experts_call(li, block_e, n_used.reshape(1), x_sorted, w_gu, bgu_r, w_down, bd, bm)
    return _combine_call(slots3, wgt, xs_mid, mod, fin, y_sorted, nl, nb, final)


def _rope_tables(nl, nc):
    rows = nl // GRID_W
    r = np.broadcast_to(np.arange(rows)[:, None], (rows, GRID_W)).reshape(-1).astype(np.float32)
    c = np.broadcast_to(np.arange(GRID_W)[None, :], (rows, GRID_W)).reshape(-1).astype(np.float32)
    n_freq = DK // 4
    inv = (ROPE_BASE ** (-jnp.arange(n_freq, dtype=F32) / n_freq))
    ang = jnp.concatenate([jnp.asarray(r)[:, None] * inv, jnp.asarray(c)[:, None] * inv], axis=-1)
    cos, sin = jnp.cos(ang), jnp.sin(ang)
    cos2 = jnp.concatenate([cos, cos], axis=-1)
    sin2 = jnp.concatenate([-sin, sin], axis=-1)
    cos_t = jnp.concatenate([cos2, jnp.ones((nc, DK), F32)], axis=0)
    sin_t = jnp.concatenate([sin2, jnp.zeros((nc, DK), F32)], axis=0)
    return cos_t, sin_t


def kernel(x, c, ctx, c_ctx, w_ada, b_ada, norm1, norm2, w_in, gla_wa2, gla_ba, gla_norm, lru_conv_w, lru_conv_b, lru_wa, lru_ba, lru_wi, lru_bi, lru_lam, ret_norm, w_branch, w_out, w_router, b_router, w_gu, b_gu, w_down, b_down, final_norm):
    nb, nl, d = x.shape
    nc = ctx.shape[1]
    depth = w_ada.shape[0]
    n_lat = nb * nl
    f = w_down.shape[2]

    n_tok = n_lat + nb * nc
    xs = (x.reshape(n_lat, d), ctx.reshape(nb * nc, d))
    mod_rows = -(-(nb + 1) // SUBLANES) * SUBLANES
    cc = jnp.zeros((mod_rows, d), F32).at[:nb].set(c).at[nb].set(c_ctx)
    mod_all = _ada_call(cc, w_ada, b_ada).reshape(depth, mod_rows, N_MOD, d)
    cos_t, sin_t = _rope_tables(nl, nc)
    bgu_r = b_gu.reshape(depth, N_EXPERTS, 2 * f // (2 * LANES), LANES, 2)
    bgu_r = jnp.swapaxes(bgu_r, -1, -2).reshape(depth, N_EXPERTS, 1, 2 * f)
    bd_r = b_down.reshape(depth, N_EXPERTS, 1, d)
    lru_buf = jnp.zeros((n_tok, lru_conv_w.shape[2]), _MXU)

    out = None
    for li in range(depth):
        last = li == depth - 1
        mod = mod_all[li]
        w = w_in[li]
        w_in_r = jnp.concatenate(
            [w[:, :3072], w[:, 3104:], w[:, 3072:3104],
             jnp.zeros((d, N_PROJ - w.shape[1]), w.dtype)], axis=1).astype(_MXU)
        p = _proj_call(xs, mod, norm1[li].reshape(1, d), w_in_r, nl, nb)

        wa2p = jnp.zeros((2, LANES, HEADS * DK), F32)
        wa2p = wa2p.at[0, :GLA_RANK].set(gla_wa2[li, 0]).at[1, GLA_RANK:2 * GLA_RANK].set(gla_wa2[li, 1])
        gla = _la_call("gla", p, (wa2p.astype(_MXU), gla_ba[li].reshape(2, 1, HEADS * DK)),
                       gla_norm[li].reshape(1, HEADS * DV), nb, nl, nc)
        ret = _la_call("ret", p, (cos_t, sin_t), ret_norm[li].reshape(1, HEADS * DV), nb, nl, nc)

        lw = (lru_conv_w[li], lru_conv_b[li].reshape(1, -1), (0.5 * lru_wa[li]).astype(_MXU),
              0.5 * lru_ba[li], (0.5 * lru_wi[li]).astype(_MXU), 0.5 * lru_bi[li], lru_lam[li])
        width = lru_conv_w.shape[2]
        lru_c, h_ctx = _lru_call(p, lru_buf, jnp.zeros((nb, 2, width), F32), lw, nb, nc, n_lat // nc)
        lru, _ = _lru_call(p, lru_c, h_ctx, lw, nb, nl, 0)
        lru_buf = lru

        n_rows = n_lat if last else n_tok
        wr = jnp.zeros((d, LANES), F32).at[:, :N_EXPERTS].set(w_router[li]).astype(_MXU)
        br = jnp.full((1, LANES), NEG_BIG, F32).at[0, :N_EXPERTS].set(b_router[li])
        xs_mid, xn2, idx, wgt, rank, counts = _finish_call(
            gla, lru, ret, p, w_branch[li].astype(_MXU), w_out[li].astype(_MXU), xs, mod,
            norm2[li].reshape(1, d), wr, br, n_rows, nl, nb)

        out = _moe(li, xn2, idx, wgt, rank, counts, xs_mid, mod, final_norm.reshape(1, d),
                   w_gu, bgu_r, w_down, bd_r, nl, nb, last)
        xs = (out,)
    return out.reshape(nb, nl, d)
```
